```python
import jax, jax.numpy as jnp
from jax import lax
import numpy as np

D_MODEL = 2048
BATCH = 2
SEQ = 4096
DEPTH = 1

RMS_EPS = 1e-6
ROPE_THETA = 10000.0
MIX_WIDTH = D_MODEL
HG_WIDTH = MIX_WIDTH // 2
HG_DK = 128
HG_DV = 128
HG_HEADS = HG_WIDTH // HG_DV
HG_CHUNK = 64
NSA_WIDTH = MIX_WIDTH - HG_WIDTH
NSA_DH = 64
NSA_HEADS = NSA_WIDTH // NSA_DH
NSA_KV = 2
NSA_REP = NSA_HEADS // NSA_KV
CMP_LEN = 32
CMP_STRIDE = 16
CMP_HIDDEN = 256
SEL_LEN = 64
SEL_TOP = 16
WINDOW = 512
Q_BLOCK = 128
MEM_LEN = 256
X_HEADS = 4
X_DH = 128
D_FF = ((8 * D_MODEL // 3) + 255) // 256 * 256
IN_SIZES = (HG_HEADS * HG_DK, HG_HEADS * HG_DK, HG_WIDTH, HG_WIDTH,
            NSA_WIDTH,
            NSA_KV * NSA_DH, NSA_KV * NSA_DH,
            NSA_KV * NSA_DH, NSA_KV * NSA_DH,
            NSA_KV * NSA_DH, NSA_KV * NSA_DH,
            NSA_HEADS * 3)
D_IN = sum(IN_SIZES)

kernel_name = "hybrid_hgrn2_nsa_macaron_layer"


def rmsnorm(x, w):
    xf = x.astype(jnp.float32)
    y = xf * lax.rsqrt(jnp.mean(xf * xf, axis=-1, keepdims=True) + RMS_EPS)
    return (y * w.astype(jnp.float32)).astype(x.dtype)


def swiglu(h, w_gate, w_up, w_down):
    return (jax.nn.silu(h @ w_gate) * (h @ w_up)) @ w_down


def rope(t, positions):
    half = t.shape[-1] // 2
    inv = ROPE_THETA ** (-jnp.arange(half, dtype=jnp.float32) / half)
    ang = positions.astype(jnp.float32)[..., None] * inv
    cos, sin = jnp.cos(ang)[:, :, None, :], jnp.sin(ang)[:, :, None, :]
    tf = t.astype(jnp.float32)
    t1, t2 = tf[..., :half], tf[..., half:]
    return jnp.concatenate([t1 * cos - t2 * sin, t2 * cos + t1 * sin], axis=-1).astype(t.dtype)


def masked_softmax(s, mask):
    s = jnp.where(mask, s.astype(jnp.float32), -jnp.inf)
    m = jnp.max(s, axis=-1, keepdims=True)
    m = jnp.where(jnp.isfinite(m), m, 0.0)
    e = jnp.where(mask, jnp.exp(s - m), 0.0)
    return e / jnp.maximum(jnp.sum(e, axis=-1, keepdims=True), 1e-30)


def hgrn2_mix(q, f_logit, i, g, lb, norm_w):
    B, S = q.shape[:2]
    nc = S // HG_CHUNK
    lb = lb.reshape(HG_HEADS, HG_DK)
    f = lb + (1.0 - lb) * jax.nn.sigmoid(f_logit.astype(jnp.float32))
    logf = jnp.log(f)
    k = 1.0 - f

    def chunks(t, d):
        return t.astype(jnp.float32).reshape(B, nc, HG_CHUNK, HG_HEADS, d).transpose(1, 0, 3, 2, 4)

    causal = jnp.tril(jnp.ones((HG_CHUNK, HG_CHUNK), dtype=bool))[:, :, None]

    def step(state, inp):
        qc, kc, lfc, vc = inp
        b = jnp.cumsum(lfc, axis=2)
        o_inter = jnp.einsum('bhck,bhkv->bhcv', qc * jnp.exp(b), state)
        diff = b[:, :, :, None, :] - b[:, :, None, :, :]
        decay = jnp.exp(jnp.where(causal, diff, -jnp.inf))
        att = jnp.einsum('bhtk,bhsk,bhtsk->bhts', qc, kc, decay)
        o_intra = jnp.einsum('bhts,bhsv->bhtv', att, vc)
        b_last = b[:, :, -1:, :]
        new_state = jnp.exp(b_last[:, :, 0, :])[..., None] * state + \
            jnp.einsum('bhsk,bhsv->bhkv', kc * jnp.exp(b_last - b), vc)
        return new_state, o_inter + o_intra

    state0 = jnp.zeros((B, HG_HEADS, HG_DK, HG_DV), jnp.float32)
    _, o = lax.scan(step, state0, (chunks(q, HG_DK), chunks(k, HG_DK), chunks(logf, HG_DK), chunks(i, HG_DV)))
    o = o.transpose(1, 0, 3, 2, 4).reshape(B, S, HG_HEADS, HG_DV)
    o = o * lax.rsqrt(jnp.mean(o * o, axis=-1, keepdims=True) + RMS_EPS)
    o = o.reshape(B, S, HG_WIDTH) * norm_w.astype(jnp.float32) * jax.nn.silu(g.astype(jnp.float32))
    return o.astype(q.dtype)


def nsa_mix(q, kc_tok, vc_tok, ks, vs, kw, vw, gate_logits, pe, ck_w1, ck_w2, cv_w1, cv_w2):
    B, S = q.shape[:2]
    G, R, dh = NSA_KV, NSA_REP, NSA_DH
    scale = dh ** -0.5
    n_cmp = (S - CMP_LEN) // CMP_STRIDE + 1
    n_sel = S // SEL_LEN
    n_top = min(SEL_TOP, n_sel)
    n_qb = S // Q_BLOCK

    cidx = jnp.arange(n_cmp)[:, None] * CMP_STRIDE + jnp.arange(CMP_LEN)[None, :]

    def compress(tok, w1, w2):
        blocks = tok[:, cidx] + pe[None, None, :, None, :]
        blocks = blocks.transpose(0, 1, 3, 2, 4).reshape(B, n_cmp, G, CMP_LEN * dh)
        return (jax.nn.silu(blocks @ w1) @ w2).transpose(0, 2, 1, 3)

    kc = compress(kc_tok, ck_w1, ck_w2)
    vc = compress(vc_tok, cv_w1, cv_w2)
    c_start = jnp.arange(n_cmp) * CMP_STRIDE
    cmp_end = c_start + CMP_LEN - 1
    s_start = jnp.arange(n_sel) * SEL_LEN
    overlap = jnp.clip(jnp.minimum(c_start[:, None] + CMP_LEN, s_start[None, :] + SEL_LEN)
                       - jnp.maximum(c_start[:, None], s_start[None, :]), 0, None).astype(jnp.float32) / CMP_LEN

    to_bg = lambda t: t.transpose(0, 2, 1, 3)
    ksb = to_bg(ks).reshape(B, G, n_sel, SEL_LEN, dh)
    vsb = to_bg(vs).reshape(B, G, n_sel, SEL_LEN, dh)
    kw_pad = jnp.pad(to_bg(kw), ((0, 0), (0, 0), (WINDOW, 0), (0, 0)))
    vw_pad = jnp.pad(to_bg(vw), ((0, 0), (0, 0), (WINDOW, 0), (0, 0)))
    qg = q.reshape(B, S, G, R, dh).transpose(0, 2, 3, 1, 4)
    bidx = jnp.arange(B)[:, None, None, None]
    gidx = jnp.arange(G)[None, :, None, None]
    blk_ids = jnp.arange(n_sel)

    def block(qb_i):
        q0 = qb_i * Q_BLOCK
        qb = lax.dynamic_slice_in_dim(qg, q0, Q_BLOCK, axis=3)
        tpos = q0 + jnp.arange(Q_BLOCK)
        s_c = jnp.einsum('bgrqd,bgnd->bgrqn', qb, kc) * scale
        p_c = masked_softmax(s_c, cmp_end[None, :] <= tpos[:, None])
        o_c = jnp.einsum('bgrqn,bgnd->bgrqd', p_c.astype(vc.dtype), vc)
        imp = jnp.einsum('bgrqn,nj->bgqj', p_c, overlap)
        cur = tpos // SEL_LEN
        forced = (blk_ids[None, :] == 0) | (blk_ids[None, :] == cur[:, None]) | (blk_ids[None, :] == cur[:, None] - 1)
        future = blk_ids[None, :] > cur[:, None]
        score = jnp.where(forced, jnp.inf, jnp.where(future, -jnp.inf, imp))
        _, sel = lax.top_k(score, n_top)
        k_sel = ksb[bidx, gidx, sel]
        v_sel = vsb[bidx, gidx, sel].reshape(B, G, Q_BLOCK, n_top * SEL_LEN, dh)
        tok = sel[..., None] * SEL_LEN + jnp.arange(SEL_LEN)
        mask_s = (tok <= tpos[None, None, :, None, None]).reshape(B, G, 1, Q_BLOCK, n_top * SEL_LEN)
        s_s = jnp.einsum('bgrqd,bgqntd->bgrqnt', qb, k_sel).reshape(B, G, R, Q_BLOCK, n_top * SEL_LEN) * scale
        p_s = masked_softmax(s_s, mask_s)
        o_s = jnp.einsum('bgrqm,bgqmd->bgrqd', p_s.astype(v_sel.dtype), v_sel)
        kwb = lax.dynamic_slice_in_dim(kw_pad, q0, Q_BLOCK + WINDOW, axis=2)
        vwb = lax.dynamic_slice_in_dim(vw_pad, q0, Q_BLOCK + WINDOW, axis=2)
        kpos = q0 - WINDOW + jnp.arange(Q_BLOCK + WINDOW)
        dpos = tpos[:, None] - kpos[None, :]
        mask_w = (dpos >= 0) & (dpos < WINDOW) & (kpos[None, :] >= 0)
        s_w = jnp.einsum('bgrqd,bgkd->bgrqk', qb, kwb) * scale
        p_w = masked_softmax(s_w, mask_w)
        o_w = jnp.einsum('bgrqk,bgkd->bgrqd', p_w.astype(vwb.dtype), vwb)
        return jnp.stack([o_c, o_s, o_w], axis=0)

    outs = lax.map(block, jnp.arange(n_qb))
    outs = outs.transpose(1, 2, 0, 5, 3, 4, 6).reshape(3, B, S, NSA_HEADS, dh)
    gates = jax.nn.sigmoid(gate_logits.reshape(B, S, NSA_HEADS, 3)).transpose(3, 0, 1, 2)[..., None]
    o = jnp.sum(gates.astype(outs.dtype) * outs, axis=0)
    return o.reshape(B, S, NSA_WIDTH)


def mem_cross_attn(h, m, wq, wk, wv, wo):
    B, S, _ = h.shape
    M = m.shape[1]
    q = (h @ wq).reshape(B, S, X_HEADS, X_DH)
    k = (m @ wk).reshape(B, M, X_HEADS, X_DH)
    v = (m @ wv).reshape(B, M, X_HEADS, X_DH)
    s = jnp.einsum('bshd,bmhd->bhsm', q, k) * (X_DH ** -0.5)
    p = jax.nn.softmax(s.astype(jnp.float32), axis=-1).astype(v.dtype)
    o = jnp.einsum('bhsm,bmhd->bshd', p, v).reshape(B, S, X_HEADS * X_DH)
    return o @ wo


def setup_inputs(seed: int = 0) -> dict:
    key = jax.random.key(seed)
    k = jax.random.split(key, 30)
    f32 = jnp.float32
    L = DEPTH

    def w(kk, shape, fan_in):
        return jax.random.normal(kk, shape, f32) * fan_in ** -0.5

    def gain(kk, shape):
        return 1.0 + 0.01 * jax.random.normal(kk, shape, f32)

    x = jax.random.normal(k[0], (BATCH, SEQ, D_MODEL), f32)
    mem = jax.random.normal(k[1], (BATCH, MEM_LEN, D_MODEL), f32)
    positions = (jax.random.randint(k[2], (BATCH, 1), 0, 1024, dtype=jnp.int32)
                 + jnp.arange(SEQ, dtype=jnp.int32)[None, :]).astype(jnp.int32)
    return {
        'x': x, 'mem': mem, 'positions': positions,
        'ffn1_norm': gain(k[3], (L, D_MODEL)),
        'ffn1_w_gate': w(k[4], (L, D_MODEL, D_FF), D_MODEL),
        'ffn1_w_up': w(k[5], (L, D_MODEL, D_FF), D_MODEL),
        'ffn1_w_down': w(k[6], (L, D_FF, D_MODEL), D_FF),
        'mix_norm': gain(k[7], (L, D_MODEL)),
        'w_in': w(k[8], (L, D_MODEL, D_IN), D_MODEL),
        'hgrn_lb_logits': 0.5 * jax.random.normal(k[9], (L + 1, HG_HEADS * HG_DK), f32),
        'hgrn_out_norm': gain(k[10], (L, HG_WIDTH)),
        'nsa_cmp_pe': 0.1 * jax.random.normal(k[11], (L, CMP_LEN, NSA_DH), f32),
        'nsa_cmp_k_w1': w(k[12], (L, CMP_LEN * NSA_DH, CMP_HIDDEN), CMP_LEN * NSA_DH),
        'nsa_cmp_k_w2': w(k[13], (L, CMP_HIDDEN, NSA_DH), CMP_HIDDEN),
        'nsa_cmp_v_w1': w(k[14], (L, CMP_LEN * NSA_DH, CMP_HIDDEN), CMP_LEN * NSA_DH),
        'nsa_cmp_v_w2': w(k[15], (L, CMP_HIDDEN, NSA_DH), CMP_HIDDEN),
        'w_out': w(k[16], (L, MIX_WIDTH, D_MODEL), MIX_WIDTH),
        'xattn_norm': gain(k[17], (L, D_MODEL)),
        'mem_norm': gain(k[18], (L, D_MODEL)),
        'xattn_wq': w(k[19], (L, D_MODEL, X_HEADS * X_DH), D_MODEL),
        'xattn_wk': w(k[20], (L, D_MODEL, X_HEADS * X_DH), D_MODEL),
        'xattn_wv': w(k[21], (L, D_MODEL, X_HEADS * X_DH), D_MODEL),
        'xattn_wo': w(k[22], (L, X_HEADS * X_DH, D_MODEL), X_HEADS * X_DH),
        'ffn2_norm': gain(k[23], (L, D_MODEL)),
        'ffn2_w_gate': w(k[24], (L, D_MODEL, D_FF), D_MODEL),
        'ffn2_w_up': w(k[25], (L, D_MODEL, D_FF), D_MODEL),
        'ffn2_w_down': w(k[26], (L, D_FF, D_MODEL), D_FF),
        'final_norm': gain(k[27], (D_MODEL,)),
    }


def reference(x, mem, positions, ffn1_norm, ffn1_w_gate, ffn1_w_up, ffn1_w_down, mix_norm, w_in,
              hgrn_lb_logits, hgrn_out_norm, nsa_cmp_pe, nsa_cmp_k_w1, nsa_cmp_k_w2, nsa_cmp_v_w1,
              nsa_cmp_v_w2, w_out, xattn_norm, mem_norm, xattn_wq, xattn_wk, xattn_wv, xattn_wo,
              ffn2_norm, ffn2_w_gate, ffn2_w_up, ffn2_w_down, final_norm):
    B, S, _ = x.shape
    lb_all = jnp.cumsum(jax.nn.softmax(hgrn_lb_logits.astype(jnp.float32), axis=0), axis=0)
    split_at = [int(v) for v in np.cumsum(IN_SIZES)[:-1]]
    kv_heads = lambda t: t.reshape(B, S, NSA_KV, NSA_DH)
    for l in range(DEPTH):
        x = x + 0.5 * swiglu(rmsnorm(x, ffn1_norm[l]), ffn1_w_gate[l], ffn1_w_up[l], ffn1_w_down[l])
        h = rmsnorm(x, mix_norm[l])
        hq, hf, hi, hg, nq, nkc, nvc, nks, nvs, nkw, nvw, ngate = jnp.split(h @ w_in[l], split_at, axis=-1)
        o_hg = hgrn2_mix(hq.reshape(B, S, HG_HEADS, HG_DK), hf.reshape(B, S, HG_HEADS, HG_DK),
                         hi.reshape(B, S, HG_HEADS, HG_DV), hg, lb_all[l], hgrn_out_norm[l])
        o_nsa = nsa_mix(rope(nq.reshape(B, S, NSA_HEADS, NSA_DH), positions),
                        rope(kv_heads(nkc), positions), kv_heads(nvc),
                        rope(kv_heads(nks), positions), kv_heads(nvs),
                        rope(kv_heads(nkw), positions), kv_heads(nvw), ngate,
                        nsa_cmp_pe[l], nsa_cmp_k_w1[l], nsa_cmp_k_w2[l], nsa_cmp_v_w1[l], nsa_cmp_v_w2[l])
        x = x + jnp.concatenate([o_hg, o_nsa], axis=-1) @ w_out[l]
        x = x + mem_cross_attn(rmsnorm(x, xattn_norm[l]), rmsnorm(mem, mem_norm[l]),
                               xattn_wq[l], xattn_wk[l], xattn_wv[l], xattn_wo[l])
        x = x + 0.5 * swiglu(rmsnorm(x, ffn2_norm[l]), ffn2_w_gate[l], ffn2_w_up[l], ffn2_w_down[l])
    return rmsnorm(x, final_norm)
```

```python
import functools

import numpy as np
import jax
import jax.numpy as jnp
from jax import lax
from jax.experimental import pallas as pl
from jax.experimental.pallas import tpu as pltpu

F32 = jnp.float32
BF16 = jnp.bfloat16

D_MODEL = 2048
BATCH = 2
SEQ = 4096
TOKENS = BATCH * SEQ
RMS_EPS = 1e-6
ROPE_THETA = 10000.0
HG_WIDTH = 1024
HG_HEADS = 8
HG_D = 128
HG_CHUNK = 64
HG_SUB = 16
NSA_WIDTH = 1024
NSA_DH = 64
NSA_HEADS = 16
NSA_KV = 2
NSA_REP = 8
CMP_LEN = 32
CMP_STRIDE = 16
CMP_HIDDEN = 256
N_CMP = (SEQ - CMP_LEN) // CMP_STRIDE + 1
N_CMP_PAD = 256
SEL_LEN = 64
N_SEL = SEQ // SEL_LEN
SEL_TOP = 16
WINDOW = 512
Q_BLOCK = 128
N_QB = SEQ // Q_BLOCK
SEL_KCHUNK = 512
WIN_KEYS = WINDOW + Q_BLOCK
MEM_LEN = 256
X_HEADS = 4
X_DH = 128
D_FF = 5632
NSA_PROJ = 1920
NEG = -1e30

V7X_VMEM_BYTES = 64 * 1024 * 1024
VMEM_LIMIT = V7X_VMEM_BYTES - 8 * 1024 * 1024


def _cparams(*sem):
    return pltpu.CompilerParams(dimension_semantics=sem, vmem_limit_bytes=VMEM_LIMIT)


def _rms(x, w):
    return x * lax.rsqrt(jnp.mean(x * x, axis=-1, keepdims=True) + RMS_EPS) * w


def _silu(x):
    return x * jax.nn.sigmoid(x)


def _dot(a, b):
    return jnp.dot(a, b, preferred_element_type=F32)


def _dot_nt(a, b):
    return lax.dot_general(a, b, (((1,), (1,)), ((), ())), preferred_element_type=F32)


FFN_TM = 512
FFN_TF = 512


def _ffn_body(x_ref, nw_ref, wg_ref, wu_ref, wd_ref, nw2_ref, *rest, final):
    if final:
        o_ref, h_scr = rest
    else:
        o_ref, hn_ref, h_scr = rest
    j = pl.program_id(1)

    @pl.when(j == 0)
    def _():
        h_scr[...] = _rms(x_ref[...], nw_ref[...]).astype(BF16)
        o_ref[...] = jnp.zeros_like(o_ref)

    h = h_scr[...]
    g = _dot(h, wg_ref[...])
    u = _dot(h, wu_ref[...])
    a = (_silu(g) * u).astype(BF16)
    o_ref[...] += _dot(a, wd_ref[...])

    @pl.when(j == pl.num_programs(1) - 1)
    def _():
        y = x_ref[...] + 0.5 * o_ref[...]
        if final:
            o_ref[...] = _rms(y, nw2_ref[...])
        else:
            o_ref[...] = y
            hn_ref[...] = _rms(y, nw2_ref[...]).astype(BF16)


def _ffn(x, nw, wg, wu, wd, nw2, final):
    grid = (TOKENS // FFN_TM, D_FF // FFN_TF)
    row = pl.BlockSpec((FFN_TM, D_MODEL), lambda i, j: (i, 0))
    vec = pl.BlockSpec((1, D_MODEL), lambda i, j: (0, 0))
    in_specs = [row, vec,
                pl.BlockSpec((D_MODEL, FFN_TF), lambda i, j: (0, j)),
                pl.BlockSpec((D_MODEL, FFN_TF), lambda i, j: (0, j)),
                pl.BlockSpec((FFN_TF, D_MODEL), lambda i, j: (j, 0)),
                vec]
    if final:
        out_shape = jax.ShapeDtypeStruct((TOKENS, D_MODEL), F32)
        out_specs = row
    else:
        out_shape = (jax.ShapeDtypeStruct((TOKENS, D_MODEL), F32),
                     jax.ShapeDtypeStruct((TOKENS, D_MODEL), BF16))
        out_specs = (row, row)
    return pl.pallas_call(
        functools.partial(_ffn_body, final=final),
        grid=grid, in_specs=in_specs, out_specs=out_specs, out_shape=out_shape,
        scratch_shapes=[pltpu.VMEM((FFN_TM, D_MODEL), BF16)],
        compiler_params=_cparams("parallel", "arbitrary"),
        name="ffn_final" if final else "ffn",
    )(x, nw, wg, wu, wd, nw2)


PROJ_TM = 512
PROJ_TN = 512


def _mm_body(a_ref, w_ref, o_ref):
    o_ref[...] = _dot(a_ref[...], w_ref[...])


def _proj(a, w, tn, name):
    m, k = a.shape
    n = w.shape[1]
    return pl.pallas_call(
        _mm_body,
        grid=(m // PROJ_TM, n // tn),
        in_specs=[pl.BlockSpec((PROJ_TM, k), lambda i, j: (i, 0)),
                  pl.BlockSpec((k, tn), lambda i, j: (0, j))],
        out_specs=pl.BlockSpec((PROJ_TM, tn), lambda i, j: (i, j)),
        out_shape=jax.ShapeDtypeStruct((m, n), F32),
        compiler_params=_cparams("parallel", "arbitrary"),
        name=name,
    )(a, w)


HG_ROWS = 256
HG_PAD = 128


def _hgrn_body(q_ref, f_ref, i_ref, g_ref, lbl_ref, nw_ref, o_ref,
               st_ref, q_s, k_s, b_s, v_s):
    c = pl.program_id(2)

    @pl.when(c == 0)
    def _():
        st_ref[...] = jnp.zeros_like(st_ref)
        k_s[...] = jnp.zeros_like(k_s)
        v_s[...] = jnp.zeros_like(v_s)

    l0 = lbl_ref[0:1, :]
    l1 = lbl_ref[1:2, :]
    lmax = jnp.maximum(l0, l1)
    e0 = jnp.exp(l0 - lmax)
    lb = e0 / (e0 + jnp.exp(l1 - lmax))

    C = HG_CHUNK
    r_i = lax.broadcasted_iota(jnp.int32, (HG_PAD, HG_PAD), 0)
    c_i = lax.broadcasted_iota(jnp.int32, (HG_PAD, HG_PAD), 1)
    tri = jnp.where((r_i >= c_i) & (r_i < C) & (c_i < C), 1.0, 0.0).astype(F32)
    sub_r = lax.broadcasted_iota(jnp.int32, (HG_SUB, HG_D), 0)
    col_pad = lax.broadcasted_iota(jnp.int32, (HG_SUB, HG_PAD), 1)
    zpad = jnp.zeros((HG_PAD - C, HG_D), F32)

    for ci in range(HG_ROWS // C):
        r0 = ci * C
        q = q_ref[r0:r0 + C, :]
        f = lb + (1.0 - lb) * jax.nn.sigmoid(f_ref[r0:r0 + C, :])
        logf = jnp.log(f)
        k = 1.0 - f
        v = i_ref[r0:r0 + C, :]
        b = jnp.dot(tri, jnp.concatenate([logf, zpad], axis=0),
                    precision=lax.Precision.HIGHEST, preferred_element_type=F32)[:C]
        q_s[...] = q
        k_s[0:C, :] = k
        b_s[...] = b
        v_s[0:C, :] = v

        st_t = st_ref[...]
        o_inter = _dot_nt((q * jnp.exp(b)).astype(BF16), st_t.astype(BF16))

        v_pad = v_s[...].astype(BF16)
        blocks = []
        for i in range(C // HG_SUB):
            s0 = i * HG_SUB
            qi = q[s0:s0 + HG_SUB]
            bi = b[s0:s0 + HG_SUB]
            oi = jnp.zeros((HG_SUB, HG_D), F32)
            if i > 0:
                bref = b_s[s0 - 1:s0, :]
                qt = (qi * jnp.exp(bi - bref)).astype(BF16)
                kt = k_s[...] * jnp.exp(jnp.minimum(bref - jnp.concatenate([b, zpad], axis=0), 0.0))
                att = _dot_nt(qt, kt.astype(BF16))
                att = jnp.where(col_pad < s0, att, 0.0)
                oi = oi + _dot(att.astype(BF16), v_pad)
            for s in range(HG_SUB):
                ks = k_s[s0 + s:s0 + s + 1, :]
                bs = b_s[s0 + s:s0 + s + 1, :]
                vs = v_s[s0 + s:s0 + s + 1, :]
                decay = jnp.exp(jnp.where(sub_r >= s, bi - bs, -jnp.inf))
                a = jnp.sum(qi * ks * decay, axis=-1, keepdims=True)
                oi = oi + a * vs
            blocks.append(oi)
        o = o_inter + jnp.concatenate(blocks, axis=0)

        bl = b_s[C - 1:C, :]
        kd = k_s[...] * jnp.exp(jnp.minimum(bl - jnp.concatenate([b, zpad], axis=0), 0.0))
        v_t = v_s[...].T
        st_ref[...] = st_t * jnp.exp(bl) + _dot(v_t.astype(BF16), kd.astype(BF16))

        o = o * lax.rsqrt(jnp.mean(o * o, axis=-1, keepdims=True) + RMS_EPS)
        o = o * nw_ref[...] * _silu(g_ref[r0:r0 + C, :])
        o_ref[r0:r0 + C, :] = o.astype(BF16)


def _hgrn(proj_hg, lb_logits, norm_w):
    p3 = proj_hg.reshape(BATCH, SEQ, 4 * HG_WIDTH)

    def col(off):
        return pl.BlockSpec((None, HG_ROWS, HG_D), lambda b, h, c: (b, c, off + h))

    return pl.pallas_call(
        _hgrn_body,
        grid=(BATCH, HG_HEADS, SEQ // HG_ROWS),
        in_specs=[col(0), col(HG_HEADS), col(2 * HG_HEADS), col(3 * HG_HEADS),
                  pl.BlockSpec((2, HG_D), lambda b, h, c: (0, h)),
                  pl.BlockSpec((1, HG_D), lambda b, h, c: (0, h))],
        out_specs=pl.BlockSpec((None, HG_ROWS, HG_D), lambda b, h, c: (b, c, h)),
        out_shape=jax.ShapeDtypeStruct((BATCH, SEQ, HG_WIDTH), BF16),
        scratch_shapes=[pltpu.VMEM((HG_D, HG_D), F32),
                        pltpu.VMEM((HG_CHUNK, HG_D), F32),
                        pltpu.VMEM((HG_PAD, HG_D), F32),
                        pltpu.VMEM((HG_CHUNK, HG_D), F32),
                        pltpu.VMEM((HG_PAD, HG_D), F32)],
        compiler_params=_cparams("parallel", "parallel", "arbitrary"),
        name="hgrn2",
    )(p3, p3, p3, p3, lb_logits, norm_w)


PREP_TM = 256


def _prep_body(p_ref, pos_ref, inv_ref, q_ref, kc_ref, vc_ref, ks_ref, kw_ref,
               vst_ref, vwt_ref, gt_ref):
    ang = pos_ref[...] * inv_ref[...]
    cos = jnp.cos(ang)
    sin = jnp.sin(ang)
    lane = lax.broadcasted_iota(jnp.int32, (PREP_TM, 128), 1)
    lo = (lane & (NSA_DH // 2)) == 0
    sin_signed = jnp.where(lo, -sin, sin)

    def rope(x):
        rot = jnp.where(lo, pltpu.roll(x, 128 - NSA_DH // 2, 1), pltpu.roll(x, NSA_DH // 2, 1))
        return x * cos + rot * sin_signed

    scale = NSA_DH ** -0.5
    for cblk in range(NSA_WIDTH // 128):
        sl = slice(cblk * 128, (cblk + 1) * 128)
        q_ref[:, sl] = (rope(p_ref[:, sl]) * scale).astype(BF16)
    kc_ref[...] = rope(p_ref[:, 1024:1152])
    ks_ref[...] = rope(p_ref[:, 1152:1280]).astype(BF16)
    kw_ref[...] = rope(p_ref[:, 1280:1408]).astype(BF16)
    vc_ref[...] = p_ref[:, 1408:1536]
    vst_ref[...] = p_ref[:, 1536:1664].T.astype(BF16)
    vwt_ref[...] = p_ref[:, 1664:1792].T.astype(BF16)
    gt_ref[...] = jax.nn.sigmoid(p_ref[:, 1792:1920]).T[0:3 * NSA_HEADS, :]


def _nsa_prep(proj_nsa, pos_f, inv128):
    nt = SEQ // PREP_TM
    p3 = proj_nsa.reshape(BATCH, SEQ, NSA_PROJ)
    nat = lambda w: pl.BlockSpec((None, PREP_TM, w), lambda b, i: (b, i, 0))
    tr = lambda r: pl.BlockSpec((None, r, PREP_TM), lambda b, i: (b, 0, i))
    sds = jax.ShapeDtypeStruct
    return pl.pallas_call(
        _prep_body,
        grid=(BATCH, nt),
        in_specs=[nat(NSA_PROJ), nat(1), pl.BlockSpec((1, 128), lambda b, i: (0, 0))],
        out_specs=(nat(NSA_WIDTH), nat(128), nat(128), nat(128), nat(128),
                   tr(128), tr(128), tr(3 * NSA_HEADS)),
        out_shape=(sds((BATCH, SEQ, NSA_WIDTH), BF16),
                   sds((BATCH, SEQ, 128), F32),
                   sds((BATCH, SEQ, 128), F32),
                   sds((BATCH, SEQ, 128), BF16),
                   sds((BATCH, SEQ, 128), BF16),
                   sds((BATCH, 128, SEQ), BF16),
                   sds((BATCH, 128, SEQ), BF16),
                   sds((BATCH, 3 * NSA_HEADS, SEQ), F32)),
        compiler_params=_cparams("parallel", "parallel"),
        name="nsa_prep",
    )(p3, pos_f, inv128)


def _cmp_body(tk_ref, tv_ref, pelo_ref, pehi_ref, kw1_ref, kw2_ref, vw1_ref, vw2_ref,
              kc_ref, vct_ref):
    half = CMP_LEN * NSA_DH // 2
    row = lax.broadcasted_iota(jnp.int32, (N_CMP_PAD, CMP_HIDDEN), 0)

    def mlp(t_ref, w1_ref, w2_ref):
        out = jnp.zeros((N_CMP_PAD, 128), F32)
        for g in range(NSA_KV):
            x = t_ref[g]
            y1 = _dot((x + pelo_ref[...]).astype(BF16), w1_ref[0:half, :])
            y2 = _dot((x + pehi_ref[...]).astype(BF16), w1_ref[half:2 * half, :])
            hid = jnp.where(row < N_CMP, y1 + pltpu.roll(y2, N_CMP_PAD - 1, 0), 0.0)
            out = out + _dot(_silu(hid).astype(BF16), w2_ref[g])
        return out

    kc_ref[...] = mlp(tk_ref, kw1_ref, kw2_ref).astype(BF16)
    vct_ref[...] = mlp(tv_ref, vw1_ref, vw2_ref).T.astype(BF16)


def _compress(tk, tv, pelo, pehi, kw1, kw2p, vw1, vw2p):
    seg = pl.BlockSpec((None, NSA_KV, N_CMP_PAD, CMP_STRIDE * NSA_DH), lambda b: (b, 0, 0, 0))
    full2 = lambda a: pl.BlockSpec(a.shape, lambda b: (0,) * a.ndim)
    return pl.pallas_call(
        _cmp_body,
        grid=(BATCH,),
        in_specs=[seg, seg, full2(pelo), full2(pehi), full2(kw1), full2(kw2p), full2(vw1), full2(vw2p)],
        out_specs=(pl.BlockSpec((None, N_CMP_PAD, 128), lambda b: (b, 0, 0)),
                   pl.BlockSpec((None, 128, N_CMP_PAD), lambda b: (b, 0, 0))),
        out_shape=(jax.ShapeDtypeStruct((BATCH, N_CMP_PAD, 128), BF16),
                   jax.ShapeDtypeStruct((BATCH, 128, N_CMP_PAD), BF16)),
        compiler_params=_cparams("parallel"),
        name="nsa_compress",
    )(tk, tv, pelo, pehi, kw1, kw2p, vw1, vw2p)


def _nsa_body(q_ref, kc_ref, vct_ref, ks_ref, vst_ref, kw_ref, vwt_ref, gt_ref, ovt_ref,
              o_ref, sel_s, sc_s):
    g = pl.program_id(1)
    qb = pl.program_id(2)
    q0 = qb * Q_BLOCK
    NL = NSA_REP * Q_BLOCK
    is_g0 = g == 0

    qblk = q_ref[...].astype(F32)
    zero_slab = jnp.zeros((NSA_DH, Q_BLOCK), F32)
    slabs = []
    for p in range(NSA_REP // 2):
        t = qblk[:, p * 128:(p + 1) * 128].T
        for hh in range(2):
            s = t[hh * NSA_DH:(hh + 1) * NSA_DH]
            slabs.append(jnp.concatenate([jnp.where(is_g0, s, zero_slab),
                                          jnp.where(is_g0, zero_slab, s)], axis=0))
    qp = jnp.concatenate(slabs, axis=1).astype(BF16)

    tq = q0 + lax.broadcasted_iota(jnp.int32, (1, Q_BLOCK), 1)

    def mask_heads(s, valid):
        return jnp.concatenate(
            [jnp.where(valid, s[:, r * Q_BLOCK:(r + 1) * Q_BLOCK], NEG) for r in range(NSA_REP)], axis=1)

    n_i = lax.broadcasted_iota(jnp.int32, (N_CMP_PAD, Q_BLOCK), 0)
    sc = mask_heads(_dot(kc_ref[...], qp),
                    (n_i * CMP_STRIDE + (CMP_LEN - 1) <= tq) & (n_i < N_CMP))
    m_c = jnp.max(sc, axis=0, keepdims=True)
    e_c = jnp.exp(sc - m_c) * jnp.where(m_c > 0.5 * NEG, 1.0, 0.0)
    p_c = e_c / jnp.maximum(jnp.sum(e_c, axis=0, keepdims=True), 1e-30)
    o_c = _dot(vct_ref[...], p_c.astype(BF16))

    p_sum = p_c[:, 0:Q_BLOCK]
    for r in range(1, NSA_REP):
        p_sum = p_sum + p_c[:, r * Q_BLOCK:(r + 1) * Q_BLOCK]
    imp = jnp.dot(ovt_ref[...], p_sum, precision=lax.Precision.HIGHEST,
                  preferred_element_type=F32)
    j_i = lax.broadcasted_iota(jnp.int32, (N_SEL, Q_BLOCK), 0)
    cur = tq // SEL_LEN
    forced = (j_i == 0) | (j_i == cur) | (j_i == cur - 1)
    score = jnp.where(forced, jnp.inf, jnp.where(j_i > cur, -jnp.inf, imp))
    sc_s[...] = score
    sub8 = lax.broadcasted_iota(jnp.int32, (8, Q_BLOCK), 0)
    score_v = [score[8 * v:8 * v + 8] for v in range(N_SEL // 8)]
    rank_v = [jnp.zeros((8, Q_BLOCK), F32) for _ in range(N_SEL // 8)]
    for jp in range(N_SEL):
        row = sc_s[jp:jp + 1, :]
        for v in range(N_SEL // 8):
            if 8 * v > jp:
                one = jnp.where(row >= score_v[v], 1.0, 0.0)
            elif 8 * v + 7 <= jp:
                one = jnp.where(row > score_v[v], 1.0, 0.0)
            else:
                one = jnp.where(sub8 + 8 * v > jp, jnp.where(row >= score_v[v], 1.0, 0.0),
                                jnp.where(row > score_v[v], 1.0, 0.0))
            rank_v[v] = rank_v[v] + one
    rank = jnp.concatenate(rank_v, axis=0)
    sel_s[...] = jnp.where(rank < SEL_TOP, 1.0, 0.0)

    k_i = lax.broadcasted_iota(jnp.int32, (SEL_KCHUNK, Q_BLOCK), 0)
    blocks_per_chunk = SEL_KCHUNK // SEL_LEN

    def sel_step(c, carry):
        m, l, acc = carry
        k0 = pl.multiple_of(c * SEL_KCHUNK, SEL_KCHUNK)
        s = _dot(ks_ref[pl.ds(k0, SEL_KCHUNK), :], qp)
        selrows = sel_s[pl.ds(pl.multiple_of(c * blocks_per_chunk, blocks_per_chunk), blocks_per_chunk), :]
        selm = jnp.concatenate(
            [jnp.broadcast_to(selrows[jj:jj + 1, :], (SEL_LEN, Q_BLOCK)) for jj in range(blocks_per_chunk)],
            axis=0)
        s = mask_heads(s, (selm > 0.5) & (k_i + k0 <= tq))
        m_new = jnp.maximum(m, jnp.max(s, axis=0, keepdims=True))
        alpha = jnp.exp(m - m_new)
        p = jnp.exp(s - m_new)
        l = l * alpha + jnp.sum(p, axis=0, keepdims=True)
        acc = acc * alpha + _dot(vst_ref[:, pl.ds(k0, SEL_KCHUNK)], p.astype(BF16))
        return m_new, l, acc

    init = (jnp.full((1, NL), NEG, F32), jnp.zeros((1, NL), F32), jnp.zeros((NSA_DH, NL), F32))
    n_chunks = (q0 + Q_BLOCK + SEL_KCHUNK - 1) // SEL_KCHUNK
    _, l_s, acc_s = lax.fori_loop(0, n_chunks, sel_step, init)
    o_s = acc_s / jnp.maximum(l_s, 1e-30)

    w0 = pl.multiple_of(jnp.maximum(q0 - WINDOW, 0), Q_BLOCK)
    sw = _dot(kw_ref[pl.ds(w0, WIN_KEYS), :], qp)
    kpos = w0 + lax.broadcasted_iota(jnp.int32, (WIN_KEYS, Q_BLOCK), 0)
    dpos = tq - kpos
    sw = mask_heads(sw, (dpos >= 0) & (dpos < WINDOW))
    e_w = jnp.exp(sw - jnp.max(sw, axis=0, keepdims=True))
    o_w = _dot(vwt_ref[:, pl.ds(w0, WIN_KEYS)], e_w.astype(BF16))
    o_w = o_w / jnp.maximum(jnp.sum(e_w, axis=0, keepdims=True), 1e-30)

    gall = gt_ref[...]
    gate = []
    for br in range(3):
        gb = gall[br * NSA_HEADS:(br + 1) * NSA_HEADS]
        gate.append(jnp.where(is_g0, gb[0:NSA_REP], gb[NSA_REP:NSA_HEADS]))
    outs = []
    for r in range(NSA_REP):
        sl = slice(r * Q_BLOCK, (r + 1) * Q_BLOCK)
        outs.append(gate[0][r:r + 1, :] * o_c[:, sl] + gate[1][r:r + 1, :] * o_s[:, sl]
                    + gate[2][r:r + 1, :] * o_w[:, sl])
    for p in range(NSA_REP // 2):
        pair = jnp.concatenate([outs[2 * p], outs[2 * p + 1]], axis=0)
        o_ref[:, p * 128:(p + 1) * 128] = pair.T.astype(BF16)


def _nsa_attn(q_r, kc, vct, ks, vst, kw, vwt, gt, ovt):
    per_b = lambda r, c: pl.BlockSpec((None, r, c), lambda b, g, i: (b, 0, 0))
    per_bg = lambda r, c: pl.BlockSpec((None, r, c), lambda b, g, i: (b, g, 0))
    return pl.pallas_call(
        _nsa_body,
        grid=(BATCH, NSA_KV, N_QB),
        in_specs=[pl.BlockSpec((None, Q_BLOCK, NSA_REP * NSA_DH), lambda b, g, i: (b, i, g)),
                  per_b(N_CMP_PAD, 128), per_bg(NSA_DH, N_CMP_PAD),
                  per_b(SEQ, 128), per_bg(NSA_DH, SEQ),
                  per_b(SEQ, 128), per_bg(NSA_DH, SEQ),
                  pl.BlockSpec((None, 3 * NSA_HEADS, Q_BLOCK), lambda b, g, i: (b, 0, i)),
                  pl.BlockSpec((N_SEL, N_CMP_PAD), lambda b, g, i: (0, 0))],
        out_specs=pl.BlockSpec((None, Q_BLOCK, NSA_REP * NSA_DH), lambda b, g, i: (b, i, g)),
        out_shape=jax.ShapeDtypeStruct((BATCH, SEQ, NSA_WIDTH), BF16),
        scratch_shapes=[pltpu.VMEM((N_SEL, Q_BLOCK), F32), pltpu.VMEM((N_SEL, Q_BLOCK), F32)],
        compiler_params=_cparams("parallel", "parallel", "arbitrary"),
        name="nsa_attn",
    )(q_r, kc, vct, ks, vst, kw, vwt, gt, ovt)


OUT_TM = 256


def _outproj_body(x_ref, oh_ref, on_ref, w_ref, nw_ref, x2_ref, hx_ref):
    y = (x_ref[...] + _dot(oh_ref[...], w_ref[0:HG_WIDTH, :])
         + _dot(on_ref[...], w_ref[HG_WIDTH:HG_WIDTH + NSA_WIDTH, :]))
    x2_ref[...] = y
    hx_ref[...] = _rms(y, nw_ref[...]).astype(BF16)


def _outproj(x1, o_hg, o_nsa, w_out, nw):
    row = lambda w: pl.BlockSpec((OUT_TM, w), lambda i: (i, 0))
    return pl.pallas_call(
        _outproj_body,
        grid=(TOKENS // OUT_TM,),
        in_specs=[row(D_MODEL), row(HG_WIDTH), row(NSA_WIDTH),
                  pl.BlockSpec((D_MODEL, D_MODEL), lambda i: (0, 0)),
                  pl.BlockSpec((1, D_MODEL), lambda i: (0, 0))],
        out_specs=(row(D_MODEL), row(D_MODEL)),
        out_shape=(jax.ShapeDtypeStruct((TOKENS, D_MODEL), F32),
                   jax.ShapeDtypeStruct((TOKENS, D_MODEL), BF16)),
        compiler_params=_cparams("parallel"),
        name="out_proj",
    )(x1, o_hg, o_nsa, w_out, nw)


def _memkv_body(m_ref, nw_ref, wk_ref, wv_ref, k_ref, v_ref):
    hm = _rms(m_ref[...], nw_ref[...]).astype(BF16)
    k_ref[...] = _dot(hm, wk_ref[...]).astype(BF16)
    v_ref[...] = _dot(hm, wv_ref[...]).astype(BF16)


def _memkv(mem, nw, wk, wv):
    width = X_HEADS * X_DH
    wspec = pl.BlockSpec((D_MODEL, width), lambda b: (0, 0))
    ospec = pl.BlockSpec((None, MEM_LEN, width), lambda b: (b, 0, 0))
    osh = jax.ShapeDtypeStruct((BATCH, MEM_LEN, width), BF16)
    return pl.pallas_call(
        _memkv_body,
        grid=(BATCH,),
        in_specs=[pl.BlockSpec((None, MEM_LEN, D_MODEL), lambda b: (b, 0, 0)),
                  pl.BlockSpec((1, D_MODEL), lambda b: (0, 0)), wspec, wspec],
        out_specs=(ospec, ospec), out_shape=(osh, osh),
        compiler_params=_cparams("parallel"),
        name="xattn_memkv",
    )(mem, nw, wk, wv)


XA_TM = 256


def _xattn_body(x_ref, hx_ref, wq_ref, k_ref, v_ref, wo_ref, o_ref):
    q = (_dot(hx_ref[...], wq_ref[...]) * (X_DH ** -0.5)).astype(BF16)
    heads = []
    for h in range(X_HEADS):
        sl = slice(h * X_DH, (h + 1) * X_DH)
        s = _dot_nt(q[:, sl], k_ref[:, sl])
        e = jnp.exp(s - jnp.max(s, axis=-1, keepdims=True))
        p = e / jnp.sum(e, axis=-1, keepdims=True)
        heads.append(_dot(p.astype(BF16), v_ref[:, sl]))
    o = jnp.concatenate(heads, axis=1).astype(BF16)
    o_ref[...] = x_ref[...] + _dot(o, wo_ref[...])


def _xattn(x2, hx, wq, k, v, wo):
    width = X_HEADS * X_DH
    tiles_per_b = SEQ // XA_TM
    kv = pl.BlockSpec((None, MEM_LEN, width), lambda i: (i // tiles_per_b, 0, 0))
    return pl.pallas_call(
        _xattn_body,
        grid=(TOKENS // XA_TM,),
        in_specs=[pl.BlockSpec((XA_TM, D_MODEL), lambda i: (i, 0)),
                  pl.BlockSpec((XA_TM, D_MODEL), lambda i: (i, 0)),
                  pl.BlockSpec((D_MODEL, width), lambda i: (0, 0)), kv, kv,
                  pl.BlockSpec((width, D_MODEL), lambda i: (0, 0))],
        out_specs=pl.BlockSpec((XA_TM, D_MODEL), lambda i: (i, 0)),
        out_shape=jax.ShapeDtypeStruct((TOKENS, D_MODEL), F32),
        compiler_params=_cparams("parallel"),
        name="xattn",
    )(x2, hx, wq, k, v, wo)


def _overlap_t():
    c0 = np.arange(N_CMP)[:, None] * CMP_STRIDE
    s0 = np.arange(N_SEL)[None, :] * SEL_LEN
    ov = np.clip(np.minimum(c0 + CMP_LEN, s0 + SEL_LEN) - np.maximum(c0, s0), 0, None) / CMP_LEN
    out = np.zeros((N_SEL, N_CMP_PAD), np.float32)
    out[:, :N_CMP] = ov.T
    return out


def kernel(x, mem, positions, ffn1_norm, ffn1_w_gate, ffn1_w_up, ffn1_w_down, mix_norm, w_in, hgrn_lb_logits, hgrn_out_norm, nsa_cmp_pe, nsa_cmp_k_w1, nsa_cmp_k_w2, nsa_cmp_v_w1, nsa_cmp_v_w2, w_out, xattn_norm, mem_norm, xattn_wq, xattn_wk, xattn_wv, xattn_wo, ffn2_norm, ffn2_w_gate, ffn2_w_up, ffn2_w_down, final_norm):
    bf = lambda a: a.astype(BF16)
    vec = lambda a: a.reshape(1, -1).astype(F32)
    x2d = x.reshape(TOKENS, D_MODEL)

    x1, h_mix = _ffn(x2d, vec(ffn1_norm[0]), bf(ffn1_w_gate[0]), bf(ffn1_w_up[0]), bf(ffn1_w_down[0]),
                     vec(mix_norm[0]), final=False)

    w = w_in[0]
    gate_cols = 5888 + (np.arange(NSA_HEADS)[None, :] * 3 + np.arange(3)[:, None]).reshape(-1)
    w_nsa = jnp.concatenate(
        [w[:, 4096:5120], w[:, 5120:5248], w[:, 5376:5504], w[:, 5632:5760],
         w[:, 5248:5376], w[:, 5504:5632], w[:, 5760:5888], w[:, gate_cols],
         jnp.zeros((D_MODEL, NSA_PROJ - 1792 - 3 * NSA_HEADS), w.dtype)], axis=1)
    proj_hg = _proj(h_mix, bf(w[:, :4 * HG_WIDTH]), PROJ_TN, "proj_hgrn")
    proj_nsa = _proj(h_mix, bf(w_nsa), NSA_PROJ // 3, "proj_nsa")

    o_hg = _hgrn(proj_hg, hgrn_lb_logits.astype(F32), vec(hgrn_out_norm[0]))

    inv = ROPE_THETA ** (-jnp.arange(NSA_DH // 2, dtype=F32) / (NSA_DH // 2))
    inv128 = jnp.tile(inv, 128 // (NSA_DH // 2)).reshape(1, 128)
    pos_f = positions.astype(F32).reshape(BATCH, SEQ, 1)
    q_r, kc_tok, vc_tok, ks, kw, vst, vwt, gt = _nsa_prep(proj_nsa, pos_f, inv128)

    def segments(t):
        t = t.reshape(BATCH, SEQ // CMP_STRIDE, CMP_STRIDE, NSA_KV, NSA_DH)
        return t.transpose(0, 3, 1, 2, 4).reshape(BATCH, NSA_KV, SEQ // CMP_STRIDE, CMP_STRIDE * NSA_DH)

    pe = nsa_cmp_pe[0].astype(F32)
    pelo = pe[:CMP_STRIDE].reshape(1, -1)
    pehi = pe[CMP_STRIDE:].reshape(1, -1)

    def pad_w2(w2):
        z = jnp.zeros_like(w2)
        return bf(jnp.stack([jnp.concatenate([w2, z], axis=1), jnp.concatenate([z, w2], axis=1)]))

    kc, vct = _compress(segments(kc_tok), segments(vc_tok), pelo, pehi,
                        bf(nsa_cmp_k_w1[0]), pad_w2(nsa_cmp_k_w2[0]),
                        bf(nsa_cmp_v_w1[0]), pad_w2(nsa_cmp_v_w2[0]))
    o_nsa = _nsa_attn(q_r, kc, vct, ks, vst, kw, vwt, gt, jnp.asarray(_overlap_t()))

    x2, hx = _outproj(x1, o_hg.reshape(TOKENS, HG_WIDTH), o_nsa.reshape(TOKENS, NSA_WIDTH),
                      bf(w_out[0]), vec(xattn_norm[0]))

    km, vm = _memkv(mem, vec(mem_norm[0]), bf(xattn_wk[0]), bf(xattn_wv[0]))
    x3 = _xattn(x2, hx, bf(xattn_wq[0]), km, vm, bf(xattn_wo[0]))

    out = _ffn(x3, vec(ffn2_norm[0]), bf(ffn2_w_gate[0]), bf(ffn2_w_up[0]), bf(ffn2_w_down[0]),
               vec(final_norm), final=True)
    return out.reshape(BATCH, SEQ, D_MODEL)
```

```python
import functools

import numpy as np
import jax
import jax.numpy as jnp
from jax import lax
from jax.experimental import pallas as pl
from jax.experimental.pallas import tpu as pltpu

F32 = jnp.float32
BF16 = jnp.bfloat16

D_MODEL = 2048
BATCH = 2
SEQ = 4096
TOKENS = BATCH * SEQ
RMS_EPS = 1e-6
ROPE_THETA = 10000.0
HG_WIDTH = 1024
HG_HEADS = 8
HG_D = 128
HG_CHUNK = 64
HG_SUB = 16
NSA_WIDTH = 1024
NSA_DH = 64
NSA_HEADS = 16
NSA_KV = 2
NSA_REP = 8
CMP_LEN = 32
CMP_STRIDE = 16
CMP_HIDDEN = 256
N_CMP = (SEQ - CMP_LEN) // CMP_STRIDE + 1
N_CMP_PAD = 256
SEL_LEN = 64
N_SEL = SEQ // SEL_LEN
SEL_TOP = 16
WINDOW = 512
Q_BLOCK = 128
N_QB = SEQ // Q_BLOCK
SEL_SUB = 256
WIN_KEYS = WINDOW + Q_BLOCK
MEM_LEN = 256
X_HEADS = 4
X_DH = 128
D_FF = 5632
NSA_PROJ = 1920
NEG = -1e30
LOG2E = 1.4426950408889634

V7X_VMEM_BYTES = 64 * 1024 * 1024
VMEM_LIMIT = V7X_VMEM_BYTES - 8 * 1024 * 1024


def _cparams(*sem, flags=None):
    return pltpu.CompilerParams(dimension_semantics=sem, vmem_limit_bytes=VMEM_LIMIT, flags=flags)


def _rms(x, w):
    return x * lax.rsqrt(jnp.mean(x * x, axis=-1, keepdims=True) + RMS_EPS) * w


def _silu(x):
    return x * jax.nn.sigmoid(x)


def _dot(a, b):
    return jnp.dot(a, b, preferred_element_type=F32)


def _dot_nt(a, b):
    return lax.dot_general(a, b, (((1,), (1,)), ((), ())), preferred_element_type=F32)


FFN_TM = 512
FFN_TF = 512


def _ffn_body(x_ref, nw_ref, wg_ref, wu_ref, wd_ref, nw2_ref, *rest, final):
    if final:
        o_ref, h_scr = rest
    else:
        o_ref, hn_ref, h_scr = rest
    j = pl.program_id(1)

    @pl.when(j == 0)
    def _():
        h_scr[...] = _rms(x_ref[...], nw_ref[...]).astype(BF16)
        o_ref[...] = jnp.zeros_like(o_ref)

    h = h_scr[...]
    g = _dot(h, wg_ref[...])
    u = _dot(h, wu_ref[...])
    a = (_silu(g) * u).astype(BF16)
    o_ref[...] += _dot(a, wd_ref[...])

    @pl.when(j == pl.num_programs(1) - 1)
    def _():
        y = x_ref[...] + 0.5 * o_ref[...]
        if final:
            o_ref[...] = _rms(y, nw2_ref[...])
        else:
            o_ref[...] = y
            hn_ref[...] = _rms(y, nw2_ref[...]).astype(BF16)


def _ffn(x, nw, wg, wu, wd, nw2, final):
    grid = (TOKENS // FFN_TM, D_FF // FFN_TF)
    row = pl.BlockSpec((FFN_TM, D_MODEL), lambda i, j: (i, 0))
    vec = pl.BlockSpec((1, D_MODEL), lambda i, j: (0, 0))
    in_specs = [row, vec,
                pl.BlockSpec((D_MODEL, FFN_TF), lambda i, j: (0, j)),
                pl.BlockSpec((D_MODEL, FFN_TF), lambda i, j: (0, j)),
                pl.BlockSpec((FFN_TF, D_MODEL), lambda i, j: (j, 0)),
                vec]
    if final:
        out_shape = jax.ShapeDtypeStruct((TOKENS, D_MODEL), F32)
        out_specs = row
    else:
        out_shape = (jax.ShapeDtypeStruct((TOKENS, D_MODEL), F32),
                     jax.ShapeDtypeStruct((TOKENS, D_MODEL), BF16))
        out_specs = (row, row)
    return pl.pallas_call(
        functools.partial(_ffn_body, final=final),
        grid=grid, in_specs=in_specs, out_specs=out_specs, out_shape=out_shape,
        scratch_shapes=[pltpu.VMEM((FFN_TM, D_MODEL), BF16)],
        compiler_params=_cparams("parallel", "arbitrary"),
        name="ffn_final" if final else "ffn",
    )(x, nw, wg, wu, wd, nw2)


PROJ_TM = 512
PROJ_TN = 512


def _mm_body(a_ref, w_ref, o_ref):
    o_ref[...] = _dot(a_ref[...], w_ref[...])


def _proj(a, w, tn, name):
    m, k = a.shape
    n = w.shape[1]
    return pl.pallas_call(
        _mm_body,
        grid=(m // PROJ_TM, n // tn),
        in_specs=[pl.BlockSpec((PROJ_TM, k), lambda i, j: (i, 0)),
                  pl.BlockSpec((k, tn), lambda i, j: (0, j))],
        out_specs=pl.BlockSpec((PROJ_TM, tn), lambda i, j: (i, j)),
        out_shape=jax.ShapeDtypeStruct((m, n), F32),
        compiler_params=_cparams("parallel", "arbitrary"),
        name=name,
    )(a, w)


HG_ROWS = 256
HG_PAD = 128


def _hgrn_body(q_ref, f_ref, i_ref, g_ref, lbl_ref, nw_ref, o_ref,
               st_ref, q_s, k_s, b_s, v_s):
    c = pl.program_id(2)

    @pl.when(c == 0)
    def _():
        st_ref[...] = jnp.zeros_like(st_ref)
        k_s[...] = jnp.zeros_like(k_s)
        v_s[...] = jnp.zeros_like(v_s)

    l0 = lbl_ref[0:1, :]
    l1 = lbl_ref[1:2, :]
    lmax = jnp.maximum(l0, l1)
    e0 = jnp.exp(l0 - lmax)
    lb = e0 / (e0 + jnp.exp(l1 - lmax))

    C = HG_CHUNK
    r_i = lax.broadcasted_iota(jnp.int32, (HG_PAD, HG_PAD), 0)
    c_i = lax.broadcasted_iota(jnp.int32, (HG_PAD, HG_PAD), 1)
    tri = jnp.where((r_i >= c_i) & (r_i < C) & (c_i < C), 1.0, 0.0).astype(F32)
    sub_r = lax.broadcasted_iota(jnp.int32, (HG_SUB, HG_D), 0)
    col_pad = lax.broadcasted_iota(jnp.int32, (HG_SUB, HG_PAD), 1)
    zpad = jnp.zeros((HG_PAD - C, HG_D), F32)

    for ci in range(HG_ROWS // C):
        r0 = ci * C
        q = q_ref[r0:r0 + C, :]
        f = lb + (1.0 - lb) * jax.nn.sigmoid(f_ref[r0:r0 + C, :])
        logf = jnp.log(f)
        k = 1.0 - f
        v = i_ref[r0:r0 + C, :]
        b = jnp.dot(tri, jnp.concatenate([logf, zpad], axis=0),
                    precision=lax.Precision.HIGHEST, preferred_element_type=F32)[:C]
        q_s[...] = q
        k_s[0:C, :] = k
        b_s[...] = b
        v_s[0:C, :] = v

        st_t = st_ref[...]
        o_inter = _dot_nt((q * jnp.exp(b)).astype(BF16), st_t.astype(BF16))

        v_pad = v_s[...].astype(BF16)
        blocks = []
        for i in range(C // HG_SUB):
            s0 = i * HG_SUB
            qi = q[s0:s0 + HG_SUB]
            bi = b[s0:s0 + HG_SUB]
            oi = jnp.zeros((HG_SUB, HG_D), F32)
            if i > 0:
                bref = b_s[s0 - 1:s0, :]
                qt = (qi * jnp.exp(bi - bref)).astype(BF16)
                kt = k_s[...] * jnp.exp(jnp.minimum(bref - jnp.concatenate([b, zpad], axis=0), 0.0))
                att = _dot_nt(qt, kt.astype(BF16))
                att = jnp.where(col_pad < s0, att, 0.0)
                oi = oi + _dot(att.astype(BF16), v_pad)
            for s in range(HG_SUB):
                ks = k_s[s0 + s:s0 + s + 1, :]
                bs = b_s[s0 + s:s0 + s + 1, :]
                vs = v_s[s0 + s:s0 + s + 1, :]
                decay = jnp.exp(jnp.where(sub_r >= s, bi - bs, -jnp.inf))
                a = jnp.sum(qi * ks * decay, axis=-1, keepdims=True)
                oi = oi + a * vs
            blocks.append(oi)
        o = o_inter + jnp.concatenate(blocks, axis=0)

        bl = b_s[C - 1:C, :]
        kd = k_s[...] * jnp.exp(jnp.minimum(bl - jnp.concatenate([b, zpad], axis=0), 0.0))
        v_t = v_s[...].T
        st_ref[...] = st_t * jnp.exp(bl) + _dot(v_t.astype(BF16), kd.astype(BF16))

        o = o * lax.rsqrt(jnp.mean(o * o, axis=-1, keepdims=True) + RMS_EPS)
        o = o * nw_ref[...] * _silu(g_ref[r0:r0 + C, :])
        o_ref[r0:r0 + C, :] = o.astype(BF16)


def _hgrn(proj_hg, lb_logits, norm_w):
    p3 = proj_hg.reshape(BATCH, SEQ, 4 * HG_WIDTH)

    def col(off):
        return pl.BlockSpec((None, HG_ROWS, HG_D), lambda b, h, c: (b, c, off + h))

    return pl.pallas_call(
        _hgrn_body,
        grid=(BATCH, HG_HEADS, SEQ // HG_ROWS),
        in_specs=[col(0), col(HG_HEADS), col(2 * HG_HEADS), col(3 * HG_HEADS),
                  pl.BlockSpec((2, HG_D), lambda b, h, c: (0, h)),
                  pl.BlockSpec((1, HG_D), lambda b, h, c: (0, h))],
        out_specs=pl.BlockSpec((None, HG_ROWS, HG_D), lambda b, h, c: (b, c, h)),
        out_shape=jax.ShapeDtypeStruct((BATCH, SEQ, HG_WIDTH), BF16),
        scratch_shapes=[pltpu.VMEM((HG_D, HG_D), F32),
                        pltpu.VMEM((HG_CHUNK, HG_D), F32),
                        pltpu.VMEM((HG_PAD, HG_D), F32),
                        pltpu.VMEM((HG_CHUNK, HG_D), F32),
                        pltpu.VMEM((HG_PAD, HG_D), F32)],
        compiler_params=_cparams("parallel", "parallel", "arbitrary"),
        name="hgrn2",
    )(p3, p3, p3, p3, lb_logits, norm_w)


PREP_TM = 256


def _prep_body(p_ref, pos_ref, inv_ref, q_ref, kc_ref, vc_ref, ks_ref, kw_ref,
               vst_ref, vwt_ref, gt_ref):
    ang = pos_ref[...] * inv_ref[...]
    cos = jnp.cos(ang)
    sin = jnp.sin(ang)
    lane = lax.broadcasted_iota(jnp.int32, (PREP_TM, 128), 1)
    lo = (lane & (NSA_DH // 2)) == 0
    sin_signed = jnp.where(lo, -sin, sin)

    def rope(x):
        rot = jnp.where(lo, pltpu.roll(x, 128 - NSA_DH // 2, 1), pltpu.roll(x, NSA_DH // 2, 1))
        return x * cos + rot * sin_signed

    scale = NSA_DH ** -0.5 * LOG2E
    for cblk in range(NSA_WIDTH // 128):
        sl = slice(cblk * 128, (cblk + 1) * 128)
        q_ref[:, sl] = (rope(p_ref[:, sl]) * scale).astype(BF16)
    kc_ref[...] = rope(p_ref[:, 1024:1152])
    ks_ref[...] = rope(p_ref[:, 1152:1280]).astype(BF16)
    kw_ref[...] = rope(p_ref[:, 1280:1408]).astype(BF16)
    vc_ref[...] = p_ref[:, 1408:1536]
    vst_ref[...] = p_ref[:, 1536:1664].T.astype(BF16)
    vwt_ref[...] = p_ref[:, 1664:1792].T.astype(BF16)
    gt_ref[...] = jax.nn.sigmoid(p_ref[:, 1792:1920]).T[0:3 * NSA_HEADS, :]


def _nsa_prep(proj_nsa, pos_f, inv128):
    nt = SEQ // PREP_TM
    p3 = proj_nsa.reshape(BATCH, SEQ, NSA_PROJ)
    nat = lambda w: pl.BlockSpec((None, PREP_TM, w), lambda b, i: (b, i, 0))
    tr = lambda r: pl.BlockSpec((None, r, PREP_TM), lambda b, i: (b, 0, i))
    sds = jax.ShapeDtypeStruct
    return pl.pallas_call(
        _prep_body,
        grid=(BATCH, nt),
        in_specs=[nat(NSA_PROJ), nat(1), pl.BlockSpec((1, 128), lambda b, i: (0, 0))],
        out_specs=(nat(NSA_WIDTH), nat(128), nat(128), nat(128), nat(128),
                   tr(128), tr(128), tr(3 * NSA_HEADS)),
        out_shape=(sds((BATCH, SEQ, NSA_WIDTH), BF16),
                   sds((BATCH, SEQ, 128), F32),
                   sds((BATCH, SEQ, 128), F32),
                   sds((BATCH, SEQ, 128), BF16),
                   sds((BATCH, SEQ, 128), BF16),
                   sds((BATCH, 128, SEQ), BF16),
                   sds((BATCH, 128, SEQ), BF16),
                   sds((BATCH, 3 * NSA_HEADS, SEQ), F32)),
        compiler_params=_cparams("parallel", "parallel"),
        name="nsa_prep",
    )(p3, pos_f, inv128)


def _cmp_body(tk_ref, tv_ref, pelo_ref, pehi_ref, kw1_ref, kw2_ref, vw1_ref, vw2_ref,
              kc_ref, vct_ref):
    half = CMP_LEN * NSA_DH // 2
    row = lax.broadcasted_iota(jnp.int32, (N_CMP_PAD, CMP_HIDDEN), 0)

    def mlp(t_ref, w1_ref, w2_ref):
        out = jnp.zeros((N_CMP_PAD, 128), F32)
        for g in range(NSA_KV):
            x = t_ref[g]
            y1 = _dot((x + pelo_ref[...]).astype(BF16), w1_ref[0:half, :])
            y2 = _dot((x + pehi_ref[...]).astype(BF16), w1_ref[half:2 * half, :])
            hid = jnp.where(row < N_CMP, y1 + pltpu.roll(y2, N_CMP_PAD - 1, 0), 0.0)
            out = out + _dot(_silu(hid).astype(BF16), w2_ref[g])
        return out

    kc_ref[...] = mlp(tk_ref, kw1_ref, kw2_ref).astype(BF16)
    vct_ref[...] = mlp(tv_ref, vw1_ref, vw2_ref).T.astype(BF16)


def _compress(tk, tv, pelo, pehi, kw1, kw2p, vw1, vw2p):
    seg = pl.BlockSpec((None, NSA_KV, N_CMP_PAD, CMP_STRIDE * NSA_DH), lambda b: (b, 0, 0, 0))
    full2 = lambda a: pl.BlockSpec(a.shape, lambda b: (0,) * a.ndim)
    return pl.pallas_call(
        _cmp_body,
        grid=(BATCH,),
        in_specs=[seg, seg, full2(pelo), full2(pehi), full2(kw1), full2(kw2p), full2(vw1), full2(vw2p)],
        out_specs=(pl.BlockSpec((None, N_CMP_PAD, 128), lambda b: (b, 0, 0)),
                   pl.BlockSpec((None, 128, N_CMP_PAD), lambda b: (b, 0, 0))),
        out_shape=(jax.ShapeDtypeStruct((BATCH, N_CMP_PAD, 128), BF16),
                   jax.ShapeDtypeStruct((BATCH, 128, N_CMP_PAD), BF16)),
        compiler_params=_cparams("parallel"),
        name="nsa_compress",
    )(tk, tv, pelo, pehi, kw1, kw2p, vw1, vw2p)


NSA_NL = NSA_REP * Q_BLOCK
NSA_GH = 8
NSA_PW = NSA_GH * Q_BLOCK
NSA_NP = NSA_REP // NSA_GH


def _nsa_body(q_ref, kc_ref, vct_ref, ks_ref, vst_ref, kw_ref, vwt_ref, gt_ref, ovt_ref, oh_ref,
              o_ref, qa_s, sc_s, s0_s, s1_s):
    g = pl.program_id(1)
    qb = pl.program_id(2)
    q0 = pl.multiple_of(qb * Q_BLOCK, Q_BLOCK)
    is_g0 = g == 0

    qblk = q_ref[...].astype(F32)
    zero_slab = jnp.zeros((NSA_DH, Q_BLOCK), F32)
    for p in range(NSA_REP // 2):
        t = qblk[:, p * 128:(p + 1) * 128].T
        for hh in range(2):
            s = t[hh * NSA_DH:(hh + 1) * NSA_DH]
            r = 2 * p + hh
            qa_s[0:128, r * Q_BLOCK:(r + 1) * Q_BLOCK] = jnp.concatenate(
                [jnp.where(is_g0, s, zero_slab), jnp.where(is_g0, zero_slab, s)], axis=0).astype(BF16)
    qa_s[128 + N_SEL:256, :] = jnp.zeros((128 - N_SEL, NSA_NL), BF16)

    tq = q0 + lax.broadcasted_iota(jnp.int32, (1, Q_BLOCK), 1)

    def pair(p):
        return slice(p * NSA_PW, (p + 1) * NSA_PW)

    def mask_pair(s, valid):
        return jnp.concatenate([jnp.where(valid, s[:, r * Q_BLOCK:(r + 1) * Q_BLOCK], NEG)
                                for r in range(NSA_GH)], axis=1)

    n_i = lax.broadcasted_iota(jnp.int32, (N_CMP_PAD, Q_BLOCK), 0)
    valid_c = (n_i * CMP_STRIDE + (CMP_LEN - 1) <= tq) & (n_i < N_CMP)
    p_sum = jnp.zeros((N_CMP_PAD, Q_BLOCK), F32)
    o_c = []
    for p in range(NSA_NP):
        sc = mask_pair(_dot(kc_ref[...], qa_s[0:128, pair(p)]), valid_c)
        m_c = jnp.max(sc, axis=0, keepdims=True)
        e_c = jnp.exp2(sc - m_c) * jnp.where(m_c > 0.5 * NEG, 1.0, 0.0)
        p_c = e_c / jnp.maximum(jnp.sum(e_c, axis=0, keepdims=True), 1e-30)
        o_c.append(_dot(vct_ref[...], p_c.astype(BF16)))
        for r in range(NSA_GH):
            p_sum = p_sum + p_c[:, r * Q_BLOCK:(r + 1) * Q_BLOCK]

    imp = jnp.dot(ovt_ref[...], p_sum, precision=lax.Precision.HIGHEST,
                  preferred_element_type=F32)
    j_i = lax.broadcasted_iota(jnp.int32, (N_SEL, Q_BLOCK), 0)
    cur = tq // SEL_LEN
    forced = (j_i == 0) | (j_i == cur) | (j_i == cur - 1)
    score = jnp.where(forced, jnp.inf, jnp.where(j_i > cur, -jnp.inf, imp))
    sc_s[...] = score
    sub8 = lax.broadcasted_iota(jnp.int32, (8, Q_BLOCK), 0)
    score_v = [score[8 * v:8 * v + 8] for v in range(N_SEL // 8)]
    rank_v = [jnp.zeros((8, Q_BLOCK), F32) for _ in range(N_SEL // 8)]
    for jp in range(N_SEL):
        row = sc_s[jp:jp + 1, :]
        for v in range(N_SEL // 8):
            if 8 * v > jp:
                one = jnp.where(row >= score_v[v], 1.0, 0.0)
            elif 8 * v + 7 <= jp:
                one = jnp.where(row > score_v[v], 1.0, 0.0)
            else:
                one = jnp.where(sub8 + 8 * v > jp, jnp.where(row >= score_v[v], 1.0, 0.0),
                                jnp.where(row > score_v[v], 1.0, 0.0))
            rank_v[v] = rank_v[v] + one
    rank = jnp.concatenate(rank_v, axis=0)
    bias = jnp.where((rank < SEL_TOP) & (j_i < 2 * qb), 0.0, NEG).astype(BF16)
    for r in range(NSA_REP):
        qa_s[128:128 + N_SEL, r * Q_BLOCK:(r + 1) * Q_BLOCK] = bias

    d_i = lax.broadcasted_iota(jnp.int32, (Q_BLOCK, Q_BLOCK), 0)
    t_i = lax.broadcasted_iota(jnp.int32, (Q_BLOCK, Q_BLOCK), 1)
    causal = d_i <= t_i
    kd = ks_ref[pl.ds(q0, Q_BLOCK), :]
    vd = vst_ref[:, pl.ds(q0, Q_BLOCK)]
    carry = []
    for p in range(NSA_NP):
        s = mask_pair(_dot(kd, qa_s[0:128, pair(p)]), causal)
        m = jnp.max(s, axis=0, keepdims=True)
        e = jnp.exp2(s - m)
        carry += [m, jnp.sum(e, axis=0, keepdims=True), _dot(vd, e.astype(BF16))]

    last_sub = SEQ // SEL_SUB - 1

    def scores_into(buf, c):
        k0 = pl.multiple_of(jnp.minimum(c, last_sub) * SEL_SUB, SEL_SUB)
        kaug = jnp.concatenate([ks_ref[pl.ds(k0, SEL_SUB), :], oh_ref[pl.ds(k0, SEL_SUB), :]], axis=1)
        buf[...] = _dot(kaug, qa_s[...])

    def softmax_from(buf, c, carry):
        m, l, acc = carry
        k0 = pl.multiple_of(c * SEL_SUB, SEL_SUB)
        s = buf[...]
        m_new = jnp.maximum(m, jnp.max(s, axis=0, keepdims=True))
        alpha = jnp.exp2(m - m_new)
        pr = jnp.exp2(s - m_new)
        return (m_new, l * alpha + jnp.sum(pr, axis=0, keepdims=True),
                acc * alpha + _dot(vst_ref[:, pl.ds(k0, SEL_SUB)], pr.astype(BF16)))

    scores_into(s0_s, 0)

    def sel_step(i, carry):
        c = 2 * i
        scores_into(s1_s, c + 1)
        carry = softmax_from(s0_s, c, carry)
        scores_into(s0_s, c + 2)
        return softmax_from(s1_s, c + 1, carry)

    n_main = (qb * Q_BLOCK + 2 * SEL_SUB - 1) // (2 * SEL_SUB)
    carry = lax.fori_loop(0, n_main, sel_step, tuple(carry))
    o_s = [carry[2] / jnp.maximum(carry[1], 1e-30)]

    w0 = pl.multiple_of(jnp.maximum(q0 - WINDOW, 0), Q_BLOCK)
    kwin = kw_ref[pl.ds(w0, WIN_KEYS), :]
    vwin = vwt_ref[:, pl.ds(w0, WIN_KEYS)]
    dpos = tq - (w0 + lax.broadcasted_iota(jnp.int32, (WIN_KEYS, Q_BLOCK), 0))
    valid_w = (dpos >= 0) & (dpos < WINDOW)
    o_w = []
    for p in range(NSA_NP):
        sw = mask_pair(_dot(kwin, qa_s[0:128, pair(p)]), valid_w)
        e_w = jnp.exp2(sw - jnp.max(sw, axis=0, keepdims=True))
        o_w.append(_dot(vwin, e_w.astype(BF16)) / jnp.maximum(jnp.sum(e_w, axis=0, keepdims=True), 1e-30))

    gall = gt_ref[...]
    gate = []
    for br in range(3):
        gb = gall[br * NSA_HEADS:(br + 1) * NSA_HEADS]
        gate.append(jnp.where(is_g0, gb[0:NSA_REP], gb[NSA_REP:NSA_HEADS]))
    for p2 in range(NSA_REP // 2):
        halves = []
        for hh in range(2):
            r = 2 * p2 + hh
            p = r // NSA_GH
            sl = slice((r % NSA_GH) * Q_BLOCK, (r % NSA_GH + 1) * Q_BLOCK)
            halves.append(gate[0][r:r + 1, :] * o_c[p][:, sl] + gate[1][r:r + 1, :] * o_s[p][:, sl]
                          + gate[2][r:r + 1, :] * o_w[p][:, sl])
        o_ref[:, p2 * 128:(p2 + 1) * 128] = jnp.concatenate(halves, axis=0).T.astype(BF16)


def _nsa_attn(q_r, kc, vct, ks, vst, kw, vwt, gt, ovt, onehot):
    per_b = lambda r, c: pl.BlockSpec((None, r, c), lambda b, g, i: (b, 0, 0))
    per_bg = lambda r, c: pl.BlockSpec((None, r, c), lambda b, g, i: (b, g, 0))
    const = lambda a: pl.BlockSpec(a.shape, lambda b, g, i: (0, 0))
    return pl.pallas_call(
        _nsa_body,
        grid=(BATCH, NSA_KV, N_QB),
        in_specs=[pl.BlockSpec((None, Q_BLOCK, NSA_REP * NSA_DH), lambda b, g, i: (b, i, g)),
                  per_b(N_CMP_PAD, 128), per_bg(NSA_DH, N_CMP_PAD),
                  per_b(SEQ, 128), per_bg(NSA_DH, SEQ),
                  per_b(SEQ, 128), per_bg(NSA_DH, SEQ),
                  pl.BlockSpec((None, 3 * NSA_HEADS, Q_BLOCK), lambda b, g, i: (b, 0, i)),
                  const(ovt), const(onehot)],
        out_specs=pl.BlockSpec((None, Q_BLOCK, NSA_REP * NSA_DH), lambda b, g, i: (b, i, g)),
        out_shape=jax.ShapeDtypeStruct((BATCH, SEQ, NSA_WIDTH), BF16),
        scratch_shapes=[pltpu.VMEM((256, NSA_NL), BF16), pltpu.VMEM((N_SEL, Q_BLOCK), F32),
                        pltpu.VMEM((SEL_SUB, NSA_NL), F32), pltpu.VMEM((SEL_SUB, NSA_NL), F32)],
        compiler_params=_cparams("parallel", "parallel", "arbitrary"),
        name="nsa_attn",
    )(q_r, kc, vct, ks, vst, kw, vwt, gt, ovt, onehot)


OUT_TM = 256


def _outproj_body(x_ref, oh_ref, on_ref, w_ref, nw_ref, x2_ref, hx_ref):
    y = (x_ref[...] + _dot(oh_ref[...], w_ref[0:HG_WIDTH, :])
         + _dot(on_ref[...], w_ref[HG_WIDTH:HG_WIDTH + NSA_WIDTH, :]))
    x2_ref[...] = y
    hx_ref[...] = _rms(y, nw_ref[...]).astype(BF16)


def _outproj(x1, o_hg, o_nsa, w_out, nw):
    row = lambda w: pl.BlockSpec((OUT_TM, w), lambda i: (i, 0))
    return pl.pallas_call(
        _outproj_body,
        grid=(TOKENS // OUT_TM,),
        in_specs=[row(D_MODEL), row(HG_WIDTH), row(NSA_WIDTH),
                  pl.BlockSpec((D_MODEL, D_MODEL), lambda i: (0, 0)),
                  pl.BlockSpec((1, D_MODEL), lambda i: (0, 0))],
        out_specs=(row(D_MODEL), row(D_MODEL)),
        out_shape=(jax.ShapeDtypeStruct((TOKENS, D_MODEL), F32),
                   jax.ShapeDtypeStruct((TOKENS, D_MODEL), BF16)),
        compiler_params=_cparams("parallel"),
        name="out_proj",
    )(x1, o_hg, o_nsa, w_out, nw)


def _memkv_body(m_ref, nw_ref, wk_ref, wv_ref, k_ref, v_ref):
    hm = _rms(m_ref[...], nw_ref[...]).astype(BF16)
    k_ref[...] = _dot(hm, wk_ref[...]).astype(BF16)
    v_ref[...] = _dot(hm, wv_ref[...]).astype(BF16)


def _memkv(mem, nw, wk, wv):
    width = X_HEADS * X_DH
    wspec = pl.BlockSpec((D_MODEL, width), lambda b: (0, 0))
    ospec = pl.BlockSpec((None, MEM_LEN, width), lambda b: (b, 0, 0))
    osh = jax.ShapeDtypeStruct((BATCH, MEM_LEN, width), BF16)
    return pl.pallas_call(
        _memkv_body,
        grid=(BATCH,),
        in_specs=[pl.BlockSpec((None, MEM_LEN, D_MODEL), lambda b: (b, 0, 0)),
                  pl.BlockSpec((1, D_MODEL), lambda b: (0, 0)), wspec, wspec],
        out_specs=(ospec, ospec), out_shape=(osh, osh),
        compiler_params=_cparams("parallel"),
        name="xattn_memkv",
    )(mem, nw, wk, wv)


XA_TM = 256


def _xattn_body(x_ref, hx_ref, wq_ref, k_ref, v_ref, wo_ref, o_ref):
    q = (_dot(hx_ref[...], wq_ref[...]) * (X_DH ** -0.5)).astype(BF16)
    heads = []
    for h in range(X_HEADS):
        sl = slice(h * X_DH, (h + 1) * X_DH)
        s = _dot_nt(q[:, sl], k_ref[:, sl])
        e = jnp.exp(s - jnp.max(s, axis=-1, keepdims=True))
        p = e / jnp.sum(e, axis=-1, keepdims=True)
        heads.append(_dot(p.astype(BF16), v_ref[:, sl]))
    o = jnp.concatenate(heads, axis=1).astype(BF16)
    o_ref[...] = x_ref[...] + _dot(o, wo_ref[...])


def _xattn(x2, hx, wq, k, v, wo):
    width = X_HEADS * X_DH
    tiles_per_b = SEQ // XA_TM
    kv = pl.BlockSpec((None, MEM_LEN, width), lambda i: (i // tiles_per_b, 0, 0))
    return pl.pallas_call(
        _xattn_body,
        grid=(TOKENS // XA_TM,),
        in_specs=[pl.BlockSpec((XA_TM, D_MODEL), lambda i: (i, 0)),
                  pl.BlockSpec((XA_TM, D_MODEL), lambda i: (i, 0)),
                  pl.BlockSpec((D_MODEL, width), lambda i: (0, 0)), kv, kv,
                  pl.BlockSpec((width, D_MODEL), lambda i: (0, 0))],
        out_specs=pl.BlockSpec((XA_TM, D_MODEL), lambda i: (i, 0)),
        out_shape=jax.ShapeDtypeStruct((TOKENS, D_MODEL), F32),
        compiler_params=_cparams("parallel"),
        name="xattn",
    )(x2, hx, wq, k, v, wo)


def _overlap_t():
    c0 = np.arange(N_CMP)[:, None] * CMP_STRIDE
    s0 = np.arange(N_SEL)[None, :] * SEL_LEN
    ov = np.clip(np.minimum(c0 + CMP_LEN, s0 + SEL_LEN) - np.maximum(c0, s0), 0, None) / CMP_LEN
    out = np.zeros((N_SEL, N_CMP_PAD), np.float32)
    out[:, :N_CMP] = ov.T
    return out


def _block_onehot():
    out = np.zeros((SEQ, 128), np.float32)
    out[np.arange(SEQ), np.arange(SEQ) // SEL_LEN] = 1.0
    return out


def kernel(x, mem, positions, ffn1_norm, ffn1_w_gate, ffn1_w_up, ffn1_w_down, mix_norm, w_in, hgrn_lb_logits, hgrn_out_norm, nsa_cmp_pe, nsa_cmp_k_w1, nsa_cmp_k_w2, nsa_cmp_v_w1, nsa_cmp_v_w2, w_out, xattn_norm, mem_norm, xattn_wq, xattn_wk, xattn_wv, xattn_wo, ffn2_norm, ffn2_w_gate, ffn2_w_up, ffn2_w_down, final_norm):
    bf = lambda a: a.astype(BF16)
    vec = lambda a: a.reshape(1, -1).astype(F32)
    x2d = x.reshape(TOKENS, D_MODEL)

    x1, h_mix = _ffn(x2d, vec(ffn1_norm[0]), bf(ffn1_w_gate[0]), bf(ffn1_w_up[0]), bf(ffn1_w_down[0]),
                     vec(mix_norm[0]), final=False)

    w = w_in[0]
    gate_cols = 5888 + (np.arange(NSA_HEADS)[None, :] * 3 + np.arange(3)[:, None]).reshape(-1)
    w_nsa = jnp.concatenate(
        [w[:, 4096:5120], w[:, 5120:5248], w[:, 5376:5504], w[:, 5632:5760],
         w[:, 5248:5376], w[:, 5504:5632], w[:, 5760:5888], w[:, gate_cols],
         jnp.zeros((D_MODEL, NSA_PROJ - 1792 - 3 * NSA_HEADS), w.dtype)], axis=1)
    proj_hg = _proj(h_mix, bf(w[:, :4 * HG_WIDTH]), PROJ_TN, "proj_hgrn")
    proj_nsa = _proj(h_mix, bf(w_nsa), NSA_PROJ // 3, "proj_nsa")

    o_hg = _hgrn(proj_hg, hgrn_lb_logits.astype(F32), vec(hgrn_out_norm[0]))

    inv = ROPE_THETA ** (-jnp.arange(NSA_DH // 2, dtype=F32) / (NSA_DH // 2))
    inv128 = jnp.tile(inv, 128 // (NSA_DH // 2)).reshape(1, 128)
    pos_f = positions.astype(F32).reshape(BATCH, SEQ, 1)
    q_r, kc_tok, vc_tok, ks, kw, vst, vwt, gt = _nsa_prep(proj_nsa, pos_f, inv128)

    def segments(t):
        t = t.reshape(BATCH, SEQ // CMP_STRIDE, CMP_STRIDE, NSA_KV, NSA_DH)
        return t.transpose(0, 3, 1, 2, 4).reshape(BATCH, NSA_KV, SEQ // CMP_STRIDE, CMP_STRIDE * NSA_DH)

    pe = nsa_cmp_pe[0].astype(F32)
    pelo = pe[:CMP_STRIDE].reshape(1, -1)
    pehi = pe[CMP_STRIDE:].reshape(1, -1)

    def pad_w2(w2):
        z = jnp.zeros_like(w2)
        return bf(jnp.stack([jnp.concatenate([w2, z], axis=1), jnp.concatenate([z, w2], axis=1)]))

    kc, vct = _compress(segments(kc_tok), segments(vc_tok), pelo, pehi,
                        bf(nsa_cmp_k_w1[0]), pad_w2(nsa_cmp_k_w2[0]),
                        bf(nsa_cmp_v_w1[0]), pad_w2(nsa_cmp_v_w2[0]))
    o_nsa = _nsa_attn(q_r, kc, vct, ks, vst, kw, vwt, gt, jnp.asarray(_overlap_t()),
                      jnp.asarray(_block_onehot(), dtype=BF16))

    x2, hx = _outproj(x1, o_hg.reshape(TOKENS, HG_WIDTH), o_nsa.reshape(TOKENS, NSA_WIDTH),
                      bf(w_out[0]), vec(xattn_norm[0]))

    km, vm = _memkv(mem, vec(mem_norm[0]), bf(xattn_wk[0]), bf(xattn_wv[0]))
    x3 = _xattn(x2, hx, bf(xattn_wq[0]), km, vm, bf(xattn_wo[0]))

    out = _ffn(x3, vec(ffn2_norm[0]), bf(ffn2_w_gate[0]), bf(ffn2_w_up[0]), bf(ffn2_w_down[0]),
               vec(final_norm), final=True)
    return out.reshape(BATCH, SEQ, D_MODEL)
```

```python
import functools

import numpy as np
import jax
import jax.numpy as jnp
from jax import lax
from jax.experimental import pallas as pl
from jax.experimental.pallas import tpu as pltpu

F32 = jnp.float32
BF16 = jnp.bfloat16

D_MODEL = 2048
BATCH = 2
SEQ = 4096
TOKENS = BATCH * SEQ
RMS_EPS = 1e-6
ROPE_THETA = 10000.0
HG_WIDTH = 1024
HG_HEADS = 8
HG_D = 128
HG_CHUNK = 128
HG_LEVELS = (64, 32, 16, 8, 4, 2, 1)
NSA_WIDTH = 1024
NSA_DH = 64
NSA_HEADS = 16
NSA_KV = 2
NSA_REP = 8
CMP_LEN = 32
CMP_STRIDE = 16
CMP_HIDDEN = 256
N_CMP = (SEQ - CMP_LEN) // CMP_STRIDE + 1
N_CMP_PAD = 256
SEL_LEN = 64
N_SEL = SEQ // SEL_LEN
SEL_TOP = 16
WINDOW = 512
Q_BLOCK = 128
N_QB = SEQ // Q_BLOCK
SEL_SUB = 256
WIN_KEYS = WINDOW + Q_BLOCK
MEM_LEN = 256
X_HEADS = 4
X_DH = 128
D_FF = 5632
NSA_PROJ = 1920
NEG = -1e30
LOG2E = 1.4426950408889634

V7X_VMEM_BYTES = 64 * 1024 * 1024
VMEM_LIMIT = V7X_VMEM_BYTES - 8 * 1024 * 1024


def _cparams(*sem, flags=None):
    return pltpu.CompilerParams(dimension_semantics=sem, vmem_limit_bytes=VMEM_LIMIT, flags=flags)


def _rms(x, w):
    return x * lax.rsqrt(jnp.mean(x * x, axis=-1, keepdims=True) + RMS_EPS) * w


def _silu(x):
    return x * jax.nn.sigmoid(x)


def _dot(a, b):
    return jnp.dot(a, b, preferred_element_type=F32)


def _dot_f32_by_01(sel, x):
    hi = x.astype(BF16)
    r1 = x - hi.astype(F32)
    mid = r1.astype(BF16)
    lo = (r1 - mid.astype(F32)).astype(BF16)
    n = x.shape[1]
    y = _dot(sel, jnp.concatenate([hi, mid, lo], axis=1))
    return y[:, 0:n] + y[:, n:2 * n] + y[:, 2 * n:3 * n]


def _dot_nt(a, b):
    return lax.dot_general(a, b, (((1,), (1,)), ((), ())), preferred_element_type=F32)


FFN_TM = 512
FFN_TF = 512


def _ffn_body(x_ref, nw_ref, wg_ref, wu_ref, wd_ref, nw2_ref, *rest, final):
    if final:
        o_ref, h_scr = rest
    else:
        o_ref, hn_ref, h_scr = rest
    j = pl.program_id(1)

    @pl.when(j == 0)
    def _():
        h_scr[...] = _rms(x_ref[...], nw_ref[...]).astype(BF16)
        o_ref[...] = jnp.zeros_like(o_ref)

    h = h_scr[...]
    g = _dot(h, wg_ref[...])
    u = _dot(h, wu_ref[...])
    a = (_silu(g) * u).astype(BF16)
    o_ref[...] += _dot(a, wd_ref[...])

    @pl.when(j == pl.num_programs(1) - 1)
    def _():
        y = x_ref[...] + 0.5 * o_ref[...]
        if final:
            o_ref[...] = _rms(y, nw2_ref[...])
        else:
            o_ref[...] = y
            hn_ref[...] = _rms(y, nw2_ref[...]).astype(BF16)


def _ffn(x, nw, wg, wu, wd, nw2, final):
    grid = (TOKENS // FFN_TM, D_FF // FFN_TF)
    row = pl.BlockSpec((FFN_TM, D_MODEL), lambda i, j: (i, 0))
    vec = pl.BlockSpec((1, D_MODEL), lambda i, j: (0, 0))
    in_specs = [row, vec,
                pl.BlockSpec((D_MODEL, FFN_TF), lambda i, j: (0, j)),
                pl.BlockSpec((D_MODEL, FFN_TF), lambda i, j: (0, j)),
                pl.BlockSpec((FFN_TF, D_MODEL), lambda i, j: (j, 0)),
                vec]
    if final:
        out_shape = jax.ShapeDtypeStruct((TOKENS, D_MODEL), F32)
        out_specs = row
    else:
        out_shape = (jax.ShapeDtypeStruct((TOKENS, D_MODEL), F32),
                     jax.ShapeDtypeStruct((TOKENS, D_MODEL), BF16))
        out_specs = (row, row)
    return pl.pallas_call(
        functools.partial(_ffn_body, final=final),
        grid=grid, in_specs=in_specs, out_specs=out_specs, out_shape=out_shape,
        scratch_shapes=[pltpu.VMEM((FFN_TM, D_MODEL), BF16)],
        compiler_params=_cparams("parallel", "arbitrary"),
        name="ffn_final" if final else "ffn",
    )(x, nw, wg, wu, wd, nw2)


PROJ_TM = 512
PROJ_TN = 512


def _mm_body(a_ref, w_ref, o_ref):
    o_ref[...] = _dot(a_ref[...], w_ref[...])


def _proj(a, w, tn, name):
    m, k = a.shape
    n = w.shape[1]
    return pl.pallas_call(
        _mm_body,
        grid=(m // PROJ_TM, n // tn),
        in_specs=[pl.BlockSpec((PROJ_TM, k), lambda i, j: (i, 0)),
                  pl.BlockSpec((k, tn), lambda i, j: (0, j))],
        out_specs=pl.BlockSpec((PROJ_TM, tn), lambda i, j: (i, j)),
        out_shape=jax.ShapeDtypeStruct((m, n), F32),
        compiler_params=_cparams("parallel", "arbitrary"),
        name=name,
    )(a, w)


HG_ROWS = 1024
HG_CUM = 256


def _hgrn_body(q_ref, f_ref, i_ref, g_ref, lbl_ref, nw_ref, o_ref, st_ref, k_s, b_s):
    c = pl.program_id(2)

    @pl.when(c == 0)
    def _():
        st_ref[...] = jnp.zeros_like(st_ref)

    l0 = lbl_ref[0:1, :]
    l1 = lbl_ref[1:2, :]
    lmax = jnp.maximum(l0, l1)
    e0 = jnp.exp(l0 - lmax)
    lb = e0 / (e0 + jnp.exp(l1 - lmax))

    C = HG_CHUNK
    f = lb + (1.0 - lb) * jax.nn.sigmoid(f_ref[...])
    k_s[...] = 1.0 - f
    r_i = lax.broadcasted_iota(jnp.int32, (HG_CUM, HG_CUM), 0)
    c_i = lax.broadcasted_iota(jnp.int32, (HG_CUM, HG_CUM), 1)
    tri = jnp.where((r_i >= c_i) & (r_i // C == c_i // C), 1.0, 0.0).astype(BF16)
    logf = jnp.log(f)
    for r0 in range(0, HG_ROWS, HG_CUM):
        b_s[r0:r0 + HG_CUM, :] = _dot_f32_by_01(tri, logf[r0:r0 + HG_CUM])

    t_i = lax.broadcasted_iota(jnp.int32, (C, C), 0)
    s_i = lax.broadcasted_iota(jnp.int32, (C, C), 1)
    level_mask = [(t_i // (2 * w) == s_i // (2 * w)) & (t_i % (2 * w) >= w) & (s_i % (2 * w) < w)
                  for w in HG_LEVELS]
    sub_r = lax.broadcasted_iota(jnp.int32, (8, HG_D), 0)
    odd_row = (lax.broadcasted_iota(jnp.int32, (C, HG_D), 0) & 1) == 1

    def bref_rows(w, r0, b):
        row = lambda r, n: jnp.broadcast_to(b_s[r0 + r:r0 + r + 1, :], (n, HG_D))
        if w >= 8:
            return jnp.concatenate([row(p0 + w - 1, 2 * w) for p0 in range(0, C, 2 * w)], axis=0)
        if w == 4:
            return jnp.concatenate([row(p0 + 3, 8) for p0 in range(0, C, 8)], axis=0)
        if w == 2:
            return jnp.concatenate([jnp.where(sub_r < 4, row(p0 + 1, 8), row(p0 + 5, 8))
                                    for p0 in range(0, C, 8)], axis=0)
        return jnp.where(odd_row, pltpu.roll(b, 1, 0), b)

    chunks = [ci * C for ci in range(HG_ROWS // C)]
    rows = lambda ref, r0: ref[r0:r0 + C, :]
    att = [jnp.zeros((C, C), F32) for _ in chunks]
    q16 = [rows(q_ref, r0).astype(BF16) for r0 in chunks]
    k16 = [rows(k_s, r0).astype(BF16) for r0 in chunks]
    for w, mask in zip(HG_LEVELS, level_mask):
        for n, r0 in enumerate(chunks):
            b = rows(b_s, r0)
            e = jnp.exp2(jnp.abs(b - bref_rows(w, r0, b)) * (-LOG2E)).astype(BF16)
            att[n] = jnp.where(mask, _dot_nt(q16[n] * e, k16[n] * e), att[n])
    o_intra = []
    for n, r0 in enumerate(chunks):
        q, k, v = rows(q_ref, r0), rows(k_s, r0), rows(i_ref, r0)
        o_intra.append(_dot(att[n].astype(BF16), v.astype(BF16))
                       + jnp.sum(q * k, axis=-1, keepdims=True) * v)
    upd = []
    for r0 in chunks:
        bl = b_s[r0 + C - 1:r0 + C, :]
        kd = rows(k_s, r0) * jnp.exp(bl - rows(b_s, r0))
        upd.append((jnp.exp(bl), _dot(rows(i_ref, r0).T.astype(BF16), kd.astype(BF16))))
    st_t = st_ref[...]
    for n, r0 in enumerate(chunks):
        qe = (rows(q_ref, r0) * jnp.exp(rows(b_s, r0))).astype(BF16)
        o = o_intra[n] + _dot_nt(qe, st_t.astype(BF16))
        st_t = st_t * upd[n][0] + upd[n][1]
        o = o * lax.rsqrt(jnp.mean(o * o, axis=-1, keepdims=True) + RMS_EPS)
        o_ref[r0:r0 + C, :] = (o * nw_ref[...] * _silu(rows(g_ref, r0))).astype(BF16)
    st_ref[...] = st_t


def _hgrn(proj_hg, lb_logits, norm_w):
    p3 = proj_hg.reshape(BATCH, SEQ, 4 * HG_WIDTH)

    def col(off):
        return pl.BlockSpec((None, HG_ROWS, HG_D), lambda b, h, c: (b, c, off + h))

    return pl.pallas_call(
        _hgrn_body,
        grid=(BATCH, HG_HEADS, SEQ // HG_ROWS),
        in_specs=[col(0), col(HG_HEADS), col(2 * HG_HEADS), col(3 * HG_HEADS),
                  pl.BlockSpec((2, HG_D), lambda b, h, c: (0, h)),
                  pl.BlockSpec((1, HG_D), lambda b, h, c: (0, h))],
        out_specs=pl.BlockSpec((None, HG_ROWS, HG_D), lambda b, h, c: (b, c, h)),
        out_shape=jax.ShapeDtypeStruct((BATCH, SEQ, HG_WIDTH), BF16),
        scratch_shapes=[pltpu.VMEM((HG_D, HG_D), F32),
                        pltpu.VMEM((HG_ROWS, HG_D), F32),
                        pltpu.VMEM((HG_ROWS, HG_D), F32)],
        compiler_params=_cparams("parallel", "parallel", "arbitrary"),
        name="hgrn2",
    )(p3, p3, p3, p3, lb_logits, norm_w)


PREP_TM = 256


def _prep_body(p_ref, pos_ref, inv_ref, q_ref, kc_ref, vc_ref, ks_ref, kw_ref,
               vst_ref, vwt_ref, gt_ref):
    ang = pos_ref[...] * inv_ref[...]
    cos = jnp.cos(ang)
    sin = jnp.sin(ang)
    lane = lax.broadcasted_iota(jnp.int32, (PREP_TM, 128), 1)
    lo = (lane & (NSA_DH // 2)) == 0
    sin_signed = jnp.where(lo, -sin, sin)

    def rope(x):
        rot = jnp.where(lo, pltpu.roll(x, 128 - NSA_DH // 2, 1), pltpu.roll(x, NSA_DH // 2, 1))
        return x * cos + rot * sin_signed

    scale = NSA_DH ** -0.5 * LOG2E
    for cblk in range(NSA_WIDTH // 128):
        sl = slice(cblk * 128, (cblk + 1) * 128)
        q_ref[:, sl] = (rope(p_ref[:, sl]) * scale).astype(BF16)
    kc_ref[...] = rope(p_ref[:, 1024:1152])
    ks_ref[...] = rope(p_ref[:, 1152:1280]).astype(BF16)
    kw_ref[...] = rope(p_ref[:, 1280:1408]).astype(BF16)
    vc_ref[...] = p_ref[:, 1408:1536]
    vst_ref[...] = p_ref[:, 1536:1664].T.astype(BF16)
    vwt_ref[...] = p_ref[:, 1664:1792].T.astype(BF16)
    gt_ref[...] = jax.nn.sigmoid(p_ref[:, 1792:1920]).T[0:3 * NSA_HEADS, :]


def _nsa_prep(proj_nsa, pos_f, inv128):
    nt = SEQ // PREP_TM
    p3 = proj_nsa.reshape(BATCH, SEQ, NSA_PROJ)
    nat = lambda w: pl.BlockSpec((None, PREP_TM, w), lambda b, i: (b, i, 0))
    tr = lambda r: pl.BlockSpec((None, r, PREP_TM), lambda b, i: (b, 0, i))
    sds = jax.ShapeDtypeStruct
    return pl.pallas_call(
        _prep_body,
        grid=(BATCH, nt),
        in_specs=[nat(NSA_PROJ), nat(1), pl.BlockSpec((1, 128), lambda b, i: (0, 0))],
        out_specs=(nat(NSA_WIDTH), nat(128), nat(128), nat(128), nat(128),
                   tr(128), tr(128), tr(3 * NSA_HEADS)),
        out_shape=(sds((BATCH, SEQ, NSA_WIDTH), BF16),
                   sds((BATCH, SEQ, 128), F32),
                   sds((BATCH, SEQ, 128), F32),
                   sds((BATCH, SEQ, 128), BF16),
                   sds((BATCH, SEQ, 128), BF16),
                   sds((BATCH, 128, SEQ), BF16),
                   sds((BATCH, 128, SEQ), BF16),
                   sds((BATCH, 3 * NSA_HEADS, SEQ), F32)),
        compiler_params=_cparams("parallel", "parallel"),
        name="nsa_prep",
    )(p3, pos_f, inv128)


def _cmp_body(tk_ref, tv_ref, pelo_ref, pehi_ref, kw1_ref, kw2_ref, vw1_ref, vw2_ref,
              kc_ref, vct_ref):
    half = CMP_LEN * NSA_DH // 2
    row = lax.broadcasted_iota(jnp.int32, (N_CMP_PAD, CMP_HIDDEN), 0)

    def mlp(t_ref, w1_ref, w2_ref):
        out = jnp.zeros((N_CMP_PAD, 128), F32)
        for g in range(NSA_KV):
            x = t_ref[g]
            y1 = _dot((x + pelo_ref[...]).astype(BF16), w1_ref[0:half, :])
            y2 = _dot((x + pehi_ref[...]).astype(BF16), w1_ref[half:2 * half, :])
            hid = jnp.where(row < N_CMP, y1 + pltpu.roll(y2, N_CMP_PAD - 1, 0), 0.0)
            out = out + _dot(_silu(hid).astype(BF16), w2_ref[g])
        return out

    kc_ref[...] = mlp(tk_ref, kw1_ref, kw2_ref).astype(BF16)
    vct_ref[...] = mlp(tv_ref, vw1_ref, vw2_ref).T.astype(BF16)


def _compress(tk, tv, pelo, pehi, kw1, kw2p, vw1, vw2p):
    seg = pl.BlockSpec((None, NSA_KV, N_CMP_PAD, CMP_STRIDE * NSA_DH), lambda b: (b, 0, 0, 0))
    full2 = lambda a: pl.BlockSpec(a.shape, lambda b: (0,) * a.ndim)
    return pl.pallas_call(
        _cmp_body,
        grid=(BATCH,),
        in_specs=[seg, seg, full2(pelo), full2(pehi), full2(kw1), full2(kw2p), full2(vw1), full2(vw2p)],
        out_specs=(pl.BlockSpec((None, N_CMP_PAD, 128), lambda b: (b, 0, 0)),
                   pl.BlockSpec((None, 128, N_CMP_PAD), lambda b: (b, 0, 0))),
        out_shape=(jax.ShapeDtypeStruct((BATCH, N_CMP_PAD, 128), BF16),
                   jax.ShapeDtypeStruct((BATCH, 128, N_CMP_PAD), BF16)),
        compiler_params=_cparams("parallel"),
        name="nsa_compress",
    )(tk, tv, pelo, pehi, kw1, kw2p, vw1, vw2p)


NSA_NL = NSA_REP * Q_BLOCK
NSA_GH = 8
NSA_PW = NSA_GH * Q_BLOCK
NSA_NP = NSA_REP // NSA_GH


def _nsa_body(q_ref, kc_ref, vct_ref, ks_ref, vst_ref, kw_ref, vwt_ref, gt_ref, ovt_ref, oh_ref,
              o_ref, qa_s, sc_s, s0_s, s1_s):
    g = pl.program_id(1)
    qb = pl.program_id(2)
    q0 = pl.multiple_of(qb * Q_BLOCK, Q_BLOCK)
    is_g0 = g == 0

    qblk = q_ref[...].astype(F32)
    zero_slab = jnp.zeros((NSA_DH, Q_BLOCK), F32)
    for p in range(NSA_REP // 2):
        t = qblk[:, p * 128:(p + 1) * 128].T
        for hh in range(2):
            s = t[hh * NSA_DH:(hh + 1) * NSA_DH]
            r = 2 * p + hh
            qa_s[0:128, r * Q_BLOCK:(r + 1) * Q_BLOCK] = jnp.concatenate(
                [jnp.where(is_g0, s, zero_slab), jnp.where(is_g0, zero_slab, s)], axis=0).astype(BF16)
    qa_s[128 + N_SEL:256, :] = jnp.zeros((128 - N_SEL, NSA_NL), BF16)

    tq = q0 + lax.broadcasted_iota(jnp.int32, (1, Q_BLOCK), 1)

    def pair(p):
        return slice(p * NSA_PW, (p + 1) * NSA_PW)

    def mask_pair(s, valid):
        return jnp.concatenate([jnp.where(valid, s[:, r * Q_BLOCK:(r + 1) * Q_BLOCK], NEG)
                                for r in range(NSA_GH)], axis=1)

    n_i = lax.broadcasted_iota(jnp.int32, (N_CMP_PAD, Q_BLOCK), 0)
    valid_c = (n_i * CMP_STRIDE + (CMP_LEN - 1) <= tq) & (n_i < N_CMP)
    p_sum = jnp.zeros((N_CMP_PAD, Q_BLOCK), F32)
    o_c = []
    for p in range(NSA_NP):
        sc = mask_pair(_dot(kc_ref[...], qa_s[0:128, pair(p)]), valid_c)
        m_c = jnp.max(sc, axis=0, keepdims=True)
        e_c = jnp.exp2(sc - m_c) * jnp.where(m_c > 0.5 * NEG, 1.0, 0.0)
        p_c = e_c / jnp.maximum(jnp.sum(e_c, axis=0, keepdims=True), 1e-30)
        o_c.append(_dot(vct_ref[...], p_c.astype(BF16)))
        for r in range(NSA_GH):
            p_sum = p_sum + p_c[:, r * Q_BLOCK:(r + 1) * Q_BLOCK]

    imp = jnp.dot(ovt_ref[...], p_sum, precision=lax.Precision.HIGHEST,
                  preferred_element_type=F32)
    j_i = lax.broadcasted_iota(jnp.int32, (N_SEL, Q_BLOCK), 0)
    cur = tq // SEL_LEN
    forced = (j_i == 0) | (j_i == cur) | (j_i == cur - 1)
    score = jnp.where(forced, jnp.inf, jnp.where(j_i > cur, -jnp.inf, imp))
    sc_s[...] = score
    sub8 = lax.broadcasted_iota(jnp.int32, (8, Q_BLOCK), 0)
    score_v = [score[8 * v:8 * v + 8] for v in range(N_SEL // 8)]
    rank_v = [jnp.zeros((8, Q_BLOCK), F32) for _ in range(N_SEL // 8)]
    for jp in range(N_SEL):
        row = sc_s[jp:jp + 1, :]
        for v in range(N_SEL // 8):
            if 8 * v > jp:
                one = jnp.where(row >= score_v[v], 1.0, 0.0)
            elif 8 * v + 7 <= jp:
                one = jnp.where(row > score_v[v], 1.0, 0.0)
            else:
                one = jnp.where(sub8 + 8 * v > jp, jnp.where(row >= score_v[v], 1.0, 0.0),
                                jnp.where(row > score_v[v], 1.0, 0.0))
            rank_v[v] = rank_v[v] + one
    rank = jnp.concatenate(rank_v, axis=0)
    bias = jnp.where((rank < SEL_TOP) & (j_i < 2 * qb), 0.0, NEG).astype(BF16)
    for r in range(NSA_REP):
        qa_s[128:128 + N_SEL, r * Q_BLOCK:(r + 1) * Q_BLOCK] = bias

    d_i = lax.broadcasted_iota(jnp.int32, (Q_BLOCK, Q_BLOCK), 0)
    t_i = lax.broadcasted_iota(jnp.int32, (Q_BLOCK, Q_BLOCK), 1)
    causal = d_i <= t_i
    kd = ks_ref[pl.ds(q0, Q_BLOCK), :]
    vd = vst_ref[:, pl.ds(q0, Q_BLOCK)]
    carry = []
    for p in range(NSA_NP):
        s = mask_pair(_dot(kd, qa_s[0:128, pair(p)]), causal)
        m = jnp.max(s, axis=0, keepdims=True)
        e = jnp.exp2(s - m)
        carry += [m, jnp.sum(e, axis=0, keepdims=True), _dot(vd, e.astype(BF16))]

    last_sub = SEQ // SEL_SUB - 1

    def scores_into(buf, c):
        k0 = pl.multiple_of(jnp.minimum(c, last_sub) * SEL_SUB, SEL_SUB)
        kaug = jnp.concatenate([ks_ref[pl.ds(k0, SEL_SUB), :], oh_ref[pl.ds(k0, SEL_SUB), :]], axis=1)
        buf[...] = _dot(kaug, qa_s[...])

    def softmax_from(buf, c, carry):
        m, l, acc = carry
        k0 = pl.multiple_of(c * SEL_SUB, SEL_SUB)
        s = buf[...]
        m_new = jnp.maximum(m, jnp.max(s, axis=0, keepdims=True))
        alpha = jnp.exp2(m - m_new)
        pr = jnp.exp2(s - m_new)
        return (m_new, l * alpha + jnp.sum(pr, axis=0, keepdims=True),
                acc * alpha + _dot(vst_ref[:, pl.ds(k0, SEL_SUB)], pr.astype(BF16)))

    scores_into(s0_s, 0)

    def sel_step(i, carry):
        c = 2 * i
        scores_into(s1_s, c + 1)
        carry = softmax_from(s0_s, c, carry)
        scores_into(s0_s, c + 2)
        return softmax_from(s1_s, c + 1, carry)

    n_main = (qb * Q_BLOCK + 2 * SEL_SUB - 1) // (2 * SEL_SUB)
    carry = lax.fori_loop(0, n_main, sel_step, tuple(carry))
    o_s = [carry[2] / jnp.maximum(carry[1], 1e-30)]

    w0 = pl.multiple_of(jnp.maximum(q0 - WINDOW, 0), Q_BLOCK)
    kwin = kw_ref[pl.ds(w0, WIN_KEYS), :]
    vwin = vwt_ref[:, pl.ds(w0, WIN_KEYS)]
    dpos = tq - (w0 + lax.broadcasted_iota(jnp.int32, (WIN_KEYS, Q_BLOCK), 0))
    valid_w = (dpos >= 0) & (dpos < WINDOW)
    o_w = []
    for p in range(NSA_NP):
        sw = mask_pair(_dot(kwin, qa_s[0:128, pair(p)]), valid_w)
        e_w = jnp.exp2(sw - jnp.max(sw, axis=0, keepdims=True))
        o_w.append(_dot(vwin, e_w.astype(BF16)) / jnp.maximum(jnp.sum(e_w, axis=0, keepdims=True), 1e-30))

    gall = gt_ref[...]
    gate = []
    for br in range(3):
        gb = gall[br * NSA_HEADS:(br + 1) * NSA_HEADS]
        gate.append(jnp.where(is_g0, gb[0:NSA_REP], gb[NSA_REP:NSA_HEADS]))
    for p2 in range(NSA_REP // 2):
        halves = []
        for hh in range(2):
            r = 2 * p2 + hh
            p = r // NSA_GH
            sl = slice((r % NSA_GH) * Q_BLOCK, (r % NSA_GH + 1) * Q_BLOCK)
            halves.append(gate[0][r:r + 1, :] * o_c[p][:, sl] + gate[1][r:r + 1, :] * o_s[p][:, sl]
                          + gate[2][r:r + 1, :] * o_w[p][:, sl])
        o_ref[:, p2 * 128:(p2 + 1) * 128] = jnp.concatenate(halves, axis=0).T.astype(BF16)


def _nsa_attn(q_r, kc, vct, ks, vst, kw, vwt, gt, ovt, onehot):
    per_b = lambda r, c: pl.BlockSpec((None, r, c), lambda b, g, i: (b, 0, 0))
    per_bg = lambda r, c: pl.BlockSpec((None, r, c), lambda b, g, i: (b, g, 0))
    const = lambda a: pl.BlockSpec(a.shape, lambda b, g, i: (0, 0))
    return pl.pallas_call(
        _nsa_body,
        grid=(BATCH, NSA_KV, N_QB),
        in_specs=[pl.BlockSpec((None, Q_BLOCK, NSA_REP * NSA_DH), lambda b, g, i: (b, i, g)),
                  per_b(N_CMP_PAD, 128), per_bg(NSA_DH, N_CMP_PAD),
                  per_b(SEQ, 128), per_bg(NSA_DH, SEQ),
                  per_b(SEQ, 128), per_bg(NSA_DH, SEQ),
                  pl.BlockSpec((None, 3 * NSA_HEADS, Q_BLOCK), lambda b, g, i: (b, 0, i)),
                  const(ovt), const(onehot)],
        out_specs=pl.BlockSpec((None, Q_BLOCK, NSA_REP * NSA_DH), lambda b, g, i: (b, i, g)),
        out_shape=jax.ShapeDtypeStruct((BATCH, SEQ, NSA_WIDTH), BF16),
        scratch_shapes=[pltpu.VMEM((256, NSA_NL), BF16), pltpu.VMEM((N_SEL, Q_BLOCK), F32),
                        pltpu.VMEM((SEL_SUB, NSA_NL), F32), pltpu.VMEM((SEL_SUB, NSA_NL), F32)],
        compiler_params=_cparams("parallel", "parallel", "arbitrary"),
        name="nsa_attn",
    )(q_r, kc, vct, ks, vst, kw, vwt, gt, ovt, onehot)


OUT_TM = 256


def _outproj_body(x_ref, oh_ref, on_ref, w_ref, nw_ref, x2_ref, hx_ref):
    y = (x_ref[...] + _dot(oh_ref[...], w_ref[0:HG_WIDTH, :])
         + _dot(on_ref[...], w_ref[HG_WIDTH:HG_WIDTH + NSA_WIDTH, :]))
    x2_ref[...] = y
    hx_ref[...] = _rms(y, nw_ref[...]).astype(BF16)


def _outproj(x1, o_hg, o_nsa, w_out, nw):
    row = lambda w: pl.BlockSpec((OUT_TM, w), lambda i: (i, 0))
    return pl.pallas_call(
        _outproj_body,
        grid=(TOKENS // OUT_TM,),
        in_specs=[row(D_MODEL), row(HG_WIDTH), row(NSA_WIDTH),
                  pl.BlockSpec((D_MODEL, D_MODEL), lambda i: (0, 0)),
                  pl.BlockSpec((1, D_MODEL), lambda i: (0, 0))],
        out_specs=(row(D_MODEL), row(D_MODEL)),
        out_shape=(jax.ShapeDtypeStruct((TOKENS, D_MODEL), F32),
                   jax.ShapeDtypeStruct((TOKENS, D_MODEL), BF16)),
        compiler_params=_cparams("parallel"),
        name="out_proj",
    )(x1, o_hg, o_nsa, w_out, nw)


def _memkv_body(m_ref, nw_ref, wk_ref, wv_ref, k_ref, v_ref):
    hm = _rms(m_ref[...], nw_ref[...]).astype(BF16)
    k_ref[...] = _dot(hm, wk_ref[...]).astype(BF16)
    v_ref[...] = _dot(hm, wv_ref[...]).astype(BF16)


def _memkv(mem, nw, wk, wv):
    width = X_HEADS * X_DH
    wspec = pl.BlockSpec((D_MODEL, width), lambda b: (0, 0))
    ospec = pl.BlockSpec((None, MEM_LEN, width), lambda b: (b, 0, 0))
    osh = jax.ShapeDtypeStruct((BATCH, MEM_LEN, width), BF16)
    return pl.pallas_call(
        _memkv_body,
        grid=(BATCH,),
        in_specs=[pl.BlockSpec((None, MEM_LEN, D_MODEL), lambda b: (b, 0, 0)),
                  pl.BlockSpec((1, D_MODEL), lambda b: (0, 0)), wspec, wspec],
        out_specs=(ospec, ospec), out_shape=(osh, osh),
        compiler_params=_cparams("parallel"),
        name="xattn_memkv",
    )(mem, nw, wk, wv)


XA_TM = 256


def _xattn_body(x_ref, hx_ref, wq_ref, k_ref, v_ref, wo_ref, o_ref):
    q = (_dot(hx_ref[...], wq_ref[...]) * (X_DH ** -0.5)).astype(BF16)
    heads = []
    for h in range(X_HEADS):
        sl = slice(h * X_DH, (h + 1) * X_DH)
        s = _dot_nt(q[:, sl], k_ref[:, sl])
        e = jnp.exp(s - jnp.max(s, axis=-1, keepdims=True))
        p = e / jnp.sum(e, axis=-1, keepdims=True)
        heads.append(_dot(p.astype(BF16), v_ref[:, sl]))
    o = jnp.concatenate(heads, axis=1).astype(BF16)
    o_ref[...] = x_ref[...] + _dot(o, wo_ref[...])


def _xattn(x2, hx, wq, k, v, wo):
    width = X_HEADS * X_DH
    tiles_per_b = SEQ // XA_TM
    kv = pl.BlockSpec((None, MEM_LEN, width), lambda i: (i // tiles_per_b, 0, 0))
    return pl.pallas_call(
        _xattn_body,
        grid=(TOKENS // XA_TM,),
        in_specs=[pl.BlockSpec((XA_TM, D_MODEL), lambda i: (i, 0)),
                  pl.BlockSpec((XA_TM, D_MODEL), lambda i: (i, 0)),
                  pl.BlockSpec((D_MODEL, width), lambda i: (0, 0)), kv, kv,
                  pl.BlockSpec((width, D_MODEL), lambda i: (0, 0))],
        out_specs=pl.BlockSpec((XA_TM, D_MODEL), lambda i: (i, 0)),
        out_shape=jax.ShapeDtypeStruct((TOKENS, D_MODEL), F32),
        compiler_params=_cparams("parallel"),
        name="xattn",
    )(x2, hx, wq, k, v, wo)


def _overlap_t():
    c0 = np.arange(N_CMP)[:, None] * CMP_STRIDE
    s0 = np.arange(N_SEL)[None, :] * SEL_LEN
    ov = np.clip(np.minimum(c0 + CMP_LEN, s0 + SEL_LEN) - np.maximum(c0, s0), 0, None) / CMP_LEN
    out = np.zeros((N_SEL, N_CMP_PAD), np.float32)
    out[:, :N_CMP] = ov.T
    return out


def _block_onehot():
    out = np.zeros((SEQ, 128), np.float32)
    out[np.arange(SEQ), np.arange(SEQ) // SEL_LEN] = 1.0
    return out


def kernel(x, mem, positions, ffn1_norm, ffn1_w_gate, ffn1_w_up, ffn1_w_down, mix_norm, w_in, hgrn_lb_logits, hgrn_out_norm, nsa_cmp_pe, nsa_cmp_k_w1, nsa_cmp_k_w2, nsa_cmp_v_w1, nsa_cmp_v_w2, w_out, xattn_norm, mem_norm, xattn_wq, xattn_wk, xattn_wv, xattn_wo, ffn2_norm, ffn2_w_gate, ffn2_w_up, ffn2_w_down, final_norm):
    bf = lambda a: a.astype(BF16)
    vec = lambda a: a.reshape(1, -1).astype(F32)
    x2d = x.reshape(TOKENS, D_MODEL)

    x1, h_mix = _ffn(x2d, vec(ffn1_norm[0]), bf(ffn1_w_gate[0]), bf(ffn1_w_up[0]), bf(ffn1_w_down[0]),
                     vec(mix_norm[0]), final=False)

    w = w_in[0]
    gate_cols = 5888 + (np.arange(NSA_HEADS)[None, :] * 3 + np.arange(3)[:, None]).reshape(-1)
    w_nsa = jnp.concatenate(
        [w[:, 4096:5120], w[:, 5120:5248], w[:, 5376:5504], w[:, 5632:5760],
         w[:, 5248:5376], w[:, 5504:5632], w[:, 5760:5888], w[:, gate_cols],
         jnp.zeros((D_MODEL, NSA_PROJ - 1792 - 3 * NSA_HEADS), w.dtype)], axis=1)
    proj_hg = _proj(h_mix, bf(w[:, :4 * HG_WIDTH]), PROJ_TN, "proj_hgrn")
    proj_nsa = _proj(h_mix, bf(w_nsa), NSA_PROJ // 3, "proj_nsa")

    o_hg = _hgrn(proj_hg, hgrn_lb_logits.astype(F32), vec(hgrn_out_norm[0]))

    inv = ROPE_THETA ** (-jnp.arange(NSA_DH // 2, dtype=F32) / (NSA_DH // 2))
    inv128 = jnp.tile(inv, 128 // (NSA_DH // 2)).reshape(1, 128)
    pos_f = positions.astype(F32).reshape(BATCH, SEQ, 1)
    q_r, kc_tok, vc_tok, ks, kw, vst, vwt, gt = _nsa_prep(proj_nsa, pos_f, inv128)

    def segments(t):
        t = t.reshape(BATCH, SEQ // CMP_STRIDE, CMP_STRIDE, NSA_KV, NSA_DH)
        return t.transpose(0, 3, 1, 2, 4).reshape(BATCH, NSA_KV, SEQ // CMP_STRIDE, CMP_STRIDE * NSA_DH)

    pe = nsa_cmp_pe[0].astype(F32)
    pelo = pe[:CMP_STRIDE].reshape(1, -1)
    pehi = pe[CMP_STRIDE:].reshape(1, -1)

    def pad_w2(w2):
        z = jnp.zeros_like(w2)
        return bf(jnp.stack([jnp.concatenate([w2, z], axis=1), jnp.concatenate([z, w2], axis=1)]))

    kc, vct = _compress(segments(kc_tok), segments(vc_tok), pelo, pehi,
                        bf(nsa_cmp_k_w1[0]), pad_w2(nsa_cmp_k_w2[0]),
                        bf(nsa_cmp_v_w1[0]), pad_w2(nsa_cmp_v_w2[0]))
    o_nsa = _nsa_attn(q_r, kc, vct, ks, vst, kw, vwt, gt, jnp.asarray(_overlap_t()),
                      jnp.asarray(_block_onehot(), dtype=BF16))

    x2, hx = _outproj(x1, o_hg.reshape(TOKENS, HG_WIDTH), o_nsa.reshape(TOKENS, NSA_WIDTH),
                      bf(w_out[0]), vec(xattn_norm[0]))

    km, vm = _memkv(mem, vec(mem_norm[0]), bf(xattn_wk[0]), bf(xattn_wv[0]))
    x3 = _xattn(x2, hx, bf(xattn_wq[0]), km, vm, bf(xattn_wo[0]))

    out = _ffn(x3, vec(ffn2_norm[0]), bf(ffn2_w_gate[0]), bf(ffn2_w_up[0]), bf(ffn2_w_down[0]),
               vec(final_norm), final=True)
    return out.reshape(BATCH, SEQ, D_MODEL)
```

```python
import functools

import numpy as np
import jax
import jax.numpy as jnp
from jax import lax
from jax.experimental import pallas as pl
from jax.experimental.pallas import tpu as pltpu

F32 = jnp.float32
BF16 = jnp.bfloat16

D_MODEL = 2048
BATCH = 2
SEQ = 4096
TOKENS = BATCH * SEQ
RMS_EPS = 1e-6
ROPE_THETA = 10000.0
HG_WIDTH = 1024
HG_HEADS = 8
HG_D = 128
HG_CHUNK = 128
HG_LEVELS = (64, 32, 16, 8, 4, 2, 1)
NSA_WIDTH = 1024
NSA_DH = 64
NSA_HEADS = 16
NSA_KV = 2
NSA_REP = 8
NSA_VROWS = NSA_DH + 16
CMP_LEN = 32
CMP_STRIDE = 16
CMP_HIDDEN = 256
N_CMP = (SEQ - CMP_LEN) // CMP_STRIDE + 1
N_CMP_PAD = 256
SEL_LEN = 64
N_SEL = SEQ // SEL_LEN
SEL_TOP = 16
WINDOW = 512
Q_BLOCK = 128
N_QB = SEQ // Q_BLOCK
SEL_SUB = 256
WIN_KEYS = WINDOW + Q_BLOCK
MEM_LEN = 256
X_HEADS = 4
X_DH = 128
D_FF = 5632
NSA_PROJ = 1920
NEG = -1e30
LOG2E = 1.4426950408889634

V7X_VMEM_BYTES = 64 * 1024 * 1024
VMEM_LIMIT = V7X_VMEM_BYTES - 8 * 1024 * 1024


def _cparams(*sem, flags=None):
    return pltpu.CompilerParams(dimension_semantics=sem, vmem_limit_bytes=VMEM_LIMIT, flags=flags)


def _rms(x, w):
    return x * lax.rsqrt(jnp.mean(x * x, axis=-1, keepdims=True) + RMS_EPS) * w


def _silu(x):
    return x * jax.nn.sigmoid(x)


def _dot(a, b):
    return jnp.dot(a, b, preferred_element_type=F32)


def _dot_f32_by_01(sel, x):
    hi = x.astype(BF16)
    r1 = x - hi.astype(F32)
    mid = r1.astype(BF16)
    lo = (r1 - mid.astype(F32)).astype(BF16)
    n = x.shape[1]
    y = _dot(sel, jnp.concatenate([hi, mid, lo], axis=1))
    return y[:, 0:n] + y[:, n:2 * n] + y[:, 2 * n:3 * n]


def _dot_nt(a, b):
    return lax.dot_general(a, b, (((1,), (1,)), ((), ())), preferred_element_type=F32)


FFN_TM = 512
FFN_TF = 512


def _ffn_body(x_ref, nw_ref, wg_ref, wu_ref, wd_ref, nw2_ref, *rest, final):
    if final:
        o_ref, h_scr = rest
    else:
        o_ref, hn_ref, h_scr = rest
    j = pl.program_id(1)

    @pl.when(j == 0)
    def _():
        h_scr[...] = _rms(x_ref[...], nw_ref[...]).astype(BF16)
        o_ref[...] = jnp.zeros_like(o_ref)

    h = h_scr[...]
    g = _dot(h, wg_ref[...])
    u = _dot(h, wu_ref[...])
    a = (_silu(g) * u).astype(BF16)
    o_ref[...] += _dot(a, wd_ref[...])

    @pl.when(j == pl.num_programs(1) - 1)
    def _():
        y = x_ref[...] + 0.5 * o_ref[...]
        if final:
            o_ref[...] = _rms(y, nw2_ref[...])
        else:
            o_ref[...] = y
            hn_ref[...] = _rms(y, nw2_ref[...]).astype(BF16)


def _ffn(x, nw, wg, wu, wd, nw2, final):
    grid = (TOKENS // FFN_TM, D_FF // FFN_TF)
    row = pl.BlockSpec((FFN_TM, D_MODEL), lambda i, j: (i, 0))
    vec = pl.BlockSpec((1, D_MODEL), lambda i, j: (0, 0))
    in_specs = [row, vec,
                pl.BlockSpec((D_MODEL, FFN_TF), lambda i, j: (0, j)),
                pl.BlockSpec((D_MODEL, FFN_TF), lambda i, j: (0, j)),
                pl.BlockSpec((FFN_TF, D_MODEL), lambda i, j: (j, 0)),
                vec]
    if final:
        out_shape = jax.ShapeDtypeStruct((TOKENS, D_MODEL), F32)
        out_specs = row
    else:
        out_shape = (jax.ShapeDtypeStruct((TOKENS, D_MODEL), F32),
                     jax.ShapeDtypeStruct((TOKENS, D_MODEL), BF16))
        out_specs = (row, row)
    return pl.pallas_call(
        functools.partial(_ffn_body, final=final),
        grid=grid, in_specs=in_specs, out_specs=out_specs, out_shape=out_shape,
        scratch_shapes=[pltpu.VMEM((FFN_TM, D_MODEL), BF16)],
        compiler_params=_cparams("parallel", "arbitrary"),
        name="ffn_final" if final else "ffn",
    )(x, nw, wg, wu, wd, nw2)


PROJ_TM = 512
PROJ_TN = 512


def _mm_body(a_ref, w_ref, o_ref):
    o_ref[...] = _dot(a_ref[...], w_ref[...])


def _proj(a, w, tn, name):
    m, k = a.shape
    n = w.shape[1]
    return pl.pallas_call(
        _mm_body,
        grid=(m // PROJ_TM, n // tn),
        in_specs=[pl.BlockSpec((PROJ_TM, k), lambda i, j: (i, 0)),
                  pl.BlockSpec((k, tn), lambda i, j: (0, j))],
        out_specs=pl.BlockSpec((PROJ_TM, tn), lambda i, j: (i, j)),
        out_shape=jax.ShapeDtypeStruct((m, n), F32),
        compiler_params=_cparams("parallel", "arbitrary"),
        name=name,
    )(a, w)


HG_ROWS = 1024
HG_CUM = 256


def _hgrn_body(q_ref, f_ref, i_ref, g_ref, lbl_ref, nw_ref, o_ref, st_ref, k_s, b_s):
    c = pl.program_id(2)

    @pl.when(c == 0)
    def _():
        st_ref[...] = jnp.zeros_like(st_ref)

    l0 = lbl_ref[0:1, :]
    l1 = lbl_ref[1:2, :]
    lmax = jnp.maximum(l0, l1)
    e0 = jnp.exp(l0 - lmax)
    lb = e0 / (e0 + jnp.exp(l1 - lmax))

    C = HG_CHUNK
    f = lb + (1.0 - lb) * jax.nn.sigmoid(f_ref[...])
    k_s[...] = 1.0 - f
    r_i = lax.broadcasted_iota(jnp.int32, (HG_CUM, HG_CUM), 0)
    c_i = lax.broadcasted_iota(jnp.int32, (HG_CUM, HG_CUM), 1)
    tri = jnp.where((r_i >= c_i) & (r_i // C == c_i // C), 1.0, 0.0).astype(BF16)
    logf = jnp.log(f)
    for r0 in range(0, HG_ROWS, HG_CUM):
        b_s[r0:r0 + HG_CUM, :] = _dot_f32_by_01(tri, logf[r0:r0 + HG_CUM])

    t_i = lax.broadcasted_iota(jnp.int32, (C, C), 0)
    s_i = lax.broadcasted_iota(jnp.int32, (C, C), 1)
    level_mask = [(t_i // (2 * w) == s_i // (2 * w)) & (t_i % (2 * w) >= w) & (s_i % (2 * w) < w)
                  for w in HG_LEVELS]
    sub_r = lax.broadcasted_iota(jnp.int32, (8, HG_D), 0)
    odd_row = (lax.broadcasted_iota(jnp.int32, (C, HG_D), 0) & 1) == 1

    def bref_rows(w, r0, b):
        row = lambda r, n: jnp.broadcast_to(b_s[r0 + r:r0 + r + 1, :], (n, HG_D))
        if w >= 8:
            return jnp.concatenate([row(p0 + w - 1, 2 * w) for p0 in range(0, C, 2 * w)], axis=0)
        if w == 4:
            return jnp.concatenate([row(p0 + 3, 8) for p0 in range(0, C, 8)], axis=0)
        if w == 2:
            return jnp.concatenate([jnp.where(sub_r < 4, row(p0 + 1, 8), row(p0 + 5, 8))
                                    for p0 in range(0, C, 8)], axis=0)
        return jnp.where(odd_row, pltpu.roll(b, 1, 0), b)

    chunks = [ci * C for ci in range(HG_ROWS // C)]
    rows = lambda ref, r0: ref[r0:r0 + C, :]
    att = [jnp.zeros((C, C), F32) for _ in chunks]
    q16 = [rows(q_ref, r0).astype(BF16) for r0 in chunks]
    k16 = [rows(k_s, r0).astype(BF16) for r0 in chunks]
    for w, mask in zip(HG_LEVELS, level_mask):
        for n, r0 in enumerate(chunks):
            b = rows(b_s, r0)
            e = jnp.exp2(jnp.abs(b - bref_rows(w, r0, b)) * (-LOG2E)).astype(BF16)
            att[n] = jnp.where(mask, _dot_nt(q16[n] * e, k16[n] * e), att[n])
    o_intra = []
    for n, r0 in enumerate(chunks):
        q, k, v = rows(q_ref, r0), rows(k_s, r0), rows(i_ref, r0)
        o_intra.append(_dot(att[n].astype(BF16), v.astype(BF16))
                       + jnp.sum(q * k, axis=-1, keepdims=True) * v)
    upd = []
    for r0 in chunks:
        bl = b_s[r0 + C - 1:r0 + C, :]
        kd = rows(k_s, r0) * jnp.exp(bl - rows(b_s, r0))
        upd.append((jnp.exp(bl), _dot(rows(i_ref, r0).T.astype(BF16), kd.astype(BF16))))
    st_t = st_ref[...]
    for n, r0 in enumerate(chunks):
        qe = (rows(q_ref, r0) * jnp.exp(rows(b_s, r0))).astype(BF16)
        o = o_intra[n] + _dot_nt(qe, st_t.astype(BF16))
        st_t = st_t * upd[n][0] + upd[n][1]
        o = o * lax.rsqrt(jnp.mean(o * o, axis=-1, keepdims=True) + RMS_EPS)
        o_ref[r0:r0 + C, :] = (o * nw_ref[...] * _silu(rows(g_ref, r0))).astype(BF16)
    st_ref[...] = st_t


def _hgrn(proj_hg, lb_logits, norm_w):
    p3 = proj_hg.reshape(BATCH, SEQ, 4 * HG_WIDTH)

    def col(off):
        return pl.BlockSpec((None, HG_ROWS, HG_D), lambda b, h, c: (b, c, off + h))

    return pl.pallas_call(
        _hgrn_body,
        grid=(BATCH, HG_HEADS, SEQ // HG_ROWS),
        in_specs=[col(0), col(HG_HEADS), col(2 * HG_HEADS), col(3 * HG_HEADS),
                  pl.BlockSpec((2, HG_D), lambda b, h, c: (0, h)),
                  pl.BlockSpec((1, HG_D), lambda b, h, c: (0, h))],
        out_specs=pl.BlockSpec((None, HG_ROWS, HG_D), lambda b, h, c: (b, c, h)),
        out_shape=jax.ShapeDtypeStruct((BATCH, SEQ, HG_WIDTH), BF16),
        scratch_shapes=[pltpu.VMEM((HG_D, HG_D), F32),
                        pltpu.VMEM((HG_ROWS, HG_D), F32),
                        pltpu.VMEM((HG_ROWS, HG_D), F32)],
        compiler_params=_cparams("parallel", "parallel", "arbitrary"),
        name="hgrn2",
    )(p3, p3, p3, p3, lb_logits, norm_w)


PREP_TM = 256


def _prep_body(p_ref, pos_ref, inv_ref, q_ref, kc_ref, vc_ref, ks_ref, kw_ref,
               vst_ref, vwt_ref, gt_ref):
    ang = pos_ref[...] * inv_ref[...]
    cos = jnp.cos(ang)
    sin = jnp.sin(ang)
    lane = lax.broadcasted_iota(jnp.int32, (PREP_TM, 128), 1)
    lo = (lane & (NSA_DH // 2)) == 0
    sin_signed = jnp.where(lo, -sin, sin)

    def rope(x):
        rot = jnp.where(lo, pltpu.roll(x, 128 - NSA_DH // 2, 1), pltpu.roll(x, NSA_DH // 2, 1))
        return x * cos + rot * sin_signed

    scale = NSA_DH ** -0.5 * LOG2E
    for cblk in range(NSA_WIDTH // 128):
        sl = slice(cblk * 128, (cblk + 1) * 128)
        q_ref[:, sl] = (rope(p_ref[:, sl]) * scale).astype(BF16)
    kc_ref[...] = rope(p_ref[:, 1024:1152])
    ks_ref[...] = rope(p_ref[:, 1152:1280]).astype(BF16)
    kw_ref[...] = rope(p_ref[:, 1280:1408]).astype(BF16)
    vc_ref[...] = p_ref[:, 1408:1536]
    ones = jnp.ones((NSA_VROWS - NSA_DH, PREP_TM), BF16)
    for v_ref, c0 in ((vst_ref, 1536), (vwt_ref, 1664)):
        vt = p_ref[:, c0:c0 + 128].T.astype(BF16)
        for g in range(NSA_KV):
            v_ref[g * NSA_VROWS:g * NSA_VROWS + NSA_DH, :] = vt[g * NSA_DH:(g + 1) * NSA_DH]
            v_ref[g * NSA_VROWS + NSA_DH:(g + 1) * NSA_VROWS, :] = ones
    gt_ref[...] = jax.nn.sigmoid(p_ref[:, 1792:1920]).T[0:3 * NSA_HEADS, :]


def _nsa_prep(proj_nsa, pos_f, inv128):
    nt = SEQ // PREP_TM
    p3 = proj_nsa.reshape(BATCH, SEQ, NSA_PROJ)
    nat = lambda w: pl.BlockSpec((None, PREP_TM, w), lambda b, i: (b, i, 0))
    tr = lambda r: pl.BlockSpec((None, r, PREP_TM), lambda b, i: (b, 0, i))
    sds = jax.ShapeDtypeStruct
    return pl.pallas_call(
        _prep_body,
        grid=(BATCH, nt),
        in_specs=[nat(NSA_PROJ), nat(1), pl.BlockSpec((1, 128), lambda b, i: (0, 0))],
        out_specs=(nat(NSA_WIDTH), nat(128), nat(128), nat(128), nat(128),
                   tr(NSA_KV * NSA_VROWS), tr(NSA_KV * NSA_VROWS), tr(3 * NSA_HEADS)),
        out_shape=(sds((BATCH, SEQ, NSA_WIDTH), BF16),
                   sds((BATCH, SEQ, 128), F32),
                   sds((BATCH, SEQ, 128), F32),
                   sds((BATCH, SEQ, 128), BF16),
                   sds((BATCH, SEQ, 128), BF16),
                   sds((BATCH, NSA_KV * NSA_VROWS, SEQ), BF16),
                   sds((BATCH, NSA_KV * NSA_VROWS, SEQ), BF16),
                   sds((BATCH, 3 * NSA_HEADS, SEQ), F32)),
        compiler_params=_cparams("parallel", "parallel"),
        name="nsa_prep",
    )(p3, pos_f, inv128)


def _cmp_body(tk_ref, tv_ref, pelo_ref, pehi_ref, kw1_ref, kw2_ref, vw1_ref, vw2_ref,
              kc_ref, vct_ref):
    half = CMP_LEN * NSA_DH // 2
    row = lax.broadcasted_iota(jnp.int32, (N_CMP_PAD, CMP_HIDDEN), 0)

    def mlp(t_ref, w1_ref, w2_ref):
        out = jnp.zeros((N_CMP_PAD, 128), F32)
        for g in range(NSA_KV):
            x = t_ref[g]
            y1 = _dot((x + pelo_ref[...]).astype(BF16), w1_ref[0:half, :])
            y2 = _dot((x + pehi_ref[...]).astype(BF16), w1_ref[half:2 * half, :])
            hid = jnp.where(row < N_CMP, y1 + pltpu.roll(y2, N_CMP_PAD - 1, 0), 0.0)
            out = out + _dot(_silu(hid).astype(BF16), w2_ref[g])
        return out

    kc_ref[...] = mlp(tk_ref, kw1_ref, kw2_ref).astype(BF16)
    vct_ref[...] = mlp(tv_ref, vw1_ref, vw2_ref).T.astype(BF16)


def _compress(tk, tv, pelo, pehi, kw1, kw2p, vw1, vw2p):
    seg = pl.BlockSpec((None, NSA_KV, N_CMP_PAD, CMP_STRIDE * NSA_DH), lambda b: (b, 0, 0, 0))
    full2 = lambda a: pl.BlockSpec(a.shape, lambda b: (0,) * a.ndim)
    return pl.pallas_call(
        _cmp_body,
        grid=(BATCH,),
        in_specs=[seg, seg, full2(pelo), full2(pehi), full2(kw1), full2(kw2p), full2(vw1), full2(vw2p)],
        out_specs=(pl.BlockSpec((None, N_CMP_PAD, 128), lambda b: (b, 0, 0)),
                   pl.BlockSpec((None, 128, N_CMP_PAD), lambda b: (b, 0, 0))),
        out_shape=(jax.ShapeDtypeStruct((BATCH, N_CMP_PAD, 128), BF16),
                   jax.ShapeDtypeStruct((BATCH, 128, N_CMP_PAD), BF16)),
        compiler_params=_cparams("parallel"),
        name="nsa_compress",
    )(tk, tv, pelo, pehi, kw1, kw2p, vw1, vw2p)


NSA_NL = NSA_REP * Q_BLOCK
NSA_GH = 8
NSA_PW = NSA_GH * Q_BLOCK
NSA_NP = NSA_REP // NSA_GH


def _nsa_body(q_ref, kc_ref, vct_ref, ks_ref, vst_ref, kw_ref, vwt_ref, gt_ref, ovt_ref, oh_ref,
              o_ref, qa_s, sc_s, rk_s, s0_s, s1_s):
    g = pl.program_id(1)
    qb = pl.program_id(2)
    q0 = pl.multiple_of(qb * Q_BLOCK, Q_BLOCK)
    is_g0 = g == 0

    qblk = q_ref[...].astype(F32)
    zero_slab = jnp.zeros((NSA_DH, Q_BLOCK), F32)
    for p in range(NSA_REP // 2):
        t = qblk[:, p * 128:(p + 1) * 128].T
        for hh in range(2):
            s = t[hh * NSA_DH:(hh + 1) * NSA_DH]
            r = 2 * p + hh
            qa_s[0:128, r * Q_BLOCK:(r + 1) * Q_BLOCK] = jnp.concatenate(
                [jnp.where(is_g0, s, zero_slab), jnp.where(is_g0, zero_slab, s)], axis=0).astype(BF16)
    qa_s[128 + N_SEL:256, :] = jnp.zeros((128 - N_SEL, NSA_NL), BF16)

    tq = q0 + lax.broadcasted_iota(jnp.int32, (1, Q_BLOCK), 1)

    def pair(p):
        return slice(p * NSA_PW, (p + 1) * NSA_PW)

    def mask_pair(s, valid):
        return jnp.concatenate([jnp.where(valid, s[:, r * Q_BLOCK:(r + 1) * Q_BLOCK], NEG)
                                for r in range(NSA_GH)], axis=1)

    n_i = lax.broadcasted_iota(jnp.int32, (N_CMP_PAD, Q_BLOCK), 0)
    valid_c = (n_i * CMP_STRIDE + (CMP_LEN - 1) <= tq) & (n_i < N_CMP)
    p_sum = jnp.zeros((N_CMP_PAD, Q_BLOCK), F32)
    o_c = []
    for p in range(NSA_NP):
        sc = mask_pair(_dot(kc_ref[...], qa_s[0:128, pair(p)]), valid_c)
        m_c = jnp.max(sc, axis=0, keepdims=True)
        e_c = jnp.exp2(sc - m_c)
        inv = jnp.where(m_c > 0.5 * NEG, 1.0 / jnp.maximum(jnp.sum(e_c, axis=0, keepdims=True), 1e-30), 0.0)
        p_c = e_c * inv
        o_c.append(_dot(vct_ref[...], p_c.astype(BF16)))
        for r in range(NSA_GH):
            p_sum = p_sum + p_c[:, r * Q_BLOCK:(r + 1) * Q_BLOCK]

    imp = jnp.dot(ovt_ref[...], p_sum, precision=lax.Precision.HIGHEST,
                  preferred_element_type=F32)
    j_i = lax.broadcasted_iota(jnp.int32, (N_SEL, Q_BLOCK), 0)
    cur = tq // SEL_LEN
    forced = (j_i == 0) | (j_i == cur) | (j_i == cur - 1)
    score = jnp.where(forced, jnp.inf, jnp.where(j_i > cur, -jnp.inf, imp))
    sc_s[...] = score
    rk_s[...] = jnp.zeros_like(rk_s)
    sub8 = lax.broadcasted_iota(jnp.int32, (8, Q_BLOCK), 0)
    for grp in range(N_SEL // 8):
        @pl.when(8 * grp <= 2 * qb + 1)
        def _():
            for v in range(N_SEL // 8):
                sv = sc_s[8 * v:8 * v + 8, :]
                part = jnp.zeros((8, Q_BLOCK), F32)
                for jp in range(8 * grp, 8 * grp + 8):
                    row = sc_s[jp:jp + 1, :]
                    if v > grp:
                        part = part + jnp.where(row >= sv, 1.0, 0.0)
                    elif v < grp:
                        part = part + jnp.where(row > sv, 1.0, 0.0)
                    else:
                        part = part + jnp.where(sub8 + 8 * v > jp, jnp.where(row >= sv, 1.0, 0.0),
                                                jnp.where(row > sv, 1.0, 0.0))
                rk_s[8 * v:8 * v + 8, :] += part
    bias = jnp.where((rk_s[...] < SEL_TOP) & (j_i < 2 * qb), 0.0, NEG).astype(BF16)
    for r in range(NSA_REP):
        qa_s[128:128 + N_SEL, r * Q_BLOCK:(r + 1) * Q_BLOCK] = bias

    d_i = lax.broadcasted_iota(jnp.int32, (Q_BLOCK, Q_BLOCK), 0)
    t_i = lax.broadcasted_iota(jnp.int32, (Q_BLOCK, Q_BLOCK), 1)
    causal = d_i <= t_i
    kd = ks_ref[pl.ds(q0, Q_BLOCK), :]
    s = mask_pair(_dot(kd, qa_s[0:128, :]).astype(BF16), causal)
    m16 = jnp.max(s, axis=0, keepdims=True)
    carry = (m16.astype(F32), _dot(vst_ref[:, pl.ds(q0, Q_BLOCK)], jnp.exp2(s - m16)))

    last_sub = SEQ // SEL_SUB - 1

    def scores_into(buf, c):
        k0 = pl.multiple_of(jnp.minimum(c, last_sub) * SEL_SUB, SEL_SUB)
        kaug = jnp.concatenate([ks_ref[pl.ds(k0, SEL_SUB), :], oh_ref[pl.ds(k0, SEL_SUB), :]], axis=1)
        buf[...] = _dot(kaug, qa_s[...]).astype(BF16)

    def softmax_from(buf, c, carry):
        m, acc = carry
        k0 = pl.multiple_of(c * SEL_SUB, SEL_SUB)
        s = buf[...]
        m_new = jnp.maximum(m, jnp.max(s, axis=0, keepdims=True).astype(F32))
        pr = jnp.exp2(s - m_new.astype(BF16))
        return m_new, acc * jnp.exp2(m - m_new) + _dot(vst_ref[:, pl.ds(k0, SEL_SUB)], pr)

    scores_into(s0_s, 0)

    def sel_step(i, carry):
        c = 2 * i
        scores_into(s1_s, c + 1)
        carry = softmax_from(s0_s, c, carry)
        scores_into(s0_s, c + 2)
        return softmax_from(s1_s, c + 1, carry)

    n_main = (qb * Q_BLOCK + 2 * SEL_SUB - 1) // (2 * SEL_SUB)
    _, acc_s = lax.fori_loop(0, n_main, sel_step, carry)
    o_s = [acc_s[0:NSA_DH] / jnp.maximum(acc_s[NSA_DH:NSA_DH + 1], 1e-30)]

    w0 = pl.multiple_of(jnp.maximum(q0 - WINDOW, 0), Q_BLOCK)
    kwin = kw_ref[pl.ds(w0, WIN_KEYS), :]
    vwin = vwt_ref[:, pl.ds(w0, WIN_KEYS)]
    dpos = tq - (w0 + lax.broadcasted_iota(jnp.int32, (WIN_KEYS, Q_BLOCK), 0))
    valid_w = (dpos >= 0) & (dpos < WINDOW)
    sw = mask_pair(_dot(kwin, qa_s[0:128, :]).astype(BF16), valid_w)
    acc_w = _dot(vwin, jnp.exp2(sw - jnp.max(sw, axis=0, keepdims=True)))
    o_w = [acc_w[0:NSA_DH] / jnp.maximum(acc_w[NSA_DH:NSA_DH + 1], 1e-30)]

    gall = gt_ref[...]
    gate = []
    for br in range(3):
        gb = gall[br * NSA_HEADS:(br + 1) * NSA_HEADS]
        gate.append(jnp.where(is_g0, gb[0:NSA_REP], gb[NSA_REP:NSA_HEADS]))
    for p2 in range(NSA_REP // 2):
        halves = []
        for hh in range(2):
            r = 2 * p2 + hh
            p = r // NSA_GH
            sl = slice((r % NSA_GH) * Q_BLOCK, (r % NSA_GH + 1) * Q_BLOCK)
            halves.append(gate[0][r:r + 1, :] * o_c[p][:, sl] + gate[1][r:r + 1, :] * o_s[p][:, sl]
                          + gate[2][r:r + 1, :] * o_w[p][:, sl])
        o_ref[:, p2 * 128:(p2 + 1) * 128] = jnp.concatenate(halves, axis=0).T.astype(BF16)


def _nsa_attn(q_r, kc, vct, ks, vst, kw, vwt, gt, ovt, onehot):
    per_b = lambda r, c: pl.BlockSpec((None, r, c), lambda b, g, i: (b, 0, 0))
    per_bg = lambda r, c: pl.BlockSpec((None, r, c), lambda b, g, i: (b, g, 0))
    const = lambda a: pl.BlockSpec(a.shape, lambda b, g, i: (0, 0))
    return pl.pallas_call(
        _nsa_body,
        grid=(BATCH, NSA_KV, N_QB),
        in_specs=[pl.BlockSpec((None, Q_BLOCK, NSA_REP * NSA_DH), lambda b, g, i: (b, i, g)),
                  per_b(N_CMP_PAD, 128), per_bg(NSA_DH, N_CMP_PAD),
                  per_b(SEQ, 128), per_bg(NSA_VROWS, SEQ),
                  per_b(SEQ, 128), per_bg(NSA_VROWS, SEQ),
                  pl.BlockSpec((None, 3 * NSA_HEADS, Q_BLOCK), lambda b, g, i: (b, 0, i)),
                  const(ovt), const(onehot)],
        out_specs=pl.BlockSpec((None, Q_BLOCK, NSA_REP * NSA_DH), lambda b, g, i: (b, i, g)),
        out_shape=jax.ShapeDtypeStruct((BATCH, SEQ, NSA_WIDTH), BF16),
        scratch_shapes=[pltpu.VMEM((256, NSA_NL), BF16),
                        pltpu.VMEM((N_SEL, Q_BLOCK), F32), pltpu.VMEM((N_SEL, Q_BLOCK), F32),
                        pltpu.VMEM((SEL_SUB, NSA_NL), BF16), pltpu.VMEM((SEL_SUB, NSA_NL), BF16)],
        compiler_params=_cparams("parallel", "parallel", "arbitrary"),
        name="nsa_attn",
    )(q_r, kc, vct, ks, vst, kw, vwt, gt, ovt, onehot)


OUT_TM = 256


def _outproj_body(x_ref, oh_ref, on_ref, w_ref, nw_ref, x2_ref, hx_ref):
    y = (x_ref[...] + _dot(oh_ref[...], w_ref[0:HG_WIDTH, :])
         + _dot(on_ref[...], w_ref[HG_WIDTH:HG_WIDTH + NSA_WIDTH, :]))
    x2_ref[...] = y
    hx_ref[...] = _rms(y, nw_ref[...]).astype(BF16)


def _outproj(x1, o_hg, o_nsa, w_out, nw):
    row = lambda w: pl.BlockSpec((OUT_TM, w), lambda i: (i, 0))
    return pl.pallas_call(
        _outproj_body,
        grid=(TOKENS // OUT_TM,),
        in_specs=[row(D_MODEL), row(HG_WIDTH), row(NSA_WIDTH),
                  pl.BlockSpec((D_MODEL, D_MODEL), lambda i: (0, 0)),
                  pl.BlockSpec((1, D_MODEL), lambda i: (0, 0))],
        out_specs=(row(D_MODEL), row(D_MODEL)),
        out_shape=(jax.ShapeDtypeStruct((TOKENS, D_MODEL), F32),
                   jax.ShapeDtypeStruct((TOKENS, D_MODEL), BF16)),
        compiler_params=_cparams("parallel"),
        name="out_proj",
    )(x1, o_hg, o_nsa, w_out, nw)


def _memkv_body(m_ref, nw_ref, wk_ref, wv_ref, k_ref, v_ref):
    hm = _rms(m_ref[...], nw_ref[...]).astype(BF16)
    k_ref[...] = _dot(hm, wk_ref[...]).astype(BF16)
    v_ref[...] = _dot(hm, wv_ref[...]).astype(BF16)


def _memkv(mem, nw, wk, wv):
    width = X_HEADS * X_DH
    wspec = pl.BlockSpec((D_MODEL, width), lambda b: (0, 0))
    ospec = pl.BlockSpec((None, MEM_LEN, width), lambda b: (b, 0, 0))
    osh = jax.ShapeDtypeStruct((BATCH, MEM_LEN, width), BF16)
    return pl.pallas_call(
        _memkv_body,
        grid=(BATCH,),
        in_specs=[pl.BlockSpec((None, MEM_LEN, D_MODEL), lambda b: (b, 0, 0)),
                  pl.BlockSpec((1, D_MODEL), lambda b: (0, 0)), wspec, wspec],
        out_specs=(ospec, ospec), out_shape=(osh, osh),
        compiler_params=_cparams("parallel"),
        name="xattn_memkv",
    )(mem, nw, wk, wv)


XA_TM = 256


def _xattn_body(x_ref, hx_ref, wq_ref, k_ref, v_ref, wo_ref, o_ref):
    q = (_dot(hx_ref[...], wq_ref[...]) * (X_DH ** -0.5)).astype(BF16)
    heads = []
    for h in range(X_HEADS):
        sl = slice(h * X_DH, (h + 1) * X_DH)
        s = _dot_nt(q[:, sl], k_ref[:, sl])
        e = jnp.exp(s - jnp.max(s, axis=-1, keepdims=True))
        p = e / jnp.sum(e, axis=-1, keepdims=True)
        heads.append(_dot(p.astype(BF16), v_ref[:, sl]))
    o = jnp.concatenate(heads, axis=1).astype(BF16)
    o_ref[...] = x_ref[...] + _dot(o, wo_ref[...])


def _xattn(x2, hx, wq, k, v, wo):
    width = X_HEADS * X_DH
    tiles_per_b = SEQ // XA_TM
    kv = pl.BlockSpec((None, MEM_LEN, width), lambda i: (i // tiles_per_b, 0, 0))
    return pl.pallas_call(
        _xattn_body,
        grid=(TOKENS // XA_TM,),
        in_specs=[pl.BlockSpec((XA_TM, D_MODEL), lambda i: (i, 0)),
                  pl.BlockSpec((XA_TM, D_MODEL), lambda i: (i, 0)),
                  pl.BlockSpec((D_MODEL, width), lambda i: (0, 0)), kv, kv,
                  pl.BlockSpec((width, D_MODEL), lambda i: (0, 0))],
        out_specs=pl.BlockSpec((XA_TM, D_MODEL), lambda i: (i, 0)),
        out_shape=jax.ShapeDtypeStruct((TOKENS, D_MODEL), F32),
        compiler_params=_cparams("parallel"),
        name="xattn",
    )(x2, hx, wq, k, v, wo)


def _overlap_t():
    c0 = np.arange(N_CMP)[:, None] * CMP_STRIDE
    s0 = np.arange(N_SEL)[None, :] * SEL_LEN
    ov = np.clip(np.minimum(c0 + CMP_LEN, s0 + SEL_LEN) - np.maximum(c0, s0), 0, None) / CMP_LEN
    out = np.zeros((N_SEL, N_CMP_PAD), np.float32)
    out[:, :N_CMP] = ov.T
    return out


def _block_onehot():
    out = np.zeros((SEQ, 128), np.float32)
    out[np.arange(SEQ), np.arange(SEQ) // SEL_LEN] = 1.0
    return out


def kernel(x, mem, positions, ffn1_norm, ffn1_w_gate, ffn1_w_up, ffn1_w_down, mix_norm, w_in, hgrn_lb_logits, hgrn_out_norm, nsa_cmp_pe, nsa_cmp_k_w1, nsa_cmp_k_w2, nsa_cmp_v_w1, nsa_cmp_v_w2, w_out, xattn_norm, mem_norm, xattn_wq, xattn_wk, xattn_wv, xattn_wo, ffn2_norm, ffn2_w_gate, ffn2_w_up, ffn2_w_down, final_norm):
    bf = lambda a: a.astype(BF16)
    vec = lambda a: a.reshape(1, -1).astype(F32)
    x2d = x.reshape(TOKENS, D_MODEL)

    x1, h_mix = _ffn(x2d, vec(ffn1_norm[0]), bf(ffn1_w_gate[0]), bf(ffn1_w_up[0]), bf(ffn1_w_down[0]),
                     vec(mix_norm[0]), final=False)

    w = w_in[0]
    gate_cols = 5888 + (np.arange(NSA_HEADS)[None, :] * 3 + np.arange(3)[:, None]).reshape(-1)
    w_nsa = jnp.concatenate(
        [w[:, 4096:5120], w[:, 5120:5248], w[:, 5376:5504], w[:, 5632:5760],
         w[:, 5248:5376], w[:, 5504:5632], w[:, 5760:5888], w[:, gate_cols],
         jnp.zeros((D_MODEL, NSA_PROJ - 1792 - 3 * NSA_HEADS), w.dtype)], axis=1)
    proj_hg = _proj(h_mix, bf(w[:, :4 * HG_WIDTH]), PROJ_TN, "proj_hgrn")
    proj_nsa = _proj(h_mix, bf(w_nsa), NSA_PROJ // 3, "proj_nsa")

    o_hg = _hgrn(proj_hg, hgrn_lb_logits.astype(F32), vec(hgrn_out_norm[0]))

    inv = ROPE_THETA ** (-jnp.arange(NSA_DH // 2, dtype=F32) / (NSA_DH // 2))
    inv128 = jnp.tile(inv, 128 // (NSA_DH // 2)).reshape(1, 128)
    pos_f = positions.astype(F32).reshape(BATCH, SEQ, 1)
    q_r, kc_tok, vc_tok, ks, kw, vst, vwt, gt = _nsa_prep(proj_nsa, pos_f, inv128)

    def segments(t):
        t = t.reshape(BATCH, SEQ // CMP_STRIDE, CMP_STRIDE, NSA_KV, NSA_DH)
        return t.transpose(0, 3, 1, 2, 4).reshape(BATCH, NSA_KV, SEQ // CMP_STRIDE, CMP_STRIDE * NSA_DH)

    pe = nsa_cmp_pe[0].astype(F32)
    pelo = pe[:CMP_STRIDE].reshape(1, -1)
    pehi = pe[CMP_STRIDE:].reshape(1, -1)

    def pad_w2(w2):
        z = jnp.zeros_like(w2)
        return bf(jnp.stack([jnp.concatenate([w2, z], axis=1), jnp.concatenate([z, w2], axis=1)]))

    kc, vct = _compress(segments(kc_tok), segments(vc_tok), pelo, pehi,
                        bf(nsa_cmp_k_w1[0]), pad_w2(nsa_cmp_k_w2[0]),
                        bf(nsa_cmp_v_w1[0]), pad_w2(nsa_cmp_v_w2[0]))
    o_nsa = _nsa_attn(q_r, kc, vct, ks, vst, kw, vwt, gt, jnp.asarray(_overlap_t()),
                      jnp.asarray(_block_onehot(), dtype=BF16))

    x2, hx = _outproj(x1, o_hg.reshape(TOKENS, HG_WIDTH), o_nsa.reshape(TOKENS, NSA_WIDTH),
                      bf(w_out[0]), vec(xattn_norm[0]))

    km, vm = _memkv(mem, vec(mem_norm[0]), bf(xattn_wk[0]), bf(xattn_wv[0]))
    x3 = _xattn(x2, hx, bf(xattn_wq[0]), km, vm, bf(xattn_wo[0]))

    out = _ffn(x3, vec(ffn2_norm[0]), bf(ffn2_w_gate[0]), bf(ffn2_w_up[0]), bf(ffn2_w_down[0]),
               vec(final_norm), final=True)
    return out.reshape(BATCH, SEQ, D_MODEL)
```

```python
import functools

import numpy as np
import jax
import jax.numpy as jnp
from jax import lax
from jax.experimental import pallas as pl
from jax.experimental.pallas import tpu as pltpu

F32 = jnp.float32
BF16 = jnp.bfloat16

D_MODEL = 2048
BATCH = 2
SEQ = 4096
TOKENS = BATCH * SEQ
RMS_EPS = 1e-6
ROPE_THETA = 10000.0
HG_WIDTH = 1024
HG_HEADS = 8
HG_D = 128
HG_CHUNK = 128
HG_LEVELS = (64, 32, 16, 8, 4, 2, 1)
NSA_WIDTH = 1024
NSA_DH = 64
NSA_HEADS = 16
NSA_KV = 2
NSA_REP = 8
NSA_VROWS = NSA_DH + 16
CMP_LEN = 32
CMP_STRIDE = 16
CMP_HIDDEN = 256
N_CMP = (SEQ - CMP_LEN) // CMP_STRIDE + 1
N_CMP_PAD = 256
SEL_LEN = 64
N_SEL = SEQ // SEL_LEN
SEL_TOP = 16
WINDOW = 512
Q_BLOCK = 128
N_QB = SEQ // Q_BLOCK
SEL_SUB = 256
WIN_KEYS = WINDOW + Q_BLOCK
MEM_LEN = 256
X_HEADS = 4
X_DH = 128
D_FF = 5632
IN_SIZES = (1024, 1024, 1024, 1024, 1024, 128, 128, 128, 128, 128, 128, 48)
D_IN = sum(IN_SIZES)
PROJ_NSA_OFF = sum(IN_SIZES[:4])
NSA_PROJ = 2048
PROJ_W = PROJ_NSA_OFF + NSA_PROJ
NEG = -1e30
LOG2E = 1.4426950408889634

V7X_VMEM_BYTES = 64 * 1024 * 1024
VMEM_LIMIT = V7X_VMEM_BYTES - 8 * 1024 * 1024


def _cparams(*sem, flags=None):
    return pltpu.CompilerParams(dimension_semantics=sem, vmem_limit_bytes=VMEM_LIMIT, flags=flags)


def _rms(x, w):
    return x * lax.rsqrt(jnp.mean(x * x, axis=-1, keepdims=True) + RMS_EPS) * w


def _silu(x):
    return x * jax.nn.sigmoid(x)


def _dot(a, b):
    return jnp.dot(a, b, preferred_element_type=F32)


def _dot_f32_by_01(sel, x):
    hi = x.astype(BF16)
    r1 = x - hi.astype(F32)
    mid = r1.astype(BF16)
    lo = (r1 - mid.astype(F32)).astype(BF16)
    n = x.shape[1]
    y = _dot(sel, jnp.concatenate([hi, mid, lo], axis=1))
    return y[:, 0:n] + y[:, n:2 * n] + y[:, 2 * n:3 * n]


def _dot_nt(a, b):
    return lax.dot_general(a, b, (((1,), (1,)), ((), ())), preferred_element_type=F32)


FFN_TM = 512
FFN_TF = 512


def _ffn_body(x_ref, nw_ref, wg_ref, wu_ref, wd_ref, nw2_ref, *rest, final):
    if final:
        o_ref, h_scr = rest
    else:
        o_ref, hn_ref, h_scr = rest
    j = pl.program_id(1)

    @pl.when(j == 0)
    def _():
        h_scr[...] = _rms(x_ref[...], nw_ref[...]).astype(BF16)
        o_ref[...] = jnp.zeros_like(o_ref)

    h = h_scr[...]
    g = _dot(h, wg_ref[...])
    u = _dot(h, wu_ref[...])
    a = (_silu(g) * u).astype(BF16)
    o_ref[...] += _dot(a, wd_ref[...])

    @pl.when(j == pl.num_programs(1) - 1)
    def _():
        y = x_ref[...] + 0.5 * o_ref[...]
        if final:
            o_ref[...] = _rms(y, nw2_ref[...])
        else:
            o_ref[...] = y
            hn_ref[...] = _rms(y, nw2_ref[...]).astype(BF16)


def _ffn(x, nw, wg, wu, wd, nw2, final):
    grid = (TOKENS // FFN_TM, D_FF // FFN_TF)
    row = pl.BlockSpec((FFN_TM, D_MODEL), lambda i, j: (i, 0))
    vec = pl.BlockSpec((1, D_MODEL), lambda i, j: (0, 0))
    in_specs = [row, vec,
                pl.BlockSpec((D_MODEL, FFN_TF), lambda i, j: (0, j)),
                pl.BlockSpec((D_MODEL, FFN_TF), lambda i, j: (0, j)),
                pl.BlockSpec((FFN_TF, D_MODEL), lambda i, j: (j, 0)),
                vec]
    if final:
        out_shape = jax.ShapeDtypeStruct((TOKENS, D_MODEL), F32)
        out_specs = row
    else:
        out_shape = (jax.ShapeDtypeStruct((TOKENS, D_MODEL), F32),
                     jax.ShapeDtypeStruct((TOKENS, D_MODEL), BF16))
        out_specs = (row, row)
    return pl.pallas_call(
        functools.partial(_ffn_body, final=final),
        grid=grid, in_specs=in_specs, out_specs=out_specs, out_shape=out_shape,
        scratch_shapes=[pltpu.VMEM((FFN_TM, D_MODEL), BF16)],
        compiler_params=_cparams("parallel", "arbitrary"),
        name="ffn_final" if final else "ffn",
    )(x, nw, wg, wu, wd, nw2)


PROJ_TM = 1024
PROJ_TN = 512


def _mm_body(a_ref, w_ref, o_ref):
    o_ref[...] = _dot(a_ref[...], w_ref[...])


def _proj(a, w, tn, name):
    m, k = a.shape
    n = w.shape[1]
    return pl.pallas_call(
        _mm_body,
        grid=(m // PROJ_TM, n // tn),
        in_specs=[pl.BlockSpec((PROJ_TM, k), lambda i, j: (i, 0)),
                  pl.BlockSpec((k, tn), lambda i, j: (0, j))],
        out_specs=pl.BlockSpec((PROJ_TM, tn), lambda i, j: (i, j)),
        out_shape=jax.ShapeDtypeStruct((m, n), F32),
        compiler_params=_cparams("parallel", "arbitrary"),
        name=name,
    )(a, w)


HG_ROWS = 1024
HG_CUM = 256


def _hgrn_body(q_ref, f_ref, i_ref, g_ref, lbl_ref, nw_ref, o_ref, st_ref, k_s, b_s):
    c = pl.program_id(2)

    @pl.when(c == 0)
    def _():
        st_ref[...] = jnp.zeros_like(st_ref)

    l0 = lbl_ref[0:1, :]
    l1 = lbl_ref[1:2, :]
    lmax = jnp.maximum(l0, l1)
    e0 = jnp.exp(l0 - lmax)
    lb = e0 / (e0 + jnp.exp(l1 - lmax))

    C = HG_CHUNK
    f = lb + (1.0 - lb) * jax.nn.sigmoid(f_ref[...])
    k_s[...] = 1.0 - f
    r_i = lax.broadcasted_iota(jnp.int32, (HG_CUM, HG_CUM), 0)
    c_i = lax.broadcasted_iota(jnp.int32, (HG_CUM, HG_CUM), 1)
    tri = jnp.where((r_i >= c_i) & (r_i // C == c_i // C), 1.0, 0.0).astype(BF16)
    logf = jnp.log(f)
    for r0 in range(0, HG_ROWS, HG_CUM):
        b_s[r0:r0 + HG_CUM, :] = _dot_f32_by_01(tri, logf[r0:r0 + HG_CUM])

    t_i = lax.broadcasted_iota(jnp.int32, (C, C), 0)
    s_i = lax.broadcasted_iota(jnp.int32, (C, C), 1)
    level_mask = [(t_i // (2 * w) == s_i // (2 * w)) & (t_i % (2 * w) >= w) & (s_i % (2 * w) < w)
                  for w in HG_LEVELS]
    sub_r = lax.broadcasted_iota(jnp.int32, (8, HG_D), 0)
    odd_row = (lax.broadcasted_iota(jnp.int32, (C, HG_D), 0) & 1) == 1

    def bref_rows(w, r0, b):
        row = lambda r, n: jnp.broadcast_to(b_s[r0 + r:r0 + r + 1, :], (n, HG_D))
        if w >= 8:
            return jnp.concatenate([row(p0 + w - 1, 2 * w) for p0 in range(0, C, 2 * w)], axis=0)
        if w == 4:
            return jnp.concatenate([row(p0 + 3, 8) for p0 in range(0, C, 8)], axis=0)
        if w == 2:
            return jnp.concatenate([jnp.where(sub_r < 4, row(p0 + 1, 8), row(p0 + 5, 8))
                                    for p0 in range(0, C, 8)], axis=0)
        return jnp.where(odd_row, pltpu.roll(b, 1, 0), b)

    chunks = [ci * C for ci in range(HG_ROWS // C)]
    rows = lambda ref, r0: ref[r0:r0 + C, :]
    att = [jnp.zeros((C, C), F32) for _ in chunks]
    q16 = [rows(q_ref, r0).astype(BF16) for r0 in chunks]
    k16 = [rows(k_s, r0).astype(BF16) for r0 in chunks]
    for w, mask in zip(HG_LEVELS, level_mask):
        for n, r0 in enumerate(chunks):
            b = rows(b_s, r0)
            e = jnp.exp2(jnp.abs(b - bref_rows(w, r0, b)) * (-LOG2E)).astype(BF16)
            att[n] = jnp.where(mask, _dot_nt(q16[n] * e, k16[n] * e), att[n])
    o_intra = []
    for n, r0 in enumerate(chunks):
        q, k, v = rows(q_ref, r0), rows(k_s, r0), rows(i_ref, r0)
        o_intra.append(_dot(att[n].astype(BF16), v.astype(BF16))
                       + jnp.sum(q * k, axis=-1, keepdims=True) * v)
    upd = []
    for r0 in chunks:
        bl = b_s[r0 + C - 1:r0 + C, :]
        kd = rows(k_s, r0) * jnp.exp(bl - rows(b_s, r0))
        upd.append((jnp.exp(bl), _dot(rows(i_ref, r0).T.astype(BF16), kd.astype(BF16))))
    st_t = st_ref[...]
    for n, r0 in enumerate(chunks):
        qe = (rows(q_ref, r0) * jnp.exp(rows(b_s, r0))).astype(BF16)
        o = o_intra[n] + _dot_nt(qe, st_t.astype(BF16))
        st_t = st_t * upd[n][0] + upd[n][1]
        o = o * lax.rsqrt(jnp.mean(o * o, axis=-1, keepdims=True) + RMS_EPS)
        o_ref[r0:r0 + C, :] = (o * nw_ref[...] * _silu(rows(g_ref, r0))).astype(BF16)
    st_ref[...] = st_t


def _hgrn(proj, lb_logits, norm_w):
    p3 = proj.reshape(BATCH, SEQ, PROJ_W)

    def col(off):
        return pl.BlockSpec((None, HG_ROWS, HG_D), lambda b, h, c: (b, c, off + h))

    return pl.pallas_call(
        _hgrn_body,
        grid=(BATCH, HG_HEADS, SEQ // HG_ROWS),
        in_specs=[col(0), col(HG_HEADS), col(2 * HG_HEADS), col(3 * HG_HEADS),
                  pl.BlockSpec((2, HG_D), lambda b, h, c: (0, h)),
                  pl.BlockSpec((1, HG_D), lambda b, h, c: (0, h))],
        out_specs=pl.BlockSpec((None, HG_ROWS, HG_D), lambda b, h, c: (b, c, h)),
        out_shape=jax.ShapeDtypeStruct((BATCH, SEQ, HG_WIDTH), BF16),
        scratch_shapes=[pltpu.VMEM((HG_D, HG_D), F32),
                        pltpu.VMEM((HG_ROWS, HG_D), F32),
                        pltpu.VMEM((HG_ROWS, HG_D), F32)],
        compiler_params=_cparams("parallel", "parallel", "arbitrary"),
        name="hgrn2",
    )(p3, p3, p3, p3, lb_logits, norm_w)


PREP_TM = 256


def _prep_body(p_ref, pos_ref, inv_ref, q_ref, kc_ref, vc_ref, ks_ref, kw_ref,
               vst_ref, vwt_ref, gt_ref):
    ang = pos_ref[...] * inv_ref[...]
    cos = jnp.cos(ang)
    sin = jnp.sin(ang)
    lane = lax.broadcasted_iota(jnp.int32, (PREP_TM, 128), 1)
    lo = (lane & (NSA_DH // 2)) == 0
    sin_signed = jnp.where(lo, -sin, sin)

    def rope(x):
        rot = jnp.where(lo, pltpu.roll(x, 128 - NSA_DH // 2, 1), pltpu.roll(x, NSA_DH // 2, 1))
        return x * cos + rot * sin_signed

    scale = NSA_DH ** -0.5 * LOG2E
    for cblk in range(NSA_WIDTH // 128):
        sl = slice(cblk * 128, (cblk + 1) * 128)
        q_ref[:, sl] = (rope(p_ref[:, sl]) * scale).astype(BF16)
    kc_ref[...] = rope(p_ref[:, 1024:1152])
    vc_ref[...] = p_ref[:, 1152:1280]
    ks_ref[...] = rope(p_ref[:, 1280:1408]).astype(BF16)
    kw_ref[...] = rope(p_ref[:, 1536:1664]).astype(BF16)
    ones = jnp.ones((NSA_VROWS - NSA_DH, PREP_TM), BF16)
    for v_ref, c0 in ((vst_ref, 1408), (vwt_ref, 1664)):
        vt = p_ref[:, c0:c0 + 128].T.astype(BF16)
        for g in range(NSA_KV):
            v_ref[g * NSA_VROWS:g * NSA_VROWS + NSA_DH, :] = vt[g * NSA_DH:(g + 1) * NSA_DH]
            v_ref[g * NSA_VROWS + NSA_DH:(g + 1) * NSA_VROWS, :] = ones
    gt_ref[...] = jax.nn.sigmoid(p_ref[:, 1792:1920]).T[0:3 * NSA_HEADS, :]


def _nsa_prep(proj, pos_f, inv128):
    nt = SEQ // PREP_TM
    p3 = proj.reshape(BATCH, SEQ, PROJ_W)
    nat = lambda w: pl.BlockSpec((None, PREP_TM, w), lambda b, i: (b, i, 0))
    tr = lambda r: pl.BlockSpec((None, r, PREP_TM), lambda b, i: (b, 0, i))
    sds = jax.ShapeDtypeStruct
    return pl.pallas_call(
        _prep_body,
        grid=(BATCH, nt),
        in_specs=[pl.BlockSpec((None, PREP_TM, NSA_PROJ), lambda b, i: (b, i, PROJ_NSA_OFF // NSA_PROJ)),
                  nat(1), pl.BlockSpec((1, 128), lambda b, i: (0, 0))],
        out_specs=(nat(NSA_WIDTH), nat(128), nat(128), nat(128), nat(128),
                   tr(NSA_KV * NSA_VROWS), tr(NSA_KV * NSA_VROWS), tr(3 * NSA_HEADS)),
        out_shape=(sds((BATCH, SEQ, NSA_WIDTH), BF16),
                   sds((BATCH, SEQ, 128), F32),
                   sds((BATCH, SEQ, 128), F32),
                   sds((BATCH, SEQ, 128), BF16),
                   sds((BATCH, SEQ, 128), BF16),
                   sds((BATCH, NSA_KV * NSA_VROWS, SEQ), BF16),
                   sds((BATCH, NSA_KV * NSA_VROWS, SEQ), BF16),
                   sds((BATCH, 3 * NSA_HEADS, SEQ), F32)),
        compiler_params=_cparams("parallel", "parallel"),
        name="nsa_prep",
    )(p3, pos_f, inv128)


def _cmp_body(tk_ref, tv_ref, pe_ref, kw1_ref, kw2_ref, vw1_ref, vw2_ref, kc_ref, vct_ref, y1_s, y2_s):
    row = lax.broadcasted_iota(jnp.int32, (N_CMP_PAD, NSA_KV * CMP_HIDDEN), 0)

    def mlp(t_ref, w1_ref, w2_ref):
        y1_s[...] = jnp.zeros_like(y1_s)
        y2_s[...] = jnp.zeros_like(y2_s)
        for l in range(CMP_STRIDE):
            x = t_ref[pl.ds(l, N_CMP_PAD, stride=CMP_STRIDE), :]
            y1_s[...] += _dot((x + pe_ref[l:l + 1, :]).astype(BF16), w1_ref[l])
            y2_s[...] += _dot((x + pe_ref[CMP_STRIDE + l:CMP_STRIDE + l + 1, :]).astype(BF16),
                              w1_ref[CMP_STRIDE + l])
        hid = jnp.where(row < N_CMP, y1_s[...] + pltpu.roll(y2_s[...], N_CMP_PAD - 1, 0), 0.0)
        return _dot(_silu(hid).astype(BF16), w2_ref[...])

    kc_ref[...] = mlp(tk_ref, kw1_ref, kw2_ref).astype(BF16)
    y1_s[:, 0:128] = mlp(tv_ref, vw1_ref, vw2_ref)
    vct_ref[...] = y1_s[:, 0:128].T.astype(BF16)


def _compress(tk, tv, pe2, kw1, kw2, vw1, vw2):
    seg = pl.BlockSpec((None, SEQ, 128), lambda b: (b, 0, 0))
    full2 = lambda a: pl.BlockSpec(a.shape, lambda b: (0,) * a.ndim)
    return pl.pallas_call(
        _cmp_body,
        grid=(BATCH,),
        in_specs=[seg, seg, full2(pe2), full2(kw1), full2(kw2), full2(vw1), full2(vw2)],
        out_specs=(pl.BlockSpec((None, N_CMP_PAD, 128), lambda b: (b, 0, 0)),
                   pl.BlockSpec((None, 128, N_CMP_PAD), lambda b: (b, 0, 0))),
        out_shape=(jax.ShapeDtypeStruct((BATCH, N_CMP_PAD, 128), BF16),
                   jax.ShapeDtypeStruct((BATCH, 128, N_CMP_PAD), BF16)),
        scratch_shapes=[pltpu.VMEM((N_CMP_PAD, NSA_KV * CMP_HIDDEN), F32)] * 2,
        compiler_params=_cparams("parallel"),
        name="nsa_compress",
    )(tk, tv, pe2, kw1, kw2, vw1, vw2)


NSA_NL = NSA_REP * Q_BLOCK
NSA_GH = 8
NSA_PW = NSA_GH * Q_BLOCK
NSA_NP = NSA_REP // NSA_GH


def _nsa_body(q_ref, kc_ref, vct_ref, ks_ref, vst_ref, kw_ref, vwt_ref, gt_ref, ovt_ref, oh_ref,
              o_ref, qa_s, sc_s, rk_s, s0_s, s1_s):
    g = pl.program_id(1)
    qb = pl.program_id(2)
    q0 = pl.multiple_of(qb * Q_BLOCK, Q_BLOCK)
    is_g0 = g == 0

    qblk = q_ref[...].astype(F32)
    zero_slab = jnp.zeros((NSA_DH, Q_BLOCK), F32)
    for p in range(NSA_REP // 2):
        t = qblk[:, p * 128:(p + 1) * 128].T
        for hh in range(2):
            s = t[hh * NSA_DH:(hh + 1) * NSA_DH]
            r = 2 * p + hh
            qa_s[0:128, r * Q_BLOCK:(r + 1) * Q_BLOCK] = jnp.concatenate(
                [jnp.where(is_g0, s, zero_slab), jnp.where(is_g0, zero_slab, s)], axis=0).astype(BF16)
    qa_s[128 + N_SEL:256, :] = jnp.zeros((128 - N_SEL, NSA_NL), BF16)

    tq = q0 + lax.broadcasted_iota(jnp.int32, (1, Q_BLOCK), 1)

    def pair(p):
        return slice(p * NSA_PW, (p + 1) * NSA_PW)

    def mask_pair(s, valid):
        return jnp.concatenate([jnp.where(valid, s[:, r * Q_BLOCK:(r + 1) * Q_BLOCK], NEG)
                                for r in range(NSA_GH)], axis=1)

    n_i = lax.broadcasted_iota(jnp.int32, (N_CMP_PAD, Q_BLOCK), 0)
    valid_c = (n_i * CMP_STRIDE + (CMP_LEN - 1) <= tq) & (n_i < N_CMP)
    p_sum = jnp.zeros((N_CMP_PAD, Q_BLOCK), F32)
    o_c = []
    for p in range(NSA_NP):
        sc = mask_pair(_dot(kc_ref[...], qa_s[0:128, pair(p)]), valid_c)
        m_c = jnp.max(sc, axis=0, keepdims=True)
        e_c = jnp.exp2(sc - m_c)
        inv = jnp.where(m_c > 0.5 * NEG, 1.0 / jnp.maximum(jnp.sum(e_c, axis=0, keepdims=True), 1e-30), 0.0)
        p_c = e_c * inv
        o_c.append(_dot(vct_ref[...], p_c.astype(BF16)))
        for r in range(NSA_GH):
            p_sum = p_sum + p_c[:, r * Q_BLOCK:(r + 1) * Q_BLOCK]

    imp = jnp.dot(ovt_ref[...], p_sum, precision=lax.Precision.HIGHEST,
                  preferred_element_type=F32)
    j_i = lax.broadcasted_iota(jnp.int32, (N_SEL, Q_BLOCK), 0)
    cur = tq // SEL_LEN
    forced = (j_i == 0) | (j_i == cur) | (j_i == cur - 1)
    score = jnp.where(forced, jnp.inf, jnp.where(j_i > cur, -jnp.inf, imp))
    sc_s[...] = score
    rk_s[...] = jnp.zeros_like(rk_s)
    sub8 = lax.broadcasted_iota(jnp.int32, (8, Q_BLOCK), 0)
    for grp in range(N_SEL // 8):
        @pl.when(8 * grp <= 2 * qb + 1)
        def _():
            for v in range(N_SEL // 8):
                sv = sc_s[8 * v:8 * v + 8, :]
                part = jnp.zeros((8, Q_BLOCK), F32)
                for jp in range(8 * grp, 8 * grp + 8):
                    row = sc_s[jp:jp + 1, :]
                    if v > grp:
                        part = part + jnp.where(row >= sv, 1.0, 0.0)
                    elif v < grp:
                        part = part + jnp.where(row > sv, 1.0, 0.0)
                    else:
                        part = part + jnp.where(sub8 + 8 * v > jp, jnp.where(row >= sv, 1.0, 0.0),
                                                jnp.where(row > sv, 1.0, 0.0))
                rk_s[8 * v:8 * v + 8, :] += part
    bias = jnp.where((rk_s[...] < SEL_TOP) & (j_i < 2 * qb), 0.0, NEG).astype(BF16)
    for r in range(NSA_REP):
        qa_s[128:128 + N_SEL, r * Q_BLOCK:(r + 1) * Q_BLOCK] = bias

    d_i = lax.broadcasted_iota(jnp.int32, (Q_BLOCK, Q_BLOCK), 0)
    t_i = lax.broadcasted_iota(jnp.int32, (Q_BLOCK, Q_BLOCK), 1)
    causal = d_i <= t_i
    kd = ks_ref[pl.ds(q0, Q_BLOCK), :]
    s = mask_pair(_dot(kd, qa_s[0:128, :]).astype(BF16), causal)
    m16 = jnp.max(s, axis=0, keepdims=True)
    carry = (m16.astype(F32), _dot(vst_ref[:, pl.ds(q0, Q_BLOCK)], jnp.exp2(s - m16)))

    last_sub = SEQ // SEL_SUB - 1

    def scores_into(buf, c):
        k0 = pl.multiple_of(jnp.minimum(c, last_sub) * SEL_SUB, SEL_SUB)
        kaug = jnp.concatenate([ks_ref[pl.ds(k0, SEL_SUB), :], oh_ref[pl.ds(k0, SEL_SUB), :]], axis=1)
        buf[...] = _dot(kaug, qa_s[...]).astype(BF16)

    def softmax_from(buf, c, carry):
        m, acc = carry
        k0 = pl.multiple_of(c * SEL_SUB, SEL_SUB)
        s = buf[...]
        m_new = jnp.maximum(m, jnp.max(s, axis=0, keepdims=True).astype(F32))
        pr = jnp.exp2(s - m_new.astype(BF16))
        return m_new, acc * jnp.exp2(m - m_new) + _dot(vst_ref[:, pl.ds(k0, SEL_SUB)], pr)

    scores_into(s0_s, 0)

    def sel_step(i, carry):
        c = 2 * i
        scores_into(s1_s, c + 1)
        carry = softmax_from(s0_s, c, carry)
        scores_into(s0_s, c + 2)
        return softmax_from(s1_s, c + 1, carry)

    n_main = (qb * Q_BLOCK + 2 * SEL_SUB - 1) // (2 * SEL_SUB)
    _, acc_s = lax.fori_loop(0, n_main, sel_step, carry)
    o_s = [acc_s[0:NSA_DH] / jnp.maximum(acc_s[NSA_DH:NSA_DH + 1], 1e-30)]

    w0 = pl.multiple_of(jnp.maximum(q0 - WINDOW, 0), Q_BLOCK)
    kwin = kw_ref[pl.ds(w0, WIN_KEYS), :]
    vwin = vwt_ref[:, pl.ds(w0, WIN_KEYS)]
    dpos = tq - (w0 + lax.broadcasted_iota(jnp.int32, (WIN_KEYS, Q_BLOCK), 0))
    valid_w = (dpos >= 0) & (dpos < WINDOW)
    sw = mask_pair(_dot(kwin, qa_s[0:128, :]).astype(BF16), valid_w)
    acc_w = _dot(vwin, jnp.exp2(sw - jnp.max(sw, axis=0, keepdims=True)))
    o_w = [acc_w[0:NSA_DH] / jnp.maximum(acc_w[NSA_DH:NSA_DH + 1], 1e-30)]

    gall = gt_ref[...]
    ggrp = jnp.where(is_g0, gall[0:3 * NSA_REP], gall[3 * NSA_REP:3 * NSA_HEADS])
    gate = [[ggrp[3 * r + br:3 * r + br + 1, :] for r in range(NSA_REP)] for br in range(3)]
    for p2 in range(NSA_REP // 2):
        halves = []
        for hh in range(2):
            r = 2 * p2 + hh
            p = r // NSA_GH
            sl = slice((r % NSA_GH) * Q_BLOCK, (r % NSA_GH + 1) * Q_BLOCK)
            halves.append(gate[0][r] * o_c[p][:, sl] + gate[1][r] * o_s[p][:, sl]
                          + gate[2][r] * o_w[p][:, sl])
        o_ref[:, p2 * 128:(p2 + 1) * 128] = jnp.concatenate(halves, axis=0).T.astype(BF16)


def _nsa_attn(q_r, kc, vct, ks, vst, kw, vwt, gt, ovt, onehot):
    per_b = lambda r, c: pl.BlockSpec((None, r, c), lambda b, g, i: (b, 0, 0))
    per_bg = lambda r, c: pl.BlockSpec((None, r, c), lambda b, g, i: (b, g, 0))
    const = lambda a: pl.BlockSpec(a.shape, lambda b, g, i: (0, 0))
    return pl.pallas_call(
        _nsa_body,
        grid=(BATCH, NSA_KV, N_QB),
        in_specs=[pl.BlockSpec((None, Q_BLOCK, NSA_REP * NSA_DH), lambda b, g, i: (b, i, g)),
                  per_b(N_CMP_PAD, 128), per_bg(NSA_DH, N_CMP_PAD),
                  per_b(SEQ, 128), per_bg(NSA_VROWS, SEQ),
                  per_b(SEQ, 128), per_bg(NSA_VROWS, SEQ),
                  pl.BlockSpec((None, 3 * NSA_HEADS, Q_BLOCK), lambda b, g, i: (b, 0, i)),
                  const(ovt), const(onehot)],
        out_specs=pl.BlockSpec((None, Q_BLOCK, NSA_REP * NSA_DH), lambda b, g, i: (b, i, g)),
        out_shape=jax.ShapeDtypeStruct((BATCH, SEQ, NSA_WIDTH), BF16),
        scratch_shapes=[pltpu.VMEM((256, NSA_NL), BF16),
                        pltpu.VMEM((N_SEL, Q_BLOCK), F32), pltpu.VMEM((N_SEL, Q_BLOCK), F32),
                        pltpu.VMEM((SEL_SUB, NSA_NL), BF16), pltpu.VMEM((SEL_SUB, NSA_NL), BF16)],
        compiler_params=_cparams("parallel", "parallel", "arbitrary"),
        name="nsa_attn",
    )(q_r, kc, vct, ks, vst, kw, vwt, gt, ovt, onehot)


OUT_TM = 256


def _outx_body(x_ref, oh_ref, on_ref, w_ref, nw_ref, wq_ref, k_ref, v_ref, wo_ref, o_ref):
    y = (x_ref[...] + _dot(oh_ref[...], w_ref[0:HG_WIDTH, :])
         + _dot(on_ref[...], w_ref[HG_WIDTH:HG_WIDTH + NSA_WIDTH, :]))
    hx = _rms(y, nw_ref[...]).astype(BF16)
    q = (_dot(hx, wq_ref[...]) * (X_DH ** -0.5)).astype(BF16)
    heads = []
    for h in range(X_HEADS):
        sl = slice(h * X_DH, (h + 1) * X_DH)
        s = _dot_nt(q[:, sl], k_ref[:, sl])
        e = jnp.exp(s - jnp.max(s, axis=-1, keepdims=True))
        p = e / jnp.sum(e, axis=-1, keepdims=True)
        heads.append(_dot(p.astype(BF16), v_ref[:, sl]))
    o_ref[...] = y + _dot(jnp.concatenate(heads, axis=1).astype(BF16), wo_ref[...])


def _outproj_xattn(x1, o_hg, o_nsa, w_out, nw, wq, k, v, wo):
    width = X_HEADS * X_DH
    tiles_per_b = SEQ // OUT_TM
    row = lambda w: pl.BlockSpec((OUT_TM, w), lambda i: (i, 0))
    const = lambda r, c: pl.BlockSpec((r, c), lambda i: (0, 0))
    kv = pl.BlockSpec((None, MEM_LEN, width), lambda i: (i // tiles_per_b, 0, 0))
    return pl.pallas_call(
        _outx_body,
        grid=(TOKENS // OUT_TM,),
        in_specs=[row(D_MODEL), row(HG_WIDTH), row(NSA_WIDTH), const(D_MODEL, D_MODEL), const(1, D_MODEL),
                  const(D_MODEL, width), kv, kv, const(width, D_MODEL)],
        out_specs=row(D_MODEL),
        out_shape=jax.ShapeDtypeStruct((TOKENS, D_MODEL), F32),
        compiler_params=_cparams("parallel"),
        name="out_proj_xattn",
    )(x1, o_hg, o_nsa, w_out, nw, wq, k, v, wo)


def _memkv_body(m_ref, nw_ref, wk_ref, wv_ref, k_ref, v_ref):
    hm = _rms(m_ref[...], nw_ref[...]).astype(BF16)
    k_ref[...] = _dot(hm, wk_ref[...]).astype(BF16)
    v_ref[...] = _dot(hm, wv_ref[...]).astype(BF16)


def _memkv(mem, nw, wk, wv):
    width = X_HEADS * X_DH
    wspec = pl.BlockSpec((D_MODEL, width), lambda b: (0, 0))
    ospec = pl.BlockSpec((None, MEM_LEN, width), lambda b: (b, 0, 0))
    osh = jax.ShapeDtypeStruct((BATCH, MEM_LEN, width), BF16)
    return pl.pallas_call(
        _memkv_body,
        grid=(BATCH,),
        in_specs=[pl.BlockSpec((None, MEM_LEN, D_MODEL), lambda b: (b, 0, 0)),
                  pl.BlockSpec((1, D_MODEL), lambda b: (0, 0)), wspec, wspec],
        out_specs=(ospec, ospec), out_shape=(osh, osh),
        compiler_params=_cparams("parallel"),
        name="xattn_memkv",
    )(mem, nw, wk, wv)


def _overlap_t():
    c0 = np.arange(N_CMP)[:, None] * CMP_STRIDE
    s0 = np.arange(N_SEL)[None, :] * SEL_LEN
    ov = np.clip(np.minimum(c0 + CMP_LEN, s0 + SEL_LEN) - np.maximum(c0, s0), 0, None) / CMP_LEN
    out = np.zeros((N_SEL, N_CMP_PAD), np.float32)
    out[:, :N_CMP] = ov.T
    return out


def _block_onehot():
    out = np.zeros((SEQ, 128), np.float32)
    out[np.arange(SEQ), np.arange(SEQ) // SEL_LEN] = 1.0
    return out


def kernel(x, mem, positions, ffn1_norm, ffn1_w_gate, ffn1_w_up, ffn1_w_down, mix_norm, w_in, hgrn_lb_logits, hgrn_out_norm, nsa_cmp_pe, nsa_cmp_k_w1, nsa_cmp_k_w2, nsa_cmp_v_w1, nsa_cmp_v_w2, w_out, xattn_norm, mem_norm, xattn_wq, xattn_wk, xattn_wv, xattn_wo, ffn2_norm, ffn2_w_gate, ffn2_w_up, ffn2_w_down, final_norm):
    bf = lambda a: a.astype(BF16)
    vec = lambda a: a.reshape(1, -1).astype(F32)
    x2d = x.reshape(TOKENS, D_MODEL)

    x1, h_mix = _ffn(x2d, vec(ffn1_norm[0]), bf(ffn1_w_gate[0]), bf(ffn1_w_up[0]), bf(ffn1_w_down[0]),
                     vec(mix_norm[0]), final=False)

    proj = _proj(h_mix, jnp.pad(bf(w_in[0]), ((0, 0), (0, PROJ_W - D_IN))), PROJ_TN, "proj_in")

    o_hg = _hgrn(proj, hgrn_lb_logits.astype(F32), vec(hgrn_out_norm[0]))

    inv = ROPE_THETA ** (-jnp.arange(NSA_DH // 2, dtype=F32) / (NSA_DH // 2))
    inv128 = jnp.tile(inv, 128 // (NSA_DH // 2)).reshape(1, 128)
    pos_f = positions.astype(F32).reshape(BATCH, SEQ, 1)
    q_r, kc_tok, vc_tok, ks, kw, vst, vwt, gt = _nsa_prep(proj, pos_f, inv128)

    def over_groups(w):
        z = jnp.zeros_like(w)
        return bf(jnp.concatenate([jnp.concatenate([w, z], axis=-1), jnp.concatenate([z, w], axis=-1)], axis=-2))

    per_pos = lambda w1: over_groups(w1.reshape(CMP_LEN, NSA_DH, CMP_HIDDEN))
    pe = nsa_cmp_pe[0].astype(F32)
    kc, vct = _compress(kc_tok, vc_tok, jnp.concatenate([pe, pe], axis=1),
                        per_pos(nsa_cmp_k_w1[0]), over_groups(nsa_cmp_k_w2[0]),
                        per_pos(nsa_cmp_v_w1[0]), over_groups(nsa_cmp_v_w2[0]))
    o_nsa = _nsa_attn(q_r, kc, vct, ks, vst, kw, vwt, gt, jnp.asarray(_overlap_t()),
                      jnp.asarray(_block_onehot(), dtype=BF16))

    km, vm = _memkv(mem, vec(mem_norm[0]), bf(xattn_wk[0]), bf(xattn_wv[0]))
    x3 = _outproj_xattn(x1, o_hg.reshape(TOKENS, HG_WIDTH), o_nsa.reshape(TOKENS, NSA_WIDTH),
                        bf(w_out[0]), vec(xattn_norm[0]), bf(xattn_wq[0]), km, vm, bf(xattn_wo[0]))

    out = _ffn(x3, vec(ffn2_norm[0]), bf(ffn2_w_gate[0]), bf(ffn2_w_up[0]), bf(ffn2_w_down[0]),
               vec(final_norm), final=True)
    return out.reshape(BATCH, SEQ, D_MODEL)
```

```python
import functools

import numpy as np
import jax
import jax.numpy as jnp
from jax import lax
from jax.experimental import pallas as pl
from jax.experimental.pallas import tpu as pltpu

F32 = jnp.float32
BF16 = jnp.bfloat16

D_MODEL = 2048
BATCH = 2
SEQ = 4096
TOKENS = BATCH * SEQ
RMS_EPS = 1e-6
ROPE_THETA = 10000.0
HG_WIDTH = 1024
HG_HEADS = 8
HG_D = 128
HG_CHUNK = 128
HG_LEVELS = (64, 32, 16, 8, 4, 2, 1)
NSA_WIDTH = 1024
NSA_DH = 64
NSA_HEADS = 16
NSA_KV = 2
NSA_REP = 8
NSA_VROWS = NSA_DH + 16
CMP_LEN = 32
CMP_STRIDE = 16
CMP_HIDDEN = 256
N_CMP = (SEQ - CMP_LEN) // CMP_STRIDE + 1
N_CMP_PAD = 256
SEL_LEN = 64
N_SEL = SEQ // SEL_LEN
SEL_TOP = 16
WINDOW = 512
Q_BLOCK = 128
N_QB = SEQ // Q_BLOCK
SEL_SUB = 256
WIN_KEYS = WINDOW + Q_BLOCK
MEM_LEN = 256
X_HEADS = 4
X_DH = 128
D_FF = 5632
IN_SIZES = (1024, 1024, 1024, 1024, 1024, 128, 128, 128, 128, 128, 128, 48)
D_IN = sum(IN_SIZES)
PROJ_NSA_OFF = sum(IN_SIZES[:4])
NSA_PROJ = 2048
PROJ_W = PROJ_NSA_OFF + NSA_PROJ
NEG = -1e30
LOG2E = 1.4426950408889634

V7X_VMEM_BYTES = 64 * 1024 * 1024
VMEM_LIMIT = V7X_VMEM_BYTES - 8 * 1024 * 1024


def _cparams(*sem, flags=None):
    return pltpu.CompilerParams(dimension_semantics=sem, vmem_limit_bytes=VMEM_LIMIT, flags=flags)


def _rms(x, w):
    return x * lax.rsqrt(jnp.mean(x * x, axis=-1, keepdims=True) + RMS_EPS) * w


def _silu(x):
    return x * jax.nn.sigmoid(x)


def _dot(a, b):
    return jnp.dot(a, b, preferred_element_type=F32)


def _dot_f32_by_01(sel, x):
    hi = x.astype(BF16)
    r1 = x - hi.astype(F32)
    mid = r1.astype(BF16)
    lo = (r1 - mid.astype(F32)).astype(BF16)
    n = x.shape[1]
    y = _dot(sel, jnp.concatenate([hi, mid, lo], axis=1))
    return y[:, 0:n] + y[:, n:2 * n] + y[:, 2 * n:3 * n]


def _dot_nt(a, b):
    return lax.dot_general(a, b, (((1,), (1,)), ((), ())), preferred_element_type=F32)


FFN_TM = 512
FFN_TF = 512


def _ffn_body(x_ref, nw_ref, wg_ref, wu_ref, wd_ref, nw2_ref, *rest, final):
    if final:
        o_ref, h_scr = rest
    else:
        o_ref, hn_ref, h_scr = rest
    j = pl.program_id(1)

    @pl.when(j == 0)
    def _():
        h_scr[...] = _rms(x_ref[...], nw_ref[...]).astype(BF16)
        o_ref[...] = jnp.zeros_like(o_ref)

    h = h_scr[...]
    g = _dot(h, wg_ref[...])
    u = _dot(h, wu_ref[...])
    a = (_silu(g) * u).astype(BF16)
    o_ref[...] += _dot(a, wd_ref[...])

    @pl.when(j == pl.num_programs(1) - 1)
    def _():
        y = x_ref[...] + 0.5 * o_ref[...]
        if final:
            o_ref[...] = _rms(y, nw2_ref[...])
        else:
            o_ref[...] = y
            hn_ref[...] = _rms(y, nw2_ref[...]).astype(BF16)


def _ffn(x, nw, wg, wu, wd, nw2, final):
    grid = (TOKENS // FFN_TM, D_FF // FFN_TF)
    row = pl.BlockSpec((FFN_TM, D_MODEL), lambda i, j: (i, 0))
    vec = pl.BlockSpec((1, D_MODEL), lambda i, j: (0, 0))
    in_specs = [row, vec,
                pl.BlockSpec((D_MODEL, FFN_TF), lambda i, j: (0, j)),
                pl.BlockSpec((D_MODEL, FFN_TF), lambda i, j: (0, j)),
                pl.BlockSpec((FFN_TF, D_MODEL), lambda i, j: (j, 0)),
                vec]
    if final:
        out_shape = jax.ShapeDtypeStruct((TOKENS, D_MODEL), F32)
        out_specs = row
    else:
        out_shape = (jax.ShapeDtypeStruct((TOKENS, D_MODEL), F32),
                     jax.ShapeDtypeStruct((TOKENS, D_MODEL), BF16))
        out_specs = (row, row)
    return pl.pallas_call(
        functools.partial(_ffn_body, final=final),
        grid=grid, in_specs=in_specs, out_specs=out_specs, out_shape=out_shape,
        scratch_shapes=[pltpu.VMEM((FFN_TM, D_MODEL), BF16)],
        compiler_params=_cparams("parallel", "arbitrary"),
        name="ffn_final" if final else "ffn",
    )(x, nw, wg, wu, wd, nw2)


PROJ_TM = 1024
PROJ_TN = 512


PROJ_F_TILE0 = IN_SIZES[0] // PROJ_TN
PROJ_F_TILES = IN_SIZES[1] // PROJ_TN


def _proj_body(a_ref, w_ref, o_ref, f_ref):
    j = pl.program_id(1)
    y = _dot(a_ref[...], w_ref[...])
    o_ref[...] = y.astype(BF16)

    @pl.when((j >= PROJ_F_TILE0) & (j < PROJ_F_TILE0 + PROJ_F_TILES))
    def _():
        f_ref[...] = y


def _proj(a, w):
    m, k = a.shape
    n = w.shape[1]
    f_tile = lambda i, j: (i, jnp.clip(j - PROJ_F_TILE0, 0, PROJ_F_TILES - 1))
    return pl.pallas_call(
        _proj_body,
        grid=(m // PROJ_TM, n // PROJ_TN),
        in_specs=[pl.BlockSpec((PROJ_TM, k), lambda i, j: (i, 0)),
                  pl.BlockSpec((k, PROJ_TN), lambda i, j: (0, j))],
        out_specs=(pl.BlockSpec((PROJ_TM, PROJ_TN), lambda i, j: (i, j)),
                   pl.BlockSpec((PROJ_TM, PROJ_TN), f_tile)),
        out_shape=(jax.ShapeDtypeStruct((m, n), BF16),
                   jax.ShapeDtypeStruct((m, IN_SIZES[1]), F32)),
        compiler_params=_cparams("parallel", "arbitrary"),
        name="proj_in",
    )(a, w)


HG_ROWS = 1024
HG_CUM = 256


def _hgrn_body(q_ref, f_ref, i_ref, g_ref, lbl_ref, nw_ref, o_ref, st_ref, k_s, b_s):
    c = pl.program_id(2)

    @pl.when(c == 0)
    def _():
        st_ref[...] = jnp.zeros_like(st_ref)

    l0 = lbl_ref[0:1, :]
    l1 = lbl_ref[1:2, :]
    lmax = jnp.maximum(l0, l1)
    e0 = jnp.exp(l0 - lmax)
    lb = e0 / (e0 + jnp.exp(l1 - lmax))

    C = HG_CHUNK
    f = lb + (1.0 - lb) * jax.nn.sigmoid(f_ref[...])
    k_s[...] = 1.0 - f
    r_i = lax.broadcasted_iota(jnp.int32, (HG_CUM, HG_CUM), 0)
    c_i = lax.broadcasted_iota(jnp.int32, (HG_CUM, HG_CUM), 1)
    tri = jnp.where((r_i >= c_i) & (r_i // C == c_i // C), 1.0, 0.0).astype(BF16)
    logf = jnp.log(f)
    for r0 in range(0, HG_ROWS, HG_CUM):
        b_s[r0:r0 + HG_CUM, :] = _dot_f32_by_01(tri, logf[r0:r0 + HG_CUM])

    t_i = lax.broadcasted_iota(jnp.int32, (C, C), 0)
    s_i = lax.broadcasted_iota(jnp.int32, (C, C), 1)
    level_mask = [(t_i // (2 * w) == s_i // (2 * w)) & (t_i % (2 * w) >= w) & (s_i % (2 * w) < w)
                  for w in HG_LEVELS]
    sub_r = lax.broadcasted_iota(jnp.int32, (8, HG_D), 0)
    odd_row = (lax.broadcasted_iota(jnp.int32, (C, HG_D), 0) & 1) == 1

    def bref_rows(w, r0, b):
        row = lambda r, n: jnp.broadcast_to(b_s[r0 + r:r0 + r + 1, :], (n, HG_D))
        if w >= 8:
            return jnp.concatenate([row(p0 + w - 1, 2 * w) for p0 in range(0, C, 2 * w)], axis=0)
        if w == 4:
            return jnp.concatenate([row(p0 + 3, 8) for p0 in range(0, C, 8)], axis=0)
        if w == 2:
            return jnp.concatenate([jnp.where(sub_r < 4, row(p0 + 1, 8), row(p0 + 5, 8))
                                    for p0 in range(0, C, 8)], axis=0)
        return jnp.where(odd_row, pltpu.roll(b, 1, 0), b)

    chunks = [ci * C for ci in range(HG_ROWS // C)]
    rows = lambda ref, r0: ref[r0:r0 + C, :]
    att = [jnp.zeros((C, C), F32) for _ in chunks]
    q16 = [rows(q_ref, r0) for r0 in chunks]
    k16 = [rows(k_s, r0).astype(BF16) for r0 in chunks]
    for w, mask in zip(HG_LEVELS, level_mask):
        for n, r0 in enumerate(chunks):
            b = rows(b_s, r0)
            e = jnp.exp2(jnp.abs(b - bref_rows(w, r0, b)) * (-LOG2E)).astype(BF16)
            att[n] = jnp.where(mask, _dot_nt(q16[n] * e, k16[n] * e), att[n])
    o_intra = []
    for n, r0 in enumerate(chunks):
        q, k, v = q16[n].astype(F32), rows(k_s, r0), rows(i_ref, r0)
        o_intra.append(_dot(att[n].astype(BF16), v)
                       + jnp.sum(q * k, axis=-1, keepdims=True) * v.astype(F32))
    upd = []
    for r0 in chunks:
        bl = b_s[r0 + C - 1:r0 + C, :]
        kd = rows(k_s, r0) * jnp.exp(bl - rows(b_s, r0))
        v_t = rows(i_ref, r0).astype(F32).T.astype(BF16)
        upd.append((jnp.exp(bl), _dot(v_t, kd.astype(BF16))))
    st_t = st_ref[...]
    for n, r0 in enumerate(chunks):
        qe = (q16[n].astype(F32) * jnp.exp(rows(b_s, r0))).astype(BF16)
        o = o_intra[n] + _dot_nt(qe, st_t.astype(BF16))
        st_t = st_t * upd[n][0] + upd[n][1]
        o = o * lax.rsqrt(jnp.mean(o * o, axis=-1, keepdims=True) + RMS_EPS)
        o_ref[r0:r0 + C, :] = (o * nw_ref[...] * _silu(rows(g_ref, r0).astype(F32))).astype(BF16)
    st_ref[...] = st_t


def _hgrn(proj, proj_f, lb_logits, norm_w):
    p3 = proj.reshape(BATCH, SEQ, PROJ_W)
    f3 = proj_f.reshape(BATCH, SEQ, HG_WIDTH)

    def col(off):
        return pl.BlockSpec((None, HG_ROWS, HG_D), lambda b, h, c: (b, c, off + h))

    return pl.pallas_call(
        _hgrn_body,
        grid=(BATCH, HG_HEADS, SEQ // HG_ROWS),
        in_specs=[col(0), col(0), col(2 * HG_HEADS), col(3 * HG_HEADS),
                  pl.BlockSpec((2, HG_D), lambda b, h, c: (0, h)),
                  pl.BlockSpec((1, HG_D), lambda b, h, c: (0, h))],
        out_specs=pl.BlockSpec((None, HG_ROWS, HG_D), lambda b, h, c: (b, c, h)),
        out_shape=jax.ShapeDtypeStruct((BATCH, SEQ, HG_WIDTH), BF16),
        scratch_shapes=[pltpu.VMEM((HG_D, HG_D), F32),
                        pltpu.VMEM((HG_ROWS, HG_D), F32),
                        pltpu.VMEM((HG_ROWS, HG_D), F32)],
        compiler_params=_cparams("parallel", "parallel", "arbitrary"),
        name="hgrn2",
    )(p3, f3, p3, p3, lb_logits, norm_w)


PREP_TM = 256


def _prep_body(p_ref, pos_ref, inv_ref, q_ref, kc_ref, vc_ref, ks_ref, kw_ref,
               vst_ref, vwt_ref, gt_ref):
    ang = pos_ref[...] * inv_ref[...]
    cos = jnp.cos(ang)
    sin = jnp.sin(ang)
    lane = lax.broadcasted_iota(jnp.int32, (PREP_TM, 128), 1)
    lo = (lane & (NSA_DH // 2)) == 0
    sin_signed = jnp.where(lo, -sin, sin)

    def rope(x):
        rot = jnp.where(lo, pltpu.roll(x, 128 - NSA_DH // 2, 1), pltpu.roll(x, NSA_DH // 2, 1))
        return x * cos + rot * sin_signed

    cols = lambda c0: p_ref[:, c0:c0 + 128].astype(F32)
    scale = NSA_DH ** -0.5 * LOG2E
    for cblk in range(NSA_WIDTH // 128):
        q_ref[:, cblk * 128:(cblk + 1) * 128] = (rope(cols(cblk * 128)) * scale).astype(BF16)
    kc_ref[...] = rope(cols(1024))
    vc_ref[...] = cols(1152)
    ks_ref[:, 0:128] = rope(cols(1280)).astype(BF16)
    blk = (pl.program_id(1) * PREP_TM + lax.broadcasted_iota(jnp.int32, (PREP_TM, 128), 0)) // SEL_LEN
    ks_ref[:, 128:256] = jnp.where(lane == blk, 1.0, 0.0).astype(BF16)
    kw_ref[...] = rope(cols(1536)).astype(BF16)
    ones = jnp.ones((NSA_VROWS - NSA_DH, PREP_TM), BF16)
    for v_ref, c0 in ((vst_ref, 1408), (vwt_ref, 1664)):
        vt = cols(c0).T.astype(BF16)
        for g in range(NSA_KV):
            v_ref[g * NSA_VROWS:g * NSA_VROWS + NSA_DH, :] = vt[g * NSA_DH:(g + 1) * NSA_DH]
            v_ref[g * NSA_VROWS + NSA_DH:(g + 1) * NSA_VROWS, :] = ones
    gt_ref[...] = jax.nn.sigmoid(cols(1792)).T[0:3 * NSA_HEADS, :]


def _nsa_prep(proj, pos_f, inv128):
    nt = SEQ // PREP_TM
    p3 = proj.reshape(BATCH, SEQ, PROJ_W)
    nat = lambda w: pl.BlockSpec((None, PREP_TM, w), lambda b, i: (b, i, 0))
    tr = lambda r: pl.BlockSpec((None, r, PREP_TM), lambda b, i: (b, 0, i))
    sds = jax.ShapeDtypeStruct
    return pl.pallas_call(
        _prep_body,
        grid=(BATCH, nt),
        in_specs=[pl.BlockSpec((None, PREP_TM, NSA_PROJ), lambda b, i: (b, i, PROJ_NSA_OFF // NSA_PROJ)),
                  nat(1), pl.BlockSpec((1, 128), lambda b, i: (0, 0))],
        out_specs=(nat(NSA_WIDTH), nat(128), nat(128), nat(256), nat(128),
                   tr(NSA_KV * NSA_VROWS), tr(NSA_KV * NSA_VROWS), tr(3 * NSA_HEADS)),
        out_shape=(sds((BATCH, SEQ, NSA_WIDTH), BF16),
                   sds((BATCH, SEQ, 128), F32),
                   sds((BATCH, SEQ, 128), F32),
                   sds((BATCH, SEQ, 256), BF16),
                   sds((BATCH, SEQ, 128), BF16),
                   sds((BATCH, NSA_KV * NSA_VROWS, SEQ), BF16),
                   sds((BATCH, NSA_KV * NSA_VROWS, SEQ), BF16),
                   sds((BATCH, 3 * NSA_HEADS, SEQ), F32)),
        compiler_params=_cparams("parallel", "parallel"),
        name="nsa_prep",
    )(p3, pos_f, inv128)


def _cmp_body(tk_ref, tv_ref, pe_ref, kw1_ref, kw2_ref, vw1_ref, vw2_ref, kc_ref, vct_ref, y1_s, y2_s):
    row = lax.broadcasted_iota(jnp.int32, (N_CMP_PAD, NSA_KV * CMP_HIDDEN), 0)

    def mlp(t_ref, w1_ref, w2_ref):
        y1_s[...] = jnp.zeros_like(y1_s)
        y2_s[...] = jnp.zeros_like(y2_s)
        for l in range(CMP_STRIDE):
            x = t_ref[pl.ds(l, N_CMP_PAD, stride=CMP_STRIDE), :]
            y1_s[...] += _dot((x + pe_ref[l:l + 1, :]).astype(BF16), w1_ref[l])
            y2_s[...] += _dot((x + pe_ref[CMP_STRIDE + l:CMP_STRIDE + l + 1, :]).astype(BF16),
                              w1_ref[CMP_STRIDE + l])
        hid = jnp.where(row < N_CMP, y1_s[...] + pltpu.roll(y2_s[...], N_CMP_PAD - 1, 0), 0.0)
        return _dot(_silu(hid).astype(BF16), w2_ref[...])

    kc_ref[...] = mlp(tk_ref, kw1_ref, kw2_ref).astype(BF16)
    y1_s[:, 0:128] = mlp(tv_ref, vw1_ref, vw2_ref)
    vt = y1_s[:, 0:128].T.astype(BF16)
    ones = jnp.ones((NSA_VROWS - NSA_DH, N_CMP_PAD), BF16)
    for g in range(NSA_KV):
        vct_ref[g * NSA_VROWS:g * NSA_VROWS + NSA_DH, :] = vt[g * NSA_DH:(g + 1) * NSA_DH]
        vct_ref[g * NSA_VROWS + NSA_DH:(g + 1) * NSA_VROWS, :] = ones


def _compress(tk, tv, pe2, kw1, kw2, vw1, vw2):
    seg = pl.BlockSpec((None, SEQ, 128), lambda b: (b, 0, 0))
    full2 = lambda a: pl.BlockSpec(a.shape, lambda b: (0,) * a.ndim)
    return pl.pallas_call(
        _cmp_body,
        grid=(BATCH,),
        in_specs=[seg, seg, full2(pe2), full2(kw1), full2(kw2), full2(vw1), full2(vw2)],
        out_specs=(pl.BlockSpec((None, N_CMP_PAD, 128), lambda b: (b, 0, 0)),
                   pl.BlockSpec((None, NSA_KV * NSA_VROWS, N_CMP_PAD), lambda b: (b, 0, 0))),
        out_shape=(jax.ShapeDtypeStruct((BATCH, N_CMP_PAD, 128), BF16),
                   jax.ShapeDtypeStruct((BATCH, NSA_KV * NSA_VROWS, N_CMP_PAD), BF16)),
        scratch_shapes=[pltpu.VMEM((N_CMP_PAD, NSA_KV * CMP_HIDDEN), F32)] * 2,
        compiler_params=_cparams("parallel"),
        name="nsa_compress",
    )(tk, tv, pe2, kw1, kw2, vw1, vw2)


NSA_NL = NSA_REP * Q_BLOCK
NSA_GH = 8
NSA_PW = NSA_GH * Q_BLOCK
NSA_NP = NSA_REP // NSA_GH


def _nsa_body(q_ref, kc_ref, vct_ref, ks_ref, vst_ref, kw_ref, vwt_ref, gt_ref, ovt_ref,
              o_ref, qa_s, sc_s, rk_s, s0_s, s1_s):
    g = pl.program_id(1)
    qb = pl.program_id(2)
    q0 = pl.multiple_of(qb * Q_BLOCK, Q_BLOCK)
    is_g0 = g == 0

    qblk = q_ref[...].astype(F32)
    zero_slab = jnp.zeros((NSA_DH, Q_BLOCK), F32)
    for p in range(NSA_REP // 2):
        t = qblk[:, p * 128:(p + 1) * 128].T
        for hh in range(2):
            s = t[hh * NSA_DH:(hh + 1) * NSA_DH]
            r = 2 * p + hh
            qa_s[0:128, r * Q_BLOCK:(r + 1) * Q_BLOCK] = jnp.concatenate(
                [jnp.where(is_g0, s, zero_slab), jnp.where(is_g0, zero_slab, s)], axis=0).astype(BF16)
    qa_s[128 + N_SEL:256, :] = jnp.zeros((128 - N_SEL, NSA_NL), BF16)

    tq = q0 + lax.broadcasted_iota(jnp.int32, (1, Q_BLOCK), 1)

    def pair(p):
        return slice(p * NSA_PW, (p + 1) * NSA_PW)

    def mask_pair(s, valid):
        return jnp.concatenate([jnp.where(valid, s[:, r * Q_BLOCK:(r + 1) * Q_BLOCK], NEG)
                                for r in range(NSA_GH)], axis=1)

    n_i = lax.broadcasted_iota(jnp.int32, (N_CMP_PAD, Q_BLOCK), 0)
    valid_c = (n_i * CMP_STRIDE + (CMP_LEN - 1) <= tq) & (n_i < N_CMP)
    sc = mask_pair(_dot(kc_ref[...], qa_s[0:128, :]).astype(BF16), valid_c)
    m_c = jnp.max(sc, axis=0, keepdims=True)
    e_c = jnp.exp2(sc - m_c)
    acc_c = _dot(vct_ref[...], e_c)
    inv_c = jnp.where(m_c.astype(F32) > 0.5 * NEG, 1.0 / jnp.maximum(acc_c[NSA_DH:NSA_DH + 1], 1e-30), 0.0)
    o_c = [acc_c[0:NSA_DH] * inv_c]

    w0 = pl.multiple_of(jnp.maximum(q0 - WINDOW, 0), Q_BLOCK)
    kwin = kw_ref[pl.ds(w0, WIN_KEYS), :]
    vwin = vwt_ref[:, pl.ds(w0, WIN_KEYS)]
    dpos = tq - (w0 + lax.broadcasted_iota(jnp.int32, (WIN_KEYS, Q_BLOCK), 0))
    valid_w = (dpos >= 0) & (dpos < WINDOW)
    sw = mask_pair(_dot(kwin, qa_s[0:128, :]).astype(BF16), valid_w)
    acc_w = _dot(vwin, jnp.exp2(sw - jnp.max(sw, axis=0, keepdims=True)))
    o_w = [acc_w[0:NSA_DH] / jnp.maximum(acc_w[NSA_DH:NSA_DH + 1], 1e-30)]

    d_i = lax.broadcasted_iota(jnp.int32, (Q_BLOCK, Q_BLOCK), 0)
    t_i = lax.broadcasted_iota(jnp.int32, (Q_BLOCK, Q_BLOCK), 1)
    kd = ks_ref[pl.ds(q0, Q_BLOCK), 0:128]
    s = mask_pair(_dot(kd, qa_s[0:128, :]).astype(BF16), d_i <= t_i)
    m16 = jnp.max(s, axis=0, keepdims=True)
    carry = (m16.astype(F32), _dot(vst_ref[:, pl.ds(q0, Q_BLOCK)], jnp.exp2(s - m16)))

    imp_h = _dot(ovt_ref[...], e_c) * inv_c
    imp = imp_h[:, 0:Q_BLOCK]
    for r in range(1, NSA_REP):
        imp = imp + imp_h[:, r * Q_BLOCK:(r + 1) * Q_BLOCK]
    j_i = lax.broadcasted_iota(jnp.int32, (N_SEL, Q_BLOCK), 0)
    cur = tq // SEL_LEN
    forced = (j_i == 0) | (j_i == cur) | (j_i == cur - 1)
    score = jnp.where(forced, jnp.inf, jnp.where(j_i > cur, -jnp.inf, imp))
    sc_s[...] = score
    rk_s[...] = jnp.zeros_like(rk_s)
    sub8 = lax.broadcasted_iota(jnp.int32, (8, Q_BLOCK), 0)
    for grp in range(N_SEL // 8):
        @pl.when(8 * grp <= 2 * qb + 1)
        def _():
            for v in range(N_SEL // 8):
                sv = sc_s[8 * v:8 * v + 8, :]
                part = jnp.zeros((8, Q_BLOCK), F32)
                for jp in range(8 * grp, 8 * grp + 8):
                    row = sc_s[jp:jp + 1, :]
                    if v > grp:
                        part = part + jnp.where(row >= sv, 1.0, 0.0)
                    elif v < grp:
                        part = part + jnp.where(row > sv, 1.0, 0.0)
                    else:
                        part = part + jnp.where(sub8 + 8 * v > jp, jnp.where(row >= sv, 1.0, 0.0),
                                                jnp.where(row > sv, 1.0, 0.0))
                rk_s[8 * v:8 * v + 8, :] += part
    bias = jnp.where((rk_s[...] < SEL_TOP) & (j_i < 2 * qb), 0.0, NEG).astype(BF16)
    for r in range(NSA_REP):
        qa_s[128:128 + N_SEL, r * Q_BLOCK:(r + 1) * Q_BLOCK] = bias

    last_sub = SEQ // SEL_SUB - 1

    def scores_into(buf, c):
        k0 = pl.multiple_of(jnp.minimum(c, last_sub) * SEL_SUB, SEL_SUB)
        buf[...] = _dot(ks_ref[pl.ds(k0, SEL_SUB), :], qa_s[...]).astype(BF16)

    def softmax_from(buf, c, carry):
        m, acc = carry
        k0 = pl.multiple_of(c * SEL_SUB, SEL_SUB)
        s = buf[...]
        m_new = jnp.maximum(m, jnp.max(s, axis=0, keepdims=True).astype(F32))
        pr = jnp.exp2(s - m_new.astype(BF16))
        return m_new, acc * jnp.exp2(m - m_new) + _dot(vst_ref[:, pl.ds(k0, SEL_SUB)], pr)

    scores_into(s0_s, 0)

    def sel_step(i, carry):
        c = 2 * i
        scores_into(s1_s, c + 1)
        carry = softmax_from(s0_s, c, carry)
        scores_into(s0_s, c + 2)
        return softmax_from(s1_s, c + 1, carry)

    n_main = (qb * Q_BLOCK + 2 * SEL_SUB - 1) // (2 * SEL_SUB)
    _, acc_s = lax.fori_loop(0, n_main, sel_step, carry)
    o_s = [acc_s[0:NSA_DH] / jnp.maximum(acc_s[NSA_DH:NSA_DH + 1], 1e-30)]

    gall = gt_ref[...]
    ggrp = jnp.where(is_g0, gall[0:3 * NSA_REP], gall[3 * NSA_REP:3 * NSA_HEADS])
    gate = [[ggrp[3 * r + br:3 * r + br + 1, :] for r in range(NSA_REP)] for br in range(3)]
    for p2 in range(NSA_REP // 2):
        halves = []
        for hh in range(2):
            r = 2 * p2 + hh
            p = r // NSA_GH
            sl = slice((r % NSA_GH) * Q_BLOCK, (r % NSA_GH + 1) * Q_BLOCK)
            halves.append(gate[0][r] * o_c[p][:, sl] + gate[1][r] * o_s[p][:, sl]
                          + gate[2][r] * o_w[p][:, sl])
        o_ref[:, p2 * 128:(p2 + 1) * 128] = jnp.concatenate(halves, axis=0).T.astype(BF16)


def _nsa_attn(q_r, kc, vct, ks, vst, kw, vwt, gt, ovt):
    per_b = lambda r, c: pl.BlockSpec((None, r, c), lambda b, g, i: (b, 0, 0))
    per_bg = lambda r, c: pl.BlockSpec((None, r, c), lambda b, g, i: (b, g, 0))
    const = lambda a: pl.BlockSpec(a.shape, lambda b, g, i: (0, 0))
    return pl.pallas_call(
        _nsa_body,
        grid=(BATCH, NSA_KV, N_QB),
        in_specs=[pl.BlockSpec((None, Q_BLOCK, NSA_REP * NSA_DH), lambda b, g, i: (b, i, g)),
                  per_b(N_CMP_PAD, 128), per_bg(NSA_VROWS, N_CMP_PAD),
                  per_b(SEQ, 256), per_bg(NSA_VROWS, SEQ),
                  per_b(SEQ, 128), per_bg(NSA_VROWS, SEQ),
                  pl.BlockSpec((None, 3 * NSA_HEADS, Q_BLOCK), lambda b, g, i: (b, 0, i)),
                  const(ovt)],
        out_specs=pl.BlockSpec((None, Q_BLOCK, NSA_REP * NSA_DH), lambda b, g, i: (b, i, g)),
        out_shape=jax.ShapeDtypeStruct((BATCH, SEQ, NSA_WIDTH), BF16),
        scratch_shapes=[pltpu.VMEM((256, NSA_NL), BF16),
                        pltpu.VMEM((N_SEL, Q_BLOCK), F32), pltpu.VMEM((N_SEL, Q_BLOCK), F32),
                        pltpu.VMEM((SEL_SUB, NSA_NL), BF16), pltpu.VMEM((SEL_SUB, NSA_NL), BF16)],
        compiler_params=_cparams("parallel", "parallel", "arbitrary"),
        name="nsa_attn",
    )(q_r, kc, vct, ks, vst, kw, vwt, gt, ovt)


OUT_TM = 512


def _outx_body(x_ref, oh_ref, on_ref, w_ref, nw_ref, wq_ref, k_ref, v_ref, wo_ref, o_ref):
    y = (x_ref[...] + _dot(oh_ref[...], w_ref[0:HG_WIDTH, :])
         + _dot(on_ref[...], w_ref[HG_WIDTH:HG_WIDTH + NSA_WIDTH, :]))
    hx = _rms(y, nw_ref[...]).astype(BF16)
    q = (_dot(hx, wq_ref[...]) * (X_DH ** -0.5)).astype(BF16)
    heads = []
    for h in range(X_HEADS):
        sl = slice(h * X_DH, (h + 1) * X_DH)
        s = _dot_nt(q[:, sl], k_ref[:, sl])
        e = jnp.exp(s - jnp.max(s, axis=-1, keepdims=True))
        p = e / jnp.sum(e, axis=-1, keepdims=True)
        heads.append(_dot(p.astype(BF16), v_ref[:, sl]))
    o_ref[...] = y + _dot(jnp.concatenate(heads, axis=1).astype(BF16), wo_ref[...])


def _outproj_xattn(x1, o_hg, o_nsa, w_out, nw, wq, k, v, wo):
    width = X_HEADS * X_DH
    tiles_per_b = SEQ // OUT_TM
    row = lambda w: pl.BlockSpec((OUT_TM, w), lambda i: (i, 0))
    const = lambda r, c: pl.BlockSpec((r, c), lambda i: (0, 0))
    kv = pl.BlockSpec((None, MEM_LEN, width), lambda i: (i // tiles_per_b, 0, 0))
    return pl.pallas_call(
        _outx_body,
        grid=(TOKENS // OUT_TM,),
        in_specs=[row(D_MODEL), row(HG_WIDTH), row(NSA_WIDTH), const(D_MODEL, D_MODEL), const(1, D_MODEL),
                  const(D_MODEL, width), kv, kv, const(width, D_MODEL)],
        out_specs=row(D_MODEL),
        out_shape=jax.ShapeDtypeStruct((TOKENS, D_MODEL), F32),
        compiler_params=_cparams("parallel"),
        name="out_proj_xattn",
    )(x1, o_hg, o_nsa, w_out, nw, wq, k, v, wo)


def _memkv_body(m_ref, nw_ref, wk_ref, wv_ref, k_ref, v_ref):
    hm = _rms(m_ref[...], nw_ref[...]).astype(BF16)
    k_ref[...] = _dot(hm, wk_ref[...]).astype(BF16)
    v_ref[...] = _dot(hm, wv_ref[...]).astype(BF16)


def _memkv(mem, nw, wk, wv):
    width = X_HEADS * X_DH
    wspec = pl.BlockSpec((D_MODEL, width), lambda b: (0, 0))
    ospec = pl.BlockSpec((None, MEM_LEN, width), lambda b: (b, 0, 0))
    osh = jax.ShapeDtypeStruct((BATCH, MEM_LEN, width), BF16)
    return pl.pallas_call(
        _memkv_body,
        grid=(BATCH,),
        in_specs=[pl.BlockSpec((None, MEM_LEN, D_MODEL), lambda b: (b, 0, 0)),
                  pl.BlockSpec((1, D_MODEL), lambda b: (0, 0)), wspec, wspec],
        out_specs=(ospec, ospec), out_shape=(osh, osh),
        compiler_params=_cparams("parallel"),
        name="xattn_memkv",
    )(mem, nw, wk, wv)


def _overlap_t():
    c0 = np.arange(N_CMP)[:, None] * CMP_STRIDE
    s0 = np.arange(N_SEL)[None, :] * SEL_LEN
    ov = np.clip(np.minimum(c0 + CMP_LEN, s0 + SEL_LEN) - np.maximum(c0, s0), 0, None) / CMP_LEN
    out = np.zeros((N_SEL, N_CMP_PAD), np.float32)
    out[:, :N_CMP] = ov.T
    return out


def kernel(x, mem, positions, ffn1_norm, ffn1_w_gate, ffn1_w_up, ffn1_w_down, mix_norm, w_in, hgrn_lb_logits, hgrn_out_norm, nsa_cmp_pe, nsa_cmp_k_w1, nsa_cmp_k_w2, nsa_cmp_v_w1, nsa_cmp_v_w2, w_out, xattn_norm, mem_norm, xattn_wq, xattn_wk, xattn_wv, xattn_wo, ffn2_norm, ffn2_w_gate, ffn2_w_up, ffn2_w_down, final_norm):
    bf = lambda a: a.astype(BF16)
    vec = lambda a: a.reshape(1, -1).astype(F32)
    x2d = x.reshape(TOKENS, D_MODEL)

    x1, h_mix = _ffn(x2d, vec(ffn1_norm[0]), bf(ffn1_w_gate[0]), bf(ffn1_w_up[0]), bf(ffn1_w_down[0]),
                     vec(mix_norm[0]), final=False)

    proj, proj_f = _proj(h_mix, jnp.pad(bf(w_in[0]), ((0, 0), (0, PROJ_W - D_IN))))

    o_hg = _hgrn(proj, proj_f, hgrn_lb_logits.astype(F32), vec(hgrn_out_norm[0]))

    inv = ROPE_THETA ** (-jnp.arange(NSA_DH // 2, dtype=F32) / (NSA_DH // 2))
    inv128 = jnp.tile(inv, 128 // (NSA_DH // 2)).reshape(1, 128)
    pos_f = positions.astype(F32).reshape(BATCH, SEQ, 1)
    q_r, kc_tok, vc_tok, ks, kw, vst, vwt, gt = _nsa_prep(proj, pos_f, inv128)

    def over_groups(w):
        z = jnp.zeros_like(w)
        return bf(jnp.concatenate([jnp.concatenate([w, z], axis=-1), jnp.concatenate([z, w], axis=-1)], axis=-2))

    per_pos = lambda w1: over_groups(w1.reshape(CMP_LEN, NSA_DH, CMP_HIDDEN))
    pe = nsa_cmp_pe[0].astype(F32)
    kc, vct = _compress(kc_tok, vc_tok, jnp.concatenate([pe, pe], axis=1),
                        per_pos(nsa_cmp_k_w1[0]), over_groups(nsa_cmp_k_w2[0]),
                        per_pos(nsa_cmp_v_w1[0]), over_groups(nsa_cmp_v_w2[0]))
    o_nsa = _nsa_attn(q_r, kc, vct, ks, vst, kw, vwt, gt, jnp.asarray(_overlap_t(), dtype=BF16))

    km, vm = _memkv(mem, vec(mem_norm[0]), bf(xattn_wk[0]), bf(xattn_wv[0]))
    x3 = _outproj_xattn(x1, o_hg.reshape(TOKENS, HG_WIDTH), o_nsa.reshape(TOKENS, NSA_WIDTH),
                        bf(w_out[0]), vec(xattn_norm[0]), bf(xattn_wq[0]), km, vm, bf(xattn_wo[0]))

    out = _ffn(x3, vec(ffn2_norm[0]), bf(ffn2_w_gate[0]), bf(ffn2_w_up[0]), bf(ffn2_w_down[0]),
               vec(final_norm), final=True)
    return out.reshape(BATCH, SEQ, D_MODEL)
```

```python
import functools

import numpy as np
import jax
import jax.numpy as jnp
from jax import lax
from jax.experimental import pallas as pl
from jax.experimental.pallas import tpu as pltpu

F32 = jnp.float32
BF16 = jnp.bfloat16

D_MODEL = 2048
BATCH = 2
SEQ = 4096
TOKENS = BATCH * SEQ
RMS_EPS = 1e-6
ROPE_THETA = 10000.0
HG_WIDTH = 1024
HG_HEADS = 8
HG_D = 128
HG_CHUNK = 128
HG_LEVELS = (64, 32, 16, 8, 4, 2, 1)
NSA_WIDTH = 1024
NSA_DH = 64
NSA_HEADS = 16
NSA_KV = 2
NSA_REP = 8
NSA_VROWS = NSA_DH + 16
CMP_LEN = 32
CMP_STRIDE = 16
CMP_HIDDEN = 256
N_CMP = (SEQ - CMP_LEN) // CMP_STRIDE + 1
N_CMP_PAD = 256
SEL_LEN = 64
N_SEL = SEQ // SEL_LEN
SEL_TOP = 16
WINDOW = 512
Q_BLOCK = 128
N_QB = SEQ // Q_BLOCK
SEL_SUB = 256
WIN_KEYS = WINDOW + Q_BLOCK
MEM_LEN = 256
X_HEADS = 4
X_DH = 128
D_FF = 5632
IN_SIZES = (1024, 1024, 1024, 1024, 1024, 128, 128, 128, 128, 128, 128, 48)
D_IN = sum(IN_SIZES)
PROJ_NSA_OFF = sum(IN_SIZES[:4])
NSA_PROJ = 2048
PROJ_W = PROJ_NSA_OFF + NSA_PROJ
NEG = -1e30
LOG2E = 1.4426950408889634

V7X_VMEM_BYTES = 64 * 1024 * 1024
VMEM_LIMIT = V7X_VMEM_BYTES - 8 * 1024 * 1024


def _cparams(*sem, flags=None):
    return pltpu.CompilerParams(dimension_semantics=sem, vmem_limit_bytes=VMEM_LIMIT, flags=flags)


def _rms(x, w):
    return x * lax.rsqrt(jnp.mean(x * x, axis=-1, keepdims=True) + RMS_EPS) * w


def _silu(x):
    return x * jax.nn.sigmoid(x)


def _dot(a, b):
    return jnp.dot(a, b, preferred_element_type=F32)


def _dot_f32_by_01(sel, x):
    hi = x.astype(BF16)
    r1 = x - hi.astype(F32)
    mid = r1.astype(BF16)
    lo = (r1 - mid.astype(F32)).astype(BF16)
    n = x.shape[1]
    y = _dot(sel, jnp.concatenate([hi, mid, lo], axis=1))
    return y[:, 0:n] + y[:, n:2 * n] + y[:, 2 * n:3 * n]


def _dot_nt(a, b):
    return lax.dot_general(a, b, (((1,), (1,)), ((), ())), preferred_element_type=F32)


FFN_TM = 512
FFN_TF = 512


def _ffn_body(x_ref, nw_ref, wg_ref, wu_ref, wd_ref, nw2_ref, *rest, final):
    if final:
        o_ref, h_scr = rest
    else:
        o_ref, hn_ref, h_scr = rest
    j = pl.program_id(1)

    @pl.when(j == 0)
    def _():
        h_scr[...] = _rms(x_ref[...], nw_ref[...]).astype(BF16)
        o_ref[...] = jnp.zeros_like(o_ref)

    h = h_scr[...]
    g = _dot(h, wg_ref[...])
    u = _dot(h, wu_ref[...])
    a = (_silu(g) * u).astype(BF16)
    o_ref[...] += _dot(a, wd_ref[...])

    @pl.when(j == pl.num_programs(1) - 1)
    def _():
        y = x_ref[...] + 0.5 * o_ref[...]
        if final:
            o_ref[...] = _rms(y, nw2_ref[...])
        else:
            o_ref[...] = y
            hn_ref[...] = _rms(y, nw2_ref[...]).astype(BF16)


def _ffn(x, nw, wg, wu, wd, nw2, final):
    grid = (TOKENS // FFN_TM, D_FF // FFN_TF)
    row = pl.BlockSpec((FFN_TM, D_MODEL), lambda i, j: (i, 0))
    vec = pl.BlockSpec((1, D_MODEL), lambda i, j: (0, 0))
    in_specs = [row, vec,
                pl.BlockSpec((D_MODEL, FFN_TF), lambda i, j: (0, j)),
                pl.BlockSpec((D_MODEL, FFN_TF), lambda i, j: (0, j)),
                pl.BlockSpec((FFN_TF, D_MODEL), lambda i, j: (j, 0)),
                vec]
    if final:
        out_shape = jax.ShapeDtypeStruct((TOKENS, D_MODEL), F32)
        out_specs = row
    else:
        out_shape = (jax.ShapeDtypeStruct((TOKENS, D_MODEL), F32),
                     jax.ShapeDtypeStruct((TOKENS, D_MODEL), BF16))
        out_specs = (row, row)
    return pl.pallas_call(
        functools.partial(_ffn_body, final=final),
        grid=grid, in_specs=in_specs, out_specs=out_specs, out_shape=out_shape,
        scratch_shapes=[pltpu.VMEM((FFN_TM, D_MODEL), BF16)],
        compiler_params=_cparams("parallel", "arbitrary"),
        name="ffn_final" if final else "ffn",
    )(x, nw, wg, wu, wd, nw2)


PROJ_TM = 1024
PROJ_TN = 512


PROJ_F_TILE0 = IN_SIZES[0] // PROJ_TN
PROJ_F_TILES = IN_SIZES[1] // PROJ_TN


def _proj_body(a_ref, wt_ref, o_ref, f_ref):
    j = pl.program_id(1)
    y = _dot_nt(a_ref[...], wt_ref[...])
    o_ref[...] = y.astype(BF16)

    @pl.when((j >= PROJ_F_TILE0) & (j < PROJ_F_TILE0 + PROJ_F_TILES))
    def _():
        f_ref[...] = y


def _proj(a, wt):
    m, k = a.shape
    n = wt.shape[0]
    f_tile = lambda i, j: (i, jnp.clip(j - PROJ_F_TILE0, 0, PROJ_F_TILES - 1))
    return pl.pallas_call(
        _proj_body,
        grid=(m // PROJ_TM, n // PROJ_TN),
        in_specs=[pl.BlockSpec((PROJ_TM, k), lambda i, j: (i, 0)),
                  pl.BlockSpec((PROJ_TN, k), lambda i, j: (j, 0))],
        out_specs=(pl.BlockSpec((PROJ_TM, PROJ_TN), lambda i, j: (i, j)),
                   pl.BlockSpec((PROJ_TM, PROJ_TN), f_tile)),
        out_shape=(jax.ShapeDtypeStruct((m, n), BF16),
                   jax.ShapeDtypeStruct((m, IN_SIZES[1]), F32)),
        compiler_params=_cparams("parallel", "arbitrary"),
        name="proj_in",
    )(a, wt)


HG_ROWS = 1024
HG_CUM = 256


def _hgrn_body(q_ref, f_ref, i_ref, g_ref, lbl_ref, nw_ref, o_ref, st_ref, k_s, b_s):
    c = pl.program_id(2)

    @pl.when(c == 0)
    def _():
        st_ref[...] = jnp.zeros_like(st_ref)

    l0 = lbl_ref[0:1, :]
    l1 = lbl_ref[1:2, :]
    lmax = jnp.maximum(l0, l1)
    e0 = jnp.exp(l0 - lmax)
    lb = e0 / (e0 + jnp.exp(l1 - lmax))

    C = HG_CHUNK
    f = lb + (1.0 - lb) * jax.nn.sigmoid(f_ref[...])
    k_s[...] = 1.0 - f
    r_i = lax.broadcasted_iota(jnp.int32, (HG_CUM, HG_CUM), 0)
    c_i = lax.broadcasted_iota(jnp.int32, (HG_CUM, HG_CUM), 1)
    tri = jnp.where((r_i >= c_i) & (r_i // C == c_i // C), 1.0, 0.0).astype(BF16)
    logf = jnp.log(f)
    for r0 in range(0, HG_ROWS, HG_CUM):
        b_s[r0:r0 + HG_CUM, :] = _dot_f32_by_01(tri, logf[r0:r0 + HG_CUM])

    t_i = lax.broadcasted_iota(jnp.int32, (C, C), 0)
    s_i = lax.broadcasted_iota(jnp.int32, (C, C), 1)
    level_mask = [(t_i // (2 * w) == s_i // (2 * w)) & (t_i % (2 * w) >= w) & (s_i % (2 * w) < w)
                  for w in HG_LEVELS]
    sub_r = lax.broadcasted_iota(jnp.int32, (8, HG_D), 0)
    odd_row = (lax.broadcasted_iota(jnp.int32, (C, HG_D), 0) & 1) == 1

    def bref_rows(w, r0, b):
        row = lambda r, n: jnp.broadcast_to(b_s[r0 + r:r0 + r + 1, :], (n, HG_D))
        if w >= 8:
            return jnp.concatenate([row(p0 + w - 1, 2 * w) for p0 in range(0, C, 2 * w)], axis=0)
        if w == 4:
            return jnp.concatenate([row(p0 + 3, 8) for p0 in range(0, C, 8)], axis=0)
        if w == 2:
            return jnp.concatenate([jnp.where(sub_r < 4, row(p0 + 1, 8), row(p0 + 5, 8))
                                    for p0 in range(0, C, 8)], axis=0)
        return jnp.where(odd_row, pltpu.roll(b, 1, 0), b)

    chunks = [ci * C for ci in range(HG_ROWS // C)]
    rows = lambda ref, r0: ref[r0:r0 + C, :]
    att = [jnp.zeros((C, C), F32) for _ in chunks]
    q16 = [rows(q_ref, r0) for r0 in chunks]
    k16 = [rows(k_s, r0).astype(BF16) for r0 in chunks]
    for w, mask in zip(HG_LEVELS, level_mask):
        for n, r0 in enumerate(chunks):
            b = rows(b_s, r0)
            e = jnp.exp2(jnp.abs(b - bref_rows(w, r0, b)) * (-LOG2E)).astype(BF16)
            att[n] = jnp.where(mask, _dot_nt(q16[n] * e, k16[n] * e), att[n])
    o_intra = []
    for n, r0 in enumerate(chunks):
        q, k, v = q16[n].astype(F32), rows(k_s, r0), rows(i_ref, r0)
        o_intra.append(_dot(att[n].astype(BF16), v)
                       + jnp.sum(q * k, axis=-1, keepdims=True) * v.astype(F32))
    upd = []
    for r0 in chunks:
        bl = b_s[r0 + C - 1:r0 + C, :]
        kd = rows(k_s, r0) * jnp.exp(bl - rows(b_s, r0))
        v_t = rows(i_ref, r0).astype(F32).T.astype(BF16)
        upd.append((jnp.exp(bl), _dot(v_t, kd.astype(BF16))))
    st_t = st_ref[...]
    for n, r0 in enumerate(chunks):
        qe = (q16[n].astype(F32) * jnp.exp(rows(b_s, r0))).astype(BF16)
        o = o_intra[n] + _dot_nt(qe, st_t.astype(BF16))
        st_t = st_t * upd[n][0] + upd[n][1]
        o = o * lax.rsqrt(jnp.mean(o * o, axis=-1, keepdims=True) + RMS_EPS)
        o_ref[r0:r0 + C, :] = (o * nw_ref[...] * _silu(rows(g_ref, r0).astype(F32))).astype(BF16)
    st_ref[...] = st_t


def _hgrn(proj, proj_f, lb_logits, norm_w):
    p3 = proj.reshape(BATCH, SEQ, PROJ_W)
    f3 = proj_f.reshape(BATCH, SEQ, HG_WIDTH)

    def col(off):
        return pl.BlockSpec((None, HG_ROWS, HG_D), lambda b, h, c: (b, c, off + h))

    return pl.pallas_call(
        _hgrn_body,
        grid=(BATCH, HG_HEADS, SEQ // HG_ROWS),
        in_specs=[col(0), col(0), col(2 * HG_HEADS), col(3 * HG_HEADS),
                  pl.BlockSpec((2, HG_D), lambda b, h, c: (0, h)),
                  pl.BlockSpec((1, HG_D), lambda b, h, c: (0, h))],
        out_specs=pl.BlockSpec((None, HG_ROWS, HG_D), lambda b, h, c: (b, c, h)),
        out_shape=jax.ShapeDtypeStruct((BATCH, SEQ, HG_WIDTH), BF16),
        scratch_shapes=[pltpu.VMEM((HG_D, HG_D), F32),
                        pltpu.VMEM((HG_ROWS, HG_D), F32),
                        pltpu.VMEM((HG_ROWS, HG_D), F32)],
        compiler_params=_cparams("parallel", "parallel", "arbitrary"),
        name="hgrn2",
    )(p3, f3, p3, p3, lb_logits, norm_w)


PREP_TM = 256


def _prep_body(p_ref, pos_ref, inv_ref, q_ref, kc_ref, vc_ref, ks_ref, kw_ref,
               vst_ref, vwt_ref, gt_ref):
    ang = pos_ref[...] * inv_ref[...]
    cos = jnp.cos(ang)
    sin = jnp.sin(ang)
    lane = lax.broadcasted_iota(jnp.int32, (PREP_TM, 128), 1)
    lo = (lane & (NSA_DH // 2)) == 0
    sin_signed = jnp.where(lo, -sin, sin)

    def rope(x):
        rot = jnp.where(lo, pltpu.roll(x, 128 - NSA_DH // 2, 1), pltpu.roll(x, NSA_DH // 2, 1))
        return x * cos + rot * sin_signed

    cols = lambda c0: p_ref[:, c0:c0 + 128].astype(F32)
    scale = NSA_DH ** -0.5 * LOG2E
    for cblk in range(NSA_WIDTH // 128):
        q_ref[:, cblk * 128:(cblk + 1) * 128] = (rope(cols(cblk * 128)) * scale).astype(BF16)
    kc_ref[...] = rope(cols(1024))
    vc_ref[...] = cols(1152)
    ks_ref[:, 0:128] = rope(cols(1280)).astype(BF16)
    blk = (pl.program_id(1) * PREP_TM + lax.broadcasted_iota(jnp.int32, (PREP_TM, 128), 0)) // SEL_LEN
    ks_ref[:, 128:256] = jnp.where(lane == blk, 1.0, 0.0).astype(BF16)
    kw_ref[...] = rope(cols(1536)).astype(BF16)
    ones = jnp.ones((NSA_VROWS - NSA_DH, PREP_TM), BF16)
    for v_ref, c0 in ((vst_ref, 1408), (vwt_ref, 1664)):
        vt = cols(c0).T.astype(BF16)
        for g in range(NSA_KV):
            v_ref[g * NSA_VROWS:g * NSA_VROWS + NSA_DH, :] = vt[g * NSA_DH:(g + 1) * NSA_DH]
            v_ref[g * NSA_VROWS + NSA_DH:(g + 1) * NSA_VROWS, :] = ones
    gt_ref[...] = jax.nn.sigmoid(cols(1792)).T[0:3 * NSA_HEADS, :]


def _nsa_prep(proj, pos_f, inv128):
    nt = SEQ // PREP_TM
    p3 = proj.reshape(BATCH, SEQ, PROJ_W)
    nat = lambda w: pl.BlockSpec((None, PREP_TM, w), lambda b, i: (b, i, 0))
    tr = lambda r: pl.BlockSpec((None, r, PREP_TM), lambda b, i: (b, 0, i))
    sds = jax.ShapeDtypeStruct
    return pl.pallas_call(
        _prep_body,
        grid=(BATCH, nt),
        in_specs=[pl.BlockSpec((None, PREP_TM, NSA_PROJ), lambda b, i: (b, i, PROJ_NSA_OFF // NSA_PROJ)),
                  nat(1), pl.BlockSpec((1, 128), lambda b, i: (0, 0))],
        out_specs=(nat(NSA_WIDTH), nat(128), nat(128), nat(256), nat(128),
                   tr(NSA_KV * NSA_VROWS), tr(NSA_KV * NSA_VROWS), tr(3 * NSA_HEADS)),
        out_shape=(sds((BATCH, SEQ, NSA_WIDTH), BF16),
                   sds((BATCH, SEQ, 128), F32),
                   sds((BATCH, SEQ, 128), F32),
                   sds((BATCH, SEQ, 256), BF16),
                   sds((BATCH, SEQ, 128), BF16),
                   sds((BATCH, NSA_KV * NSA_VROWS, SEQ), BF16),
                   sds((BATCH, NSA_KV * NSA_VROWS, SEQ), BF16),
                   sds((BATCH, 3 * NSA_HEADS, SEQ), F32)),
        compiler_params=_cparams("parallel", "parallel"),
        name="nsa_prep",
    )(p3, pos_f, inv128)


def _cmp_body(tk_ref, tv_ref, pe_ref, kw1_ref, kw2_ref, vw1_ref, vw2_ref, kc_ref, vct_ref, y1_s, y2_s):
    row = lax.broadcasted_iota(jnp.int32, (N_CMP_PAD, NSA_KV * CMP_HIDDEN), 0)

    def mlp(t_ref, w1_ref, w2_ref):
        y1_s[...] = jnp.zeros_like(y1_s)
        y2_s[...] = jnp.zeros_like(y2_s)
        for l in range(CMP_STRIDE):
            x = t_ref[pl.ds(l, N_CMP_PAD, stride=CMP_STRIDE), :]
            y1_s[...] += _dot((x + pe_ref[l:l + 1, :]).astype(BF16), w1_ref[l])
            y2_s[...] += _dot((x + pe_ref[CMP_STRIDE + l:CMP_STRIDE + l + 1, :]).astype(BF16),
                              w1_ref[CMP_STRIDE + l])
        hid = jnp.where(row < N_CMP, y1_s[...] + pltpu.roll(y2_s[...], N_CMP_PAD - 1, 0), 0.0)
        return _dot(_silu(hid).astype(BF16), w2_ref[...])

    kc_ref[...] = mlp(tk_ref, kw1_ref, kw2_ref).astype(BF16)
    y1_s[:, 0:128] = mlp(tv_ref, vw1_ref, vw2_ref)
    vt = y1_s[:, 0:128].T.astype(BF16)
    ones = jnp.ones((NSA_VROWS - NSA_DH, N_CMP_PAD), BF16)
    for g in range(NSA_KV):
        vct_ref[g * NSA_VROWS:g * NSA_VROWS + NSA_DH, :] = vt[g * NSA_DH:(g + 1) * NSA_DH]
        vct_ref[g * NSA_VROWS + NSA_DH:(g + 1) * NSA_VROWS, :] = ones


def _compress(tk, tv, pe2, kw1, kw2, vw1, vw2):
    seg = pl.BlockSpec((None, SEQ, 128), lambda b: (b, 0, 0))
    full2 = lambda a: pl.BlockSpec(a.shape, lambda b: (0,) * a.ndim)
    return pl.pallas_call(
        _cmp_body,
        grid=(BATCH,),
        in_specs=[seg, seg, full2(pe2), full2(kw1), full2(kw2), full2(vw1), full2(vw2)],
        out_specs=(pl.BlockSpec((None, N_CMP_PAD, 128), lambda b: (b, 0, 0)),
                   pl.BlockSpec((None, NSA_KV * NSA_VROWS, N_CMP_PAD), lambda b: (b, 0, 0))),
        out_shape=(jax.ShapeDtypeStruct((BATCH, N_CMP_PAD, 128), BF16),
                   jax.ShapeDtypeStruct((BATCH, NSA_KV * NSA_VROWS, N_CMP_PAD), BF16)),
        scratch_shapes=[pltpu.VMEM((N_CMP_PAD, NSA_KV * CMP_HIDDEN), F32)] * 2,
        compiler_params=_cparams("parallel"),
        name="nsa_compress",
    )(tk, tv, pe2, kw1, kw2, vw1, vw2)


NSA_NL = NSA_REP * Q_BLOCK
NSA_GH = 8
NSA_PW = NSA_GH * Q_BLOCK
NSA_NP = NSA_REP // NSA_GH


def _nsa_body(q_ref, kc_ref, vct_ref, ks_ref, vst_ref, kw_ref, vwt_ref, gt_ref, ovt_ref,
              o_ref, qa_s, sc_s, rk_s, s0_s, s1_s, sc_buf, sw_buf, sd_buf, m_s, acc_s):
    g = pl.program_id(1)
    qb = pl.program_id(2)
    q0 = pl.multiple_of(qb * Q_BLOCK, Q_BLOCK)
    is_g0 = g == 0

    qblk = q_ref[...].astype(F32)
    zero_slab = jnp.zeros((NSA_DH, Q_BLOCK), F32)
    for p in range(NSA_REP // 2):
        t = qblk[:, p * 128:(p + 1) * 128].T
        for hh in range(2):
            s = t[hh * NSA_DH:(hh + 1) * NSA_DH]
            r = 2 * p + hh
            qa_s[0:128, r * Q_BLOCK:(r + 1) * Q_BLOCK] = jnp.concatenate(
                [jnp.where(is_g0, s, zero_slab), jnp.where(is_g0, zero_slab, s)], axis=0).astype(BF16)
    qa_s[128 + N_SEL:256, :] = jnp.zeros((128 - N_SEL, NSA_NL), BF16)

    tq = q0 + lax.broadcasted_iota(jnp.int32, (1, Q_BLOCK), 1)

    def pair(p):
        return slice(p * NSA_PW, (p + 1) * NSA_PW)

    def mask_pair(s, valid):
        return jnp.concatenate([jnp.where(valid, s[:, r * Q_BLOCK:(r + 1) * Q_BLOCK], NEG)
                                for r in range(NSA_GH)], axis=1)

    n_i = lax.broadcasted_iota(jnp.int32, (N_CMP_PAD, Q_BLOCK), 0)
    valid_c = (n_i * CMP_STRIDE + (CMP_LEN - 1) <= tq) & (n_i < N_CMP)
    sc_buf[...] = mask_pair(_dot(kc_ref[...], qa_s[0:128, :]).astype(BF16), valid_c)

    w0 = pl.multiple_of(jnp.maximum(q0 - WINDOW, 0), Q_BLOCK)
    dpos = tq - (w0 + lax.broadcasted_iota(jnp.int32, (WIN_KEYS, Q_BLOCK), 0))
    sw_buf[...] = mask_pair(_dot(kw_ref[pl.ds(w0, WIN_KEYS), :], qa_s[0:128, :]).astype(BF16),
                            (dpos >= 0) & (dpos < WINDOW))

    sc = sc_buf[...]
    m_c = jnp.max(sc, axis=0, keepdims=True)
    e_c = jnp.exp2(sc - m_c)
    acc_c = _dot(vct_ref[...], e_c)
    inv_c = jnp.where(m_c.astype(F32) > 0.5 * NEG, 1.0 / jnp.maximum(acc_c[NSA_DH:NSA_DH + 1], 1e-30), 0.0)
    o_c = [acc_c[0:NSA_DH] * inv_c]
    imp_h = _dot(ovt_ref[...], e_c) * inv_c

    d_i = lax.broadcasted_iota(jnp.int32, (Q_BLOCK, Q_BLOCK), 0)
    t_i = lax.broadcasted_iota(jnp.int32, (Q_BLOCK, Q_BLOCK), 1)
    sd_buf[...] = mask_pair(_dot(ks_ref[pl.ds(q0, Q_BLOCK), 0:128], qa_s[0:128, :]).astype(BF16),
                            d_i <= t_i)

    sw = sw_buf[...]
    acc_w = _dot(vwt_ref[:, pl.ds(w0, WIN_KEYS)], jnp.exp2(sw - jnp.max(sw, axis=0, keepdims=True)))
    o_w = [acc_w[0:NSA_DH] / jnp.maximum(acc_w[NSA_DH:NSA_DH + 1], 1e-30)]

    s = sd_buf[...]
    m16 = jnp.max(s, axis=0, keepdims=True)
    m_s[...] = m16.astype(F32)
    acc_s[...] = _dot(vst_ref[:, pl.ds(q0, Q_BLOCK)], jnp.exp2(s - m16))

    imp = imp_h[:, 0:Q_BLOCK]
    for r in range(1, NSA_REP):
        imp = imp + imp_h[:, r * Q_BLOCK:(r + 1) * Q_BLOCK]
    j_i = lax.broadcasted_iota(jnp.int32, (N_SEL, Q_BLOCK), 0)
    cur = tq // SEL_LEN
    forced = (j_i == 0) | (j_i == cur) | (j_i == cur - 1)
    score = jnp.where(forced, jnp.inf, jnp.where(j_i > cur, -jnp.inf, imp))
    sc_s[...] = score
    rk_s[...] = jnp.zeros_like(rk_s)
    sub8 = lax.broadcasted_iota(jnp.int32, (8, Q_BLOCK), 0)
    for grp in range(N_SEL // 8):
        @pl.when(8 * grp <= 2 * qb + 1)
        def _():
            for v in range(N_SEL // 8):
                sv = sc_s[8 * v:8 * v + 8, :]
                part = jnp.zeros((8, Q_BLOCK), F32)
                for jp in range(8 * grp, 8 * grp + 8):
                    row = sc_s[jp:jp + 1, :]
                    if v > grp:
                        part = part + jnp.where(row >= sv, 1.0, 0.0)
                    elif v < grp:
                        part = part + jnp.where(row > sv, 1.0, 0.0)
                    else:
                        part = part + jnp.where(sub8 + 8 * v > jp, jnp.where(row >= sv, 1.0, 0.0),
                                                jnp.where(row > sv, 1.0, 0.0))
                rk_s[8 * v:8 * v + 8, :] += part
    bias = jnp.where((rk_s[...] < SEL_TOP) & (j_i < 2 * qb), 0.0, NEG).astype(BF16)
    for r in range(NSA_REP):
        qa_s[128:128 + N_SEL, r * Q_BLOCK:(r + 1) * Q_BLOCK] = bias

    last_sub = SEQ // SEL_SUB - 1

    def scores_into(buf, c):
        k0 = pl.multiple_of(jnp.minimum(c, last_sub) * SEL_SUB, SEL_SUB)
        buf[...] = _dot(ks_ref[pl.ds(k0, SEL_SUB), :], qa_s[...]).astype(BF16)

    def softmax_from(buf, c):
        k0 = pl.multiple_of(c * SEL_SUB, SEL_SUB)
        s = buf[...]
        m = m_s[...]
        m_new = jnp.maximum(m, jnp.max(s, axis=0, keepdims=True).astype(F32))
        m_s[...] = m_new
        pr = jnp.exp2(s - m_new.astype(BF16))
        acc_s[...] = acc_s[...] * jnp.exp2(m - m_new) + _dot(vst_ref[:, pl.ds(k0, SEL_SUB)], pr)

    scores_into(s0_s, 0)

    def sel_step(i, carry):
        c = 2 * i
        scores_into(s1_s, c + 1)
        softmax_from(s0_s, c)
        scores_into(s0_s, c + 2)
        softmax_from(s1_s, c + 1)
        return carry

    n_main = (qb * Q_BLOCK + 2 * SEL_SUB - 1) // (2 * SEL_SUB)
    lax.fori_loop(0, n_main, sel_step, 0)
    o_s = [acc_s[0:NSA_DH, :] / jnp.maximum(acc_s[NSA_DH:NSA_DH + 1, :], 1e-30)]

    gall = gt_ref[...]
    ggrp = jnp.where(is_g0, gall[0:3 * NSA_REP], gall[3 * NSA_REP:3 * NSA_HEADS])
    gate = [[ggrp[3 * r + br:3 * r + br + 1, :] for r in range(NSA_REP)] for br in range(3)]
    for p2 in range(NSA_REP // 2):
        halves = []
        for hh in range(2):
            r = 2 * p2 + hh
            p = r // NSA_GH
            sl = slice((r % NSA_GH) * Q_BLOCK, (r % NSA_GH + 1) * Q_BLOCK)
            halves.append(gate[0][r] * o_c[p][:, sl] + gate[1][r] * o_s[p][:, sl]
                          + gate[2][r] * o_w[p][:, sl])
        o_ref[:, p2 * 128:(p2 + 1) * 128] = jnp.concatenate(halves, axis=0).T.astype(BF16)


def _nsa_attn(q_r, kc, vct, ks, vst, kw, vwt, gt, ovt):
    per_b = lambda r, c: pl.BlockSpec((None, r, c), lambda b, g, i: (b, 0, 0))
    per_bg = lambda r, c: pl.BlockSpec((None, r, c), lambda b, g, i: (b, g, 0))
    const = lambda a: pl.BlockSpec(a.shape, lambda b, g, i: (0, 0))
    return pl.pallas_call(
        _nsa_body,
        grid=(BATCH, NSA_KV, N_QB),
        in_specs=[pl.BlockSpec((None, Q_BLOCK, NSA_REP * NSA_DH), lambda b, g, i: (b, i, g)),
                  per_b(N_CMP_PAD, 128), per_bg(NSA_VROWS, N_CMP_PAD),
                  per_b(SEQ, 256), per_bg(NSA_VROWS, SEQ),
                  per_b(SEQ, 128), per_bg(NSA_VROWS, SEQ),
                  pl.BlockSpec((None, 3 * NSA_HEADS, Q_BLOCK), lambda b, g, i: (b, 0, i)),
                  const(ovt)],
        out_specs=pl.BlockSpec((None, Q_BLOCK, NSA_REP * NSA_DH), lambda b, g, i: (b, i, g)),
        out_shape=jax.ShapeDtypeStruct((BATCH, SEQ, NSA_WIDTH), BF16),
        scratch_shapes=[pltpu.VMEM((256, NSA_NL), BF16),
                        pltpu.VMEM((N_SEL, Q_BLOCK), F32), pltpu.VMEM((N_SEL, Q_BLOCK), F32),
                        pltpu.VMEM((SEL_SUB, NSA_NL), BF16), pltpu.VMEM((SEL_SUB, NSA_NL), BF16),
                        pltpu.VMEM((N_CMP_PAD, NSA_NL), BF16), pltpu.VMEM((WIN_KEYS, NSA_NL), BF16),
                        pltpu.VMEM((Q_BLOCK, NSA_NL), BF16),
                        pltpu.VMEM((1, NSA_NL), F32), pltpu.VMEM((NSA_VROWS, NSA_NL), F32)],
        compiler_params=_cparams("parallel", "parallel", "arbitrary"),
        name="nsa_attn",
    )(q_r, kc, vct, ks, vst, kw, vwt, gt, ovt)


OUT_TM = 512


def _outx_body(x_ref, oh_ref, on_ref, w_ref, nw_ref, wq_ref, k_ref, v_ref, wo_ref, o_ref):
    y = (x_ref[...] + _dot(oh_ref[...], w_ref[0:HG_WIDTH, :])
         + _dot(on_ref[...], w_ref[HG_WIDTH:HG_WIDTH + NSA_WIDTH, :]))
    hx = _rms(y, nw_ref[...]).astype(BF16)
    q = (_dot(hx, wq_ref[...]) * (X_DH ** -0.5)).astype(BF16)
    heads = []
    for h in range(X_HEADS):
        sl = slice(h * X_DH, (h + 1) * X_DH)
        s = _dot_nt(q[:, sl], k_ref[:, sl])
        e = jnp.exp(s - jnp.max(s, axis=-1, keepdims=True))
        p = e / jnp.sum(e, axis=-1, keepdims=True)
        heads.append(_dot(p.astype(BF16), v_ref[:, sl]))
    o_ref[...] = y + _dot(jnp.concatenate(heads, axis=1).astype(BF16), wo_ref[...])


def _outproj_xattn(x1, o_hg, o_nsa, w_out, nw, wq, k, v, wo):
    width = X_HEADS * X_DH
    tiles_per_b = SEQ // OUT_TM
    row = lambda w: pl.BlockSpec((OUT_TM, w), lambda i: (i, 0))
    const = lambda r, c: pl.BlockSpec((r, c), lambda i: (0, 0))
    kv = pl.BlockSpec((None, MEM_LEN, width), lambda i: (i // tiles_per_b, 0, 0))
    return pl.pallas_call(
        _outx_body,
        grid=(TOKENS // OUT_TM,),
        in_specs=[row(D_MODEL), row(HG_WIDTH), row(NSA_WIDTH), const(D_MODEL, D_MODEL), const(1, D_MODEL),
                  const(D_MODEL, width), kv, kv, const(width, D_MODEL)],
        out_specs=row(D_MODEL),
        out_shape=jax.ShapeDtypeStruct((TOKENS, D_MODEL), F32),
        compiler_params=_cparams("parallel"),
        name="out_proj_xattn",
    )(x1, o_hg, o_nsa, w_out, nw, wq, k, v, wo)


def _memkv_body(m_ref, nw_ref, wk_ref, wv_ref, k_ref, v_ref):
    hm = _rms(m_ref[...], nw_ref[...]).astype(BF16)
    k_ref[...] = _dot(hm, wk_ref[...]).astype(BF16)
    v_ref[...] = _dot(hm, wv_ref[...]).astype(BF16)


def _memkv(mem, nw, wk, wv):
    width = X_HEADS * X_DH
    wspec = pl.BlockSpec((D_MODEL, width), lambda b: (0, 0))
    ospec = pl.BlockSpec((None, MEM_LEN, width), lambda b: (b, 0, 0))
    osh = jax.ShapeDtypeStruct((BATCH, MEM_LEN, width), BF16)
    return pl.pallas_call(
        _memkv_body,
        grid=(BATCH,),
        in_specs=[pl.BlockSpec((None, MEM_LEN, D_MODEL), lambda b: (b, 0, 0)),
                  pl.BlockSpec((1, D_MODEL), lambda b: (0, 0)), wspec, wspec],
        out_specs=(ospec, ospec), out_shape=(osh, osh),
        compiler_params=_cparams("parallel"),
        name="xattn_memkv",
    )(mem, nw, wk, wv)


def _overlap_t():
    c0 = np.arange(N_CMP)[:, None] * CMP_STRIDE
    s0 = np.arange(N_SEL)[None, :] * SEL_LEN
    ov = np.clip(np.minimum(c0 + CMP_LEN, s0 + SEL_LEN) - np.maximum(c0, s0), 0, None) / CMP_LEN
    out = np.zeros((N_SEL, N_CMP_PAD), np.float32)
    out[:, :N_CMP] = ov.T
    return out


def kernel(x, mem, positions, ffn1_norm, ffn1_w_gate, ffn1_w_up, ffn1_w_down, mix_norm, w_in, hgrn_lb_logits, hgrn_out_norm, nsa_cmp_pe, nsa_cmp_k_w1, nsa_cmp_k_w2, nsa_cmp_v_w1, nsa_cmp_v_w2, w_out, xattn_norm, mem_norm, xattn_wq, xattn_wk, xattn_wv, xattn_wo, ffn2_norm, ffn2_w_gate, ffn2_w_up, ffn2_w_down, final_norm):
    bf = lambda a: a.astype(BF16)
    vec = lambda a: a.reshape(1, -1).astype(F32)
    x2d = x.reshape(TOKENS, D_MODEL)

    x1, h_mix = _ffn(x2d, vec(ffn1_norm[0]), bf(ffn1_w_gate[0]), bf(ffn1_w_up[0]), bf(ffn1_w_down[0]),
                     vec(mix_norm[0]), final=False)

    proj, proj_f = _proj(h_mix, bf(jnp.pad(w_in[0].T, ((0, PROJ_W - D_IN), (0, 0)))))

    o_hg = _hgrn(proj, proj_f, hgrn_lb_logits.astype(F32), vec(hgrn_out_norm[0]))

    inv = ROPE_THETA ** (-jnp.arange(NSA_DH // 2, dtype=F32) / (NSA_DH // 2))
    inv128 = jnp.tile(inv, 128 // (NSA_DH // 2)).reshape(1, 128)
    pos_f = positions.astype(F32).reshape(BATCH, SEQ, 1)
    q_r, kc_tok, vc_tok, ks, kw, vst, vwt, gt = _nsa_prep(proj, pos_f, inv128)

    def over_groups(w):
        z = jnp.zeros_like(w)
        return bf(jnp.concatenate([jnp.concatenate([w, z], axis=-1), jnp.concatenate([z, w], axis=-1)], axis=-2))

    per_pos = lambda w1: over_groups(w1.reshape(CMP_LEN, NSA_DH, CMP_HIDDEN))
    pe = nsa_cmp_pe[0].astype(F32)
    kc, vct = _compress(kc_tok, vc_tok, jnp.concatenate([pe, pe], axis=1),
                        per_pos(nsa_cmp_k_w1[0]), over_groups(nsa_cmp_k_w2[0]),
                        per_pos(nsa_cmp_v_w1[0]), over_groups(nsa_cmp_v_w2[0]))
    o_nsa = _nsa_attn(q_r, kc, vct, ks, vst, kw, vwt, gt, jnp.asarray(_overlap_t(), dtype=BF16))

    km, vm = _memkv(mem, vec(mem_norm[0]), bf(xattn_wk[0]), bf(xattn_wv[0]))
    x3 = _outproj_xattn(x1, o_hg.reshape(TOKENS, HG_WIDTH), o_nsa.reshape(TOKENS, NSA_WIDTH),
                        bf(w_out[0]), vec(xattn_norm[0]), bf(xattn_wq[0]), km, vm, bf(xattn_wo[0]))

    out = _ffn(x3, vec(ffn2_norm[0]), bf(ffn2_w_gate[0]), bf(ffn2_w_up[0]), bf(ffn2_w_down[0]),
               vec(final_norm), final=True)
    return out.reshape(BATCH, SEQ, D_MODEL)
```

```python
import functools

import numpy as np
import jax
import jax.numpy as jnp
from jax import lax
from jax.experimental import pallas as pl
from jax.experimental.pallas import tpu as pltpu

F32 = jnp.float32
BF16 = jnp.bfloat16

D_MODEL = 2048
BATCH = 2
SEQ = 4096
TOKENS = BATCH * SEQ
RMS_EPS = 1e-6
ROPE_THETA = 10000.0
HG_WIDTH = 1024
HG_HEADS = 8
HG_D = 128
HG_CHUNK = 128
HG_LEVELS = (64, 32, 16, 8, 4, 2, 1)
NSA_WIDTH = 1024
NSA_DH = 64
NSA_HEADS = 16
NSA_KV = 2
NSA_REP = 8
NSA_VROWS = NSA_DH + 16
CMP_LEN = 32
CMP_STRIDE = 16
CMP_HIDDEN = 256
N_CMP = (SEQ - CMP_LEN) // CMP_STRIDE + 1
N_CMP_PAD = 256
SEL_LEN = 64
N_SEL = SEQ // SEL_LEN
SEL_TOP = 16
WINDOW = 512
Q_BLOCK = 128
N_QB = SEQ // Q_BLOCK
SEL_SUB = 256
WIN_KEYS = WINDOW + Q_BLOCK
MEM_LEN = 256
X_HEADS = 4
X_DH = 128
D_FF = 5632
IN_SIZES = (1024, 1024, 1024, 1024, 1024, 128, 128, 128, 128, 128, 128, 48)
D_IN = sum(IN_SIZES)
PROJ_NSA_OFF = sum(IN_SIZES[:4])
NSA_PROJ = 2048
PROJ_W = PROJ_NSA_OFF + NSA_PROJ
NEG = -1e30
LOG2E = 1.4426950408889634

V7X_VMEM_BYTES = 64 * 1024 * 1024
VMEM_LIMIT = V7X_VMEM_BYTES - 8 * 1024 * 1024


def _cparams(*sem, flags=None):
    return pltpu.CompilerParams(dimension_semantics=sem, vmem_limit_bytes=VMEM_LIMIT, flags=flags)


def _rms(x, w):
    return x * lax.rsqrt(jnp.mean(x * x, axis=-1, keepdims=True) + RMS_EPS) * w


def _silu(x):
    return x * jax.nn.sigmoid(x)


def _dot(a, b):
    return jnp.dot(a, b, preferred_element_type=F32)


def _dot_f32_by_01(sel, x):
    hi = x.astype(BF16)
    r1 = x - hi.astype(F32)
    mid = r1.astype(BF16)
    lo = (r1 - mid.astype(F32)).astype(BF16)
    n = x.shape[1]
    y = _dot(sel, jnp.concatenate([hi, mid, lo], axis=1))
    return y[:, 0:n] + y[:, n:2 * n] + y[:, 2 * n:3 * n]


def _dot_nt(a, b):
    return lax.dot_general(a, b, (((1,), (1,)), ((), ())), preferred_element_type=F32)


FFN_TM = 512
FFN_TF = 512


def _ffn_body(x_ref, nw_ref, wg_ref, wu_ref, wd_ref, nw2_ref, *rest, final):
    if final:
        o_ref, h_scr = rest
    else:
        o_ref, hn_ref, h_scr = rest
    j = pl.program_id(1)

    @pl.when(j == 0)
    def _():
        h_scr[...] = _rms(x_ref[...], nw_ref[...]).astype(BF16)
        o_ref[...] = jnp.zeros_like(o_ref)

    h = h_scr[...]
    g = _dot(h, wg_ref[...])
    u = _dot(h, wu_ref[...])
    a = (_silu(g) * u).astype(BF16)
    o_ref[...] += _dot(a, wd_ref[...])

    @pl.when(j == pl.num_programs(1) - 1)
    def _():
        y = x_ref[...] + 0.5 * o_ref[...]
        if final:
            o_ref[...] = _rms(y, nw2_ref[...])
        else:
            o_ref[...] = y
            hn_ref[...] = _rms(y, nw2_ref[...]).astype(BF16)


def _ffn(x, nw, wg, wu, wd, nw2, final):
    grid = (TOKENS // FFN_TM, D_FF // FFN_TF)
    row = pl.BlockSpec((FFN_TM, D_MODEL), lambda i, j: (i, 0))
    vec = pl.BlockSpec((1, D_MODEL), lambda i, j: (0, 0))
    in_specs = [row, vec,
                pl.BlockSpec((D_MODEL, FFN_TF), lambda i, j: (0, j)),
                pl.BlockSpec((D_MODEL, FFN_TF), lambda i, j: (0, j)),
                pl.BlockSpec((FFN_TF, D_MODEL), lambda i, j: (j, 0)),
                vec]
    if final:
        out_shape = jax.ShapeDtypeStruct((TOKENS, D_MODEL), F32)
        out_specs = row
    else:
        out_shape = (jax.ShapeDtypeStruct((TOKENS, D_MODEL), F32),
                     jax.ShapeDtypeStruct((TOKENS, D_MODEL), BF16))
        out_specs = (row, row)
    return pl.pallas_call(
        functools.partial(_ffn_body, final=final),
        grid=grid, in_specs=in_specs, out_specs=out_specs, out_shape=out_shape,
        scratch_shapes=[pltpu.VMEM((FFN_TM, D_MODEL), BF16)],
        compiler_params=_cparams("parallel", "arbitrary"),
        name="ffn_final" if final else "ffn",
    )(x, nw, wg, wu, wd, nw2)


PROJ_TM = 1024
PROJ_TN = 512


PROJ_F_TILE0 = IN_SIZES[0] // PROJ_TN
PROJ_F_TILES = IN_SIZES[1] // PROJ_TN


PROJ_FULL_TILES = D_IN // PROJ_TN


def _proj_body(a_ref, wt_ref, tail_ref, o_ref, f_ref):
    j = pl.program_id(1)
    w = jnp.where(j >= PROJ_FULL_TILES, tail_ref[...], wt_ref[...])
    y = _dot_nt(a_ref[...], w)
    o_ref[...] = y.astype(BF16)

    @pl.when((j >= PROJ_F_TILE0) & (j < PROJ_F_TILE0 + PROJ_F_TILES))
    def _():
        f_ref[...] = y


def _proj(a, wt, wt_tail):
    m, k = a.shape
    f_tile = lambda i, j: (i, jnp.clip(j - PROJ_F_TILE0, 0, PROJ_F_TILES - 1))
    return pl.pallas_call(
        _proj_body,
        grid=(m // PROJ_TM, PROJ_W // PROJ_TN),
        in_specs=[pl.BlockSpec((PROJ_TM, k), lambda i, j: (i, 0)),
                  pl.BlockSpec((PROJ_TN, k), lambda i, j: (jnp.minimum(j, PROJ_FULL_TILES - 1), 0)),
                  pl.BlockSpec((PROJ_TN, k), lambda i, j: (0, 0))],
        out_specs=(pl.BlockSpec((PROJ_TM, PROJ_TN), lambda i, j: (i, j)),
                   pl.BlockSpec((PROJ_TM, PROJ_TN), f_tile)),
        out_shape=(jax.ShapeDtypeStruct((m, PROJ_W), BF16),
                   jax.ShapeDtypeStruct((m, IN_SIZES[1]), F32)),
        compiler_params=_cparams("parallel", "arbitrary"),
        name="proj_in",
    )(a, wt, wt_tail)


HG_ROWS = 1024
HG_CUM = 256


def _hgrn_body(q_ref, f_ref, i_ref, g_ref, lbl_ref, nw_ref, o_ref, st_ref, k_s, b_s):
    c = pl.program_id(2)

    @pl.when(c == 0)
    def _():
        st_ref[...] = jnp.zeros_like(st_ref)

    l0 = lbl_ref[0:1, :]
    l1 = lbl_ref[1:2, :]
    lmax = jnp.maximum(l0, l1)
    e0 = jnp.exp(l0 - lmax)
    lb = e0 / (e0 + jnp.exp(l1 - lmax))

    C = HG_CHUNK
    f = lb + (1.0 - lb) * jax.nn.sigmoid(f_ref[...])
    k_s[...] = 1.0 - f
    r_i = lax.broadcasted_iota(jnp.int32, (HG_CUM, HG_CUM), 0)
    c_i = lax.broadcasted_iota(jnp.int32, (HG_CUM, HG_CUM), 1)
    tri = jnp.where((r_i >= c_i) & (r_i // C == c_i // C), 1.0, 0.0).astype(BF16)
    logf = jnp.log2(f)
    for r0 in range(0, HG_ROWS, HG_CUM):
        b_s[r0:r0 + HG_CUM, :] = _dot_f32_by_01(tri, logf[r0:r0 + HG_CUM])

    t_i = lax.broadcasted_iota(jnp.int32, (C, C), 0)
    s_i = lax.broadcasted_iota(jnp.int32, (C, C), 1)
    level_mask = [(t_i // (2 * w) == s_i // (2 * w)) & (t_i % (2 * w) >= w) & (s_i % (2 * w) < w)
                  for w in HG_LEVELS]
    sub_r = lax.broadcasted_iota(jnp.int32, (8, HG_D), 0)
    row_i = lax.broadcasted_iota(jnp.int32, (C, HG_D), 0)
    right_sign = {w: jnp.where(row_i % (2 * w) >= w, 1.0, -1.0) for w in HG_LEVELS if w < 8}

    def neg_abs_diff(w, r0, b):
        row = lambda r, n: jnp.broadcast_to(b_s[r0 + r:r0 + r + 1, :], (n, HG_D))
        if w >= 8:
            parts = []
            for p0 in range(0, C, 2 * w):
                ref = row(p0 + w - 1, w)
                parts += [ref - b[p0:p0 + w], b[p0 + w:p0 + 2 * w] - ref]
            return jnp.concatenate(parts, axis=0)
        if w == 4:
            bref = jnp.concatenate([row(p0 + 3, 8) for p0 in range(0, C, 8)], axis=0)
        elif w == 2:
            bref = jnp.concatenate([jnp.where(sub_r < 4, row(p0 + 1, 8), row(p0 + 5, 8))
                                    for p0 in range(0, C, 8)], axis=0)
        else:
            bref = jnp.where(row_i % 2 == 1, pltpu.roll(b, 1, 0), b)
        return (b - bref) * right_sign[w]

    chunks = [ci * C for ci in range(HG_ROWS // C)]
    rows = lambda ref, r0: ref[r0:r0 + C, :]
    att = [jnp.zeros((C, C), F32) for _ in chunks]
    q16 = [rows(q_ref, r0) for r0 in chunks]
    k16 = [rows(k_s, r0).astype(BF16) for r0 in chunks]
    for w, mask in zip(HG_LEVELS, level_mask):
        for n, r0 in enumerate(chunks):
            b = rows(b_s, r0)
            e = jnp.exp2(neg_abs_diff(w, r0, b)).astype(BF16)
            att[n] = jnp.where(mask, _dot_nt(q16[n] * e, k16[n] * e), att[n])
    o_intra = []
    for n, r0 in enumerate(chunks):
        q, k, v = q16[n].astype(F32), rows(k_s, r0), rows(i_ref, r0)
        o_intra.append(_dot(att[n].astype(BF16), v)
                       + jnp.sum(q * k, axis=-1, keepdims=True) * v.astype(F32))
    upd = []
    for r0 in chunks:
        bl = b_s[r0 + C - 1:r0 + C, :]
        kd = rows(k_s, r0) * jnp.exp2(bl - rows(b_s, r0))
        v_t = rows(i_ref, r0).astype(F32).T.astype(BF16)
        upd.append((jnp.exp2(bl), _dot(v_t, kd.astype(BF16))))
    st_t = st_ref[...]
    for n, r0 in enumerate(chunks):
        qe = (q16[n].astype(F32) * jnp.exp2(rows(b_s, r0))).astype(BF16)
        o = o_intra[n] + _dot_nt(qe, st_t.astype(BF16))
        st_t = st_t * upd[n][0] + upd[n][1]
        o = o * lax.rsqrt(jnp.mean(o * o, axis=-1, keepdims=True) + RMS_EPS)
        o_ref[r0:r0 + C, :] = (o * nw_ref[...] * _silu(rows(g_ref, r0).astype(F32))).astype(BF16)
    st_ref[...] = st_t


def _hgrn(proj, proj_f, lb_logits, norm_w):
    p3 = proj.reshape(BATCH, SEQ, PROJ_W)
    f3 = proj_f.reshape(BATCH, SEQ, HG_WIDTH)

    def col(off):
        return pl.BlockSpec((None, HG_ROWS, HG_D), lambda b, h, c: (b, c, off + h))

    return pl.pallas_call(
        _hgrn_body,
        grid=(BATCH, HG_HEADS, SEQ // HG_ROWS),
        in_specs=[col(0), col(0), col(2 * HG_HEADS), col(3 * HG_HEADS),
                  pl.BlockSpec((2, HG_D), lambda b, h, c: (0, h)),
                  pl.BlockSpec((1, HG_D), lambda b, h, c: (0, h))],
        out_specs=pl.BlockSpec((None, HG_ROWS, HG_D), lambda b, h, c: (b, c, h)),
        out_shape=jax.ShapeDtypeStruct((BATCH, SEQ, HG_WIDTH), BF16),
        scratch_shapes=[pltpu.VMEM((HG_D, HG_D), F32),
                        pltpu.VMEM((HG_ROWS, HG_D), F32),
                        pltpu.VMEM((HG_ROWS, HG_D), F32)],
        compiler_params=_cparams("parallel", "parallel", "arbitrary"),
        name="hgrn2",
    )(p3, f3, p3, p3, lb_logits, norm_w)


PREP_TM = 256


def _prep_body(p_ref, pos_ref, inv_ref, q_ref, kc_ref, vc_ref, ks_ref, kw_ref,
               vst_ref, vwt_ref, gt_ref):
    ang = pos_ref[...] * inv_ref[...]
    cos = jnp.cos(ang)
    sin = jnp.sin(ang)
    lane = lax.broadcasted_iota(jnp.int32, (PREP_TM, 128), 1)
    lo = (lane & (NSA_DH // 2)) == 0
    sin_signed = jnp.where(lo, -sin, sin)

    def rope(x):
        rot = jnp.where(lo, pltpu.roll(x, 128 - NSA_DH // 2, 1), pltpu.roll(x, NSA_DH // 2, 1))
        return x * cos + rot * sin_signed

    cols = lambda c0: p_ref[:, c0:c0 + 128].astype(F32)
    scale = NSA_DH ** -0.5 * LOG2E
    for cblk in range(NSA_WIDTH // 128):
        q_ref[:, cblk * 128:(cblk + 1) * 128] = (rope(cols(cblk * 128)) * scale).astype(BF16)
    kc_ref[...] = rope(cols(1024))
    vc_ref[...] = cols(1152)
    ks_ref[:, 0:128] = rope(cols(1280)).astype(BF16)
    blk = (pl.program_id(1) * PREP_TM + lax.broadcasted_iota(jnp.int32, (PREP_TM, 128), 0)) // SEL_LEN
    ks_ref[:, 128:256] = jnp.where(lane == blk, 1.0, 0.0).astype(BF16)
    kw_ref[...] = rope(cols(1536)).astype(BF16)
    ones = jnp.ones((NSA_VROWS - NSA_DH, PREP_TM), BF16)
    for v_ref, c0 in ((vst_ref, 1408), (vwt_ref, 1664)):
        vt = cols(c0).T.astype(BF16)
        for g in range(NSA_KV):
            v_ref[g * NSA_VROWS:g * NSA_VROWS + NSA_DH, :] = vt[g * NSA_DH:(g + 1) * NSA_DH]
            v_ref[g * NSA_VROWS + NSA_DH:(g + 1) * NSA_VROWS, :] = ones
    gt_ref[...] = jax.nn.sigmoid(cols(1792)).T[0:3 * NSA_HEADS, :]


def _nsa_prep(proj, pos_f, inv128):
    nt = SEQ // PREP_TM
    p3 = proj.reshape(BATCH, SEQ, PROJ_W)
    nat = lambda w: pl.BlockSpec((None, PREP_TM, w), lambda b, i: (b, i, 0))
    tr = lambda r: pl.BlockSpec((None, r, PREP_TM), lambda b, i: (b, 0, i))
    sds = jax.ShapeDtypeStruct
    return pl.pallas_call(
        _prep_body,
        grid=(BATCH, nt),
        in_specs=[pl.BlockSpec((None, PREP_TM, NSA_PROJ), lambda b, i: (b, i, PROJ_NSA_OFF // NSA_PROJ)),
                  nat(1), pl.BlockSpec((1, 128), lambda b, i: (0, 0))],
        out_specs=(nat(NSA_WIDTH), nat(128), nat(128), nat(256), nat(128),
                   tr(NSA_KV * NSA_VROWS), tr(NSA_KV * NSA_VROWS), tr(3 * NSA_HEADS)),
        out_shape=(sds((BATCH, SEQ, NSA_WIDTH), BF16),
                   sds((BATCH, SEQ, 128), F32),
                   sds((BATCH, SEQ, 128), F32),
                   sds((BATCH, SEQ, 256), BF16),
                   sds((BATCH, SEQ, 128), BF16),
                   sds((BATCH, NSA_KV * NSA_VROWS, SEQ), BF16),
                   sds((BATCH, NSA_KV * NSA_VROWS, SEQ), BF16),
                   sds((BATCH, 3 * NSA_HEADS, SEQ), F32)),
        compiler_params=_cparams("parallel", "parallel"),
        name="nsa_prep",
    )(p3, pos_f, inv128)


def _cmp_body(tk_ref, tv_ref, pe_ref, kw1_ref, kw2_ref, vw1_ref, vw2_ref, kc_ref, vct_ref, y1_s, y2_s):
    row = lax.broadcasted_iota(jnp.int32, (N_CMP_PAD, NSA_KV * CMP_HIDDEN), 0)

    def mlp(t_ref, w1_ref, w2_ref):
        y1_s[...] = jnp.zeros_like(y1_s)
        y2_s[...] = jnp.zeros_like(y2_s)
        for l in range(CMP_STRIDE):
            x = t_ref[pl.ds(l, N_CMP_PAD, stride=CMP_STRIDE), :]
            y1_s[...] += _dot((x + pe_ref[l:l + 1, :]).astype(BF16), w1_ref[l])
            y2_s[...] += _dot((x + pe_ref[CMP_STRIDE + l:CMP_STRIDE + l + 1, :]).astype(BF16),
                              w1_ref[CMP_STRIDE + l])
        hid = jnp.where(row < N_CMP, y1_s[...] + pltpu.roll(y2_s[...], N_CMP_PAD - 1, 0), 0.0)
        return _dot(_silu(hid).astype(BF16), w2_ref[...])

    kc_ref[...] = mlp(tk_ref, kw1_ref, kw2_ref).astype(BF16)
    y1_s[:, 0:128] = mlp(tv_ref, vw1_ref, vw2_ref)
    vt = y1_s[:, 0:128].T.astype(BF16)
    ones = jnp.ones((NSA_VROWS - NSA_DH, N_CMP_PAD), BF16)
    for g in range(NSA_KV):
        vct_ref[g * NSA_VROWS:g * NSA_VROWS + NSA_DH, :] = vt[g * NSA_DH:(g + 1) * NSA_DH]
        vct_ref[g * NSA_VROWS + NSA_DH:(g + 1) * NSA_VROWS, :] = ones


def _compress(tk, tv, pe2, kw1, kw2, vw1, vw2):
    seg = pl.BlockSpec((None, SEQ, 128), lambda b: (b, 0, 0))
    full2 = lambda a: pl.BlockSpec(a.shape, lambda b: (0,) * a.ndim)
    return pl.pallas_call(
        _cmp_body,
        grid=(BATCH,),
        in_specs=[seg, seg, full2(pe2), full2(kw1), full2(kw2), full2(vw1), full2(vw2)],
        out_specs=(pl.BlockSpec((None, N_CMP_PAD, 128), lambda b: (b, 0, 0)),
                   pl.BlockSpec((None, NSA_KV * NSA_VROWS, N_CMP_PAD), lambda b: (b, 0, 0))),
        out_shape=(jax.ShapeDtypeStruct((BATCH, N_CMP_PAD, 128), BF16),
                   jax.ShapeDtypeStruct((BATCH, NSA_KV * NSA_VROWS, N_CMP_PAD), BF16)),
        scratch_shapes=[pltpu.VMEM((N_CMP_PAD, NSA_KV * CMP_HIDDEN), F32)] * 2,
        compiler_params=_cparams("parallel"),
        name="nsa_compress",
    )(tk, tv, pe2, kw1, kw2, vw1, vw2)


NSA_NL = NSA_REP * Q_BLOCK


def _nsa_body(q_ref, kc_ref, vct_ref, ks_ref, vst_ref, kw_ref, vwt_ref, gt_ref, ovt_ref,
              o_ref, qa_s, sc_s, rk_s, s0_s, s1_s, sc_buf, sw_buf, sd_buf, m_s, acc_s):
    g = pl.program_id(1)
    qb = pl.program_id(2)
    q0 = pl.multiple_of(qb * Q_BLOCK, Q_BLOCK)
    is_g0 = g == 0

    qblk = q_ref[...].astype(F32)
    zero_slab = jnp.zeros((NSA_DH, Q_BLOCK), F32)
    for p in range(NSA_REP // 2):
        t = qblk[:, p * 128:(p + 1) * 128].T
        for hh in range(2):
            s = t[hh * NSA_DH:(hh + 1) * NSA_DH]
            r = 2 * p + hh
            qa_s[0:128, r * Q_BLOCK:(r + 1) * Q_BLOCK] = jnp.concatenate(
                [jnp.where(is_g0, s, zero_slab), jnp.where(is_g0, zero_slab, s)], axis=0).astype(BF16)
    qa_s[128 + N_SEL:256, :] = jnp.zeros((128 - N_SEL, NSA_NL), BF16)

    tq = q0 + lax.broadcasted_iota(jnp.int32, (1, Q_BLOCK), 1)

    def mask_pair(s, valid):
        return jnp.concatenate([jnp.where(valid, s[:, r * Q_BLOCK:(r + 1) * Q_BLOCK], NEG)
                                for r in range(NSA_REP)], axis=1)

    n_i = lax.broadcasted_iota(jnp.int32, (N_CMP_PAD, Q_BLOCK), 0)
    valid_c = (n_i * CMP_STRIDE + (CMP_LEN - 1) <= tq) & (n_i < N_CMP)
    sc_buf[...] = mask_pair(_dot(kc_ref[...], qa_s[0:128, :]).astype(BF16), valid_c)

    w0 = pl.multiple_of(jnp.maximum(q0 - WINDOW, 0), Q_BLOCK)
    dpos = tq - (w0 + lax.broadcasted_iota(jnp.int32, (WIN_KEYS, Q_BLOCK), 0))
    sw_buf[...] = mask_pair(_dot(kw_ref[pl.ds(w0, WIN_KEYS), :], qa_s[0:128, :]).astype(BF16),
                            (dpos >= 0) & (dpos < WINDOW))

    sc = sc_buf[...]
    m_c = jnp.max(sc, axis=0, keepdims=True)
    e_c = jnp.exp2(sc - m_c)
    acc_c = _dot(vct_ref[...], e_c)
    inv_c = jnp.where(m_c.astype(F32) > 0.5 * NEG, 1.0 / jnp.maximum(acc_c[NSA_DH:NSA_DH + 1], 1e-30), 0.0)
    imp_h = _dot(ovt_ref[...], e_c) * inv_c

    d_i = lax.broadcasted_iota(jnp.int32, (Q_BLOCK, Q_BLOCK), 0)
    t_i = lax.broadcasted_iota(jnp.int32, (Q_BLOCK, Q_BLOCK), 1)
    sd_buf[...] = mask_pair(_dot(ks_ref[pl.ds(q0, Q_BLOCK), 0:128], qa_s[0:128, :]).astype(BF16),
                            d_i <= t_i)

    sw = sw_buf[...]
    acc_w = _dot(vwt_ref[:, pl.ds(w0, WIN_KEYS)], jnp.exp2(sw - jnp.max(sw, axis=0, keepdims=True)))
    inv_w = 1.0 / jnp.maximum(acc_w[NSA_DH:NSA_DH + 1], 1e-30)

    s = sd_buf[...]
    m16 = jnp.max(s, axis=0, keepdims=True)
    m_s[...] = m16.astype(F32)
    acc_s[...] = _dot(vst_ref[:, pl.ds(q0, Q_BLOCK)], jnp.exp2(s - m16))

    imp = imp_h[:, 0:Q_BLOCK]
    for r in range(1, NSA_REP):
        imp = imp + imp_h[:, r * Q_BLOCK:(r + 1) * Q_BLOCK]
    j_i = lax.broadcasted_iota(jnp.int32, (N_SEL, Q_BLOCK), 0)
    cur = tq // SEL_LEN
    forced = (j_i == 0) | (j_i == cur) | (j_i == cur - 1)
    score = jnp.where(forced, jnp.inf, jnp.where(j_i > cur, -jnp.inf, imp))
    sc_s[...] = score
    rk_s[...] = jnp.zeros_like(rk_s)
    sub8 = lax.broadcasted_iota(jnp.int32, (8, Q_BLOCK), 0)
    for grp in range(N_SEL // 8):
        @pl.when(8 * grp <= 2 * qb + 1)
        def _():
            for v in range(N_SEL // 8):
                sv = sc_s[8 * v:8 * v + 8, :]
                part = jnp.zeros((8, Q_BLOCK), F32)
                for jp in range(8 * grp, 8 * grp + 8):
                    row = sc_s[jp:jp + 1, :]
                    if v > grp:
                        part = part + jnp.where(row >= sv, 1.0, 0.0)
                    elif v < grp:
                        part = part + jnp.where(row > sv, 1.0, 0.0)
                    else:
                        part = part + jnp.where(sub8 + 8 * v > jp, jnp.where(row >= sv, 1.0, 0.0),
                                                jnp.where(row > sv, 1.0, 0.0))
                rk_s[8 * v:8 * v + 8, :] += part
    bias = jnp.where((rk_s[...] < SEL_TOP) & (j_i < 2 * qb), 0.0, NEG).astype(BF16)
    for r in range(NSA_REP):
        qa_s[128:128 + N_SEL, r * Q_BLOCK:(r + 1) * Q_BLOCK] = bias

    last_sub = SEQ // SEL_SUB - 1

    def scores_into(buf, c):
        k0 = pl.multiple_of(jnp.minimum(c, last_sub) * SEL_SUB, SEL_SUB)
        buf[...] = _dot(ks_ref[pl.ds(k0, SEL_SUB), :], qa_s[...]).astype(BF16)

    def softmax_from(buf, c):
        k0 = pl.multiple_of(c * SEL_SUB, SEL_SUB)
        s = buf[...]
        m = m_s[...]
        m_new = jnp.maximum(m, jnp.max(s, axis=0, keepdims=True).astype(F32))
        m_s[...] = m_new
        pr = jnp.exp2(s - m_new.astype(BF16))
        acc_s[...] = acc_s[...] * jnp.exp2(m - m_new) + _dot(vst_ref[:, pl.ds(k0, SEL_SUB)], pr)

    scores_into(s0_s, 0)

    def sel_step(i, carry):
        c = 2 * i
        scores_into(s1_s, c + 1)
        softmax_from(s0_s, c)
        scores_into(s0_s, c + 2)
        softmax_from(s1_s, c + 1)
        return carry

    n_main = (qb * Q_BLOCK + 2 * SEL_SUB - 1) // (2 * SEL_SUB)
    lax.fori_loop(0, n_main, sel_step, 0)
    inv_s = 1.0 / jnp.maximum(acc_s[NSA_DH:NSA_DH + 1, :], 1e-30)

    gall = gt_ref[...]
    ggrp = jnp.where(is_g0, gall[0:3 * NSA_REP], gall[3 * NSA_REP:3 * NSA_HEADS])
    for p2 in range(NSA_REP // 2):
        halves = []
        for hh in range(2):
            r = 2 * p2 + hh
            sl = slice(r * Q_BLOCK, (r + 1) * Q_BLOCK)
            g_c, g_s, g_w = (ggrp[3 * r + br:3 * r + br + 1, :] for br in range(3))
            halves.append((g_c * inv_c[:, sl]) * acc_c[0:NSA_DH, sl]
                          + (g_s * inv_s[:, sl]) * acc_s[0:NSA_DH, sl]
                          + (g_w * inv_w[:, sl]) * acc_w[0:NSA_DH, sl])
        o_ref[:, p2 * 128:(p2 + 1) * 128] = jnp.concatenate(halves, axis=0).T.astype(BF16)


def _nsa_attn(q_r, kc, vct, ks, vst, kw, vwt, gt, ovt):
    per_b = lambda r, c: pl.BlockSpec((None, r, c), lambda b, g, i: (b, 0, 0))
    per_bg = lambda r, c: pl.BlockSpec((None, r, c), lambda b, g, i: (b, g, 0))
    const = lambda a: pl.BlockSpec(a.shape, lambda b, g, i: (0, 0))
    return pl.pallas_call(
        _nsa_body,
        grid=(BATCH, NSA_KV, N_QB),
        in_specs=[pl.BlockSpec((None, Q_BLOCK, NSA_REP * NSA_DH), lambda b, g, i: (b, i, g)),
                  per_b(N_CMP_PAD, 128), per_bg(NSA_VROWS, N_CMP_PAD),
                  per_b(SEQ, 256), per_bg(NSA_VROWS, SEQ),
                  per_b(SEQ, 128), per_bg(NSA_VROWS, SEQ),
                  pl.BlockSpec((None, 3 * NSA_HEADS, Q_BLOCK), lambda b, g, i: (b, 0, i)),
                  const(ovt)],
        out_specs=pl.BlockSpec((None, Q_BLOCK, NSA_REP * NSA_DH), lambda b, g, i: (b, i, g)),
        out_shape=jax.ShapeDtypeStruct((BATCH, SEQ, NSA_WIDTH), BF16),
        scratch_shapes=[pltpu.VMEM((256, NSA_NL), BF16),
                        pltpu.VMEM((N_SEL, Q_BLOCK), F32), pltpu.VMEM((N_SEL, Q_BLOCK), F32),
                        pltpu.VMEM((SEL_SUB, NSA_NL), BF16), pltpu.VMEM((SEL_SUB, NSA_NL), BF16),
                        pltpu.VMEM((N_CMP_PAD, NSA_NL), BF16), pltpu.VMEM((WIN_KEYS, NSA_NL), BF16),
                        pltpu.VMEM((Q_BLOCK, NSA_NL), BF16),
                        pltpu.VMEM((1, NSA_NL), F32), pltpu.VMEM((NSA_VROWS, NSA_NL), F32)],
        compiler_params=_cparams("parallel", "parallel", "arbitrary"),
        name="nsa_attn",
    )(q_r, kc, vct, ks, vst, kw, vwt, gt, ovt)


OUT_TM = 512


def _outx_body(x_ref, oh_ref, on_ref, w_ref, nw_ref, wq_ref, k_ref, v_ref, wo_ref, o_ref):
    y = (x_ref[...] + _dot(oh_ref[...], w_ref[0:HG_WIDTH, :])
         + _dot(on_ref[...], w_ref[HG_WIDTH:HG_WIDTH + NSA_WIDTH, :]))
    hx = _rms(y, nw_ref[...]).astype(BF16)
    q = (_dot(hx, wq_ref[...]) * (X_DH ** -0.5)).astype(BF16)
    heads = []
    for h in range(X_HEADS):
        sl = slice(h * X_DH, (h + 1) * X_DH)
        s = _dot_nt(q[:, sl], k_ref[:, sl])
        e = jnp.exp(s - jnp.max(s, axis=-1, keepdims=True))
        p = e / jnp.sum(e, axis=-1, keepdims=True)
        heads.append(_dot(p.astype(BF16), v_ref[:, sl]))
    o_ref[...] = y + _dot(jnp.concatenate(heads, axis=1).astype(BF16), wo_ref[...])


def _outproj_xattn(x1, o_hg, o_nsa, w_out, nw, wq, k, v, wo):
    width = X_HEADS * X_DH
    tiles_per_b = SEQ // OUT_TM
    row = lambda w: pl.BlockSpec((OUT_TM, w), lambda i: (i, 0))
    const = lambda r, c: pl.BlockSpec((r, c), lambda i: (0, 0))
    kv = pl.BlockSpec((None, MEM_LEN, width), lambda i: (i // tiles_per_b, 0, 0))
    return pl.pallas_call(
        _outx_body,
        grid=(TOKENS // OUT_TM,),
        in_specs=[row(D_MODEL), row(HG_WIDTH), row(NSA_WIDTH), const(D_MODEL, D_MODEL), const(1, D_MODEL),
                  const(D_MODEL, width), kv, kv, const(width, D_MODEL)],
        out_specs=row(D_MODEL),
        out_shape=jax.ShapeDtypeStruct((TOKENS, D_MODEL), F32),
        compiler_params=_cparams("parallel"),
        name="out_proj_xattn",
    )(x1, o_hg, o_nsa, w_out, nw, wq, k, v, wo)


def _memkv_body(m_ref, nw_ref, wk_ref, wv_ref, k_ref, v_ref):
    hm = _rms(m_ref[...], nw_ref[...]).astype(BF16)
    k_ref[...] = _dot(hm, wk_ref[...]).astype(BF16)
    v_ref[...] = _dot(hm, wv_ref[...]).astype(BF16)


def _memkv(mem, nw, wk, wv):
    width = X_HEADS * X_DH
    wspec = pl.BlockSpec((D_MODEL, width), lambda b: (0, 0))
    ospec = pl.BlockSpec((None, MEM_LEN, width), lambda b: (b, 0, 0))
    osh = jax.ShapeDtypeStruct((BATCH, MEM_LEN, width), BF16)
    return pl.pallas_call(
        _memkv_body,
        grid=(BATCH,),
        in_specs=[pl.BlockSpec((None, MEM_LEN, D_MODEL), lambda b: (b, 0, 0)),
                  pl.BlockSpec((1, D_MODEL), lambda b: (0, 0)), wspec, wspec],
        out_specs=(ospec, ospec), out_shape=(osh, osh),
        compiler_params=_cparams("parallel"),
        name="xattn_memkv",
    )(mem, nw, wk, wv)


def _overlap_t():
    c0 = np.arange(N_CMP)[:, None] * CMP_STRIDE
    s0 = np.arange(N_SEL)[None, :] * SEL_LEN
    ov = np.clip(np.minimum(c0 + CMP_LEN, s0 + SEL_LEN) - np.maximum(c0, s0), 0, None) / CMP_LEN
    out = np.zeros((N_SEL, N_CMP_PAD), np.float32)
    out[:, :N_CMP] = ov.T
    return out


def kernel(x, mem, positions, ffn1_norm, ffn1_w_gate, ffn1_w_up, ffn1_w_down, mix_norm, w_in, hgrn_lb_logits, hgrn_out_norm, nsa_cmp_pe, nsa_cmp_k_w1, nsa_cmp_k_w2, nsa_cmp_v_w1, nsa_cmp_v_w2, w_out, xattn_norm, mem_norm, xattn_wq, xattn_wk, xattn_wv, xattn_wo, ffn2_norm, ffn2_w_gate, ffn2_w_up, ffn2_w_down, final_norm):
    bf = lambda a: a.astype(BF16)
    vec = lambda a: a.reshape(1, -1).astype(F32)
    x2d = x.reshape(TOKENS, D_MODEL)

    x1, h_mix = _ffn(x2d, vec(ffn1_norm[0]), bf(ffn1_w_gate[0]), bf(ffn1_w_up[0]), bf(ffn1_w_down[0]),
                     vec(mix_norm[0]), final=False)

    w_t = bf(w_in[0].T)
    w_tail = jnp.pad(w_t[PROJ_FULL_TILES * PROJ_TN:], ((0, PROJ_W - D_IN), (0, 0)))
    proj, proj_f = _proj(h_mix, w_t, w_tail)

    o_hg = _hgrn(proj, proj_f, hgrn_lb_logits.astype(F32), vec(hgrn_out_norm[0]))

    inv = ROPE_THETA ** (-jnp.arange(NSA_DH // 2, dtype=F32) / (NSA_DH // 2))
    inv128 = jnp.tile(inv, 128 // (NSA_DH // 2)).reshape(1, 128)
    pos_f = positions.astype(F32).reshape(BATCH, SEQ, 1)
    q_r, kc_tok, vc_tok, ks, kw, vst, vwt, gt = _nsa_prep(proj, pos_f, inv128)

    def over_groups(w):
        z = jnp.zeros_like(w)
        return bf(jnp.concatenate([jnp.concatenate([w, z], axis=-1), jnp.concatenate([z, w], axis=-1)], axis=-2))

    per_pos = lambda w1: over_groups(w1.reshape(CMP_LEN, NSA_DH, CMP_HIDDEN))
    pe = nsa_cmp_pe[0].astype(F32)
    kc, vct = _compress(kc_tok, vc_tok, jnp.concatenate([pe, pe], axis=1),
                        per_pos(nsa_cmp_k_w1[0]), over_groups(nsa_cmp_k_w2[0]),
                        per_pos(nsa_cmp_v_w1[0]), over_groups(nsa_cmp_v_w2[0]))
    o_nsa = _nsa_attn(q_r, kc, vct, ks, vst, kw, vwt, gt, jnp.asarray(_overlap_t(), dtype=BF16))

    km, vm = _memkv(mem, vec(mem_norm[0]), bf(xattn_wk[0]), bf(xattn_wv[0]))
    x3 = _outproj_xattn(x1, o_hg.reshape(TOKENS, HG_WIDTH), o_nsa.reshape(TOKENS, NSA_WIDTH),
                        bf(w_out[0]), vec(xattn_norm[0]), bf(xattn_wq[0]), km, vm, bf(xattn_wo[0]))

    out = _ffn(x3, vec(ffn2_norm[0]), bf(ffn2_w_gate[0]), bf(ffn2_w_up[0]), bf(ffn2_w_down[0]),
               vec(final_norm), final=True)
    return out.reshape(BATCH, SEQ, D_MODEL)
```

```python
import functools

import numpy as np
import jax
import jax.numpy as jnp
from jax import lax
from jax.experimental import pallas as pl
from jax.experimental.pallas import tpu as pltpu

F32 = jnp.float32
BF16 = jnp.bfloat16

D_MODEL = 2048
BATCH = 2
SEQ = 4096
TOKENS = BATCH * SEQ
RMS_EPS = 1e-6
ROPE_THETA = 10000.0
HG_WIDTH = 1024
HG_HEADS = 8
HG_D = 128
HG_CHUNK = 128
HG_LEVELS = (64, 32, 16, 8, 4, 2, 1)
NSA_WIDTH = 1024
NSA_DH = 64
NSA_HEADS = 16
NSA_KV = 2
NSA_REP = 8
NSA_VROWS = NSA_DH + 16
CMP_LEN = 32
CMP_STRIDE = 16
CMP_HIDDEN = 256
N_CMP = (SEQ - CMP_LEN) // CMP_STRIDE + 1
N_CMP_PAD = 256
SEL_LEN = 64
N_SEL = SEQ // SEL_LEN
SEL_TOP = 16
WINDOW = 512
Q_BLOCK = 128
N_QB = SEQ // Q_BLOCK
SEL_SUB = 256
WIN_KEYS = WINDOW + Q_BLOCK
MEM_LEN = 256
X_HEADS = 4
X_DH = 128
D_FF = 5632
IN_SIZES = (1024, 1024, 1024, 1024, 1024, 128, 128, 128, 128, 128, 128, 48)
D_IN = sum(IN_SIZES)
PROJ_NSA_OFF = sum(IN_SIZES[:4])
NSA_PROJ = 2048
PROJ_W = PROJ_NSA_OFF + NSA_PROJ
NEG = -1e30
LOG2E = 1.4426950408889634

V7X_VMEM_BYTES = 64 * 1024 * 1024
VMEM_LIMIT = V7X_VMEM_BYTES - 8 * 1024 * 1024


def _cparams(*sem, flags=None):
    return pltpu.CompilerParams(dimension_semantics=sem, vmem_limit_bytes=VMEM_LIMIT, flags=flags)


def _rms(x, w):
    return x * lax.rsqrt(jnp.mean(x * x, axis=-1, keepdims=True) + RMS_EPS) * w


def _silu(x):
    return x * jax.nn.sigmoid(x)


def _dot(a, b):
    return jnp.dot(a, b, preferred_element_type=F32)


def _dot_f32_by_01(sel, x):
    hi = x.astype(BF16)
    r1 = x - hi.astype(F32)
    mid = r1.astype(BF16)
    lo = (r1 - mid.astype(F32)).astype(BF16)
    n = x.shape[1]
    y = _dot(sel, jnp.concatenate([hi, mid, lo], axis=1))
    return y[:, 0:n] + y[:, n:2 * n] + y[:, 2 * n:3 * n]


def _dot_nt(a, b):
    return lax.dot_general(a, b, (((1,), (1,)), ((), ())), preferred_element_type=F32)


FFN_TM = 512
FFN_TF = 512


def _ffn_body(x_ref, nw_ref, wg_ref, wu_ref, wd_ref, nw2_ref, *rest, final):
    if final:
        o_ref, h_scr = rest
    else:
        o_ref, hn_ref, h_scr = rest
    j = pl.program_id(1)
    last = pl.num_programs(1) - 1

    def swiglu_tile(h):
        g = _dot(h, wg_ref[...])
        u = _dot(h, wu_ref[...])
        return _dot((_silu(g) * u).astype(BF16), wd_ref[...])

    @pl.when(j == 0)
    def _():
        h = _rms(x_ref[...], nw_ref[...]).astype(BF16)
        h_scr[...] = h
        o_ref[...] = swiglu_tile(h)

    @pl.when((j > 0) & (j < last))
    def _():
        o_ref[...] += swiglu_tile(h_scr[...])

    @pl.when(j == last)
    def _():
        y = x_ref[...] + 0.5 * (o_ref[...] + swiglu_tile(h_scr[...]))
        if final:
            o_ref[...] = _rms(y, nw2_ref[...])
        else:
            o_ref[...] = y
            hn_ref[...] = _rms(y, nw2_ref[...]).astype(BF16)


def _ffn(x, nw, wg, wu, wd, nw2, final):
    grid = (TOKENS // FFN_TM, D_FF // FFN_TF)
    row = pl.BlockSpec((FFN_TM, D_MODEL), lambda i, j: (i, 0))
    vec = pl.BlockSpec((1, D_MODEL), lambda i, j: (0, 0))
    in_specs = [row, vec,
                pl.BlockSpec((D_MODEL, FFN_TF), lambda i, j: (0, j)),
                pl.BlockSpec((D_MODEL, FFN_TF), lambda i, j: (0, j)),
                pl.BlockSpec((FFN_TF, D_MODEL), lambda i, j: (j, 0)),
                vec]
    if final:
        out_shape = jax.ShapeDtypeStruct((TOKENS, D_MODEL), F32)
        out_specs = row
    else:
        out_shape = (jax.ShapeDtypeStruct((TOKENS, D_MODEL), F32),
                     jax.ShapeDtypeStruct((TOKENS, D_MODEL), BF16))
        out_specs = (row, row)
    return pl.pallas_call(
        functools.partial(_ffn_body, final=final),
        grid=grid, in_specs=in_specs, out_specs=out_specs, out_shape=out_shape,
        scratch_shapes=[pltpu.VMEM((FFN_TM, D_MODEL), BF16)],
        compiler_params=_cparams("parallel", "arbitrary"),
        name="ffn_final" if final else "ffn",
    )(x, nw, wg, wu, wd, nw2)


PROJ_TM = 1024
PROJ_TN = 512


PROJ_F_TILE0 = IN_SIZES[0] // PROJ_TN
PROJ_F_TILES = IN_SIZES[1] // PROJ_TN


PROJ_FULL_TILES = D_IN // PROJ_TN


def _proj_body(a_ref, wt_ref, tail_ref, o_ref, f_ref):
    j = pl.program_id(1)
    w = jnp.where(j >= PROJ_FULL_TILES, tail_ref[...], wt_ref[...])
    y = _dot_nt(a_ref[...], w)
    o_ref[...] = y.astype(BF16)

    @pl.when((j >= PROJ_F_TILE0) & (j < PROJ_F_TILE0 + PROJ_F_TILES))
    def _():
        f_ref[...] = y


def _proj(a, wt, wt_tail):
    m, k = a.shape
    f_tile = lambda i, j: (i, jnp.clip(j - PROJ_F_TILE0, 0, PROJ_F_TILES - 1))
    return pl.pallas_call(
        _proj_body,
        grid=(m // PROJ_TM, PROJ_W // PROJ_TN),
        in_specs=[pl.BlockSpec((PROJ_TM, k), lambda i, j: (i, 0)),
                  pl.BlockSpec((PROJ_TN, k), lambda i, j: (jnp.minimum(j, PROJ_FULL_TILES - 1), 0)),
                  pl.BlockSpec((PROJ_TN, k), lambda i, j: (0, 0))],
        out_specs=(pl.BlockSpec((PROJ_TM, PROJ_TN), lambda i, j: (i, j)),
                   pl.BlockSpec((PROJ_TM, PROJ_TN), f_tile)),
        out_shape=(jax.ShapeDtypeStruct((m, PROJ_W), BF16),
                   jax.ShapeDtypeStruct((m, IN_SIZES[1]), F32)),
        compiler_params=_cparams("parallel", "arbitrary"),
        name="proj_in",
    )(a, wt, wt_tail)


HG_ROWS = 1024
HG_CUM = 256


def _hgrn_body(q_ref, f_ref, i_ref, g_ref, lbl_ref, nw_ref, o_ref, st_ref, k_s, b_s):
    c = pl.program_id(2)

    @pl.when(c == 0)
    def _():
        st_ref[...] = jnp.zeros_like(st_ref)

    l0 = lbl_ref[0:1, :]
    l1 = lbl_ref[1:2, :]
    lmax = jnp.maximum(l0, l1)
    e0 = jnp.exp(l0 - lmax)
    lb = e0 / (e0 + jnp.exp(l1 - lmax))

    C = HG_CHUNK
    f = lb + (1.0 - lb) * jax.nn.sigmoid(f_ref[...])
    k_s[...] = 1.0 - f
    r_i = lax.broadcasted_iota(jnp.int32, (HG_CUM, HG_CUM), 0)
    c_i = lax.broadcasted_iota(jnp.int32, (HG_CUM, HG_CUM), 1)
    tri = jnp.where((r_i >= c_i) & (r_i // C == c_i // C), 1.0, 0.0).astype(BF16)
    logf = jnp.log2(f)
    for r0 in range(0, HG_ROWS, HG_CUM):
        b_s[r0:r0 + HG_CUM, :] = _dot_f32_by_01(tri, logf[r0:r0 + HG_CUM])

    t_i = lax.broadcasted_iota(jnp.int32, (C, C), 0)
    s_i = lax.broadcasted_iota(jnp.int32, (C, C), 1)
    level_mask = [(t_i // (2 * w) == s_i // (2 * w)) & (t_i % (2 * w) >= w) & (s_i % (2 * w) < w)
                  for w in HG_LEVELS]
    sub_r = lax.broadcasted_iota(jnp.int32, (8, HG_D), 0)
    row_i = lax.broadcasted_iota(jnp.int32, (C, HG_D), 0)
    right_sign = {w: jnp.where(row_i % (2 * w) >= w, 1.0, -1.0) for w in HG_LEVELS if w < 8}

    def neg_abs_diff(w, r0, b):
        row = lambda r, n: jnp.broadcast_to(b_s[r0 + r:r0 + r + 1, :], (n, HG_D))
        if w >= 8:
            parts = []
            for p0 in range(0, C, 2 * w):
                ref = row(p0 + w - 1, w)
                parts += [ref - b[p0:p0 + w], b[p0 + w:p0 + 2 * w] - ref]
            return jnp.concatenate(parts, axis=0)
        if w == 4:
            bref = jnp.concatenate([row(p0 + 3, 8) for p0 in range(0, C, 8)], axis=0)
        elif w == 2:
            bref = jnp.concatenate([jnp.where(sub_r < 4, row(p0 + 1, 8), row(p0 + 5, 8))
                                    for p0 in range(0, C, 8)], axis=0)
        else:
            bref = jnp.where(row_i % 2 == 1, pltpu.roll(b, 1, 0), b)
        return (b - bref) * right_sign[w]

    chunks = [ci * C for ci in range(HG_ROWS // C)]
    rows = lambda ref, r0: ref[r0:r0 + C, :]
    att = [jnp.zeros((C, C), F32) for _ in chunks]
    q16 = [rows(q_ref, r0) for r0 in chunks]
    k16 = [rows(k_s, r0).astype(BF16) for r0 in chunks]
    for w, mask in zip(HG_LEVELS, level_mask):
        for n, r0 in enumerate(chunks):
            b = rows(b_s, r0)
            e = jnp.exp2(neg_abs_diff(w, r0, b)).astype(BF16)
            att[n] = jnp.where(mask, _dot_nt(q16[n] * e, k16[n] * e), att[n])
    o_intra = []
    for n, r0 in enumerate(chunks):
        q, k, v = q16[n].astype(F32), rows(k_s, r0), rows(i_ref, r0)
        o_intra.append(_dot(att[n].astype(BF16), v)
                       + jnp.sum(q * k, axis=-1, keepdims=True) * v.astype(F32))
    upd = []
    for r0 in chunks:
        bl = b_s[r0 + C - 1:r0 + C, :]
        kd = rows(k_s, r0) * jnp.exp2(bl - rows(b_s, r0))
        v_t = rows(i_ref, r0).astype(F32).T.astype(BF16)
        upd.append((jnp.exp2(bl), _dot(v_t, kd.astype(BF16))))
    st_t = st_ref[...]
    for n, r0 in enumerate(chunks):
        qe = (q16[n].astype(F32) * jnp.exp2(rows(b_s, r0))).astype(BF16)
        o = o_intra[n] + _dot_nt(qe, st_t.astype(BF16))
        st_t = st_t * upd[n][0] + upd[n][1]
        o = o * lax.rsqrt(jnp.mean(o * o, axis=-1, keepdims=True) + RMS_EPS)
        o_ref[r0:r0 + C, :] = (o * nw_ref[...] * _silu(rows(g_ref, r0).astype(F32))).astype(BF16)
    st_ref[...] = st_t


def _hgrn(proj, proj_f, lb_logits, norm_w):
    p3 = proj.reshape(BATCH, SEQ, PROJ_W)
    f3 = proj_f.reshape(BATCH, SEQ, HG_WIDTH)

    def col(off):
        return pl.BlockSpec((None, HG_ROWS, HG_D), lambda b, h, c: (b, c, off + h))

    return pl.pallas_call(
        _hgrn_body,
        grid=(BATCH, HG_HEADS, SEQ // HG_ROWS),
        in_specs=[col(0), col(0), col(2 * HG_HEADS), col(3 * HG_HEADS),
                  pl.BlockSpec((2, HG_D), lambda b, h, c: (0, h)),
                  pl.BlockSpec((1, HG_D), lambda b, h, c: (0, h))],
        out_specs=pl.BlockSpec((None, HG_ROWS, HG_D), lambda b, h, c: (b, c, h)),
        out_shape=jax.ShapeDtypeStruct((BATCH, SEQ, HG_WIDTH), BF16),
        scratch_shapes=[pltpu.VMEM((HG_D, HG_D), F32),
                        pltpu.VMEM((HG_ROWS, HG_D), F32),
                        pltpu.VMEM((HG_ROWS, HG_D), F32)],
        compiler_params=_cparams("parallel", "parallel", "arbitrary"),
        name="hgrn2",
    )(p3, f3, p3, p3, lb_logits, norm_w)


PREP_TM = 256


def _prep_body(p_ref, pos_ref, inv_ref, q_ref, kc_ref, vc_ref, ks_ref, kw_ref,
               vst_ref, vwt_ref, gt_ref):
    ang = pos_ref[...] * inv_ref[...]
    cos = jnp.cos(ang)
    sin = jnp.sin(ang)
    lane = lax.broadcasted_iota(jnp.int32, (PREP_TM, 128), 1)
    lo = (lane & (NSA_DH // 2)) == 0
    sin_signed = jnp.where(lo, -sin, sin)

    def rope(x):
        rot = jnp.where(lo, pltpu.roll(x, 128 - NSA_DH // 2, 1), pltpu.roll(x, NSA_DH // 2, 1))
        return x * cos + rot * sin_signed

    cols = lambda c0: p_ref[:, c0:c0 + 128].astype(F32)
    scale = NSA_DH ** -0.5 * LOG2E
    for cblk in range(NSA_WIDTH // 128):
        q_ref[:, cblk * 128:(cblk + 1) * 128] = (rope(cols(cblk * 128)) * scale).astype(BF16)
    kc_ref[...] = rope(cols(1024))
    vc_ref[...] = cols(1152)
    ks_ref[:, 0:128] = rope(cols(1280)).astype(BF16)
    blk = (pl.program_id(1) * PREP_TM + lax.broadcasted_iota(jnp.int32, (PREP_TM, 128), 0)) // SEL_LEN
    ks_ref[:, 128:256] = jnp.where(lane == blk, 1.0, 0.0).astype(BF16)
    kw_ref[...] = rope(cols(1536)).astype(BF16)
    ones = jnp.ones((NSA_VROWS - NSA_DH, PREP_TM), BF16)
    for v_ref, c0 in ((vst_ref, 1408), (vwt_ref, 1664)):
        vt = cols(c0).T.astype(BF16)
        for g in range(NSA_KV):
            v_ref[g * NSA_VROWS:g * NSA_VROWS + NSA_DH, :] = vt[g * NSA_DH:(g + 1) * NSA_DH]
            v_ref[g * NSA_VROWS + NSA_DH:(g + 1) * NSA_VROWS, :] = ones
    gt_ref[...] = jax.nn.sigmoid(cols(1792)).T[0:3 * NSA_HEADS, :]


def _nsa_prep(proj, pos_f, inv128):
    nt = SEQ // PREP_TM
    p3 = proj.reshape(BATCH, SEQ, PROJ_W)
    nat = lambda w: pl.BlockSpec((None, PREP_TM, w), lambda b, i: (b, i, 0))
    tr = lambda r: pl.BlockSpec((None, r, PREP_TM), lambda b, i: (b, 0, i))
    sds = jax.ShapeDtypeStruct
    return pl.pallas_call(
        _prep_body,
        grid=(BATCH, nt),
        in_specs=[pl.BlockSpec((None, PREP_TM, NSA_PROJ), lambda b, i: (b, i, PROJ_NSA_OFF // NSA_PROJ)),
                  nat(1), pl.BlockSpec((1, 128), lambda b, i: (0, 0))],
        out_specs=(nat(NSA_WIDTH), nat(128), nat(128), nat(256), nat(128),
                   tr(NSA_KV * NSA_VROWS), tr(NSA_KV * NSA_VROWS), tr(3 * NSA_HEADS)),
        out_shape=(sds((BATCH, SEQ, NSA_WIDTH), BF16),
                   sds((BATCH, SEQ, 128), F32),
                   sds((BATCH, SEQ, 128), F32),
                   sds((BATCH, SEQ, 256), BF16),
                   sds((BATCH, SEQ, 128), BF16),
                   sds((BATCH, NSA_KV * NSA_VROWS, SEQ), BF16),
                   sds((BATCH, NSA_KV * NSA_VROWS, SEQ), BF16),
                   sds((BATCH, 3 * NSA_HEADS, SEQ), F32)),
        compiler_params=_cparams("parallel", "parallel"),
        name="nsa_prep",
    )(p3, pos_f, inv128)


def _cmp_body(tk_ref, tv_ref, pe_ref, kw1_ref, kw2_ref, vw1_ref, vw2_ref, kc_ref, vct_ref, y1_s, y2_s):
    row = lax.broadcasted_iota(jnp.int32, (N_CMP_PAD, NSA_KV * CMP_HIDDEN), 0)

    def mlp(t_ref, w1_ref, w2_ref):
        y1_s[...] = jnp.zeros_like(y1_s)
        y2_s[...] = jnp.zeros_like(y2_s)
        for l in range(CMP_STRIDE):
            x = t_ref[pl.ds(l, N_CMP_PAD, stride=CMP_STRIDE), :]
            y1_s[...] += _dot((x + pe_ref[l:l + 1, :]).astype(BF16), w1_ref[l])
            y2_s[...] += _dot((x + pe_ref[CMP_STRIDE + l:CMP_STRIDE + l + 1, :]).astype(BF16),
                              w1_ref[CMP_STRIDE + l])
        hid = jnp.where(row < N_CMP, y1_s[...] + pltpu.roll(y2_s[...], N_CMP_PAD - 1, 0), 0.0)
        return _dot(_silu(hid).astype(BF16), w2_ref[...])

    kc_ref[...] = mlp(tk_ref, kw1_ref, kw2_ref).astype(BF16)
    y1_s[:, 0:128] = mlp(tv_ref, vw1_ref, vw2_ref)
    vt = y1_s[:, 0:128].T.astype(BF16)
    ones = jnp.ones((NSA_VROWS - NSA_DH, N_CMP_PAD), BF16)
    for g in range(NSA_KV):
        vct_ref[g * NSA_VROWS:g * NSA_VROWS + NSA_DH, :] = vt[g * NSA_DH:(g + 1) * NSA_DH]
        vct_ref[g * NSA_VROWS + NSA_DH:(g + 1) * NSA_VROWS, :] = ones


def _compress(tk, tv, pe2, kw1, kw2, vw1, vw2):
    seg = pl.BlockSpec((None, SEQ, 128), lambda b: (b, 0, 0))
    full2 = lambda a: pl.BlockSpec(a.shape, lambda b: (0,) * a.ndim)
    return pl.pallas_call(
        _cmp_body,
        grid=(BATCH,),
        in_specs=[seg, seg, full2(pe2), full2(kw1), full2(kw2), full2(vw1), full2(vw2)],
        out_specs=(pl.BlockSpec((None, N_CMP_PAD, 128), lambda b: (b, 0, 0)),
                   pl.BlockSpec((None, NSA_KV * NSA_VROWS, N_CMP_PAD), lambda b: (b, 0, 0))),
        out_shape=(jax.ShapeDtypeStruct((BATCH, N_CMP_PAD, 128), BF16),
                   jax.ShapeDtypeStruct((BATCH, NSA_KV * NSA_VROWS, N_CMP_PAD), BF16)),
        scratch_shapes=[pltpu.VMEM((N_CMP_PAD, NSA_KV * CMP_HIDDEN), F32)] * 2,
        compiler_params=_cparams("parallel"),
        name="nsa_compress",
    )(tk, tv, pe2, kw1, kw2, vw1, vw2)


NSA_NL = NSA_REP * Q_BLOCK


def _nsa_body(q_ref, kc_ref, vct_ref, ks_ref, vst_ref, kw_ref, vwt_ref, gt_ref, ovt_ref,
              o_ref, qa_s, sc_s, rk_s, s0_s, s1_s, sc_buf, sw_buf, sd_buf, m_s, acc_s):
    g = pl.program_id(1)
    qb = pl.program_id(2)
    q0 = pl.multiple_of(qb * Q_BLOCK, Q_BLOCK)
    is_g0 = g == 0

    qblk = q_ref[...].astype(F32)
    zero_slab = jnp.zeros((NSA_DH, Q_BLOCK), F32)
    for p in range(NSA_REP // 2):
        t = qblk[:, p * 128:(p + 1) * 128].T
        for hh in range(2):
            s = t[hh * NSA_DH:(hh + 1) * NSA_DH]
            r = 2 * p + hh
            qa_s[0:128, r * Q_BLOCK:(r + 1) * Q_BLOCK] = jnp.concatenate(
                [jnp.where(is_g0, s, zero_slab), jnp.where(is_g0, zero_slab, s)], axis=0).astype(BF16)
    qa_s[128 + N_SEL:256, :] = jnp.zeros((128 - N_SEL, NSA_NL), BF16)

    tq = q0 + lax.broadcasted_iota(jnp.int32, (1, Q_BLOCK), 1)

    def mask_pair(s, valid):
        return jnp.concatenate([jnp.where(valid, s[:, r * Q_BLOCK:(r + 1) * Q_BLOCK], NEG)
                                for r in range(NSA_REP)], axis=1)

    n_i = lax.broadcasted_iota(jnp.int32, (N_CMP_PAD, Q_BLOCK), 0)
    valid_c = (n_i * CMP_STRIDE + (CMP_LEN - 1) <= tq) & (n_i < N_CMP)
    sc_buf[...] = mask_pair(_dot(kc_ref[...], qa_s[0:128, :]).astype(BF16), valid_c)

    w0 = pl.multiple_of(jnp.maximum(q0 - WINDOW, 0), Q_BLOCK)
    dpos = tq - (w0 + lax.broadcasted_iota(jnp.int32, (WIN_KEYS, Q_BLOCK), 0))
    sw_buf[...] = mask_pair(_dot(kw_ref[pl.ds(w0, WIN_KEYS), :], qa_s[0:128, :]).astype(BF16),
                            (dpos >= 0) & (dpos < WINDOW))

    sc = sc_buf[...]
    m_c = jnp.max(sc, axis=0, keepdims=True)
    e_c = jnp.exp2(sc - m_c)
    acc_c = _dot(vct_ref[...], e_c)
    inv_c = jnp.where(m_c.astype(F32) > 0.5 * NEG, 1.0 / jnp.maximum(acc_c[NSA_DH:NSA_DH + 1], 1e-30), 0.0)
    imp_h = _dot(ovt_ref[...], e_c) * inv_c

    d_i = lax.broadcasted_iota(jnp.int32, (Q_BLOCK, Q_BLOCK), 0)
    t_i = lax.broadcasted_iota(jnp.int32, (Q_BLOCK, Q_BLOCK), 1)
    sd_buf[...] = mask_pair(_dot(ks_ref[pl.ds(q0, Q_BLOCK), 0:128], qa_s[0:128, :]).astype(BF16),
                            d_i <= t_i)

    sw = sw_buf[...]
    acc_w = _dot(vwt_ref[:, pl.ds(w0, WIN_KEYS)], jnp.exp2(sw - jnp.max(sw, axis=0, keepdims=True)))
    inv_w = 1.0 / jnp.maximum(acc_w[NSA_DH:NSA_DH + 1], 1e-30)

    s = sd_buf[...]
    m16 = jnp.max(s, axis=0, keepdims=True)
    m_s[...] = m16.astype(F32)
    acc_s[...] = _dot(vst_ref[:, pl.ds(q0, Q_BLOCK)], jnp.exp2(s - m16))

    imp = imp_h[:, 0:Q_BLOCK]
    for r in range(1, NSA_REP):
        imp = imp + imp_h[:, r * Q_BLOCK:(r + 1) * Q_BLOCK]
    j_i = lax.broadcasted_iota(jnp.int32, (N_SEL, Q_BLOCK), 0)
    cur = tq // SEL_LEN
    forced = (j_i == 0) | (j_i == cur) | (j_i == cur - 1)
    score = jnp.where(forced, jnp.inf, jnp.where(j_i > cur, -jnp.inf, imp))
    sc_s[...] = score
    rk_s[...] = jnp.zeros_like(rk_s)
    sub8 = lax.broadcasted_iota(jnp.int32, (8, Q_BLOCK), 0)
    for grp in range(N_SEL // 8):
        @pl.when(8 * grp <= 2 * qb + 1)
        def _():
            for v in range(N_SEL // 8):
                sv = sc_s[8 * v:8 * v + 8, :]
                part = jnp.zeros((8, Q_BLOCK), F32)
                for jp in range(8 * grp, 8 * grp + 8):
                    row = sc_s[jp:jp + 1, :]
                    if v > grp:
                        part = part + jnp.where(row >= sv, 1.0, 0.0)
                    elif v < grp:
                        part = part + jnp.where(row > sv, 1.0, 0.0)
                    else:
                        part = part + jnp.where(sub8 + 8 * v > jp, jnp.where(row >= sv, 1.0, 0.0),
                                                jnp.where(row > sv, 1.0, 0.0))
                rk_s[8 * v:8 * v + 8, :] += part
    bias = jnp.where((rk_s[...] < SEL_TOP) & (j_i < 2 * qb), 0.0, NEG).astype(BF16)
    for r in range(NSA_REP):
        qa_s[128:128 + N_SEL, r * Q_BLOCK:(r + 1) * Q_BLOCK] = bias

    last_sub = SEQ // SEL_SUB - 1

    def scores_into(buf, c):
        k0 = pl.multiple_of(jnp.minimum(c, last_sub) * SEL_SUB, SEL_SUB)
        buf[...] = _dot(ks_ref[pl.ds(k0, SEL_SUB), :], qa_s[...]).astype(BF16)

    def softmax_from(buf, c):
        k0 = pl.multiple_of(c * SEL_SUB, SEL_SUB)
        s = buf[...]
        m = m_s[...]
        m_new = jnp.maximum(m, jnp.max(s, axis=0, keepdims=True).astype(F32))
        m_s[...] = m_new
        pr = jnp.exp2(s - m_new.astype(BF16))
        acc_s[...] = acc_s[...] * jnp.exp2(m - m_new) + _dot(vst_ref[:, pl.ds(k0, SEL_SUB)], pr)

    scores_into(s0_s, 0)

    def sel_step(i, carry):
        c = 2 * i
        scores_into(s1_s, c + 1)
        softmax_from(s0_s, c)
        scores_into(s0_s, c + 2)
        softmax_from(s1_s, c + 1)
        return carry

    n_main = (qb * Q_BLOCK + 2 * SEL_SUB - 1) // (2 * SEL_SUB)
    lax.fori_loop(0, n_main, sel_step, 0)
    inv_s = 1.0 / jnp.maximum(acc_s[NSA_DH:NSA_DH + 1, :], 1e-30)

    gall = gt_ref[...]
    ggrp = jnp.where(is_g0, gall[0:3 * NSA_REP], gall[3 * NSA_REP:3 * NSA_HEADS])
    for p2 in range(NSA_REP // 2):
        halves = []
        for hh in range(2):
            r = 2 * p2 + hh
            sl = slice(r * Q_BLOCK, (r + 1) * Q_BLOCK)
            g_c, g_s, g_w = (ggrp[3 * r + br:3 * r + br + 1, :] for br in range(3))
            halves.append((g_c * inv_c[:, sl]) * acc_c[0:NSA_DH, sl]
                          + (g_s * inv_s[:, sl]) * acc_s[0:NSA_DH, sl]
                          + (g_w * inv_w[:, sl]) * acc_w[0:NSA_DH, sl])
        o_ref[:, p2 * 128:(p2 + 1) * 128] = jnp.concatenate(halves, axis=0).T.astype(BF16)


def _nsa_attn(q_r, kc, vct, ks, vst, kw, vwt, gt, ovt):
    per_b = lambda r, c: pl.BlockSpec((None, r, c), lambda b, g, i: (b, 0, 0))
    per_bg = lambda r, c: pl.BlockSpec((None, r, c), lambda b, g, i: (b, g, 0))
    const = lambda a: pl.BlockSpec(a.shape, lambda b, g, i: (0, 0))
    return pl.pallas_call(
        _nsa_body,
        grid=(BATCH, NSA_KV, N_QB),
        in_specs=[pl.BlockSpec((None, Q_BLOCK, NSA_REP * NSA_DH), lambda b, g, i: (b, i, g)),
                  per_b(N_CMP_PAD, 128), per_bg(NSA_VROWS, N_CMP_PAD),
                  per_b(SEQ, 256), per_bg(NSA_VROWS, SEQ),
                  per_b(SEQ, 128), per_bg(NSA_VROWS, SEQ),
                  pl.BlockSpec((None, 3 * NSA_HEADS, Q_BLOCK), lambda b, g, i: (b, 0, i)),
                  const(ovt)],
        out_specs=pl.BlockSpec((None, Q_BLOCK, NSA_REP * NSA_DH), lambda b, g, i: (b, i, g)),
        out_shape=jax.ShapeDtypeStruct((BATCH, SEQ, NSA_WIDTH), BF16),
        scratch_shapes=[pltpu.VMEM((256, NSA_NL), BF16),
                        pltpu.VMEM((N_SEL, Q_BLOCK), F32), pltpu.VMEM((N_SEL, Q_BLOCK), F32),
                        pltpu.VMEM((SEL_SUB, NSA_NL), BF16), pltpu.VMEM((SEL_SUB, NSA_NL), BF16),
                        pltpu.VMEM((N_CMP_PAD, NSA_NL), BF16), pltpu.VMEM((WIN_KEYS, NSA_NL), BF16),
                        pltpu.VMEM((Q_BLOCK, NSA_NL), BF16),
                        pltpu.VMEM((1, NSA_NL), F32), pltpu.VMEM((NSA_VROWS, NSA_NL), F32)],
        compiler_params=_cparams("parallel", "parallel", "arbitrary"),
        name="nsa_attn",
    )(q_r, kc, vct, ks, vst, kw, vwt, gt, ovt)


OUT_TM = 512


def _outx_body(x_ref, oh_ref, on_ref, w_ref, nw_ref, wq_ref, k_ref, v_ref, wo_ref, o_ref):
    y = (x_ref[...] + _dot(oh_ref[...], w_ref[0:HG_WIDTH, :])
         + _dot(on_ref[...], w_ref[HG_WIDTH:HG_WIDTH + NSA_WIDTH, :]))
    hx = _rms(y, nw_ref[...]).astype(BF16)
    q = (_dot(hx, wq_ref[...]) * (X_DH ** -0.5)).astype(BF16)
    heads = []
    for h in range(X_HEADS):
        sl = slice(h * X_DH, (h + 1) * X_DH)
        s = _dot_nt(q[:, sl], k_ref[:, sl])
        e = jnp.exp(s - jnp.max(s, axis=-1, keepdims=True))
        p = e / jnp.sum(e, axis=-1, keepdims=True)
        heads.append(_dot(p.astype(BF16), v_ref[:, sl]))
    o_ref[...] = y + _dot(jnp.concatenate(heads, axis=1).astype(BF16), wo_ref[...])


def _outproj_xattn(x1, o_hg, o_nsa, w_out, nw, wq, k, v, wo):
    width = X_HEADS * X_DH
    tiles_per_b = SEQ // OUT_TM
    row = lambda w: pl.BlockSpec((OUT_TM, w), lambda i: (i, 0))
    const = lambda r, c: pl.BlockSpec((r, c), lambda i: (0, 0))
    kv = pl.BlockSpec((None, MEM_LEN, width), lambda i: (i // tiles_per_b, 0, 0))
    return pl.pallas_call(
        _outx_body,
        grid=(TOKENS // OUT_TM,),
        in_specs=[row(D_MODEL), row(HG_WIDTH), row(NSA_WIDTH), const(D_MODEL, D_MODEL), const(1, D_MODEL),
                  const(D_MODEL, width), kv, kv, const(width, D_MODEL)],
        out_specs=row(D_MODEL),
        out_shape=jax.ShapeDtypeStruct((TOKENS, D_MODEL), F32),
        compiler_params=_cparams("parallel"),
        name="out_proj_xattn",
    )(x1, o_hg, o_nsa, w_out, nw, wq, k, v, wo)


def _memkv_body(m_ref, nw_ref, wk_ref, wv_ref, k_ref, v_ref):
    hm = _rms(m_ref[...], nw_ref[...]).astype(BF16)
    k_ref[...] = _dot(hm, wk_ref[...]).astype(BF16)
    v_ref[...] = _dot(hm, wv_ref[...]).astype(BF16)


def _memkv(mem, nw, wk, wv):
    width = X_HEADS * X_DH
    wspec = pl.BlockSpec((D_MODEL, width), lambda b: (0, 0))
    ospec = pl.BlockSpec((None, MEM_LEN, width), lambda b: (b, 0, 0))
    osh = jax.ShapeDtypeStruct((BATCH, MEM_LEN, width), BF16)
    return pl.pallas_call(
        _memkv_body,
        grid=(BATCH,),
        in_specs=[pl.BlockSpec((None, MEM_LEN, D_MODEL), lambda b: (b, 0, 0)),
                  pl.BlockSpec((1, D_MODEL), lambda b: (0, 0)), wspec, wspec],
        out_specs=(ospec, ospec), out_shape=(osh, osh),
        compiler_params=_cparams("parallel"),
        name="xattn_memkv",
    )(mem, nw, wk, wv)


def _overlap_t():
    c0 = np.arange(N_CMP)[:, None] * CMP_STRIDE
    s0 = np.arange(N_SEL)[None, :] * SEL_LEN
    ov = np.clip(np.minimum(c0 + CMP_LEN, s0 + SEL_LEN) - np.maximum(c0, s0), 0, None) / CMP_LEN
    out = np.zeros((N_SEL, N_CMP_PAD), np.float32)
    out[:, :N_CMP] = ov.T
    return out


def kernel(x, mem, positions, ffn1_norm, ffn1_w_gate, ffn1_w_up, ffn1_w_down, mix_norm, w_in, hgrn_lb_logits, hgrn_out_norm, nsa_cmp_pe, nsa_cmp_k_w1, nsa_cmp_k_w2, nsa_cmp_v_w1, nsa_cmp_v_w2, w_out, xattn_norm, mem_norm, xattn_wq, xattn_wk, xattn_wv, xattn_wo, ffn2_norm, ffn2_w_gate, ffn2_w_up, ffn2_w_down, final_norm):
    bf = lambda a: a.astype(BF16)
    vec = lambda a: a.reshape(1, -1).astype(F32)
    x2d = x.reshape(TOKENS, D_MODEL)

    x1, h_mix = _ffn(x2d, vec(ffn1_norm[0]), bf(ffn1_w_gate[0]), bf(ffn1_w_up[0]), bf(ffn1_w_down[0]),
                     vec(mix_norm[0]), final=False)

    w_t = bf(w_in[0].T)
    w_tail = jnp.pad(w_t[PROJ_FULL_TILES * PROJ_TN:], ((0, PROJ_W - D_IN), (0, 0)))
    proj, proj_f = _proj(h_mix, w_t, w_tail)

    o_hg = _hgrn(proj, proj_f, hgrn_lb_logits.astype(F32), vec(hgrn_out_norm[0]))

    inv = ROPE_THETA ** (-jnp.arange(NSA_DH // 2, dtype=F32) / (NSA_DH // 2))
    inv128 = jnp.tile(inv, 128 // (NSA_DH // 2)).reshape(1, 128)
    pos_f = positions.astype(F32).reshape(BATCH, SEQ, 1)
    q_r, kc_tok, vc_tok, ks, kw, vst, vwt, gt = _nsa_prep(proj, pos_f, inv128)

    def over_groups(w):
        z = jnp.zeros_like(w)
        return bf(jnp.concatenate([jnp.concatenate([w, z], axis=-1), jnp.concatenate([z, w], axis=-1)], axis=-2))

    per_pos = lambda w1: over_groups(w1.reshape(CMP_LEN, NSA_DH, CMP_HIDDEN))
    pe = nsa_cmp_pe[0].astype(F32)
    kc, vct = _compress(kc_tok, vc_tok, jnp.concatenate([pe, pe], axis=1),
                        per_pos(nsa_cmp_k_w1[0]), over_groups(nsa_cmp_k_w2[0]),
                        per_pos(nsa_cmp_v_w1[0]), over_groups(nsa_cmp_v_w2[0]))
    o_nsa = _nsa_attn(q_r, kc, vct, ks, vst, kw, vwt, gt, jnp.asarray(_overlap_t(), dtype=BF16))

    km, vm = _memkv(mem, vec(mem_norm[0]), bf(xattn_wk[0]), bf(xattn_wv[0]))
    x3 = _outproj_xattn(x1, o_hg.reshape(TOKENS, HG_WIDTH), o_nsa.reshape(TOKENS, NSA_WIDTH),
                        bf(w_out[0]), vec(xattn_norm[0]), bf(xattn_wq[0]), km, vm, bf(xattn_wo[0]))

    out = _ffn(x3, vec(ffn2_norm[0]), bf(ffn2_w_gate[0]), bf(ffn2_w_up[0]), bf(ffn2_w_down[0]),
               vec(final_norm), final=True)
    return out.reshape(BATCH, SEQ, D_MODEL)
```

```python
import functools

import numpy as np
import jax
import jax.numpy as jnp
from jax import lax
from jax.experimental import pallas as pl
from jax.experimental.pallas import tpu as pltpu

F32 = jnp.float32
BF16 = jnp.bfloat16

D_MODEL = 2048
BATCH = 2
SEQ = 4096
TOKENS = BATCH * SEQ
RMS_EPS = 1e-6
ROPE_THETA = 10000.0
HG_WIDTH = 1024
HG_HEADS = 8
HG_D = 128
HG_CHUNK = 128
HG_LEVELS = (64, 32, 16, 8, 4, 2, 1)
NSA_WIDTH = 1024
NSA_DH = 64
NSA_HEADS = 16
NSA_KV = 2
NSA_REP = 8
NSA_VROWS = NSA_DH + 16
CMP_LEN = 32
CMP_STRIDE = 16
CMP_HIDDEN = 256
N_CMP = (SEQ - CMP_LEN) // CMP_STRIDE + 1
N_CMP_PAD = 256
SEL_LEN = 64
N_SEL = SEQ // SEL_LEN
SEL_TOP = 16
WINDOW = 512
Q_BLOCK = 128
N_QB = SEQ // Q_BLOCK
SEL_SUB = 256
WIN_KEYS = WINDOW + Q_BLOCK
MEM_LEN = 256
X_HEADS = 4
X_DH = 128
D_FF = 5632
IN_SIZES = (1024, 1024, 1024, 1024, 1024, 128, 128, 128, 128, 128, 128, 48)
D_IN = sum(IN_SIZES)
PROJ_NSA_OFF = sum(IN_SIZES[:4])
NSA_PROJ = 2048
PROJ_W = PROJ_NSA_OFF + NSA_PROJ
NEG = -1e30
LOG2E = 1.4426950408889634

V7X_VMEM_BYTES = 64 * 1024 * 1024
VMEM_LIMIT = V7X_VMEM_BYTES - 8 * 1024 * 1024


def _cparams(*sem, flags=None):
    return pltpu.CompilerParams(dimension_semantics=sem, vmem_limit_bytes=VMEM_LIMIT, flags=flags)


def _rms(x, w):
    return x * lax.rsqrt(jnp.mean(x * x, axis=-1, keepdims=True) + RMS_EPS) * w


def _silu(x):
    return x * jax.nn.sigmoid(x)


def _dot(a, b):
    return jnp.dot(a, b, preferred_element_type=F32)


def _dot_f32_by_01(sel, x):
    hi = x.astype(BF16)
    r1 = x - hi.astype(F32)
    mid = r1.astype(BF16)
    lo = (r1 - mid.astype(F32)).astype(BF16)
    n = x.shape[1]
    y = _dot(sel, jnp.concatenate([hi, mid, lo], axis=1))
    return y[:, 0:n] + y[:, n:2 * n] + y[:, 2 * n:3 * n]


def _dot_nt(a, b):
    return lax.dot_general(a, b, (((1,), (1,)), ((), ())), preferred_element_type=F32)


FFN_TM = 512
FFN_TF = 512


def _ffn_body(x_ref, nw_ref, wg_ref, wu_ref, wd_ref, nw2_ref, *rest, final, n_step_casts, n_tile_casts):
    n_casts = n_step_casts + n_tile_casts
    cast_in, rest = rest[:n_casts], rest[n_casts:]
    if final:
        o_ref, rest = rest[0], rest[1:]
    else:
        (o_ref, hn_ref), rest = rest[:2], rest[2:]
    cast_out, (h_scr,) = rest[:n_casts], rest[n_casts:]
    j = pl.program_id(1)
    last = pl.num_programs(1) - 1

    def swiglu_tile(h):
        g = _dot(h, wg_ref[...])
        u = _dot(h, wu_ref[...])
        return _dot((_silu(g) * u).astype(BF16), wd_ref[...])

    def ride_along(first):
        n = n_casts if first else n_step_casts
        for src, dst in zip(cast_in[:n], cast_out[:n]):
            dst[...] = src[...].astype(BF16)

    @pl.when(j == 0)
    def _():
        h = _rms(x_ref[...], nw_ref[...]).astype(BF16)
        h_scr[...] = h
        o_ref[...] = swiglu_tile(h)
        ride_along(True)

    @pl.when((j > 0) & (j < last))
    def _():
        o_ref[...] += swiglu_tile(h_scr[...])
        ride_along(False)

    @pl.when(j == last)
    def _():
        y = x_ref[...] + 0.5 * (o_ref[...] + swiglu_tile(h_scr[...]))
        if final:
            o_ref[...] = _rms(y, nw2_ref[...])
        else:
            o_ref[...] = y
            hn_ref[...] = _rms(y, nw2_ref[...]).astype(BF16)
        ride_along(False)


def _ffn(x, nw, wg, wu, wd, nw2, final, step_casts=(), tile_casts=()):
    ni, nj = TOKENS // FFN_TM, D_FF // FFN_TF
    row = pl.BlockSpec((FFN_TM, D_MODEL), lambda i, j: (i, 0))
    vec = pl.BlockSpec((1, D_MODEL), lambda i, j: (0, 0))
    in_specs = [row, vec,
                pl.BlockSpec((D_MODEL, FFN_TF), lambda i, j: (0, j)),
                pl.BlockSpec((D_MODEL, FFN_TF), lambda i, j: (0, j)),
                pl.BlockSpec((FFN_TF, D_MODEL), lambda i, j: (j, 0)),
                vec]
    cast_specs = []
    for a in step_casts:
        r, c = a.shape
        if r % ni == 0 and c % nj == 0:
            cast_specs.append(pl.BlockSpec((r // ni, c // nj), lambda i, j: (i, j)))
        else:
            cast_specs.append(pl.BlockSpec((r // nj, c // ni), lambda i, j: (j, i)))
    for a in tile_casts:
        cast_specs.append(pl.BlockSpec((a.shape[0] // ni, a.shape[1]), lambda i, j: (i, 0)))
    casts = tuple(step_casts) + tuple(tile_casts)
    cast_shapes = [jax.ShapeDtypeStruct(a.shape, BF16) for a in casts]
    main_shapes = [jax.ShapeDtypeStruct((TOKENS, D_MODEL), F32)]
    if not final:
        main_shapes.append(jax.ShapeDtypeStruct((TOKENS, D_MODEL), BF16))
    return pl.pallas_call(
        functools.partial(_ffn_body, final=final, n_step_casts=len(step_casts), n_tile_casts=len(tile_casts)),
        grid=(ni, nj), in_specs=in_specs + cast_specs,
        out_specs=tuple([row] * len(main_shapes) + cast_specs),
        out_shape=tuple(main_shapes + cast_shapes),
        scratch_shapes=[pltpu.VMEM((FFN_TM, D_MODEL), BF16)],
        compiler_params=_cparams("parallel", "arbitrary"),
        name="ffn_final" if final else "ffn",
    )(x, nw, wg, wu, wd, nw2, *casts)


PROJ_TM = 1024
PROJ_TN = 512


PROJ_F_TILE0 = IN_SIZES[0] // PROJ_TN
PROJ_F_TILES = IN_SIZES[1] // PROJ_TN


PROJ_FULL_TILES = D_IN // PROJ_TN


def _proj_body(a_ref, wt_ref, tail_ref, o_ref, f_ref):
    j = pl.program_id(1)
    w = jnp.where(j >= PROJ_FULL_TILES, tail_ref[...], wt_ref[...])
    y = _dot_nt(a_ref[...], w)
    o_ref[...] = y.astype(BF16)

    @pl.when((j >= PROJ_F_TILE0) & (j < PROJ_F_TILE0 + PROJ_F_TILES))
    def _():
        f_ref[...] = y


def _proj(a, wt, wt_tail):
    m, k = a.shape
    f_tile = lambda i, j: (i, jnp.clip(j - PROJ_F_TILE0, 0, PROJ_F_TILES - 1))
    return pl.pallas_call(
        _proj_body,
        grid=(m // PROJ_TM, PROJ_W // PROJ_TN),
        in_specs=[pl.BlockSpec((PROJ_TM, k), lambda i, j: (i, 0)),
                  pl.BlockSpec((PROJ_TN, k), lambda i, j: (jnp.minimum(j, PROJ_FULL_TILES - 1), 0)),
                  pl.BlockSpec((PROJ_TN, k), lambda i, j: (0, 0))],
        out_specs=(pl.BlockSpec((PROJ_TM, PROJ_TN), lambda i, j: (i, j)),
                   pl.BlockSpec((PROJ_TM, PROJ_TN), f_tile)),
        out_shape=(jax.ShapeDtypeStruct((m, PROJ_W), BF16),
                   jax.ShapeDtypeStruct((m, IN_SIZES[1]), F32)),
        compiler_params=_cparams("parallel", "arbitrary"),
        name="proj_in",
    )(a, wt, wt_tail)


HG_ROWS = 1024
HG_CUM = 256


def _hgrn_body(q_ref, f_ref, i_ref, g_ref, lbl_ref, nw_ref, o_ref, st_ref, k_s, b_s):
    c = pl.program_id(2)

    @pl.when(c == 0)
    def _():
        st_ref[...] = jnp.zeros_like(st_ref)

    l0 = lbl_ref[0:1, :]
    l1 = lbl_ref[1:2, :]
    lmax = jnp.maximum(l0, l1)
    e0 = jnp.exp(l0 - lmax)
    lb = e0 / (e0 + jnp.exp(l1 - lmax))

    C = HG_CHUNK
    f = lb + (1.0 - lb) * jax.nn.sigmoid(f_ref[...])
    k_s[...] = 1.0 - f
    r_i = lax.broadcasted_iota(jnp.int32, (HG_CUM, HG_CUM), 0)
    c_i = lax.broadcasted_iota(jnp.int32, (HG_CUM, HG_CUM), 1)
    tri = jnp.where((r_i >= c_i) & (r_i // C == c_i // C), 1.0, 0.0).astype(BF16)
    logf = jnp.log2(f)
    for r0 in range(0, HG_ROWS, HG_CUM):
        b_s[r0:r0 + HG_CUM, :] = _dot_f32_by_01(tri, logf[r0:r0 + HG_CUM])

    t_i = lax.broadcasted_iota(jnp.int32, (C, C), 0)
    s_i = lax.broadcasted_iota(jnp.int32, (C, C), 1)
    level_mask = [(t_i // (2 * w) == s_i // (2 * w)) & (t_i % (2 * w) >= w) & (s_i % (2 * w) < w)
                  for w in HG_LEVELS]
    sub_r = lax.broadcasted_iota(jnp.int32, (8, HG_D), 0)
    row_i = lax.broadcasted_iota(jnp.int32, (C, HG_D), 0)
    right_sign = {w: jnp.where(row_i % (2 * w) >= w, 1.0, -1.0) for w in HG_LEVELS if w < 8}

    def neg_abs_diff(w, r0, b):
        row = lambda r, n: jnp.broadcast_to(b_s[r0 + r:r0 + r + 1, :], (n, HG_D))
        if w >= 8:
            parts = []
            for p0 in range(0, C, 2 * w):
                ref = row(p0 + w - 1, w)
                parts += [ref - b[p0:p0 + w], b[p0 + w:p0 + 2 * w] - ref]
            return jnp.concatenate(parts, axis=0)
        if w == 4:
            bref = jnp.concatenate([row(p0 + 3, 8) for p0 in range(0, C, 8)], axis=0)
        elif w == 2:
            bref = jnp.concatenate([jnp.where(sub_r < 4, row(p0 + 1, 8), row(p0 + 5, 8))
                                    for p0 in range(0, C, 8)], axis=0)
        else:
            bref = jnp.where(row_i % 2 == 1, pltpu.roll(b, 1, 0), b)
        return (b - bref) * right_sign[w]

    chunks = [ci * C for ci in range(HG_ROWS // C)]
    rows = lambda ref, r0: ref[r0:r0 + C, :]
    att = [jnp.zeros((C, C), F32) for _ in chunks]
    q16 = [rows(q_ref, r0) for r0 in chunks]
    k16 = [rows(k_s, r0).astype(BF16) for r0 in chunks]
    for w, mask in zip(HG_LEVELS, level_mask):
        for n, r0 in enumerate(chunks):
            b = rows(b_s, r0)
            e = jnp.exp2(neg_abs_diff(w, r0, b)).astype(BF16)
            att[n] = jnp.where(mask, _dot_nt(q16[n] * e, k16[n] * e), att[n])
    o_intra = []
    for n, r0 in enumerate(chunks):
        q, k, v = q16[n].astype(F32), rows(k_s, r0), rows(i_ref, r0)
        o_intra.append(_dot(att[n].astype(BF16), v)
                       + jnp.sum(q * k, axis=-1, keepdims=True) * v.astype(F32))
    upd = []
    for r0 in chunks:
        bl = b_s[r0 + C - 1:r0 + C, :]
        kd = rows(k_s, r0) * jnp.exp2(bl - rows(b_s, r0))
        v_t = rows(i_ref, r0).astype(F32).T.astype(BF16)
        upd.append((jnp.exp2(bl), _dot(v_t, kd.astype(BF16))))
    st_t = st_ref[...]
    for n, r0 in enumerate(chunks):
        qe = (q16[n].astype(F32) * jnp.exp2(rows(b_s, r0))).astype(BF16)
        o = o_intra[n] + _dot_nt(qe, st_t.astype(BF16))
        st_t = st_t * upd[n][0] + upd[n][1]
        o = o * lax.rsqrt(jnp.mean(o * o, axis=-1, keepdims=True) + RMS_EPS)
        o_ref[r0:r0 + C, :] = (o * nw_ref[...] * _silu(rows(g_ref, r0).astype(F32))).astype(BF16)
    st_ref[...] = st_t


def _hgrn(proj, proj_f, lb_logits, norm_w):
    p3 = proj.reshape(BATCH, SEQ, PROJ_W)
    f3 = proj_f.reshape(BATCH, SEQ, HG_WIDTH)

    def col(off):
        return pl.BlockSpec((None, HG_ROWS, HG_D), lambda b, h, c: (b, c, off + h))

    return pl.pallas_call(
        _hgrn_body,
        grid=(BATCH, HG_HEADS, SEQ // HG_ROWS),
        in_specs=[col(0), col(0), col(2 * HG_HEADS), col(3 * HG_HEADS),
                  pl.BlockSpec((2, HG_D), lambda b, h, c: (0, h)),
                  pl.BlockSpec((1, HG_D), lambda b, h, c: (0, h))],
        out_specs=pl.BlockSpec((None, HG_ROWS, HG_D), lambda b, h, c: (b, c, h)),
        out_shape=jax.ShapeDtypeStruct((BATCH, SEQ, HG_WIDTH), BF16),
        scratch_shapes=[pltpu.VMEM((HG_D, HG_D), F32),
                        pltpu.VMEM((HG_ROWS, HG_D), F32),
                        pltpu.VMEM((HG_ROWS, HG_D), F32)],
        compiler_params=_cparams("parallel", "parallel", "arbitrary"),
        name="hgrn2",
    )(p3, f3, p3, p3, lb_logits, norm_w)


PREP_TM = 256


def _prep_body(p_ref, pos_ref, inv_ref, q_ref, kc_ref, vc_ref, ks_ref, kw_ref,
               vst_ref, vwt_ref, gt_ref):
    ang = pos_ref[...] * inv_ref[...]
    cos = jnp.cos(ang)
    sin = jnp.sin(ang)
    lane = lax.broadcasted_iota(jnp.int32, (PREP_TM, 128), 1)
    lo = (lane & (NSA_DH // 2)) == 0
    sin_signed = jnp.where(lo, -sin, sin)

    def rope(x):
        rot = jnp.where(lo, pltpu.roll(x, 128 - NSA_DH // 2, 1), pltpu.roll(x, NSA_DH // 2, 1))
        return x * cos + rot * sin_signed

    cols = lambda c0: p_ref[:, c0:c0 + 128].astype(F32)
    scale = NSA_DH ** -0.5 * LOG2E
    for cblk in range(NSA_WIDTH // 128):
        q_ref[:, cblk * 128:(cblk + 1) * 128] = (rope(cols(cblk * 128)) * scale).astype(BF16)
    kc_ref[...] = rope(cols(1024))
    vc_ref[...] = cols(1152)
    ks_ref[:, 0:128] = rope(cols(1280)).astype(BF16)
    blk = (pl.program_id(1) * PREP_TM + lax.broadcasted_iota(jnp.int32, (PREP_TM, 128), 0)) // SEL_LEN
    ks_ref[:, 128:256] = jnp.where(lane == blk, 1.0, 0.0).astype(BF16)
    kw_ref[...] = rope(cols(1536)).astype(BF16)
    ones = jnp.ones((NSA_VROWS - NSA_DH, PREP_TM), BF16)
    for v_ref, c0 in ((vst_ref, 1408), (vwt_ref, 1664)):
        vt = cols(c0).T.astype(BF16)
        for g in range(NSA_KV):
            v_ref[g * NSA_VROWS:g * NSA_VROWS + NSA_DH, :] = vt[g * NSA_DH:(g + 1) * NSA_DH]
            v_ref[g * NSA_VROWS + NSA_DH:(g + 1) * NSA_VROWS, :] = ones
    gt_ref[...] = jax.nn.sigmoid(cols(1792)).T[0:3 * NSA_HEADS, :]


def _nsa_prep(proj, pos_f, inv128):
    nt = SEQ // PREP_TM
    p3 = proj.reshape(BATCH, SEQ, PROJ_W)
    nat = lambda w: pl.BlockSpec((None, PREP_TM, w), lambda b, i: (b, i, 0))
    tr = lambda r: pl.BlockSpec((None, r, PREP_TM), lambda b, i: (b, 0, i))
    sds = jax.ShapeDtypeStruct
    return pl.pallas_call(
        _prep_body,
        grid=(BATCH, nt),
        in_specs=[pl.BlockSpec((None, PREP_TM, NSA_PROJ), lambda b, i: (b, i, PROJ_NSA_OFF // NSA_PROJ)),
                  nat(1), pl.BlockSpec((1, 128), lambda b, i: (0, 0))],
        out_specs=(nat(NSA_WIDTH), nat(128), nat(128), nat(256), nat(128),
                   tr(NSA_KV * NSA_VROWS), tr(NSA_KV * NSA_VROWS), tr(3 * NSA_HEADS)),
        out_shape=(sds((BATCH, SEQ, NSA_WIDTH), BF16),
                   sds((BATCH, SEQ, 128), F32),
                   sds((BATCH, SEQ, 128), F32),
                   sds((BATCH, SEQ, 256), BF16),
                   sds((BATCH, SEQ, 128), BF16),
                   sds((BATCH, NSA_KV * NSA_VROWS, SEQ), BF16),
                   sds((BATCH, NSA_KV * NSA_VROWS, SEQ), BF16),
                   sds((BATCH, 3 * NSA_HEADS, SEQ), F32)),
        compiler_params=_cparams("parallel", "parallel"),
        name="nsa_prep",
    )(p3, pos_f, inv128)


def _cmp_body(tk_ref, tv_ref, pe_ref, kw1_ref, kw2_ref, vw1_ref, vw2_ref, kc_ref, vct_ref, y1_s, y2_s):
    row = lax.broadcasted_iota(jnp.int32, (N_CMP_PAD, NSA_KV * CMP_HIDDEN), 0)

    def mlp(t_ref, w1_ref, w2_ref):
        y1_s[...] = jnp.zeros_like(y1_s)
        y2_s[...] = jnp.zeros_like(y2_s)
        for l in range(CMP_STRIDE):
            x = t_ref[pl.ds(l, N_CMP_PAD, stride=CMP_STRIDE), :]
            y1_s[...] += _dot((x + pe_ref[l:l + 1, :]).astype(BF16), w1_ref[l])
            y2_s[...] += _dot((x + pe_ref[CMP_STRIDE + l:CMP_STRIDE + l + 1, :]).astype(BF16),
                              w1_ref[CMP_STRIDE + l])
        hid = jnp.where(row < N_CMP, y1_s[...] + pltpu.roll(y2_s[...], N_CMP_PAD - 1, 0), 0.0)
        return _dot(_silu(hid).astype(BF16), w2_ref[...])

    kc_ref[...] = mlp(tk_ref, kw1_ref, kw2_ref).astype(BF16)
    y1_s[:, 0:128] = mlp(tv_ref, vw1_ref, vw2_ref)
    vt = y1_s[:, 0:128].T.astype(BF16)
    ones = jnp.ones((NSA_VROWS - NSA_DH, N_CMP_PAD), BF16)
    for g in range(NSA_KV):
        vct_ref[g * NSA_VROWS:g * NSA_VROWS + NSA_DH, :] = vt[g * NSA_DH:(g + 1) * NSA_DH]
        vct_ref[g * NSA_VROWS + NSA_DH:(g + 1) * NSA_VROWS, :] = ones


def _compress(tk, tv, pe2, kw1, kw2, vw1, vw2):
    seg = pl.BlockSpec((None, SEQ, 128), lambda b: (b, 0, 0))
    full2 = lambda a: pl.BlockSpec(a.shape, lambda b: (0,) * a.ndim)
    return pl.pallas_call(
        _cmp_body,
        grid=(BATCH,),
        in_specs=[seg, seg, full2(pe2), full2(kw1), full2(kw2), full2(vw1), full2(vw2)],
        out_specs=(pl.BlockSpec((None, N_CMP_PAD, 128), lambda b: (b, 0, 0)),
                   pl.BlockSpec((None, NSA_KV * NSA_VROWS, N_CMP_PAD), lambda b: (b, 0, 0))),
        out_shape=(jax.ShapeDtypeStruct((BATCH, N_CMP_PAD, 128), BF16),
                   jax.ShapeDtypeStruct((BATCH, NSA_KV * NSA_VROWS, N_CMP_PAD), BF16)),
        scratch_shapes=[pltpu.VMEM((N_CMP_PAD, NSA_KV * CMP_HIDDEN), F32)] * 2,
        compiler_params=_cparams("parallel"),
        name="nsa_compress",
    )(tk, tv, pe2, kw1, kw2, vw1, vw2)


NSA_NL = NSA_REP * Q_BLOCK


def _nsa_body(q_ref, kc_ref, vct_ref, ks_ref, vst_ref, kw_ref, vwt_ref, gt_ref, ovt_ref,
              o_ref, qa_s, sc_s, rk_s, s0_s, s1_s, sc_buf, sw_buf, sd_buf, m_s, acc_s):
    g = pl.program_id(1)
    qb = pl.program_id(2)
    q0 = pl.multiple_of(qb * Q_BLOCK, Q_BLOCK)
    is_g0 = g == 0

    qblk = q_ref[...].astype(F32)
    zero_slab = jnp.zeros((NSA_DH, Q_BLOCK), F32)
    for p in range(NSA_REP // 2):
        t = qblk[:, p * 128:(p + 1) * 128].T
        for hh in range(2):
            s = t[hh * NSA_DH:(hh + 1) * NSA_DH]
            r = 2 * p + hh
            qa_s[0:128, r * Q_BLOCK:(r + 1) * Q_BLOCK] = jnp.concatenate(
                [jnp.where(is_g0, s, zero_slab), jnp.where(is_g0, zero_slab, s)], axis=0).astype(BF16)
    qa_s[128 + N_SEL:256, :] = jnp.zeros((128 - N_SEL, NSA_NL), BF16)

    tq = q0 + lax.broadcasted_iota(jnp.int32, (1, Q_BLOCK), 1)

    def mask_pair(s, valid):
        return jnp.concatenate([jnp.where(valid, s[:, r * Q_BLOCK:(r + 1) * Q_BLOCK], NEG)
                                for r in range(NSA_REP)], axis=1)

    n_i = lax.broadcasted_iota(jnp.int32, (N_CMP_PAD, Q_BLOCK), 0)
    valid_c = (n_i * CMP_STRIDE + (CMP_LEN - 1) <= tq) & (n_i < N_CMP)
    sc_buf[...] = mask_pair(_dot(kc_ref[...], qa_s[0:128, :]).astype(BF16), valid_c)

    w0 = pl.multiple_of(jnp.maximum(q0 - WINDOW, 0), Q_BLOCK)
    dpos = tq - (w0 + lax.broadcasted_iota(jnp.int32, (WIN_KEYS, Q_BLOCK), 0))
    sw_buf[...] = mask_pair(_dot(kw_ref[pl.ds(w0, WIN_KEYS), :], qa_s[0:128, :]).astype(BF16),
                            (dpos >= 0) & (dpos < WINDOW))

    sc = sc_buf[...]
    m_c = jnp.max(sc, axis=0, keepdims=True)
    e_c = jnp.exp2(sc - m_c)
    acc_c = _dot(vct_ref[...], e_c)
    inv_c = jnp.where(m_c.astype(F32) > 0.5 * NEG, 1.0 / jnp.maximum(acc_c[NSA_DH:NSA_DH + 1], 1e-30), 0.0)
    imp_h = _dot(ovt_ref[...], e_c) * inv_c

    d_i = lax.broadcasted_iota(jnp.int32, (Q_BLOCK, Q_BLOCK), 0)
    t_i = lax.broadcasted_iota(jnp.int32, (Q_BLOCK, Q_BLOCK), 1)
    sd_buf[...] = mask_pair(_dot(ks_ref[pl.ds(q0, Q_BLOCK), 0:128], qa_s[0:128, :]).astype(BF16),
                            d_i <= t_i)

    sw = sw_buf[...]
    acc_w = _dot(vwt_ref[:, pl.ds(w0, WIN_KEYS)], jnp.exp2(sw - jnp.max(sw, axis=0, keepdims=True)))
    inv_w = 1.0 / jnp.maximum(acc_w[NSA_DH:NSA_DH + 1], 1e-30)

    s = sd_buf[...]
    m16 = jnp.max(s, axis=0, keepdims=True)
    m_s[...] = m16.astype(F32)
    acc_s[...] = _dot(vst_ref[:, pl.ds(q0, Q_BLOCK)], jnp.exp2(s - m16))

    imp = imp_h[:, 0:Q_BLOCK]
    for r in range(1, NSA_REP):
        imp = imp + imp_h[:, r * Q_BLOCK:(r + 1) * Q_BLOCK]
    j_i = lax.broadcasted_iota(jnp.int32, (N_SEL, Q_BLOCK), 0)
    cur = tq // SEL_LEN
    forced = (j_i == 0) | (j_i == cur) | (j_i == cur - 1)
    score = jnp.where(forced, jnp.inf, jnp.where(j_i > cur, -jnp.inf, imp))
    sc_s[...] = score
    rk_s[...] = jnp.zeros_like(rk_s)
    sub8 = lax.broadcasted_iota(jnp.int32, (8, Q_BLOCK), 0)
    for grp in range(N_SEL // 8):
        @pl.when(8 * grp <= 2 * qb + 1)
        def _():
            for v in range(N_SEL // 8):
                sv = sc_s[8 * v:8 * v + 8, :]
                part = jnp.zeros((8, Q_BLOCK), F32)
                for jp in range(8 * grp, 8 * grp + 8):
                    row = sc_s[jp:jp + 1, :]
                    if v > grp:
                        part = part + jnp.where(row >= sv, 1.0, 0.0)
                    elif v < grp:
                        part = part + jnp.where(row > sv, 1.0, 0.0)
                    else:
                        part = part + jnp.where(sub8 + 8 * v > jp, jnp.where(row >= sv, 1.0, 0.0),
                                                jnp.where(row > sv, 1.0, 0.0))
                rk_s[8 * v:8 * v + 8, :] += part
    bias = jnp.where((rk_s[...] < SEL_TOP) & (j_i < 2 * qb), 0.0, NEG).astype(BF16)
    for r in range(NSA_REP):
        qa_s[128:128 + N_SEL, r * Q_BLOCK:(r + 1) * Q_BLOCK] = bias

    last_sub = SEQ // SEL_SUB - 1

    def scores_into(buf, c):
        k0 = pl.multiple_of(jnp.minimum(c, last_sub) * SEL_SUB, SEL_SUB)
        buf[...] = _dot(ks_ref[pl.ds(k0, SEL_SUB), :], qa_s[...]).astype(BF16)

    def softmax_from(buf, c):
        k0 = pl.multiple_of(c * SEL_SUB, SEL_SUB)
        s = buf[...]
        m = m_s[...]
        m_new = jnp.maximum(m, jnp.max(s, axis=0, keepdims=True).astype(F32))
        m_s[...] = m_new
        pr = jnp.exp2(s - m_new.astype(BF16))
        acc_s[...] = acc_s[...] * jnp.exp2(m - m_new) + _dot(vst_ref[:, pl.ds(k0, SEL_SUB)], pr)

    scores_into(s0_s, 0)

    def sel_step(i, carry):
        c = 2 * i
        scores_into(s1_s, c + 1)
        softmax_from(s0_s, c)
        scores_into(s0_s, c + 2)
        softmax_from(s1_s, c + 1)
        return carry

    n_main = (qb * Q_BLOCK + 2 * SEL_SUB - 1) // (2 * SEL_SUB)
    lax.fori_loop(0, n_main, sel_step, 0)
    inv_s = 1.0 / jnp.maximum(acc_s[NSA_DH:NSA_DH + 1, :], 1e-30)

    gall = gt_ref[...]
    ggrp = jnp.where(is_g0, gall[0:3 * NSA_REP], gall[3 * NSA_REP:3 * NSA_HEADS])
    for p2 in range(NSA_REP // 2):
        halves = []
        for hh in range(2):
            r = 2 * p2 + hh
            sl = slice(r * Q_BLOCK, (r + 1) * Q_BLOCK)
            g_c, g_s, g_w = (ggrp[3 * r + br:3 * r + br + 1, :] for br in range(3))
            halves.append((g_c * inv_c[:, sl]) * acc_c[0:NSA_DH, sl]
                          + (g_s * inv_s[:, sl]) * acc_s[0:NSA_DH, sl]
                          + (g_w * inv_w[:, sl]) * acc_w[0:NSA_DH, sl])
        o_ref[:, p2 * 128:(p2 + 1) * 128] = jnp.concatenate(halves, axis=0).T.astype(BF16)


def _nsa_attn(q_r, kc, vct, ks, vst, kw, vwt, gt, ovt):
    per_b = lambda r, c: pl.BlockSpec((None, r, c), lambda b, g, i: (b, 0, 0))
    per_bg = lambda r, c: pl.BlockSpec((None, r, c), lambda b, g, i: (b, g, 0))
    const = lambda a: pl.BlockSpec(a.shape, lambda b, g, i: (0, 0))
    return pl.pallas_call(
        _nsa_body,
        grid=(BATCH, NSA_KV, N_QB),
        in_specs=[pl.BlockSpec((None, Q_BLOCK, NSA_REP * NSA_DH), lambda b, g, i: (b, i, g)),
                  per_b(N_CMP_PAD, 128), per_bg(NSA_VROWS, N_CMP_PAD),
                  per_b(SEQ, 256), per_bg(NSA_VROWS, SEQ),
                  per_b(SEQ, 128), per_bg(NSA_VROWS, SEQ),
                  pl.BlockSpec((None, 3 * NSA_HEADS, Q_BLOCK), lambda b, g, i: (b, 0, i)),
                  const(ovt)],
        out_specs=pl.BlockSpec((None, Q_BLOCK, NSA_REP * NSA_DH), lambda b, g, i: (b, i, g)),
        out_shape=jax.ShapeDtypeStruct((BATCH, SEQ, NSA_WIDTH), BF16),
        scratch_shapes=[pltpu.VMEM((256, NSA_NL), BF16),
                        pltpu.VMEM((N_SEL, Q_BLOCK), F32), pltpu.VMEM((N_SEL, Q_BLOCK), F32),
                        pltpu.VMEM((SEL_SUB, NSA_NL), BF16), pltpu.VMEM((SEL_SUB, NSA_NL), BF16),
                        pltpu.VMEM((N_CMP_PAD, NSA_NL), BF16), pltpu.VMEM((WIN_KEYS, NSA_NL), BF16),
                        pltpu.VMEM((Q_BLOCK, NSA_NL), BF16),
                        pltpu.VMEM((1, NSA_NL), F32), pltpu.VMEM((NSA_VROWS, NSA_NL), F32)],
        compiler_params=_cparams("parallel", "parallel", "arbitrary"),
        name="nsa_attn",
    )(q_r, kc, vct, ks, vst, kw, vwt, gt, ovt)


OUT_TM = 512


def _outx_body(x_ref, oh_ref, on_ref, w_ref, nw_ref, wq_ref, k_ref, v_ref, wo_ref, o_ref):
    y = (x_ref[...] + _dot(oh_ref[...], w_ref[0:HG_WIDTH, :])
         + _dot(on_ref[...], w_ref[HG_WIDTH:HG_WIDTH + NSA_WIDTH, :]))
    hx = _rms(y, nw_ref[...]).astype(BF16)
    q = (_dot(hx, wq_ref[...]) * (X_DH ** -0.5)).astype(BF16)
    heads = []
    for h in range(X_HEADS):
        sl = slice(h * X_DH, (h + 1) * X_DH)
        s = _dot_nt(q[:, sl], k_ref[:, sl])
        e = jnp.exp(s - jnp.max(s, axis=-1, keepdims=True))
        p = e / jnp.sum(e, axis=-1, keepdims=True)
        heads.append(_dot(p.astype(BF16), v_ref[:, sl]))
    o_ref[...] = y + _dot(jnp.concatenate(heads, axis=1).astype(BF16), wo_ref[...])


def _outproj_xattn(x1, o_hg, o_nsa, w_out, nw, wq, k, v, wo):
    width = X_HEADS * X_DH
    tiles_per_b = SEQ // OUT_TM
    row = lambda w: pl.BlockSpec((OUT_TM, w), lambda i: (i, 0))
    const = lambda r, c: pl.BlockSpec((r, c), lambda i: (0, 0))
    kv = pl.BlockSpec((None, MEM_LEN, width), lambda i: (i // tiles_per_b, 0, 0))
    return pl.pallas_call(
        _outx_body,
        grid=(TOKENS // OUT_TM,),
        in_specs=[row(D_MODEL), row(HG_WIDTH), row(NSA_WIDTH), const(D_MODEL, D_MODEL), const(1, D_MODEL),
                  const(D_MODEL, width), kv, kv, const(width, D_MODEL)],
        out_specs=row(D_MODEL),
        out_shape=jax.ShapeDtypeStruct((TOKENS, D_MODEL), F32),
        compiler_params=_cparams("parallel"),
        name="out_proj_xattn",
    )(x1, o_hg, o_nsa, w_out, nw, wq, k, v, wo)


def _memkv_body(m_ref, nw_ref, wk_ref, wv_ref, k_ref, v_ref):
    hm = _rms(m_ref[...], nw_ref[...]).astype(BF16)
    k_ref[...] = _dot(hm, wk_ref[...]).astype(BF16)
    v_ref[...] = _dot(hm, wv_ref[...]).astype(BF16)


def _memkv(mem, nw, wk, wv):
    width = X_HEADS * X_DH
    wspec = pl.BlockSpec((D_MODEL, width), lambda b: (0, 0))
    ospec = pl.BlockSpec((None, MEM_LEN, width), lambda b: (b, 0, 0))
    osh = jax.ShapeDtypeStruct((BATCH, MEM_LEN, width), BF16)
    return pl.pallas_call(
        _memkv_body,
        grid=(BATCH,),
        in_specs=[pl.BlockSpec((None, MEM_LEN, D_MODEL), lambda b: (b, 0, 0)),
                  pl.BlockSpec((1, D_MODEL), lambda b: (0, 0)), wspec, wspec],
        out_specs=(ospec, ospec), out_shape=(osh, osh),
        compiler_params=_cparams("parallel"),
        name="xattn_memkv",
    )(mem, nw, wk, wv)


def _overlap_t():
    c0 = np.arange(N_CMP)[:, None] * CMP_STRIDE
    s0 = np.arange(N_SEL)[None, :] * SEL_LEN
    ov = np.clip(np.minimum(c0 + CMP_LEN, s0 + SEL_LEN) - np.maximum(c0, s0), 0, None) / CMP_LEN
    out = np.zeros((N_SEL, N_CMP_PAD), np.float32)
    out[:, :N_CMP] = ov.T
    return out


def kernel(x, mem, positions, ffn1_norm, ffn1_w_gate, ffn1_w_up, ffn1_w_down, mix_norm, w_in, hgrn_lb_logits, hgrn_out_norm, nsa_cmp_pe, nsa_cmp_k_w1, nsa_cmp_k_w2, nsa_cmp_v_w1, nsa_cmp_v_w2, w_out, xattn_norm, mem_norm, xattn_wq, xattn_wk, xattn_wv, xattn_wo, ffn2_norm, ffn2_w_gate, ffn2_w_up, ffn2_w_down, final_norm):
    bf = lambda a: a.astype(BF16)
    vec = lambda a: a.reshape(1, -1).astype(F32)
    x2d = x.reshape(TOKENS, D_MODEL)

    x1, h_mix, w2_gate, w2_up, w2_down, w_out_b, wq_b, wk_b, wv_b, wo_b = _ffn(
        x2d, vec(ffn1_norm[0]), bf(ffn1_w_gate[0]), bf(ffn1_w_up[0]), bf(ffn1_w_down[0]),
        vec(mix_norm[0]), final=False,
        step_casts=(ffn2_w_gate[0], ffn2_w_up[0], ffn2_w_down[0]),
        tile_casts=(w_out[0], xattn_wq[0], xattn_wk[0], xattn_wv[0], xattn_wo[0]))

    w_t = bf(w_in[0].T)
    w_tail = jnp.pad(w_t[PROJ_FULL_TILES * PROJ_TN:], ((0, PROJ_W - D_IN), (0, 0)))
    proj, proj_f = _proj(h_mix, w_t, w_tail)

    o_hg = _hgrn(proj, proj_f, hgrn_lb_logits.astype(F32), vec(hgrn_out_norm[0]))

    inv = ROPE_THETA ** (-jnp.arange(NSA_DH // 2, dtype=F32) / (NSA_DH // 2))
    inv128 = jnp.tile(inv, 128 // (NSA_DH // 2)).reshape(1, 128)
    pos_f = positions.astype(F32).reshape(BATCH, SEQ, 1)
    q_r, kc_tok, vc_tok, ks, kw, vst, vwt, gt = _nsa_prep(proj, pos_f, inv128)

    def over_groups(w):
        z = jnp.zeros_like(w)
        return bf(jnp.concatenate([jnp.concatenate([w, z], axis=-1), jnp.concatenate([z, w], axis=-1)], axis=-2))

    per_pos = lambda w1: over_groups(w1.reshape(CMP_LEN, NSA_DH, CMP_HIDDEN))
    pe = nsa_cmp_pe[0].astype(F32)
    kc, vct = _compress(kc_tok, vc_tok, jnp.concatenate([pe, pe], axis=1),
                        per_pos(nsa_cmp_k_w1[0]), over_groups(nsa_cmp_k_w2[0]),
                        per_pos(nsa_cmp_v_w1[0]), over_groups(nsa_cmp_v_w2[0]))
    o_nsa = _nsa_attn(q_r, kc, vct, ks, vst, kw, vwt, gt, jnp.asarray(_overlap_t(), dtype=BF16))

    km, vm = _memkv(mem, vec(mem_norm[0]), wk_b, wv_b)
    x3 = _outproj_xattn(x1, o_hg.reshape(TOKENS, HG_WIDTH), o_nsa.reshape(TOKENS, NSA_WIDTH),
                        w_out_b, vec(xattn_norm[0]), wq_b, km, vm, wo_b)

    (out,) = _ffn(x3, vec(ffn2_norm[0]), w2_gate, w2_up, w2_down, vec(final_norm), final=True)
    return out.reshape(BATCH, SEQ, D_MODEL)
```

```python
import functools

import numpy as np
import jax
import jax.numpy as jnp
from jax import lax
from jax.experimental import pallas as pl
from jax.experimental.pallas import tpu as pltpu

F32 = jnp.float32
BF16 = jnp.bfloat16

D_MODEL = 2048
BATCH = 2
SEQ = 4096
TOKENS = BATCH * SEQ
RMS_EPS = 1e-6
ROPE_THETA = 10000.0
HG_WIDTH = 1024
HG_HEADS = 8
HG_D = 128
HG_CHUNK = 128
HG_LEVELS = (64, 32, 16, 8, 4, 2, 1)
NSA_WIDTH = 1024
NSA_DH = 64
NSA_HEADS = 16
NSA_KV = 2
NSA_REP = 8
NSA_VROWS = NSA_DH + 16
CMP_LEN = 32
CMP_STRIDE = 16
CMP_HIDDEN = 256
N_CMP = (SEQ - CMP_LEN) // CMP_STRIDE + 1
N_CMP_PAD = 256
SEL_LEN = 64
N_SEL = SEQ // SEL_LEN
SEL_TOP = 16
WINDOW = 512
Q_BLOCK = 128
N_QB = SEQ // Q_BLOCK
SEL_SUB = 256
WIN_KEYS = WINDOW + Q_BLOCK
MEM_LEN = 256
X_HEADS = 4
X_DH = 128
D_FF = 5632
IN_SIZES = (1024, 1024, 1024, 1024, 1024, 128, 128, 128, 128, 128, 128, 48)
D_IN = sum(IN_SIZES)
PROJ_NSA_OFF = sum(IN_SIZES[:4])
NSA_PROJ = 2048
PROJ_W = PROJ_NSA_OFF + NSA_PROJ
NEG = -1e30
LOG2E = 1.4426950408889634

V7X_VMEM_BYTES = 64 * 1024 * 1024
VMEM_LIMIT = V7X_VMEM_BYTES - 8 * 1024 * 1024


def _cparams(*sem, flags=None):
    return pltpu.CompilerParams(dimension_semantics=sem, vmem_limit_bytes=VMEM_LIMIT, flags=flags)


def _rms(x, w):
    return x * lax.rsqrt(jnp.mean(x * x, axis=-1, keepdims=True) + RMS_EPS) * w


def _silu(x):
    return x * jax.nn.sigmoid(x)


def _dot(a, b):
    return jnp.dot(a, b, preferred_element_type=F32)


def _dot_f32_by_01(sel, x):
    hi = x.astype(BF16)
    r1 = x - hi.astype(F32)
    mid = r1.astype(BF16)
    lo = (r1 - mid.astype(F32)).astype(BF16)
    n = x.shape[1]
    y = _dot(sel, jnp.concatenate([hi, mid, lo], axis=1))
    return y[:, 0:n] + y[:, n:2 * n] + y[:, 2 * n:3 * n]


def _dot_nt(a, b):
    return lax.dot_general(a, b, (((1,), (1,)), ((), ())), preferred_element_type=F32)


FFN_TM = 512
FFN_TF = 512


def _ffn_body(x_ref, nw_ref, wg_ref, wu_ref, wd_ref, nw2_ref, *rest, final, n_step_casts, n_tile_casts):
    n_casts = n_step_casts + n_tile_casts
    cast_in, rest = rest[:n_casts], rest[n_casts:]
    if final:
        o_ref, rest = rest[0], rest[1:]
    else:
        (o_ref, hn_ref), rest = rest[:2], rest[2:]
    cast_out, (h_scr,) = rest[:n_casts], rest[n_casts:]
    j = pl.program_id(1)
    last = pl.num_programs(1) - 1

    def swiglu_tile(h):
        g = _dot(h, wg_ref[...])
        u = _dot(h, wu_ref[...])
        return _dot((_silu(g) * u).astype(BF16), wd_ref[...])

    def ride_along(first):
        n = n_casts if first else n_step_casts
        for src, dst in zip(cast_in[:n], cast_out[:n]):
            dst[...] = src[...].astype(BF16)

    @pl.when(j == 0)
    def _():
        h = _rms(x_ref[...], nw_ref[...]).astype(BF16)
        h_scr[...] = h
        o_ref[...] = swiglu_tile(h)
        ride_along(True)

    @pl.when((j > 0) & (j < last))
    def _():
        o_ref[...] += swiglu_tile(h_scr[...])
        ride_along(False)

    @pl.when(j == last)
    def _():
        y = x_ref[...] + 0.5 * (o_ref[...] + swiglu_tile(h_scr[...]))
        if final:
            o_ref[...] = _rms(y, nw2_ref[...])
        else:
            o_ref[...] = y
            hn_ref[...] = _rms(y, nw2_ref[...]).astype(BF16)
        ride_along(False)


def _ffn(x, nw, wg, wu, wd, nw2, final, step_casts=(), tile_casts=()):
    ni, nj = TOKENS // FFN_TM, D_FF // FFN_TF
    row = pl.BlockSpec((FFN_TM, D_MODEL), lambda i, j: (i, 0))
    vec = pl.BlockSpec((1, D_MODEL), lambda i, j: (0, 0))
    in_specs = [row, vec,
                pl.BlockSpec((D_MODEL, FFN_TF), lambda i, j: (0, j)),
                pl.BlockSpec((D_MODEL, FFN_TF), lambda i, j: (0, j)),
                pl.BlockSpec((FFN_TF, D_MODEL), lambda i, j: (j, 0)),
                vec]
    cast_specs = []
    for a in step_casts:
        r, c = a.shape
        if r % ni == 0 and c % nj == 0:
            cast_specs.append(pl.BlockSpec((r // ni, c // nj), lambda i, j: (i, j)))
        else:
            cast_specs.append(pl.BlockSpec((r // nj, c // ni), lambda i, j: (j, i)))
    for a in tile_casts:
        cast_specs.append(pl.BlockSpec((a.shape[0] // ni, a.shape[1]), lambda i, j: (i, 0)))
    casts = tuple(step_casts) + tuple(tile_casts)
    cast_shapes = [jax.ShapeDtypeStruct(a.shape, BF16) for a in casts]
    main_shapes = [jax.ShapeDtypeStruct((TOKENS, D_MODEL), F32)]
    if not final:
        main_shapes.append(jax.ShapeDtypeStruct((TOKENS, D_MODEL), BF16))
    return pl.pallas_call(
        functools.partial(_ffn_body, final=final, n_step_casts=len(step_casts), n_tile_casts=len(tile_casts)),
        grid=(ni, nj), in_specs=in_specs + cast_specs,
        out_specs=tuple([row] * len(main_shapes) + cast_specs),
        out_shape=tuple(main_shapes + cast_shapes),
        scratch_shapes=[pltpu.VMEM((FFN_TM, D_MODEL), BF16)],
        compiler_params=_cparams("parallel", "arbitrary"),
        name="ffn_final" if final else "ffn",
    )(x, nw, wg, wu, wd, nw2, *casts)


PROJ_TM = 1024
PROJ_TN = 512


PROJ_F_TILE0 = IN_SIZES[0] // PROJ_TN
PROJ_F_TILES = IN_SIZES[1] // PROJ_TN


PROJ_FULL_TILES = D_IN // PROJ_TN


def _proj_body(a_ref, wt_ref, tail_ref, o_ref, f_ref):
    j = pl.program_id(1)
    w = jnp.where(j >= PROJ_FULL_TILES, tail_ref[...], wt_ref[...]).astype(BF16)
    y = _dot_nt(a_ref[...], w)
    o_ref[...] = y.astype(BF16)

    @pl.when((j >= PROJ_F_TILE0) & (j < PROJ_F_TILE0 + PROJ_F_TILES))
    def _():
        f_ref[...] = y


def _proj(a, wt, wt_tail):
    m, k = a.shape
    f_tile = lambda i, j: (i, jnp.clip(j - PROJ_F_TILE0, 0, PROJ_F_TILES - 1))
    return pl.pallas_call(
        _proj_body,
        grid=(m // PROJ_TM, PROJ_W // PROJ_TN),
        in_specs=[pl.BlockSpec((PROJ_TM, k), lambda i, j: (i, 0)),
                  pl.BlockSpec((PROJ_TN, k), lambda i, j: (jnp.minimum(j, PROJ_FULL_TILES - 1), 0)),
                  pl.BlockSpec((PROJ_TN, k), lambda i, j: (0, 0))],
        out_specs=(pl.BlockSpec((PROJ_TM, PROJ_TN), lambda i, j: (i, j)),
                   pl.BlockSpec((PROJ_TM, PROJ_TN), f_tile)),
        out_shape=(jax.ShapeDtypeStruct((m, PROJ_W), BF16),
                   jax.ShapeDtypeStruct((m, IN_SIZES[1]), F32)),
        compiler_params=_cparams("parallel", "arbitrary"),
        name="proj_in",
    )(a, wt, wt_tail)


HG_ROWS = 1024
HG_CUM = 256


def _hgrn_body(q_ref, f_ref, i_ref, g_ref, lbl_ref, nw_ref, o_ref, st_ref, k_s, b_s):
    c = pl.program_id(2)

    @pl.when(c == 0)
    def _():
        st_ref[...] = jnp.zeros_like(st_ref)

    l0 = lbl_ref[0:1, :]
    l1 = lbl_ref[1:2, :]
    lmax = jnp.maximum(l0, l1)
    e0 = jnp.exp(l0 - lmax)
    lb = e0 / (e0 + jnp.exp(l1 - lmax))

    C = HG_CHUNK
    f = lb + (1.0 - lb) * jax.nn.sigmoid(f_ref[...])
    k_s[...] = 1.0 - f
    r_i = lax.broadcasted_iota(jnp.int32, (HG_CUM, HG_CUM), 0)
    c_i = lax.broadcasted_iota(jnp.int32, (HG_CUM, HG_CUM), 1)
    tri = jnp.where((r_i >= c_i) & (r_i // C == c_i // C), 1.0, 0.0).astype(BF16)
    logf = jnp.log2(f)
    for r0 in range(0, HG_ROWS, HG_CUM):
        b_s[r0:r0 + HG_CUM, :] = _dot_f32_by_01(tri, logf[r0:r0 + HG_CUM])

    t_i = lax.broadcasted_iota(jnp.int32, (C, C), 0)
    s_i = lax.broadcasted_iota(jnp.int32, (C, C), 1)
    level_mask = [(t_i // (2 * w) == s_i // (2 * w)) & (t_i % (2 * w) >= w) & (s_i % (2 * w) < w)
                  for w in HG_LEVELS]
    sub_r = lax.broadcasted_iota(jnp.int32, (8, HG_D), 0)
    row_i = lax.broadcasted_iota(jnp.int32, (C, HG_D), 0)
    right_sign = {w: jnp.where(row_i % (2 * w) >= w, 1.0, -1.0) for w in HG_LEVELS if w < 8}

    def neg_abs_diff(w, r0, b):
        row = lambda r, n: jnp.broadcast_to(b_s[r0 + r:r0 + r + 1, :], (n, HG_D))
        if w >= 8:
            parts = []
            for p0 in range(0, C, 2 * w):
                ref = row(p0 + w - 1, w)
                parts += [ref - b[p0:p0 + w], b[p0 + w:p0 + 2 * w] - ref]
            return jnp.concatenate(parts, axis=0)
        if w == 4:
            bref = jnp.concatenate([row(p0 + 3, 8) for p0 in range(0, C, 8)], axis=0)
        elif w == 2:
            bref = jnp.concatenate([jnp.where(sub_r < 4, row(p0 + 1, 8), row(p0 + 5, 8))
                                    for p0 in range(0, C, 8)], axis=0)
        else:
            bref = jnp.where(row_i % 2 == 1, pltpu.roll(b, 1, 0), b)
        return (b - bref) * right_sign[w]

    chunks = [ci * C for ci in range(HG_ROWS // C)]
    rows = lambda ref, r0: ref[r0:r0 + C, :]
    att = [jnp.zeros((C, C), F32) for _ in chunks]
    q16 = [rows(q_ref, r0) for r0 in chunks]
    k16 = [rows(k_s, r0).astype(BF16) for r0 in chunks]
    for w, mask in zip(HG_LEVELS, level_mask):
        for n, r0 in enumerate(chunks):
            b = rows(b_s, r0)
            e = jnp.exp2(neg_abs_diff(w, r0, b)).astype(BF16)
            att[n] = jnp.where(mask, _dot_nt(q16[n] * e, k16[n] * e), att[n])
    o_intra = []
    for n, r0 in enumerate(chunks):
        q, k, v = q16[n].astype(F32), rows(k_s, r0), rows(i_ref, r0)
        o_intra.append(_dot(att[n].astype(BF16), v)
                       + jnp.sum(q * k, axis=-1, keepdims=True) * v.astype(F32))
    upd = []
    for r0 in chunks:
        bl = b_s[r0 + C - 1:r0 + C, :]
        kd = rows(k_s, r0) * jnp.exp2(bl - rows(b_s, r0))
        v_t = rows(i_ref, r0).astype(F32).T.astype(BF16)
        upd.append((jnp.exp2(bl), _dot(v_t, kd.astype(BF16))))
    st_t = st_ref[...]
    for n, r0 in enumerate(chunks):
        qe = (q16[n].astype(F32) * jnp.exp2(rows(b_s, r0))).astype(BF16)
        o = o_intra[n] + _dot_nt(qe, st_t.astype(BF16))
        st_t = st_t * upd[n][0] + upd[n][1]
        o = o * lax.rsqrt(jnp.mean(o * o, axis=-1, keepdims=True) + RMS_EPS)
        o_ref[r0:r0 + C, :] = (o * nw_ref[...] * _silu(rows(g_ref, r0).astype(F32))).astype(BF16)
    st_ref[...] = st_t


def _hgrn(proj, proj_f, lb_logits, norm_w):
    p3 = proj.reshape(BATCH, SEQ, PROJ_W)
    f3 = proj_f.reshape(BATCH, SEQ, HG_WIDTH)

    def col(off):
        return pl.BlockSpec((None, HG_ROWS, HG_D), lambda b, h, c: (b, c, off + h))

    return pl.pallas_call(
        _hgrn_body,
        grid=(BATCH, HG_HEADS, SEQ // HG_ROWS),
        in_specs=[col(0), col(0), col(2 * HG_HEADS), col(3 * HG_HEADS),
                  pl.BlockSpec((2, HG_D), lambda b, h, c: (0, h)),
                  pl.BlockSpec((1, HG_D), lambda b, h, c: (0, h))],
        out_specs=pl.BlockSpec((None, HG_ROWS, HG_D), lambda b, h, c: (b, c, h)),
        out_shape=jax.ShapeDtypeStruct((BATCH, SEQ, HG_WIDTH), BF16),
        scratch_shapes=[pltpu.VMEM((HG_D, HG_D), F32),
                        pltpu.VMEM((HG_ROWS, HG_D), F32),
                        pltpu.VMEM((HG_ROWS, HG_D), F32)],
        compiler_params=_cparams("parallel", "parallel", "arbitrary"),
        name="hgrn2",
    )(p3, f3, p3, p3, lb_logits, norm_w)


PREP_TM = 256


def _prep_body(p_ref, pos_ref, inv_ref, q_ref, kc_ref, vc_ref, ks_ref, kw_ref,
               vst_ref, vwt_ref, gt_ref):
    ang = pos_ref[...] * inv_ref[...]
    cos = jnp.cos(ang)
    sin = jnp.sin(ang)
    lane = lax.broadcasted_iota(jnp.int32, (PREP_TM, 128), 1)
    lo = (lane & (NSA_DH // 2)) == 0
    sin_signed = jnp.where(lo, -sin, sin)

    def rope(x):
        rot = jnp.where(lo, pltpu.roll(x, 128 - NSA_DH // 2, 1), pltpu.roll(x, NSA_DH // 2, 1))
        return x * cos + rot * sin_signed

    cols = lambda c0: p_ref[:, c0:c0 + 128].astype(F32)
    scale = NSA_DH ** -0.5 * LOG2E
    for cblk in range(NSA_WIDTH // 128):
        q_ref[cblk * 128:(cblk + 1) * 128, :] = (rope(cols(cblk * 128)) * scale).T.astype(BF16)
    kc_ref[...] = rope(cols(1024))
    vc_ref[...] = cols(1152)
    ks_ref[:, 0:128] = rope(cols(1280)).astype(BF16)
    blk = (pl.program_id(1) * PREP_TM + lax.broadcasted_iota(jnp.int32, (PREP_TM, 128), 0)) // SEL_LEN
    ks_ref[:, 128:256] = jnp.where(lane == blk, 1.0, 0.0).astype(BF16)
    kw_ref[...] = rope(cols(1536)).astype(BF16)
    ones = jnp.ones((NSA_VROWS - NSA_DH, PREP_TM), BF16)
    for v_ref, c0 in ((vst_ref, 1408), (vwt_ref, 1664)):
        vt = cols(c0).T.astype(BF16)
        for g in range(NSA_KV):
            v_ref[g * NSA_VROWS:g * NSA_VROWS + NSA_DH, :] = vt[g * NSA_DH:(g + 1) * NSA_DH]
            v_ref[g * NSA_VROWS + NSA_DH:(g + 1) * NSA_VROWS, :] = ones
    gt_ref[...] = jax.nn.sigmoid(cols(1792)).T[0:3 * NSA_HEADS, :]


def _nsa_prep(proj, pos_f, inv128):
    nt = SEQ // PREP_TM
    p3 = proj.reshape(BATCH, SEQ, PROJ_W)
    nat = lambda w: pl.BlockSpec((None, PREP_TM, w), lambda b, i: (b, i, 0))
    tr = lambda r: pl.BlockSpec((None, r, PREP_TM), lambda b, i: (b, 0, i))
    sds = jax.ShapeDtypeStruct
    return pl.pallas_call(
        _prep_body,
        grid=(BATCH, nt),
        in_specs=[pl.BlockSpec((None, PREP_TM, NSA_PROJ), lambda b, i: (b, i, PROJ_NSA_OFF // NSA_PROJ)),
                  nat(1), pl.BlockSpec((1, 128), lambda b, i: (0, 0))],
        out_specs=(tr(NSA_WIDTH), nat(128), nat(128), nat(256), nat(128),
                   tr(NSA_KV * NSA_VROWS), tr(NSA_KV * NSA_VROWS), tr(3 * NSA_HEADS)),
        out_shape=(sds((BATCH, NSA_WIDTH, SEQ), BF16),
                   sds((BATCH, SEQ, 128), F32),
                   sds((BATCH, SEQ, 128), F32),
                   sds((BATCH, SEQ, 256), BF16),
                   sds((BATCH, SEQ, 128), BF16),
                   sds((BATCH, NSA_KV * NSA_VROWS, SEQ), BF16),
                   sds((BATCH, NSA_KV * NSA_VROWS, SEQ), BF16),
                   sds((BATCH, 3 * NSA_HEADS, SEQ), F32)),
        compiler_params=_cparams("parallel", "parallel"),
        name="nsa_prep",
    )(p3, pos_f, inv128)


def _cmp_body(tk_ref, tv_ref, pe_ref, kw1_ref, kw2_ref, vw1_ref, vw2_ref, kc_ref, vct_ref, y1_s, y2_s):
    row = lax.broadcasted_iota(jnp.int32, (N_CMP_PAD, NSA_KV * CMP_HIDDEN), 0)

    def mlp(t_ref, w1_ref, w2_ref):
        y1_s[...] = jnp.zeros_like(y1_s)
        y2_s[...] = jnp.zeros_like(y2_s)
        for l in range(CMP_STRIDE):
            x = t_ref[pl.ds(l, N_CMP_PAD, stride=CMP_STRIDE), :]
            y1_s[...] += _dot((x + pe_ref[l:l + 1, :]).astype(BF16), w1_ref[l])
            y2_s[...] += _dot((x + pe_ref[CMP_STRIDE + l:CMP_STRIDE + l + 1, :]).astype(BF16),
                              w1_ref[CMP_STRIDE + l])
        hid = jnp.where(row < N_CMP, y1_s[...] + pltpu.roll(y2_s[...], N_CMP_PAD - 1, 0), 0.0)
        return _dot(_silu(hid).astype(BF16), w2_ref[...])

    kc_ref[...] = mlp(tk_ref, kw1_ref, kw2_ref).astype(BF16)
    y1_s[:, 0:128] = mlp(tv_ref, vw1_ref, vw2_ref)
    vt = y1_s[:, 0:128].T.astype(BF16)
    ones = jnp.ones((NSA_VROWS - NSA_DH, N_CMP_PAD), BF16)
    for g in range(NSA_KV):
        vct_ref[g * NSA_VROWS:g * NSA_VROWS + NSA_DH, :] = vt[g * NSA_DH:(g + 1) * NSA_DH]
        vct_ref[g * NSA_VROWS + NSA_DH:(g + 1) * NSA_VROWS, :] = ones


def _compress(tk, tv, pe2, kw1, kw2, vw1, vw2):
    seg = pl.BlockSpec((None, SEQ, 128), lambda b: (b, 0, 0))
    full2 = lambda a: pl.BlockSpec(a.shape, lambda b: (0,) * a.ndim)
    return pl.pallas_call(
        _cmp_body,
        grid=(BATCH,),
        in_specs=[seg, seg, full2(pe2), full2(kw1), full2(kw2), full2(vw1), full2(vw2)],
        out_specs=(pl.BlockSpec((None, N_CMP_PAD, 128), lambda b: (b, 0, 0)),
                   pl.BlockSpec((None, NSA_KV * NSA_VROWS, N_CMP_PAD), lambda b: (b, 0, 0))),
        out_shape=(jax.ShapeDtypeStruct((BATCH, N_CMP_PAD, 128), BF16),
                   jax.ShapeDtypeStruct((BATCH, NSA_KV * NSA_VROWS, N_CMP_PAD), BF16)),
        scratch_shapes=[pltpu.VMEM((N_CMP_PAD, NSA_KV * CMP_HIDDEN), F32)] * 2,
        compiler_params=_cparams("parallel"),
        name="nsa_compress",
    )(tk, tv, pe2, kw1, kw2, vw1, vw2)


NSA_NL = NSA_REP * Q_BLOCK


def _nsa_body(q_ref, kc_ref, vct_ref, ks_ref, vst_ref, kw_ref, vwt_ref, gt_ref, ovt_ref,
              o_ref, qa_s, sc_s, rk_s, s0_s, s1_s, sc_buf, sw_buf, sd_buf, m_s, acc_s):
    g = pl.program_id(1)
    qb = pl.program_id(2)
    q0 = pl.multiple_of(qb * Q_BLOCK, Q_BLOCK)
    is_g0 = g == 0

    zero_slab = jnp.zeros((NSA_DH, Q_BLOCK), BF16)
    for r in range(NSA_REP):
        s = q_ref[r * NSA_DH:(r + 1) * NSA_DH, :]
        qa_s[0:NSA_DH, r * Q_BLOCK:(r + 1) * Q_BLOCK] = jnp.where(is_g0, s, zero_slab)
        qa_s[NSA_DH:2 * NSA_DH, r * Q_BLOCK:(r + 1) * Q_BLOCK] = jnp.where(is_g0, zero_slab, s)
    qa_s[128 + N_SEL:256, :] = jnp.zeros((128 - N_SEL, NSA_NL), BF16)

    tq = q0 + lax.broadcasted_iota(jnp.int32, (1, Q_BLOCK), 1)

    def mask_pair(s, valid):
        return jnp.concatenate([jnp.where(valid, s[:, r * Q_BLOCK:(r + 1) * Q_BLOCK], NEG)
                                for r in range(NSA_REP)], axis=1)

    n_i = lax.broadcasted_iota(jnp.int32, (N_CMP_PAD, Q_BLOCK), 0)
    valid_c = (n_i * CMP_STRIDE + (CMP_LEN - 1) <= tq) & (n_i < N_CMP)
    sc_buf[...] = mask_pair(_dot(kc_ref[...], qa_s[0:128, :]).astype(BF16), valid_c)

    w0 = pl.multiple_of(jnp.maximum(q0 - WINDOW, 0), Q_BLOCK)
    dpos = tq - (w0 + lax.broadcasted_iota(jnp.int32, (WIN_KEYS, Q_BLOCK), 0))
    sw_buf[...] = mask_pair(_dot(kw_ref[pl.ds(w0, WIN_KEYS), :], qa_s[0:128, :]).astype(BF16),
                            (dpos >= 0) & (dpos < WINDOW))

    sc = sc_buf[...]
    m_c = jnp.max(sc, axis=0, keepdims=True)
    e_c = jnp.exp2(sc - m_c)
    acc_c = _dot(vct_ref[...], e_c)
    inv_c = jnp.where(m_c.astype(F32) > 0.5 * NEG, 1.0 / jnp.maximum(acc_c[NSA_DH:NSA_DH + 1], 1e-30), 0.0)
    imp_h = _dot(ovt_ref[...], e_c) * inv_c

    d_i = lax.broadcasted_iota(jnp.int32, (Q_BLOCK, Q_BLOCK), 0)
    t_i = lax.broadcasted_iota(jnp.int32, (Q_BLOCK, Q_BLOCK), 1)
    sd_buf[...] = mask_pair(_dot(ks_ref[pl.ds(q0, Q_BLOCK), 0:128], qa_s[0:128, :]).astype(BF16),
                            d_i <= t_i)

    sw = sw_buf[...]
    acc_w = _dot(vwt_ref[:, pl.ds(w0, WIN_KEYS)], jnp.exp2(sw - jnp.max(sw, axis=0, keepdims=True)))
    inv_w = 1.0 / jnp.maximum(acc_w[NSA_DH:NSA_DH + 1], 1e-30)

    s = sd_buf[...]
    m16 = jnp.max(s, axis=0, keepdims=True)
    m_s[...] = m16.astype(F32)
    acc_s[...] = _dot(vst_ref[:, pl.ds(q0, Q_BLOCK)], jnp.exp2(s - m16))

    imp = imp_h[:, 0:Q_BLOCK]
    for r in range(1, NSA_REP):
        imp = imp + imp_h[:, r * Q_BLOCK:(r + 1) * Q_BLOCK]
    j_i = lax.broadcasted_iota(jnp.int32, (N_SEL, Q_BLOCK), 0)
    cur = tq // SEL_LEN
    forced = (j_i == 0) | (j_i == cur) | (j_i == cur - 1)
    score = jnp.where(forced, jnp.inf, jnp.where(j_i > cur, -jnp.inf, imp))
    sc_s[...] = score
    rk_s[...] = jnp.zeros_like(rk_s)
    sub8 = lax.broadcasted_iota(jnp.int32, (8, Q_BLOCK), 0)
    for grp in range(N_SEL // 8):
        @pl.when(8 * grp <= 2 * qb + 1)
        def _():
            for v in range(N_SEL // 8):
                sv = sc_s[8 * v:8 * v + 8, :]
                part = jnp.zeros((8, Q_BLOCK), F32)
                for jp in range(8 * grp, 8 * grp + 8):
                    row = sc_s[jp:jp + 1, :]
                    if v > grp:
                        part = part + jnp.where(row >= sv, 1.0, 0.0)
                    elif v < grp:
                        part = part + jnp.where(row > sv, 1.0, 0.0)
                    else:
                        part = part + jnp.where(sub8 + 8 * v > jp, jnp.where(row >= sv, 1.0, 0.0),
                                                jnp.where(row > sv, 1.0, 0.0))
                rk_s[8 * v:8 * v + 8, :] += part
    bias = jnp.where((rk_s[...] < SEL_TOP) & (j_i < 2 * qb), 0.0, NEG).astype(BF16)
    for r in range(NSA_REP):
        qa_s[128:128 + N_SEL, r * Q_BLOCK:(r + 1) * Q_BLOCK] = bias

    last_sub = SEQ // SEL_SUB - 1

    def scores_into(buf, c):
        k0 = pl.multiple_of(jnp.minimum(c, last_sub) * SEL_SUB, SEL_SUB)
        buf[...] = _dot(ks_ref[pl.ds(k0, SEL_SUB), :], qa_s[...]).astype(BF16)

    def softmax_from(buf, c):
        k0 = pl.multiple_of(c * SEL_SUB, SEL_SUB)
        s = buf[...]
        m = m_s[...]
        m_new = jnp.maximum(m, jnp.max(s, axis=0, keepdims=True).astype(F32))
        m_s[...] = m_new
        pr = jnp.exp2(s - m_new.astype(BF16))
        acc_s[...] = acc_s[...] * jnp.exp2(m - m_new) + _dot(vst_ref[:, pl.ds(k0, SEL_SUB)], pr)

    scores_into(s0_s, 0)

    def sel_step(i, carry):
        c = 2 * i
        scores_into(s1_s, c + 1)
        softmax_from(s0_s, c)
        scores_into(s0_s, c + 2)
        softmax_from(s1_s, c + 1)
        return carry

    n_main = (qb * Q_BLOCK + 2 * SEL_SUB - 1) // (2 * SEL_SUB)
    lax.fori_loop(0, n_main, sel_step, 0)
    inv_s = 1.0 / jnp.maximum(acc_s[NSA_DH:NSA_DH + 1, :], 1e-30)

    gall = gt_ref[...]
    ggrp = jnp.where(is_g0, gall[0:3 * NSA_REP], gall[3 * NSA_REP:3 * NSA_HEADS])
    for p2 in range(NSA_REP // 2):
        halves = []
        for hh in range(2):
            r = 2 * p2 + hh
            sl = slice(r * Q_BLOCK, (r + 1) * Q_BLOCK)
            g_c, g_s, g_w = (ggrp[3 * r + br:3 * r + br + 1, :] for br in range(3))
            halves.append((g_c * inv_c[:, sl]) * acc_c[0:NSA_DH, sl]
                          + (g_s * inv_s[:, sl]) * acc_s[0:NSA_DH, sl]
                          + (g_w * inv_w[:, sl]) * acc_w[0:NSA_DH, sl])
        o_ref[:, p2 * 128:(p2 + 1) * 128] = jnp.concatenate(halves, axis=0).T.astype(BF16)


def _nsa_attn(q_r, kc, vct, ks, vst, kw, vwt, gt, ovt):
    per_b = lambda r, c: pl.BlockSpec((None, r, c), lambda b, g, i: (b, 0, 0))
    per_bg = lambda r, c: pl.BlockSpec((None, r, c), lambda b, g, i: (b, g, 0))
    const = lambda a: pl.BlockSpec(a.shape, lambda b, g, i: (0, 0))
    return pl.pallas_call(
        _nsa_body,
        grid=(BATCH, NSA_KV, N_QB),
        in_specs=[pl.BlockSpec((None, NSA_REP * NSA_DH, Q_BLOCK), lambda b, g, i: (b, g, i)),
                  per_b(N_CMP_PAD, 128), per_bg(NSA_VROWS, N_CMP_PAD),
                  per_b(SEQ, 256), per_bg(NSA_VROWS, SEQ),
                  per_b(SEQ, 128), per_bg(NSA_VROWS, SEQ),
                  pl.BlockSpec((None, 3 * NSA_HEADS, Q_BLOCK), lambda b, g, i: (b, 0, i)),
                  const(ovt)],
        out_specs=pl.BlockSpec((None, Q_BLOCK, NSA_REP * NSA_DH), lambda b, g, i: (b, i, g)),
        out_shape=jax.ShapeDtypeStruct((BATCH, SEQ, NSA_WIDTH), BF16),
        scratch_shapes=[pltpu.VMEM((256, NSA_NL), BF16),
                        pltpu.VMEM((N_SEL, Q_BLOCK), F32), pltpu.VMEM((N_SEL, Q_BLOCK), F32),
                        pltpu.VMEM((SEL_SUB, NSA_NL), BF16), pltpu.VMEM((SEL_SUB, NSA_NL), BF16),
                        pltpu.VMEM((N_CMP_PAD, NSA_NL), BF16), pltpu.VMEM((WIN_KEYS, NSA_NL), BF16),
                        pltpu.VMEM((Q_BLOCK, NSA_NL), BF16),
                        pltpu.VMEM((1, NSA_NL), F32), pltpu.VMEM((NSA_VROWS, NSA_NL), F32)],
        compiler_params=_cparams("parallel", "parallel", "arbitrary"),
        name="nsa_attn",
    )(q_r, kc, vct, ks, vst, kw, vwt, gt, ovt)


OUT_TM = 512


def _outx_body(x_ref, oh_ref, on_ref, w_ref, nw_ref, wq_ref, k_ref, v_ref, wo_ref, o_ref):
    y = (x_ref[...] + _dot(oh_ref[...], w_ref[0:HG_WIDTH, :])
         + _dot(on_ref[...], w_ref[HG_WIDTH:HG_WIDTH + NSA_WIDTH, :]))
    hx = _rms(y, nw_ref[...]).astype(BF16)
    q = (_dot(hx, wq_ref[...]) * (X_DH ** -0.5)).astype(BF16)
    heads = []
    for h in range(X_HEADS):
        sl = slice(h * X_DH, (h + 1) * X_DH)
        s = _dot_nt(q[:, sl], k_ref[:, sl])
        e = jnp.exp(s - jnp.max(s, axis=-1, keepdims=True))
        p = e / jnp.sum(e, axis=-1, keepdims=True)
        heads.append(_dot(p.astype(BF16), v_ref[:, sl]))
    o_ref[...] = y + _dot(jnp.concatenate(heads, axis=1).astype(BF16), wo_ref[...])


def _outproj_xattn(x1, o_hg, o_nsa, w_out, nw, wq, k, v, wo):
    width = X_HEADS * X_DH
    tiles_per_b = SEQ // OUT_TM
    row = lambda w: pl.BlockSpec((OUT_TM, w), lambda i: (i, 0))
    const = lambda r, c: pl.BlockSpec((r, c), lambda i: (0, 0))
    kv = pl.BlockSpec((None, MEM_LEN, width), lambda i: (i // tiles_per_b, 0, 0))
    return pl.pallas_call(
        _outx_body,
        grid=(TOKENS // OUT_TM,),
        in_specs=[row(D_MODEL), row(HG_WIDTH), row(NSA_WIDTH), const(D_MODEL, D_MODEL), const(1, D_MODEL),
                  const(D_MODEL, width), kv, kv, const(width, D_MODEL)],
        out_specs=row(D_MODEL),
        out_shape=jax.ShapeDtypeStruct((TOKENS, D_MODEL), F32),
        compiler_params=_cparams("parallel"),
        name="out_proj_xattn",
    )(x1, o_hg, o_nsa, w_out, nw, wq, k, v, wo)


def _memkv_body(m_ref, nw_ref, wk_ref, wv_ref, k_ref, v_ref):
    hm = _rms(m_ref[...], nw_ref[...]).astype(BF16)
    k_ref[...] = _dot(hm, wk_ref[...]).astype(BF16)
    v_ref[...] = _dot(hm, wv_ref[...]).astype(BF16)


def _memkv(mem, nw, wk, wv):
    width = X_HEADS * X_DH
    wspec = pl.BlockSpec((D_MODEL, width), lambda b: (0, 0))
    ospec = pl.BlockSpec((None, MEM_LEN, width), lambda b: (b, 0, 0))
    osh = jax.ShapeDtypeStruct((BATCH, MEM_LEN, width), BF16)
    return pl.pallas_call(
        _memkv_body,
        grid=(BATCH,),
        in_specs=[pl.BlockSpec((None, MEM_LEN, D_MODEL), lambda b: (b, 0, 0)),
                  pl.BlockSpec((1, D_MODEL), lambda b: (0, 0)), wspec, wspec],
        out_specs=(ospec, ospec), out_shape=(osh, osh),
        compiler_params=_cparams("parallel"),
        name="xattn_memkv",
    )(mem, nw, wk, wv)


def _overlap_t():
    c0 = np.arange(N_CMP)[:, None] * CMP_STRIDE
    s0 = np.arange(N_SEL)[None, :] * SEL_LEN
    ov = np.clip(np.minimum(c0 + CMP_LEN, s0 + SEL_LEN) - np.maximum(c0, s0), 0, None) / CMP_LEN
    out = np.zeros((N_SEL, N_CMP_PAD), np.float32)
    out[:, :N_CMP] = ov.T
    return out


def kernel(x, mem, positions, ffn1_norm, ffn1_w_gate, ffn1_w_up, ffn1_w_down, mix_norm, w_in, hgrn_lb_logits, hgrn_out_norm, nsa_cmp_pe, nsa_cmp_k_w1, nsa_cmp_k_w2, nsa_cmp_v_w1, nsa_cmp_v_w2, w_out, xattn_norm, mem_norm, xattn_wq, xattn_wk, xattn_wv, xattn_wo, ffn2_norm, ffn2_w_gate, ffn2_w_up, ffn2_w_down, final_norm):
    bf = lambda a: a.astype(BF16)
    vec = lambda a: a.reshape(1, -1).astype(F32)
    x2d = x.reshape(TOKENS, D_MODEL)

    x1, h_mix, w2_gate, w2_up, w2_down, w_out_b, wq_b, wk_b, wv_b, wo_b = _ffn(
        x2d, vec(ffn1_norm[0]), bf(ffn1_w_gate[0]), bf(ffn1_w_up[0]), bf(ffn1_w_down[0]),
        vec(mix_norm[0]), final=False,
        step_casts=(ffn2_w_gate[0], ffn2_w_up[0], ffn2_w_down[0]),
        tile_casts=(w_out[0], xattn_wq[0], xattn_wk[0], xattn_wv[0], xattn_wo[0]))

    w_t = w_in[0].T
    w_tail = jnp.pad(w_t[PROJ_FULL_TILES * PROJ_TN:], ((0, PROJ_W - D_IN), (0, 0)))
    proj, proj_f = _proj(h_mix, w_t, w_tail)

    o_hg = _hgrn(proj, proj_f, hgrn_lb_logits.astype(F32), vec(hgrn_out_norm[0]))

    inv = ROPE_THETA ** (-jnp.arange(NSA_DH // 2, dtype=F32) / (NSA_DH // 2))
    inv128 = jnp.tile(inv, 128 // (NSA_DH // 2)).reshape(1, 128)
    pos_f = positions.astype(F32).reshape(BATCH, SEQ, 1)
    q_r, kc_tok, vc_tok, ks, kw, vst, vwt, gt = _nsa_prep(proj, pos_f, inv128)

    def over_groups(w):
        z = jnp.zeros_like(w)
        return bf(jnp.concatenate([jnp.concatenate([w, z], axis=-1), jnp.concatenate([z, w], axis=-1)], axis=-2))

    per_pos = lambda w1: over_groups(w1.reshape(CMP_LEN, NSA_DH, CMP_HIDDEN))
    pe = nsa_cmp_pe[0].astype(F32)
    kc, vct = _compress(kc_tok, vc_tok, jnp.concatenate([pe, pe], axis=1),
                        per_pos(nsa_cmp_k_w1[0]), over_groups(nsa_cmp_k_w2[0]),
                        per_pos(nsa_cmp_v_w1[0]), over_groups(nsa_cmp_v_w2[0]))
    o_nsa = _nsa_attn(q_r, kc, vct, ks, vst, kw, vwt, gt, jnp.asarray(_overlap_t(), dtype=BF16))

    km, vm = _memkv(mem, vec(mem_norm[0]), wk_b, wv_b)
    x3 = _outproj_xattn(x1, o_hg.reshape(TOKENS, HG_WIDTH), o_nsa.reshape(TOKENS, NSA_WIDTH),
                        w_out_b, vec(xattn_norm[0]), wq_b, km, vm, wo_b)

    (out,) = _ffn(x3, vec(ffn2_norm[0]), w2_gate, w2_up, w2_down, vec(final_norm), final=True)
    return out.reshape(BATCH, SEQ, D_MODEL)
```

```python
import functools

import numpy as np
import jax
import jax.numpy as jnp
from jax import lax
from jax.experimental import pallas as pl
from jax.experimental.pallas import tpu as pltpu

F32 = jnp.float32
BF16 = jnp.bfloat16

D_MODEL = 2048
BATCH = 2
SEQ = 4096
TOKENS = BATCH * SEQ
RMS_EPS = 1e-6
ROPE_THETA = 10000.0
HG_WIDTH = 1024
HG_HEADS = 8
HG_D = 128
HG_CHUNK = 128
HG_LEVELS = (64, 32, 16, 8, 4, 2, 1)
NSA_WIDTH = 1024
NSA_DH = 64
NSA_HEADS = 16
NSA_KV = 2
NSA_REP = 8
NSA_VROWS = NSA_DH + 16
CMP_LEN = 32
CMP_STRIDE = 16
CMP_HIDDEN = 256
N_CMP = (SEQ - CMP_LEN) // CMP_STRIDE + 1
N_CMP_PAD = 256
SEL_LEN = 64
N_SEL = SEQ // SEL_LEN
SEL_TOP = 16
WINDOW = 512
Q_BLOCK = 128
N_QB = SEQ // Q_BLOCK
SEL_SUB = 256
WIN_KEYS = WINDOW + Q_BLOCK
MEM_LEN = 256
X_HEADS = 4
X_DH = 128
D_FF = 5632
IN_SIZES = (1024, 1024, 1024, 1024, 1024, 128, 128, 128, 128, 128, 128, 48)
D_IN = sum(IN_SIZES)
PROJ_NSA_OFF = sum(IN_SIZES[:4])
NSA_PROJ = 2048
PROJ_W = PROJ_NSA_OFF + NSA_PROJ
NEG = -1e30
LOG2E = 1.4426950408889634

V7X_VMEM_BYTES = 64 * 1024 * 1024
VMEM_LIMIT = V7X_VMEM_BYTES - 8 * 1024 * 1024


def _cparams(*sem, flags=None):
    return pltpu.CompilerParams(dimension_semantics=sem, vmem_limit_bytes=VMEM_LIMIT, flags=flags)


def _rms(x, w):
    return x * lax.rsqrt(jnp.mean(x * x, axis=-1, keepdims=True) + RMS_EPS) * w


def _silu(x):
    return x * jax.nn.sigmoid(x)


def _dot(a, b):
    return jnp.dot(a, b, preferred_element_type=F32)


def _dot_f32_by_01(sel, x):
    hi = x.astype(BF16)
    r1 = x - hi.astype(F32)
    mid = r1.astype(BF16)
    lo = (r1 - mid.astype(F32)).astype(BF16)
    n = x.shape[1]
    y = _dot(sel, jnp.concatenate([hi, mid, lo], axis=1))
    return y[:, 0:n] + y[:, n:2 * n] + y[:, 2 * n:3 * n]


def _dot_nt(a, b):
    return lax.dot_general(a, b, (((1,), (1,)), ((), ())), preferred_element_type=F32)


FFN_TM = 512
FFN_TF = 512
FFN_HEAD_TF = 256


def _ffn_body(x_ref, nw_ref, wg_ref, wu_ref, wd_ref, nw2_ref, *rest,
              final, n_step_casts, n_tile_casts, has_head, f32_weights):
    n_casts = n_step_casts + n_tile_casts
    cast_in, rest = rest[:n_casts], rest[n_casts:]
    if has_head:
        (head_x_ref, head_hn_ref), rest = rest[:2], rest[2:]
    if final:
        o_ref, rest = rest[0], rest[1:]
    else:
        (o_ref, hn_ref), rest = rest[:2], rest[2:]
    cast_out, rest = rest[:n_casts], rest[n_casts:]
    if f32_weights:
        w_copy, rest = rest[:3], rest[3:]
    (h_scr,) = rest
    i = pl.program_id(0)
    j = pl.program_id(1)
    last = pl.num_programs(1) - 1

    def swiglu_tile(h):
        wg, wu, wd = wg_ref[...], wu_ref[...], wd_ref[...]
        if f32_weights:
            wg, wu, wd = wg.astype(BF16), wu.astype(BF16), wd.astype(BF16)
            for dst, w in zip(w_copy, (wg, wu, wd)):
                dst[...] = w
        g = _dot(h, wg)
        u = _dot(h, wu)
        return _dot((_silu(g) * u).astype(BF16), wd)

    def ride_along(first):
        n = n_casts if first else n_step_casts
        for src, dst in zip(cast_in[:n], cast_out[:n]):
            dst[...] = src[...].astype(BF16)

    def when(cond):
        return pl.when(cond & (i > 0)) if has_head else pl.when(cond)

    if has_head:
        @pl.when((i == 0) & (j == 0))
        def _():
            o_ref[...] = head_x_ref[...]
            hn_ref[...] = head_hn_ref[...]
            ride_along(True)

        @pl.when((i == 0) & (j > 0))
        def _():
            ride_along(False)

    @when(j == 0)
    def _():
        h = _rms(x_ref[...], nw_ref[...]).astype(BF16)
        h_scr[...] = h
        o_ref[...] = swiglu_tile(h)
        ride_along(True)

    @when((j > 0) & (j < last))
    def _():
        o_ref[...] += swiglu_tile(h_scr[...])
        ride_along(False)

    @when(j == last)
    def _():
        y = x_ref[...] + 0.5 * (o_ref[...] + swiglu_tile(h_scr[...]))
        if final:
            o_ref[...] = _rms(y, nw2_ref[...])
        else:
            o_ref[...] = y
            hn_ref[...] = _rms(y, nw2_ref[...]).astype(BF16)
        ride_along(False)


def _ffn(x, nw, wg, wu, wd, nw2, final, step_casts=(), tile_casts=(), head=None, n_tiles=None, tf=FFN_TF):
    ni, nj = (n_tiles or TOKENS // FFN_TM), D_FF // tf
    f32_weights = wg.dtype == F32
    row = pl.BlockSpec((FFN_TM, D_MODEL), lambda i, j: (i, 0))
    vec = pl.BlockSpec((1, D_MODEL), lambda i, j: (0, 0))
    wj = (lambda i, j: jnp.where(i == 0, 0, j)) if head is not None else (lambda i, j: j)
    in_specs = [row, vec,
                pl.BlockSpec((D_MODEL, tf), lambda i, j: (0, wj(i, j))),
                pl.BlockSpec((D_MODEL, tf), lambda i, j: (0, wj(i, j))),
                pl.BlockSpec((tf, D_MODEL), lambda i, j: (wj(i, j), 0)),
                vec]
    cast_specs = []
    for a in step_casts:
        r, c = a.shape
        if r % ni == 0 and c % nj == 0:
            cast_specs.append(pl.BlockSpec((r // ni, c // nj), lambda i, j: (i, j)))
        else:
            cast_specs.append(pl.BlockSpec((r // nj, c // ni), lambda i, j: (j, i)))
    for a in tile_casts:
        cast_specs.append(pl.BlockSpec((a.shape[0] // ni, a.shape[1]), lambda i, j: (i, 0)))
    casts = tuple(step_casts) + tuple(tile_casts)
    cast_shapes = [jax.ShapeDtypeStruct(a.shape, BF16) for a in casts]
    main_shapes = [jax.ShapeDtypeStruct((ni * FFN_TM, D_MODEL), F32)]
    if not final:
        main_shapes.append(jax.ShapeDtypeStruct((ni * FFN_TM, D_MODEL), BF16))
    head_specs = [pl.BlockSpec((FFN_TM, D_MODEL), lambda i, j: (0, 0))] * 2 if head is not None else []
    w_copy_specs = in_specs[2:5] if f32_weights else []
    w_copy_shapes = [jax.ShapeDtypeStruct(w.shape, BF16) for w in (wg, wu, wd)] if f32_weights else []
    return pl.pallas_call(
        functools.partial(_ffn_body, final=final, n_step_casts=len(step_casts), n_tile_casts=len(tile_casts),
                          has_head=head is not None, f32_weights=f32_weights),
        grid=(ni, nj), in_specs=in_specs + cast_specs + head_specs,
        out_specs=tuple([row] * len(main_shapes) + cast_specs + w_copy_specs),
        out_shape=tuple(main_shapes + cast_shapes + w_copy_shapes),
        scratch_shapes=[pltpu.VMEM((FFN_TM, D_MODEL), BF16)],
        compiler_params=_cparams("parallel", "arbitrary"),
        name="ffn_final" if final else ("ffn_head" if f32_weights else "ffn"),
    )(x, nw, wg, wu, wd, nw2, *casts, *(head or ()))


PROJ_TM = 1024
PROJ_TN = 512


PROJ_F_TILE0 = IN_SIZES[0] // PROJ_TN
PROJ_F_TILES = IN_SIZES[1] // PROJ_TN


PROJ_FULL_TILES = D_IN // PROJ_TN


def _proj_body(a_ref, wt_ref, tail_ref, o_ref, f_ref):
    j = pl.program_id(1)
    w = jnp.where(j >= PROJ_FULL_TILES, tail_ref[...], wt_ref[...]).astype(BF16)
    y = _dot_nt(a_ref[...], w)
    o_ref[...] = y.astype(BF16)

    @pl.when((j >= PROJ_F_TILE0) & (j < PROJ_F_TILE0 + PROJ_F_TILES))
    def _():
        f_ref[...] = y


def _proj(a, wt, wt_tail):
    m, k = a.shape
    f_tile = lambda i, j: (i, jnp.clip(j - PROJ_F_TILE0, 0, PROJ_F_TILES - 1))
    return pl.pallas_call(
        _proj_body,
        grid=(m // PROJ_TM, PROJ_W // PROJ_TN),
        in_specs=[pl.BlockSpec((PROJ_TM, k), lambda i, j: (i, 0)),
                  pl.BlockSpec((PROJ_TN, k), lambda i, j: (jnp.minimum(j, PROJ_FULL_TILES - 1), 0)),
                  pl.BlockSpec((PROJ_TN, k), lambda i, j: (0, 0))],
        out_specs=(pl.BlockSpec((PROJ_TM, PROJ_TN), lambda i, j: (i, j)),
                   pl.BlockSpec((PROJ_TM, PROJ_TN), f_tile)),
        out_shape=(jax.ShapeDtypeStruct((m, PROJ_W), BF16),
                   jax.ShapeDtypeStruct((m, IN_SIZES[1]), F32)),
        compiler_params=_cparams("parallel", "arbitrary"),
        name="proj_in",
    )(a, wt, wt_tail)


HG_ROWS = 1024
HG_CUM = 256


def _hgrn_body(q_ref, f_ref, i_ref, g_ref, lbl_ref, nw_ref, o_ref, st_ref, k_s, b_s):
    c = pl.program_id(2)

    @pl.when(c == 0)
    def _():
        st_ref[...] = jnp.zeros_like(st_ref)

    l0 = lbl_ref[0:1, :]
    l1 = lbl_ref[1:2, :]
    lmax = jnp.maximum(l0, l1)
    e0 = jnp.exp(l0 - lmax)
    lb = e0 / (e0 + jnp.exp(l1 - lmax))

    C = HG_CHUNK
    f = lb + (1.0 - lb) * jax.nn.sigmoid(f_ref[...])
    k_s[...] = 1.0 - f
    r_i = lax.broadcasted_iota(jnp.int32, (HG_CUM, HG_CUM), 0)
    c_i = lax.broadcasted_iota(jnp.int32, (HG_CUM, HG_CUM), 1)
    tri = jnp.where((r_i >= c_i) & (r_i // C == c_i // C), 1.0, 0.0).astype(BF16)
    logf = jnp.log2(f)
    for r0 in range(0, HG_ROWS, HG_CUM):
        b_s[r0:r0 + HG_CUM, :] = _dot_f32_by_01(tri, logf[r0:r0 + HG_CUM])

    t_i = lax.broadcasted_iota(jnp.int32, (C, C), 0)
    s_i = lax.broadcasted_iota(jnp.int32, (C, C), 1)
    level_mask = [(t_i // (2 * w) == s_i // (2 * w)) & (t_i % (2 * w) >= w) & (s_i % (2 * w) < w)
                  for w in HG_LEVELS]
    sub_r = lax.broadcasted_iota(jnp.int32, (8, HG_D), 0)
    row_i = lax.broadcasted_iota(jnp.int32, (C, HG_D), 0)
    right_sign = {w: jnp.where(row_i % (2 * w) >= w, 1.0, -1.0) for w in HG_LEVELS if w < 8}

    def neg_abs_diff(w, r0, b):
        row = lambda r, n: jnp.broadcast_to(b_s[r0 + r:r0 + r + 1, :], (n, HG_D))
        if w >= 8:
            parts = []
            for p0 in range(0, C, 2 * w):
                ref = row(p0 + w - 1, w)
                parts += [ref - b[p0:p0 + w], b[p0 + w:p0 + 2 * w] - ref]
            return jnp.concatenate(parts, axis=0)
        if w == 4:
            bref = jnp.concatenate([row(p0 + 3, 8) for p0 in range(0, C, 8)], axis=0)
        elif w == 2:
            bref = jnp.concatenate([jnp.where(sub_r < 4, row(p0 + 1, 8), row(p0 + 5, 8))
                                    for p0 in range(0, C, 8)], axis=0)
        else:
            bref = jnp.where(row_i % 2 == 1, pltpu.roll(b, 1, 0), b)
        return (b - bref) * right_sign[w]

    chunks = [ci * C for ci in range(HG_ROWS // C)]
    rows = lambda ref, r0: ref[r0:r0 + C, :]
    att = [jnp.zeros((C, C), F32) for _ in chunks]
    q16 = [rows(q_ref, r0) for r0 in chunks]
    k16 = [rows(k_s, r0).astype(BF16) for r0 in chunks]
    for w, mask in zip(HG_LEVELS, level_mask):
        for n, r0 in enumerate(chunks):
            b = rows(b_s, r0)
            e = jnp.exp2(neg_abs_diff(w, r0, b)).astype(BF16)
            att[n] = jnp.where(mask, _dot_nt(q16[n] * e, k16[n] * e), att[n])
    o_intra = []
    for n, r0 in enumerate(chunks):
        q, k, v = q16[n].astype(F32), rows(k_s, r0), rows(i_ref, r0)
        o_intra.append(_dot(att[n].astype(BF16), v)
                       + jnp.sum(q * k, axis=-1, keepdims=True) * v.astype(F32))
    upd = []
    for r0 in chunks:
        bl = b_s[r0 + C - 1:r0 + C, :]
        kd = rows(k_s, r0) * jnp.exp2(bl - rows(b_s, r0))
        v_t = rows(i_ref, r0).astype(F32).T.astype(BF16)
        upd.append((jnp.exp2(bl), _dot(v_t, kd.astype(BF16))))
    st_t = st_ref[...]
    for n, r0 in enumerate(chunks):
        qe = (q16[n].astype(F32) * jnp.exp2(rows(b_s, r0))).astype(BF16)
        o = o_intra[n] + _dot_nt(qe, st_t.astype(BF16))
        st_t = st_t * upd[n][0] + upd[n][1]
        o = o * lax.rsqrt(jnp.mean(o * o, axis=-1, keepdims=True) + RMS_EPS)
        o_ref[r0:r0 + C, :] = (o * nw_ref[...] * _silu(rows(g_ref, r0).astype(F32))).astype(BF16)
    st_ref[...] = st_t


def _hgrn(proj, proj_f, lb_logits, norm_w):
    p3 = proj.reshape(BATCH, SEQ, PROJ_W)
    f3 = proj_f.reshape(BATCH, SEQ, HG_WIDTH)

    def col(off):
        return pl.BlockSpec((None, HG_ROWS, HG_D), lambda b, h, c: (b, c, off + h))

    return pl.pallas_call(
        _hgrn_body,
        grid=(BATCH, HG_HEADS, SEQ // HG_ROWS),
        in_specs=[col(0), col(0), col(2 * HG_HEADS), col(3 * HG_HEADS),
                  pl.BlockSpec((2, HG_D), lambda b, h, c: (0, h)),
                  pl.BlockSpec((1, HG_D), lambda b, h, c: (0, h))],
        out_specs=pl.BlockSpec((None, HG_ROWS, HG_D), lambda b, h, c: (b, c, h)),
        out_shape=jax.ShapeDtypeStruct((BATCH, SEQ, HG_WIDTH), BF16),
        scratch_shapes=[pltpu.VMEM((HG_D, HG_D), F32),
                        pltpu.VMEM((HG_ROWS, HG_D), F32),
                        pltpu.VMEM((HG_ROWS, HG_D), F32)],
        compiler_params=_cparams("parallel", "parallel", "arbitrary"),
        name="hgrn2",
    )(p3, f3, p3, p3, lb_logits, norm_w)


PREP_TM = 256


def _prep_body(p_ref, pos_ref, inv_ref, q_ref, kc_ref, vc_ref, ks_ref, kw_ref,
               vst_ref, vwt_ref, gt_ref):
    ang = pos_ref[...] * inv_ref[...]
    cos = jnp.cos(ang)
    sin = jnp.sin(ang)
    lane = lax.broadcasted_iota(jnp.int32, (PREP_TM, 128), 1)
    lo = (lane & (NSA_DH // 2)) == 0
    sin_signed = jnp.where(lo, -sin, sin)

    def rope(x):
        rot = jnp.where(lo, pltpu.roll(x, 128 - NSA_DH // 2, 1), pltpu.roll(x, NSA_DH // 2, 1))
        return x * cos + rot * sin_signed

    cols = lambda c0: p_ref[:, c0:c0 + 128].astype(F32)
    scale = NSA_DH ** -0.5 * LOG2E
    for cblk in range(NSA_WIDTH // 128):
        q_ref[cblk * 128:(cblk + 1) * 128, :] = (rope(cols(cblk * 128)) * scale).T.astype(BF16)
    kc_ref[...] = rope(cols(1024))
    vc_ref[...] = cols(1152)
    ks_ref[:, 0:128] = rope(cols(1280)).astype(BF16)
    blk = (pl.program_id(1) * PREP_TM + lax.broadcasted_iota(jnp.int32, (PREP_TM, 128), 0)) // SEL_LEN
    ks_ref[:, 128:256] = jnp.where(lane == blk, 1.0, 0.0).astype(BF16)
    kw_ref[...] = rope(cols(1536)).astype(BF16)
    ones = jnp.ones((NSA_VROWS - NSA_DH, PREP_TM), BF16)
    for v_ref, c0 in ((vst_ref, 1408), (vwt_ref, 1664)):
        vt = cols(c0).T.astype(BF16)
        for g in range(NSA_KV):
            v_ref[g * NSA_VROWS:g * NSA_VROWS + NSA_DH, :] = vt[g * NSA_DH:(g + 1) * NSA_DH]
            v_ref[g * NSA_VROWS + NSA_DH:(g + 1) * NSA_VROWS, :] = ones
    gt_ref[...] = jax.nn.sigmoid(cols(1792)).T[0:3 * NSA_HEADS, :]


def _nsa_prep(proj, pos_f, inv128):
    nt = SEQ // PREP_TM
    p3 = proj.reshape(BATCH, SEQ, PROJ_W)
    nat = lambda w: pl.BlockSpec((None, PREP_TM, w), lambda b, i: (b, i, 0))
    tr = lambda r: pl.BlockSpec((None, r, PREP_TM), lambda b, i: (b, 0, i))
    sds = jax.ShapeDtypeStruct
    return pl.pallas_call(
        _prep_body,
        grid=(BATCH, nt),
        in_specs=[pl.BlockSpec((None, PREP_TM, NSA_PROJ), lambda b, i: (b, i, PROJ_NSA_OFF // NSA_PROJ)),
                  nat(1), pl.BlockSpec((1, 128), lambda b, i: (0, 0))],
        out_specs=(tr(NSA_WIDTH), nat(128), nat(128), nat(256), nat(128),
                   tr(NSA_KV * NSA_VROWS), tr(NSA_KV * NSA_VROWS), tr(3 * NSA_HEADS)),
        out_shape=(sds((BATCH, NSA_WIDTH, SEQ), BF16),
                   sds((BATCH, SEQ, 128), F32),
                   sds((BATCH, SEQ, 128), F32),
                   sds((BATCH, SEQ, 256), BF16),
                   sds((BATCH, SEQ, 128), BF16),
                   sds((BATCH, NSA_KV * NSA_VROWS, SEQ), BF16),
                   sds((BATCH, NSA_KV * NSA_VROWS, SEQ), BF16),
                   sds((BATCH, 3 * NSA_HEADS, SEQ), F32)),
        compiler_params=_cparams("parallel", "parallel"),
        name="nsa_prep",
    )(p3, pos_f, inv128)


def _cmp_body(tk_ref, tv_ref, pe_ref, kw1_ref, kw2_ref, vw1_ref, vw2_ref, kc_ref, vct_ref, y1_s, y2_s):
    row = lax.broadcasted_iota(jnp.int32, (N_CMP_PAD, NSA_KV * CMP_HIDDEN), 0)

    def mlp(t_ref, w1_ref, w2_ref):
        y1_s[...] = jnp.zeros_like(y1_s)
        y2_s[...] = jnp.zeros_like(y2_s)
        for l in range(CMP_STRIDE):
            x = t_ref[pl.ds(l, N_CMP_PAD, stride=CMP_STRIDE), :]
            y1_s[...] += _dot((x + pe_ref[l:l + 1, :]).astype(BF16), w1_ref[l])
            y2_s[...] += _dot((x + pe_ref[CMP_STRIDE + l:CMP_STRIDE + l + 1, :]).astype(BF16),
                              w1_ref[CMP_STRIDE + l])
        hid = jnp.where(row < N_CMP, y1_s[...] + pltpu.roll(y2_s[...], N_CMP_PAD - 1, 0), 0.0)
        return _dot(_silu(hid).astype(BF16), w2_ref[...])

    kc_ref[...] = mlp(tk_ref, kw1_ref, kw2_ref).astype(BF16)
    y1_s[:, 0:128] = mlp(tv_ref, vw1_ref, vw2_ref)
    vt = y1_s[:, 0:128].T.astype(BF16)
    ones = jnp.ones((NSA_VROWS - NSA_DH, N_CMP_PAD), BF16)
    for g in range(NSA_KV):
        vct_ref[g * NSA_VROWS:g * NSA_VROWS + NSA_DH, :] = vt[g * NSA_DH:(g + 1) * NSA_DH]
        vct_ref[g * NSA_VROWS + NSA_DH:(g + 1) * NSA_VROWS, :] = ones


def _compress(tk, tv, pe2, kw1, kw2, vw1, vw2):
    seg = pl.BlockSpec((None, SEQ, 128), lambda b: (b, 0, 0))
    full2 = lambda a: pl.BlockSpec(a.shape, lambda b: (0,) * a.ndim)
    return pl.pallas_call(
        _cmp_body,
        grid=(BATCH,),
        in_specs=[seg, seg, full2(pe2), full2(kw1), full2(kw2), full2(vw1), full2(vw2)],
        out_specs=(pl.BlockSpec((None, N_CMP_PAD, 128), lambda b: (b, 0, 0)),
                   pl.BlockSpec((None, NSA_KV * NSA_VROWS, N_CMP_PAD), lambda b: (b, 0, 0))),
        out_shape=(jax.ShapeDtypeStruct((BATCH, N_CMP_PAD, 128), BF16),
                   jax.ShapeDtypeStruct((BATCH, NSA_KV * NSA_VROWS, N_CMP_PAD), BF16)),
        scratch_shapes=[pltpu.VMEM((N_CMP_PAD, NSA_KV * CMP_HIDDEN), F32)] * 2,
        compiler_params=_cparams("parallel"),
        name="nsa_compress",
    )(tk, tv, pe2, kw1, kw2, vw1, vw2)


NSA_NL = NSA_REP * Q_BLOCK


def _nsa_body(q_ref, kc_ref, vct_ref, ks_ref, vst_ref, kw_ref, vwt_ref, gt_ref, ovt_ref,
              o_ref, qa_s, sc_s, rk_s, s0_s, s1_s, sc_buf, sw_buf, sd_buf, m_s, acc_s):
    g = pl.program_id(1)
    qb = pl.program_id(2)
    q0 = pl.multiple_of(qb * Q_BLOCK, Q_BLOCK)
    is_g0 = g == 0

    zero_slab = jnp.zeros((NSA_DH, Q_BLOCK), BF16)
    for r in range(NSA_REP):
        s = q_ref[r * NSA_DH:(r + 1) * NSA_DH, :]
        qa_s[0:NSA_DH, r * Q_BLOCK:(r + 1) * Q_BLOCK] = jnp.where(is_g0, s, zero_slab)
        qa_s[NSA_DH:2 * NSA_DH, r * Q_BLOCK:(r + 1) * Q_BLOCK] = jnp.where(is_g0, zero_slab, s)
    qa_s[128 + N_SEL:256, :] = jnp.zeros((128 - N_SEL, NSA_NL), BF16)

    tq = q0 + lax.broadcasted_iota(jnp.int32, (1, Q_BLOCK), 1)

    def mask_pair(s, valid):
        return jnp.concatenate([jnp.where(valid, s[:, r * Q_BLOCK:(r + 1) * Q_BLOCK], NEG)
                                for r in range(NSA_REP)], axis=1)

    n_i = lax.broadcasted_iota(jnp.int32, (N_CMP_PAD, Q_BLOCK), 0)
    valid_c = (n_i * CMP_STRIDE + (CMP_LEN - 1) <= tq) & (n_i < N_CMP)
    sc_buf[...] = mask_pair(_dot(kc_ref[...], qa_s[0:128, :]).astype(BF16), valid_c)

    w0 = pl.multiple_of(jnp.maximum(q0 - WINDOW, 0), Q_BLOCK)
    dpos = tq - (w0 + lax.broadcasted_iota(jnp.int32, (WIN_KEYS, Q_BLOCK), 0))
    sw_buf[...] = mask_pair(_dot(kw_ref[pl.ds(w0, WIN_KEYS), :], qa_s[0:128, :]).astype(BF16),
                            (dpos >= 0) & (dpos < WINDOW))

    sc = sc_buf[...]
    m_c = jnp.max(sc, axis=0, keepdims=True)
    e_c = jnp.exp2(sc - m_c)
    acc_c = _dot(vct_ref[...], e_c)
    inv_c = jnp.where(m_c.astype(F32) > 0.5 * NEG, 1.0 / jnp.maximum(acc_c[NSA_DH:NSA_DH + 1], 1e-30), 0.0)
    imp_h = _dot(ovt_ref[...], e_c) * inv_c

    d_i = lax.broadcasted_iota(jnp.int32, (Q_BLOCK, Q_BLOCK), 0)
    t_i = lax.broadcasted_iota(jnp.int32, (Q_BLOCK, Q_BLOCK), 1)
    sd_buf[...] = mask_pair(_dot(ks_ref[pl.ds(q0, Q_BLOCK), 0:128], qa_s[0:128, :]).astype(BF16),
                            d_i <= t_i)

    sw = sw_buf[...]
    acc_w = _dot(vwt_ref[:, pl.ds(w0, WIN_KEYS)], jnp.exp2(sw - jnp.max(sw, axis=0, keepdims=True)))
    inv_w = 1.0 / jnp.maximum(acc_w[NSA_DH:NSA_DH + 1], 1e-30)

    s = sd_buf[...]
    m16 = jnp.max(s, axis=0, keepdims=True)
    m_s[...] = m16.astype(F32)
    acc_s[...] = _dot(vst_ref[:, pl.ds(q0, Q_BLOCK)], jnp.exp2(s - m16))

    imp = imp_h[:, 0:Q_BLOCK]
    for r in range(1, NSA_REP):
        imp = imp + imp_h[:, r * Q_BLOCK:(r + 1) * Q_BLOCK]
    j_i = lax.broadcasted_iota(jnp.int32, (N_SEL, Q_BLOCK), 0)
    cur = tq // SEL_LEN
    forced = (j_i == 0) | (j_i == cur) | (j_i == cur - 1)
    score = jnp.where(forced, jnp.inf, jnp.where(j_i > cur, -jnp.inf, imp))
    sc_s[...] = score
    rk_s[...] = jnp.zeros_like(rk_s)
    sub8 = lax.broadcasted_iota(jnp.int32, (8, Q_BLOCK), 0)
    for grp in range(N_SEL // 8):
        @pl.when(8 * grp <= 2 * qb + 1)
        def _():
            for v in range(N_SEL // 8):
                sv = sc_s[8 * v:8 * v + 8, :]
                part = jnp.zeros((8, Q_BLOCK), F32)
                for jp in range(8 * grp, 8 * grp + 8):
                    row = sc_s[jp:jp + 1, :]
                    if v > grp:
                        part = part + jnp.where(row >= sv, 1.0, 0.0)
                    elif v < grp:
                        part = part + jnp.where(row > sv, 1.0, 0.0)
                    else:
                        part = part + jnp.where(sub8 + 8 * v > jp, jnp.where(row >= sv, 1.0, 0.0),
                                                jnp.where(row > sv, 1.0, 0.0))
                rk_s[8 * v:8 * v + 8, :] += part
    bias = jnp.where((rk_s[...] < SEL_TOP) & (j_i < 2 * qb), 0.0, NEG).astype(BF16)
    for r in range(NSA_REP):
        qa_s[128:128 + N_SEL, r * Q_BLOCK:(r + 1) * Q_BLOCK] = bias

    last_sub = SEQ // SEL_SUB - 1

    def scores_into(buf, c):
        k0 = pl.multiple_of(jnp.minimum(c, last_sub) * SEL_SUB, SEL_SUB)
        buf[...] = _dot(ks_ref[pl.ds(k0, SEL_SUB), :], qa_s[...]).astype(BF16)

    def softmax_from(buf, c):
        k0 = pl.multiple_of(c * SEL_SUB, SEL_SUB)
        s = buf[...]
        m = m_s[...]
        m_new = jnp.maximum(m, jnp.max(s, axis=0, keepdims=True).astype(F32))
        m_s[...] = m_new
        pr = jnp.exp2(s - m_new.astype(BF16))
        acc_s[...] = acc_s[...] * jnp.exp2(m - m_new) + _dot(vst_ref[:, pl.ds(k0, SEL_SUB)], pr)

    scores_into(s0_s, 0)

    def sel_step(i, carry):
        c = 2 * i
        scores_into(s1_s, c + 1)
        softmax_from(s0_s, c)
        scores_into(s0_s, c + 2)
        softmax_from(s1_s, c + 1)
        return carry

    n_main = (qb * Q_BLOCK + 2 * SEL_SUB - 1) // (2 * SEL_SUB)
    lax.fori_loop(0, n_main, sel_step, 0)
    inv_s = 1.0 / jnp.maximum(acc_s[NSA_DH:NSA_DH + 1, :], 1e-30)

    gall = gt_ref[...]
    ggrp = jnp.where(is_g0, gall[0:3 * NSA_REP], gall[3 * NSA_REP:3 * NSA_HEADS])
    for p2 in range(NSA_REP // 2):
        halves = []
        for hh in range(2):
            r = 2 * p2 + hh
            sl = slice(r * Q_BLOCK, (r + 1) * Q_BLOCK)
            g_c, g_s, g_w = (ggrp[3 * r + br:3 * r + br + 1, :] for br in range(3))
            halves.append((g_c * inv_c[:, sl]) * acc_c[0:NSA_DH, sl]
                          + (g_s * inv_s[:, sl]) * acc_s[0:NSA_DH, sl]
                          + (g_w * inv_w[:, sl]) * acc_w[0:NSA_DH, sl])
        o_ref[:, p2 * 128:(p2 + 1) * 128] = jnp.concatenate(halves, axis=0).T.astype(BF16)


def _nsa_attn(q_r, kc, vct, ks, vst, kw, vwt, gt, ovt):
    per_b = lambda r, c: pl.BlockSpec((None, r, c), lambda b, g, i: (b, 0, 0))
    per_bg = lambda r, c: pl.BlockSpec((None, r, c), lambda b, g, i: (b, g, 0))
    const = lambda a: pl.BlockSpec(a.shape, lambda b, g, i: (0, 0))
    return pl.pallas_call(
        _nsa_body,
        grid=(BATCH, NSA_KV, N_QB),
        in_specs=[pl.BlockSpec((None, NSA_REP * NSA_DH, Q_BLOCK), lambda b, g, i: (b, g, i)),
                  per_b(N_CMP_PAD, 128), per_bg(NSA_VROWS, N_CMP_PAD),
                  per_b(SEQ, 256), per_bg(NSA_VROWS, SEQ),
                  per_b(SEQ, 128), per_bg(NSA_VROWS, SEQ),
                  pl.BlockSpec((None, 3 * NSA_HEADS, Q_BLOCK), lambda b, g, i: (b, 0, i)),
                  const(ovt)],
        out_specs=pl.BlockSpec((None, Q_BLOCK, NSA_REP * NSA_DH), lambda b, g, i: (b, i, g)),
        out_shape=jax.ShapeDtypeStruct((BATCH, SEQ, NSA_WIDTH), BF16),
        scratch_shapes=[pltpu.VMEM((256, NSA_NL), BF16),
                        pltpu.VMEM((N_SEL, Q_BLOCK), F32), pltpu.VMEM((N_SEL, Q_BLOCK), F32),
                        pltpu.VMEM((SEL_SUB, NSA_NL), BF16), pltpu.VMEM((SEL_SUB, NSA_NL), BF16),
                        pltpu.VMEM((N_CMP_PAD, NSA_NL), BF16), pltpu.VMEM((WIN_KEYS, NSA_NL), BF16),
                        pltpu.VMEM((Q_BLOCK, NSA_NL), BF16),
                        pltpu.VMEM((1, NSA_NL), F32), pltpu.VMEM((NSA_VROWS, NSA_NL), F32)],
        compiler_params=_cparams("parallel", "parallel", "arbitrary"),
        name="nsa_attn",
    )(q_r, kc, vct, ks, vst, kw, vwt, gt, ovt)


OUT_TM = 512


def _outx_body(x_ref, oh_ref, on_ref, w_ref, nw_ref, wq_ref, k_ref, v_ref, wo_ref, o_ref):
    y = (x_ref[...] + _dot(oh_ref[...], w_ref[0:HG_WIDTH, :])
         + _dot(on_ref[...], w_ref[HG_WIDTH:HG_WIDTH + NSA_WIDTH, :]))
    hx = _rms(y, nw_ref[...]).astype(BF16)
    q = (_dot(hx, wq_ref[...]) * (X_DH ** -0.5)).astype(BF16)
    heads = []
    for h in range(X_HEADS):
        sl = slice(h * X_DH, (h + 1) * X_DH)
        s = _dot_nt(q[:, sl], k_ref[:, sl])
        e = jnp.exp(s - jnp.max(s, axis=-1, keepdims=True))
        p = e / jnp.sum(e, axis=-1, keepdims=True)
        heads.append(_dot(p.astype(BF16), v_ref[:, sl]))
    o_ref[...] = y + _dot(jnp.concatenate(heads, axis=1).astype(BF16), wo_ref[...])


def _outproj_xattn(x1, o_hg, o_nsa, w_out, nw, wq, k, v, wo):
    width = X_HEADS * X_DH
    tiles_per_b = SEQ // OUT_TM
    row = lambda w: pl.BlockSpec((OUT_TM, w), lambda i: (i, 0))
    const = lambda r, c: pl.BlockSpec((r, c), lambda i: (0, 0))
    kv = pl.BlockSpec((None, MEM_LEN, width), lambda i: (i // tiles_per_b, 0, 0))
    return pl.pallas_call(
        _outx_body,
        grid=(TOKENS // OUT_TM,),
        in_specs=[row(D_MODEL), row(HG_WIDTH), row(NSA_WIDTH), const(D_MODEL, D_MODEL), const(1, D_MODEL),
                  const(D_MODEL, width), kv, kv, const(width, D_MODEL)],
        out_specs=row(D_MODEL),
        out_shape=jax.ShapeDtypeStruct((TOKENS, D_MODEL), F32),
        compiler_params=_cparams("parallel"),
        name="out_proj_xattn",
    )(x1, o_hg, o_nsa, w_out, nw, wq, k, v, wo)


def _memkv_body(m_ref, nw_ref, wk_ref, wv_ref, k_ref, v_ref):
    hm = _rms(m_ref[...], nw_ref[...]).astype(BF16)
    k_ref[...] = _dot(hm, wk_ref[...]).astype(BF16)
    v_ref[...] = _dot(hm, wv_ref[...]).astype(BF16)


def _memkv(mem, nw, wk, wv):
    width = X_HEADS * X_DH
    wspec = pl.BlockSpec((D_MODEL, width), lambda b: (0, 0))
    ospec = pl.BlockSpec((None, MEM_LEN, width), lambda b: (b, 0, 0))
    osh = jax.ShapeDtypeStruct((BATCH, MEM_LEN, width), BF16)
    return pl.pallas_call(
        _memkv_body,
        grid=(BATCH,),
        in_specs=[pl.BlockSpec((None, MEM_LEN, D_MODEL), lambda b: (b, 0, 0)),
                  pl.BlockSpec((1, D_MODEL), lambda b: (0, 0)), wspec, wspec],
        out_specs=(ospec, ospec), out_shape=(osh, osh),
        compiler_params=_cparams("parallel"),
        name="xattn_memkv",
    )(mem, nw, wk, wv)


def _overlap_t():
    c0 = np.arange(N_CMP)[:, None] * CMP_STRIDE
    s0 = np.arange(N_SEL)[None, :] * SEL_LEN
    ov = np.clip(np.minimum(c0 + CMP_LEN, s0 + SEL_LEN) - np.maximum(c0, s0), 0, None) / CMP_LEN
    out = np.zeros((N_SEL, N_CMP_PAD), np.float32)
    out[:, :N_CMP] = ov.T
    return out


def kernel(x, mem, positions, ffn1_norm, ffn1_w_gate, ffn1_w_up, ffn1_w_down, mix_norm, w_in, hgrn_lb_logits, hgrn_out_norm, nsa_cmp_pe, nsa_cmp_k_w1, nsa_cmp_k_w2, nsa_cmp_v_w1, nsa_cmp_v_w2, w_out, xattn_norm, mem_norm, xattn_wq, xattn_wk, xattn_wv, xattn_wo, ffn2_norm, ffn2_w_gate, ffn2_w_up, ffn2_w_down, final_norm):
    bf = lambda a: a.astype(BF16)
    vec = lambda a: a.reshape(1, -1).astype(F32)
    x2d = x.reshape(TOKENS, D_MODEL)

    x1_head, h_head, w1_gate, w1_up, w1_down = _ffn(
        x2d, vec(ffn1_norm[0]), ffn1_w_gate[0], ffn1_w_up[0], ffn1_w_down[0], vec(mix_norm[0]),
        final=False, n_tiles=1, tf=FFN_HEAD_TF)
    x1, h_mix, w2_gate, w2_up, w2_down, w_out_b, wq_b, wk_b, wv_b, wo_b = _ffn(
        x2d, vec(ffn1_norm[0]), w1_gate, w1_up, w1_down, vec(mix_norm[0]), final=False,
        head=(x1_head, h_head),
        step_casts=(ffn2_w_gate[0], ffn2_w_up[0], ffn2_w_down[0]),
        tile_casts=(w_out[0], xattn_wq[0], xattn_wk[0], xattn_wv[0], xattn_wo[0]))

    w_t = w_in[0].T
    w_tail = jnp.pad(w_t[PROJ_FULL_TILES * PROJ_TN:], ((0, PROJ_W - D_IN), (0, 0)))
    proj, proj_f = _proj(h_mix, w_t, w_tail)

    o_hg = _hgrn(proj, proj_f, hgrn_lb_logits.astype(F32), vec(hgrn_out_norm[0]))

    inv = ROPE_THETA ** (-jnp.arange(NSA_DH // 2, dtype=F32) / (NSA_DH // 2))
    inv128 = jnp.tile(inv, 128 // (NSA_DH // 2)).reshape(1, 128)
    pos_f = positions.astype(F32).reshape(BATCH, SEQ, 1)
    q_r, kc_tok, vc_tok, ks, kw, vst, vwt, gt = _nsa_prep(proj, pos_f, inv128)

    def over_groups(w):
        z = jnp.zeros_like(w)
        return bf(jnp.concatenate([jnp.concatenate([w, z], axis=-1), jnp.concatenate([z, w], axis=-1)], axis=-2))

    per_pos = lambda w1: over_groups(w1.reshape(CMP_LEN, NSA_DH, CMP_HIDDEN))
    pe = nsa_cmp_pe[0].astype(F32)
    kc, vct = _compress(kc_tok, vc_tok, jnp.concatenate([pe, pe], axis=1),
                        per_pos(nsa_cmp_k_w1[0]), over_groups(nsa_cmp_k_w2[0]),
                        per_pos(nsa_cmp_v_w1[0]), over_groups(nsa_cmp_v_w2[0]))
    o_nsa = _nsa_attn(q_r, kc, vct, ks, vst, kw, vwt, gt, jnp.asarray(_overlap_t(), dtype=BF16))

    km, vm = _memkv(mem, vec(mem_norm[0]), wk_b, wv_b)
    x3 = _outproj_xattn(x1, o_hg.reshape(TOKENS, HG_WIDTH), o_nsa.reshape(TOKENS, NSA_WIDTH),
                        w_out_b, vec(xattn_norm[0]), wq_b, km, vm, wo_b)

    (out,) = _ffn(x3, vec(ffn2_norm[0]), w2_gate, w2_up, w2_down, vec(final_norm), final=True)
    return out.reshape(BATCH, SEQ, D_MODEL)
```

```python
import functools

import numpy as np
import jax
import jax.numpy as jnp
from jax import lax
from jax.experimental import pallas as pl
from jax.experimental.pallas import tpu as pltpu

F32 = jnp.float32
BF16 = jnp.bfloat16

D_MODEL = 2048
BATCH = 2
SEQ = 4096
TOKENS = BATCH * SEQ
RMS_EPS = 1e-6
ROPE_THETA = 10000.0
HG_WIDTH = 1024
HG_HEADS = 8
HG_D = 128
HG_CHUNK = 128
HG_LEVELS = (64, 32, 16, 8, 4, 2, 1)
NSA_WIDTH = 1024
NSA_DH = 64
NSA_HEADS = 16
NSA_KV = 2
NSA_REP = 8
NSA_VROWS = NSA_DH + 16
CMP_LEN = 32
CMP_STRIDE = 16
CMP_HIDDEN = 256
N_CMP = (SEQ - CMP_LEN) // CMP_STRIDE + 1
N_CMP_PAD = 256
SEL_LEN = 64
N_SEL = SEQ // SEL_LEN
SEL_TOP = 16
WINDOW = 512
Q_BLOCK = 128
N_QB = SEQ // Q_BLOCK
SEL_SUB = 256
WIN_KEYS = WINDOW + Q_BLOCK
MEM_LEN = 256
X_HEADS = 4
X_DH = 128
D_FF = 5632
IN_SIZES = (1024, 1024, 1024, 1024, 1024, 128, 128, 128, 128, 128, 128, 48)
D_IN = sum(IN_SIZES)
PROJ_NSA_OFF = sum(IN_SIZES[:4])
NSA_PROJ = 2048
PROJ_W = PROJ_NSA_OFF + NSA_PROJ
NEG = -1e30
LOG2E = 1.4426950408889634

V7X_VMEM_BYTES = 64 * 1024 * 1024
VMEM_LIMIT = V7X_VMEM_BYTES - 8 * 1024 * 1024


def _cparams(*sem, flags=None):
    return pltpu.CompilerParams(dimension_semantics=sem, vmem_limit_bytes=VMEM_LIMIT, flags=flags)


def _rms(x, w):
    return x * lax.rsqrt(jnp.mean(x * x, axis=-1, keepdims=True) + RMS_EPS) * w


def _silu(x):
    return x * jax.nn.sigmoid(x)


def _dot(a, b):
    return jnp.dot(a, b, preferred_element_type=F32)


def _dot_f32_by_01(sel, x):
    hi = x.astype(BF16)
    r1 = x - hi.astype(F32)
    mid = r1.astype(BF16)
    lo = (r1 - mid.astype(F32)).astype(BF16)
    n = x.shape[1]
    y = _dot(sel, jnp.concatenate([hi, mid, lo], axis=1))
    return y[:, 0:n] + y[:, n:2 * n] + y[:, 2 * n:3 * n]


def _dot_nt(a, b):
    return lax.dot_general(a, b, (((1,), (1,)), ((), ())), preferred_element_type=F32)


FFN_TM = 512
FFN_TF = 512
FFN_HEAD_TM = 1024
FFN_HEAD_TF = 256


def _ffn_body(x_ref, nw_ref, wg_ref, wu_ref, wd_ref, nw2_ref, *rest,
              final, n_step_casts, n_tile_casts, n_head, f32_weights):
    n_casts = n_step_casts + n_tile_casts
    cast_in, rest = rest[:n_casts], rest[n_casts:]
    if n_head:
        (head_x_ref, head_hn_ref), rest = rest[:2], rest[2:]
    if final:
        o_ref, rest = rest[0], rest[1:]
    else:
        (o_ref, hn_ref), rest = rest[:2], rest[2:]
    cast_out, rest = rest[:n_casts], rest[n_casts:]
    if f32_weights:
        w_copy, rest = rest[:3], rest[3:]
    (h_scr,) = rest
    i = pl.program_id(0)
    j = pl.program_id(1)
    last = pl.num_programs(1) - 1

    def swiglu_tile(h):
        wg, wu, wd = wg_ref[...], wu_ref[...], wd_ref[...]
        if f32_weights:
            wg, wu, wd = wg.astype(BF16), wu.astype(BF16), wd.astype(BF16)
            for dst, w in zip(w_copy, (wg, wu, wd)):
                dst[...] = w
        g = _dot(h, wg)
        u = _dot(h, wu)
        return _dot((_silu(g) * u).astype(BF16), wd)

    def ride_along(first):
        n = n_casts if first else n_step_casts
        for src, dst in zip(cast_in[:n], cast_out[:n]):
            dst[...] = src[...].astype(BF16)

    def when(cond):
        return pl.when(cond & (i >= n_head)) if n_head else pl.when(cond)

    if n_head:
        @pl.when((i < n_head) & (j == 0))
        def _():
            o_ref[...] = head_x_ref[...]
            hn_ref[...] = head_hn_ref[...]
            ride_along(True)

        @pl.when((i < n_head) & (j > 0))
        def _():
            ride_along(False)

    @when(j == 0)
    def _():
        h = _rms(x_ref[...], nw_ref[...]).astype(BF16)
        h_scr[...] = h
        o_ref[...] = swiglu_tile(h)
        ride_along(True)

    @when((j > 0) & (j < last))
    def _():
        o_ref[...] += swiglu_tile(h_scr[...])
        ride_along(False)

    @when(j == last)
    def _():
        y = x_ref[...] + 0.5 * (o_ref[...] + swiglu_tile(h_scr[...]))
        if final:
            o_ref[...] = _rms(y, nw2_ref[...])
        else:
            o_ref[...] = y
            hn_ref[...] = _rms(y, nw2_ref[...]).astype(BF16)
        ride_along(False)


def _ffn(x, nw, wg, wu, wd, nw2, final, step_casts=(), tile_casts=(), head=None,
         n_tiles=None, tm=FFN_TM, tf=FFN_TF):
    ni, nj = (n_tiles or TOKENS // tm), D_FF // tf
    f32_weights = wg.dtype == F32
    n_head = head[0].shape[0] // tm if head is not None else 0
    once = dict(pipeline_mode=pl.Buffered(1)) if ni == 1 else {}
    row = pl.BlockSpec((tm, D_MODEL), lambda i, j: (i, 0), **once)
    vec = pl.BlockSpec((1, D_MODEL), lambda i, j: (0, 0))
    wj = (lambda i, j: jnp.where(i < n_head, 0, j)) if n_head else (lambda i, j: j)
    in_specs = [row, vec,
                pl.BlockSpec((D_MODEL, tf), lambda i, j: (0, wj(i, j))),
                pl.BlockSpec((D_MODEL, tf), lambda i, j: (0, wj(i, j))),
                pl.BlockSpec((tf, D_MODEL), lambda i, j: (wj(i, j), 0)),
                vec]
    cast_specs = []
    for a in step_casts:
        r, c = a.shape
        if r % ni == 0 and c % nj == 0:
            cast_specs.append(pl.BlockSpec((r // ni, c // nj), lambda i, j: (i, j)))
        else:
            cast_specs.append(pl.BlockSpec((r // nj, c // ni), lambda i, j: (j, i)))
    for a in tile_casts:
        cast_specs.append(pl.BlockSpec((a.shape[0] // ni, a.shape[1]), lambda i, j: (i, 0)))
    casts = tuple(step_casts) + tuple(tile_casts)
    cast_shapes = [jax.ShapeDtypeStruct(a.shape, BF16) for a in casts]
    main_shapes = [jax.ShapeDtypeStruct((ni * tm, D_MODEL), F32)]
    if not final:
        main_shapes.append(jax.ShapeDtypeStruct((ni * tm, D_MODEL), BF16))
    head_specs = [pl.BlockSpec((tm, D_MODEL), lambda i, j: (jnp.minimum(i, n_head - 1), 0),
                               pipeline_mode=pl.Buffered(1))] * 2 if n_head else []
    w_copy_specs = in_specs[2:5] if f32_weights else []
    w_copy_shapes = [jax.ShapeDtypeStruct(w.shape, BF16) for w in (wg, wu, wd)] if f32_weights else []
    return pl.pallas_call(
        functools.partial(_ffn_body, final=final, n_step_casts=len(step_casts), n_tile_casts=len(tile_casts),
                          n_head=n_head, f32_weights=f32_weights),
        grid=(ni, nj), in_specs=in_specs + cast_specs + head_specs,
        out_specs=tuple([row] * len(main_shapes) + cast_specs + w_copy_specs),
        out_shape=tuple(main_shapes + cast_shapes + w_copy_shapes),
        scratch_shapes=[pltpu.VMEM((tm, D_MODEL), BF16)],
        compiler_params=_cparams("parallel", "arbitrary"),
        name="ffn_final" if final else ("ffn_head" if f32_weights else "ffn"),
    )(x, nw, wg, wu, wd, nw2, *casts, *(head or ()))


PROJ_TM = 1024
PROJ_TN = 512


PROJ_F_TILE0 = IN_SIZES[0] // PROJ_TN
PROJ_F_TILES = IN_SIZES[1] // PROJ_TN


PROJ_FULL_TILES = D_IN // PROJ_TN


def _proj_body(a_ref, wt_ref, tail_ref, o_ref, f_ref):
    j = pl.program_id(1)
    w = jnp.where(j >= PROJ_FULL_TILES, tail_ref[...], wt_ref[...]).astype(BF16)
    y = _dot_nt(a_ref[...], w)
    o_ref[...] = y.astype(BF16)

    @pl.when((j >= PROJ_F_TILE0) & (j < PROJ_F_TILE0 + PROJ_F_TILES))
    def _():
        f_ref[...] = y


def _proj(a, wt, wt_tail):
    m, k = a.shape
    f_tile = lambda i, j: (i, jnp.clip(j - PROJ_F_TILE0, 0, PROJ_F_TILES - 1))
    return pl.pallas_call(
        _proj_body,
        grid=(m // PROJ_TM, PROJ_W // PROJ_TN),
        in_specs=[pl.BlockSpec((PROJ_TM, k), lambda i, j: (i, 0)),
                  pl.BlockSpec((PROJ_TN, k), lambda i, j: (jnp.minimum(j, PROJ_FULL_TILES - 1), 0)),
                  pl.BlockSpec((PROJ_TN, k), lambda i, j: (0, 0))],
        out_specs=(pl.BlockSpec((PROJ_TM, PROJ_TN), lambda i, j: (i, j)),
                   pl.BlockSpec((PROJ_TM, PROJ_TN), f_tile)),
        out_shape=(jax.ShapeDtypeStruct((m, PROJ_W), BF16),
                   jax.ShapeDtypeStruct((m, IN_SIZES[1]), F32)),
        compiler_params=_cparams("parallel", "arbitrary"),
        name="proj_in",
    )(a, wt, wt_tail)


HG_ROWS = 1024
HG_CUM = 256


def _hgrn_body(q_ref, f_ref, i_ref, g_ref, lbl_ref, nw_ref, o_ref, st_ref, k_s, b_s):
    c = pl.program_id(2)

    @pl.when(c == 0)
    def _():
        st_ref[...] = jnp.zeros_like(st_ref)

    l0 = lbl_ref[0:1, :]
    l1 = lbl_ref[1:2, :]
    lmax = jnp.maximum(l0, l1)
    e0 = jnp.exp(l0 - lmax)
    lb = e0 / (e0 + jnp.exp(l1 - lmax))

    C = HG_CHUNK
    f = lb + (1.0 - lb) * jax.nn.sigmoid(f_ref[...])
    k_s[...] = 1.0 - f
    r_i = lax.broadcasted_iota(jnp.int32, (HG_CUM, HG_CUM), 0)
    c_i = lax.broadcasted_iota(jnp.int32, (HG_CUM, HG_CUM), 1)
    tri = jnp.where((r_i >= c_i) & (r_i // C == c_i // C), 1.0, 0.0).astype(BF16)
    logf = jnp.log2(f)
    for r0 in range(0, HG_ROWS, HG_CUM):
        b_s[r0:r0 + HG_CUM, :] = _dot_f32_by_01(tri, logf[r0:r0 + HG_CUM])

    t_i = lax.broadcasted_iota(jnp.int32, (C, C), 0)
    s_i = lax.broadcasted_iota(jnp.int32, (C, C), 1)
    level_mask = [(t_i // (2 * w) == s_i // (2 * w)) & (t_i % (2 * w) >= w) & (s_i % (2 * w) < w)
                  for w in HG_LEVELS]
    sub_r = lax.broadcasted_iota(jnp.int32, (8, HG_D), 0)
    row_i = lax.broadcasted_iota(jnp.int32, (C, HG_D), 0)
    right_sign = {w: jnp.where(row_i % (2 * w) >= w, 1.0, -1.0) for w in HG_LEVELS if w < 8}

    def neg_abs_diff(w, r0, b):
        row = lambda r, n: jnp.broadcast_to(b_s[r0 + r:r0 + r + 1, :], (n, HG_D))
        if w >= 8:
            parts = []
            for p0 in range(0, C, 2 * w):
                ref = row(p0 + w - 1, w)
                parts += [ref - b[p0:p0 + w], b[p0 + w:p0 + 2 * w] - ref]
            return jnp.concatenate(parts, axis=0)
        if w == 4:
            bref = jnp.concatenate([row(p0 + 3, 8) for p0 in range(0, C, 8)], axis=0)
        elif w == 2:
            bref = jnp.concatenate([jnp.where(sub_r < 4, row(p0 + 1, 8), row(p0 + 5, 8))
                                    for p0 in range(0, C, 8)], axis=0)
        else:
            bref = jnp.where(row_i % 2 == 1, pltpu.roll(b, 1, 0), b)
        return (b - bref) * right_sign[w]

    chunks = [ci * C for ci in range(HG_ROWS // C)]
    rows = lambda ref, r0: ref[r0:r0 + C, :]
    att = [jnp.zeros((C, C), F32) for _ in chunks]
    q16 = [rows(q_ref, r0) for r0 in chunks]
    k16 = [rows(k_s, r0).astype(BF16) for r0 in chunks]
    for w, mask in zip(HG_LEVELS, level_mask):
        for n, r0 in enumerate(chunks):
            b = rows(b_s, r0)
            e = jnp.exp2(neg_abs_diff(w, r0, b)).astype(BF16)
            att[n] = jnp.where(mask, _dot_nt(q16[n] * e, k16[n] * e), att[n])
    o_intra = []
    for n, r0 in enumerate(chunks):
        q, k, v = q16[n].astype(F32), rows(k_s, r0), rows(i_ref, r0)
        o_intra.append(_dot(att[n].astype(BF16), v)
                       + jnp.sum(q * k, axis=-1, keepdims=True) * v.astype(F32))
    upd = []
    for r0 in chunks:
        bl = b_s[r0 + C - 1:r0 + C, :]
        kd = rows(k_s, r0) * jnp.exp2(bl - rows(b_s, r0))
        v_t = rows(i_ref, r0).astype(F32).T.astype(BF16)
        upd.append((jnp.exp2(bl), _dot(v_t, kd.astype(BF16))))
    st_t = st_ref[...]
    for n, r0 in enumerate(chunks):
        qe = (q16[n].astype(F32) * jnp.exp2(rows(b_s, r0))).astype(BF16)
        o = o_intra[n] + _dot_nt(qe, st_t.astype(BF16))
        st_t = st_t * upd[n][0] + upd[n][1]
        o = o * lax.rsqrt(jnp.mean(o * o, axis=-1, keepdims=True) + RMS_EPS)
        o_ref[r0:r0 + C, :] = (o * nw_ref[...] * _silu(rows(g_ref, r0).astype(F32))).astype(BF16)
    st_ref[...] = st_t


def _hgrn(proj, proj_f, lb_logits, norm_w):
    p3 = proj.reshape(BATCH, SEQ, PROJ_W)
    f3 = proj_f.reshape(BATCH, SEQ, HG_WIDTH)

    def col(off):
        return pl.BlockSpec((None, HG_ROWS, HG_D), lambda b, h, c: (b, c, off + h))

    return pl.pallas_call(
        _hgrn_body,
        grid=(BATCH, HG_HEADS, SEQ // HG_ROWS),
        in_specs=[col(0), col(0), col(2 * HG_HEADS), col(3 * HG_HEADS),
                  pl.BlockSpec((2, HG_D), lambda b, h, c: (0, h)),
                  pl.BlockSpec((1, HG_D), lambda b, h, c: (0, h))],
        out_specs=pl.BlockSpec((None, HG_ROWS, HG_D), lambda b, h, c: (b, c, h)),
        out_shape=jax.ShapeDtypeStruct((BATCH, SEQ, HG_WIDTH), BF16),
        scratch_shapes=[pltpu.VMEM((HG_D, HG_D), F32),
                        pltpu.VMEM((HG_ROWS, HG_D), F32),
                        pltpu.VMEM((HG_ROWS, HG_D), F32)],
        compiler_params=_cparams("parallel", "parallel", "arbitrary"),
        name="hgrn2",
    )(p3, f3, p3, p3, lb_logits, norm_w)


PREP_TM = 256


def _prep_body(p_ref, pos_ref, inv_ref, q_ref, kc_ref, vc_ref, ks_ref, kw_ref,
               vst_ref, vwt_ref, gt_ref):
    ang = pos_ref[...] * inv_ref[...]
    cos = jnp.cos(ang)
    sin = jnp.sin(ang)
    lane = lax.broadcasted_iota(jnp.int32, (PREP_TM, 128), 1)
    lo = (lane & (NSA_DH // 2)) == 0
    sin_signed = jnp.where(lo, -sin, sin)

    def rope(x):
        rot = jnp.where(lo, pltpu.roll(x, 128 - NSA_DH // 2, 1), pltpu.roll(x, NSA_DH // 2, 1))
        return x * cos + rot * sin_signed

    cols = lambda c0: p_ref[:, c0:c0 + 128].astype(F32)
    scale = NSA_DH ** -0.5 * LOG2E
    for cblk in range(NSA_WIDTH // 128):
        q_ref[cblk * 128:(cblk + 1) * 128, :] = (rope(cols(cblk * 128)) * scale).T.astype(BF16)
    kc_ref[...] = rope(cols(1024))
    vc_ref[...] = cols(1152)
    ks_ref[:, 0:128] = rope(cols(1280)).astype(BF16)
    blk = (pl.program_id(1) * PREP_TM + lax.broadcasted_iota(jnp.int32, (PREP_TM, 128), 0)) // SEL_LEN
    ks_ref[:, 128:256] = jnp.where(lane == blk, 1.0, 0.0).astype(BF16)
    kw_ref[...] = rope(cols(1536)).astype(BF16)
    ones = jnp.ones((NSA_VROWS - NSA_DH, PREP_TM), BF16)
    for v_ref, c0 in ((vst_ref, 1408), (vwt_ref, 1664)):
        vt = cols(c0).T.astype(BF16)
        for g in range(NSA_KV):
            v_ref[g * NSA_VROWS:g * NSA_VROWS + NSA_DH, :] = vt[g * NSA_DH:(g + 1) * NSA_DH]
            v_ref[g * NSA_VROWS + NSA_DH:(g + 1) * NSA_VROWS, :] = ones
    gt_ref[...] = jax.nn.sigmoid(cols(1792)).T[0:3 * NSA_HEADS, :]


def _nsa_prep(proj, pos_f, inv128):
    nt = SEQ // PREP_TM
    p3 = proj.reshape(BATCH, SEQ, PROJ_W)
    nat = lambda w: pl.BlockSpec((None, PREP_TM, w), lambda b, i: (b, i, 0))
    tr = lambda r: pl.BlockSpec((None, r, PREP_TM), lambda b, i: (b, 0, i))
    sds = jax.ShapeDtypeStruct
    return pl.pallas_call(
        _prep_body,
        grid=(BATCH, nt),
        in_specs=[pl.BlockSpec((None, PREP_TM, NSA_PROJ), lambda b, i: (b, i, PROJ_NSA_OFF // NSA_PROJ)),
                  nat(1), pl.BlockSpec((1, 128), lambda b, i: (0, 0))],
        out_specs=(tr(NSA_WIDTH), nat(128), nat(128), nat(256), nat(128),
                   tr(NSA_KV * NSA_VROWS), tr(NSA_KV * NSA_VROWS), tr(3 * NSA_HEADS)),
        out_shape=(sds((BATCH, NSA_WIDTH, SEQ), BF16),
                   sds((BATCH, SEQ, 128), F32),
                   sds((BATCH, SEQ, 128), F32),
                   sds((BATCH, SEQ, 256), BF16),
                   sds((BATCH, SEQ, 128), BF16),
                   sds((BATCH, NSA_KV * NSA_VROWS, SEQ), BF16),
                   sds((BATCH, NSA_KV * NSA_VROWS, SEQ), BF16),
                   sds((BATCH, 3 * NSA_HEADS, SEQ), F32)),
        compiler_params=_cparams("parallel", "parallel"),
        name="nsa_prep",
    )(p3, pos_f, inv128)


def _cmp_body(tk_ref, tv_ref, pe_ref, kw1_ref, kw2_ref, vw1_ref, vw2_ref, kc_ref, vct_ref, y1_s, y2_s):
    row = lax.broadcasted_iota(jnp.int32, (N_CMP_PAD, NSA_KV * CMP_HIDDEN), 0)

    def mlp(t_ref, w1_ref, w2_ref):
        y1_s[...] = jnp.zeros_like(y1_s)
        y2_s[...] = jnp.zeros_like(y2_s)
        for l in range(CMP_STRIDE):
            x = t_ref[pl.ds(l, N_CMP_PAD, stride=CMP_STRIDE), :]
            y1_s[...] += _dot((x + pe_ref[l:l + 1, :]).astype(BF16), w1_ref[l])
            y2_s[...] += _dot((x + pe_ref[CMP_STRIDE + l:CMP_STRIDE + l + 1, :]).astype(BF16),
                              w1_ref[CMP_STRIDE + l])
        hid = jnp.where(row < N_CMP, y1_s[...] + pltpu.roll(y2_s[...], N_CMP_PAD - 1, 0), 0.0)
        return _dot(_silu(hid).astype(BF16), w2_ref[...])

    kc_ref[...] = mlp(tk_ref, kw1_ref, kw2_ref).astype(BF16)
    y1_s[:, 0:128] = mlp(tv_ref, vw1_ref, vw2_ref)
    vt = y1_s[:, 0:128].T.astype(BF16)
    ones = jnp.ones((NSA_VROWS - NSA_DH, N_CMP_PAD), BF16)
    for g in range(NSA_KV):
        vct_ref[g * NSA_VROWS:g * NSA_VROWS + NSA_DH, :] = vt[g * NSA_DH:(g + 1) * NSA_DH]
        vct_ref[g * NSA_VROWS + NSA_DH:(g + 1) * NSA_VROWS, :] = ones


def _compress(tk, tv, pe2, kw1, kw2, vw1, vw2):
    seg = pl.BlockSpec((None, SEQ, 128), lambda b: (b, 0, 0))
    full2 = lambda a: pl.BlockSpec(a.shape, lambda b: (0,) * a.ndim)
    return pl.pallas_call(
        _cmp_body,
        grid=(BATCH,),
        in_specs=[seg, seg, full2(pe2), full2(kw1), full2(kw2), full2(vw1), full2(vw2)],
        out_specs=(pl.BlockSpec((None, N_CMP_PAD, 128), lambda b: (b, 0, 0)),
                   pl.BlockSpec((None, NSA_KV * NSA_VROWS, N_CMP_PAD), lambda b: (b, 0, 0))),
        out_shape=(jax.ShapeDtypeStruct((BATCH, N_CMP_PAD, 128), BF16),
                   jax.ShapeDtypeStruct((BATCH, NSA_KV * NSA_VROWS, N_CMP_PAD), BF16)),
        scratch_shapes=[pltpu.VMEM((N_CMP_PAD, NSA_KV * CMP_HIDDEN), F32)] * 2,
        compiler_params=_cparams("parallel"),
        name="nsa_compress",
    )(tk, tv, pe2, kw1, kw2, vw1, vw2)


NSA_NL = NSA_REP * Q_BLOCK


def _nsa_body(q_ref, kc_ref, vct_ref, ks_ref, vst_ref, kw_ref, vwt_ref, gt_ref, ovt_ref,
              o_ref, qa_s, sc_s, rk_s, s0_s, s1_s, sc_buf, sw_buf, sd_buf, m_s, acc_s):
    g = pl.program_id(1)
    qb = pl.program_id(2)
    q0 = pl.multiple_of(qb * Q_BLOCK, Q_BLOCK)
    is_g0 = g == 0

    zero_slab = jnp.zeros((NSA_DH, Q_BLOCK), BF16)
    for r in range(NSA_REP):
        s = q_ref[r * NSA_DH:(r + 1) * NSA_DH, :]
        qa_s[0:NSA_DH, r * Q_BLOCK:(r + 1) * Q_BLOCK] = jnp.where(is_g0, s, zero_slab)
        qa_s[NSA_DH:2 * NSA_DH, r * Q_BLOCK:(r + 1) * Q_BLOCK] = jnp.where(is_g0, zero_slab, s)
    qa_s[128 + N_SEL:256, :] = jnp.zeros((128 - N_SEL, NSA_NL), BF16)

    tq = q0 + lax.broadcasted_iota(jnp.int32, (1, Q_BLOCK), 1)

    def mask_pair(s, valid):
        return jnp.concatenate([jnp.where(valid, s[:, r * Q_BLOCK:(r + 1) * Q_BLOCK], NEG)
                                for r in range(NSA_REP)], axis=1)

    n_i = lax.broadcasted_iota(jnp.int32, (N_CMP_PAD, Q_BLOCK), 0)
    valid_c = (n_i * CMP_STRIDE + (CMP_LEN - 1) <= tq) & (n_i < N_CMP)
    sc_buf[...] = mask_pair(_dot(kc_ref[...], qa_s[0:128, :]).astype(BF16), valid_c)

    w0 = pl.multiple_of(jnp.maximum(q0 - WINDOW, 0), Q_BLOCK)
    dpos = tq - (w0 + lax.broadcasted_iota(jnp.int32, (WIN_KEYS, Q_BLOCK), 0))
    sw_buf[...] = mask_pair(_dot(kw_ref[pl.ds(w0, WIN_KEYS), :], qa_s[0:128, :]).astype(BF16),
                            (dpos >= 0) & (dpos < WINDOW))

    sc = sc_buf[...]
    m_c = jnp.max(sc, axis=0, keepdims=True)
    e_c = jnp.exp2(sc - m_c)
    acc_c = _dot(vct_ref[...], e_c)
    inv_c = jnp.where(m_c.astype(F32) > 0.5 * NEG, 1.0 / jnp.maximum(acc_c[NSA_DH:NSA_DH + 1], 1e-30), 0.0)
    imp_h = _dot(ovt_ref[...], e_c) * inv_c

    d_i = lax.broadcasted_iota(jnp.int32, (Q_BLOCK, Q_BLOCK), 0)
    t_i = lax.broadcasted_iota(jnp.int32, (Q_BLOCK, Q_BLOCK), 1)
    sd_buf[...] = mask_pair(_dot(ks_ref[pl.ds(q0, Q_BLOCK), 0:128], qa_s[0:128, :]).astype(BF16),
                            d_i <= t_i)

    sw = sw_buf[...]
    acc_w = _dot(vwt_ref[:, pl.ds(w0, WIN_KEYS)], jnp.exp2(sw - jnp.max(sw, axis=0, keepdims=True)))
    inv_w = 1.0 / jnp.maximum(acc_w[NSA_DH:NSA_DH + 1], 1e-30)

    s = sd_buf[...]
    m16 = jnp.max(s, axis=0, keepdims=True)
    m_s[...] = m16.astype(F32)
    acc_s[...] = _dot(vst_ref[:, pl.ds(q0, Q_BLOCK)], jnp.exp2(s - m16))

    imp = imp_h[:, 0:Q_BLOCK]
    for r in range(1, NSA_REP):
        imp = imp + imp_h[:, r * Q_BLOCK:(r + 1) * Q_BLOCK]
    j_i = lax.broadcasted_iota(jnp.int32, (N_SEL, Q_BLOCK), 0)
    cur = tq // SEL_LEN
    forced = (j_i == 0) | (j_i == cur) | (j_i == cur - 1)
    score = jnp.where(forced, jnp.inf, jnp.where(j_i > cur, -jnp.inf, imp))
    sc_s[...] = score
    rk_s[...] = jnp.zeros_like(rk_s)
    sub8 = lax.broadcasted_iota(jnp.int32, (8, Q_BLOCK), 0)
    for grp in range(N_SEL // 8):
        @pl.when(8 * grp <= 2 * qb + 1)
        def _():
            for v in range(N_SEL // 8):
                sv = sc_s[8 * v:8 * v + 8, :]
                part = jnp.zeros((8, Q_BLOCK), F32)
                for jp in range(8 * grp, 8 * grp + 8):
                    row = sc_s[jp:jp + 1, :]
                    if v > grp:
                        part = part + jnp.where(row >= sv, 1.0, 0.0)
                    elif v < grp:
                        part = part + jnp.where(row > sv, 1.0, 0.0)
                    else:
                        part = part + jnp.where(sub8 + 8 * v > jp, jnp.where(row >= sv, 1.0, 0.0),
                                                jnp.where(row > sv, 1.0, 0.0))
                rk_s[8 * v:8 * v + 8, :] += part
    bias = jnp.where((rk_s[...] < SEL_TOP) & (j_i < 2 * qb), 0.0, NEG).astype(BF16)
    for r in range(NSA_REP):
        qa_s[128:128 + N_SEL, r * Q_BLOCK:(r + 1) * Q_BLOCK] = bias

    last_sub = SEQ // SEL_SUB - 1

    def scores_into(buf, c):
        k0 = pl.multiple_of(jnp.minimum(c, last_sub) * SEL_SUB, SEL_SUB)
        buf[...] = _dot(ks_ref[pl.ds(k0, SEL_SUB), :], qa_s[...]).astype(BF16)

    def softmax_from(buf, c):
        k0 = pl.multiple_of(c * SEL_SUB, SEL_SUB)
        s = buf[...]
        m = m_s[...]
        m_new = jnp.maximum(m, jnp.max(s, axis=0, keepdims=True).astype(F32))
        m_s[...] = m_new
        pr = jnp.exp2(s - m_new.astype(BF16))
        acc_s[...] = acc_s[...] * jnp.exp2(m - m_new) + _dot(vst_ref[:, pl.ds(k0, SEL_SUB)], pr)

    scores_into(s0_s, 0)

    def sel_step(i, carry):
        c = 2 * i
        scores_into(s1_s, c + 1)
        softmax_from(s0_s, c)
        scores_into(s0_s, c + 2)
        softmax_from(s1_s, c + 1)
        return carry

    n_main = (qb * Q_BLOCK + 2 * SEL_SUB - 1) // (2 * SEL_SUB)
    lax.fori_loop(0, n_main, sel_step, 0)
    inv_s = 1.0 / jnp.maximum(acc_s[NSA_DH:NSA_DH + 1, :], 1e-30)

    gall = gt_ref[...]
    ggrp = jnp.where(is_g0, gall[0:3 * NSA_REP], gall[3 * NSA_REP:3 * NSA_HEADS])
    for p2 in range(NSA_REP // 2):
        halves = []
        for hh in range(2):
            r = 2 * p2 + hh
            sl = slice(r * Q_BLOCK, (r + 1) * Q_BLOCK)
            g_c, g_s, g_w = (ggrp[3 * r + br:3 * r + br + 1, :] for br in range(3))
            halves.append((g_c * inv_c[:, sl]) * acc_c[0:NSA_DH, sl]
                          + (g_s * inv_s[:, sl]) * acc_s[0:NSA_DH, sl]
                          + (g_w * inv_w[:, sl]) * acc_w[0:NSA_DH, sl])
        o_ref[:, p2 * 128:(p2 + 1) * 128] = jnp.concatenate(halves, axis=0).T.astype(BF16)


def _nsa_attn(q_r, kc, vct, ks, vst, kw, vwt, gt, ovt):
    per_b = lambda r, c: pl.BlockSpec((None, r, c), lambda b, g, i: (b, 0, 0))
    per_bg = lambda r, c: pl.BlockSpec((None, r, c), lambda b, g, i: (b, g, 0))
    const = lambda a: pl.BlockSpec(a.shape, lambda b, g, i: (0, 0))
    return pl.pallas_call(
        _nsa_body,
        grid=(BATCH, NSA_KV, N_QB),
        in_specs=[pl.BlockSpec((None, NSA_REP * NSA_DH, Q_BLOCK), lambda b, g, i: (b, g, i)),
                  per_b(N_CMP_PAD, 128), per_bg(NSA_VROWS, N_CMP_PAD),
                  per_b(SEQ, 256), per_bg(NSA_VROWS, SEQ),
                  per_b(SEQ, 128), per_bg(NSA_VROWS, SEQ),
                  pl.BlockSpec((None, 3 * NSA_HEADS, Q_BLOCK), lambda b, g, i: (b, 0, i)),
                  const(ovt)],
        out_specs=pl.BlockSpec((None, Q_BLOCK, NSA_REP * NSA_DH), lambda b, g, i: (b, i, g)),
        out_shape=jax.ShapeDtypeStruct((BATCH, SEQ, NSA_WIDTH), BF16),
        scratch_shapes=[pltpu.VMEM((256, NSA_NL), BF16),
                        pltpu.VMEM((N_SEL, Q_BLOCK), F32), pltpu.VMEM((N_SEL, Q_BLOCK), F32),
                        pltpu.VMEM((SEL_SUB, NSA_NL), BF16), pltpu.VMEM((SEL_SUB, NSA_NL), BF16),
                        pltpu.VMEM((N_CMP_PAD, NSA_NL), BF16), pltpu.VMEM((WIN_KEYS, NSA_NL), BF16),
                        pltpu.VMEM((Q_BLOCK, NSA_NL), BF16),
                        pltpu.VMEM((1, NSA_NL), F32), pltpu.VMEM((NSA_VROWS, NSA_NL), F32)],
        compiler_params=_cparams("parallel", "parallel", "arbitrary"),
        name="nsa_attn",
    )(q_r, kc, vct, ks, vst, kw, vwt, gt, ovt)


OUT_TM = 512


def _outx_body(x_ref, oh_ref, on_ref, w_ref, nw_ref, wq_ref, k_ref, v_ref, wo_ref, o_ref):
    y = (x_ref[...] + _dot(oh_ref[...], w_ref[0:HG_WIDTH, :])
         + _dot(on_ref[...], w_ref[HG_WIDTH:HG_WIDTH + NSA_WIDTH, :]))
    hx = _rms(y, nw_ref[...]).astype(BF16)
    q = (_dot(hx, wq_ref[...]) * (X_DH ** -0.5)).astype(BF16)
    heads = []
    for h in range(X_HEADS):
        sl = slice(h * X_DH, (h + 1) * X_DH)
        s = _dot_nt(q[:, sl], k_ref[:, sl])
        e = jnp.exp(s - jnp.max(s, axis=-1, keepdims=True))
        p = e / jnp.sum(e, axis=-1, keepdims=True)
        heads.append(_dot(p.astype(BF16), v_ref[:, sl]))
    o_ref[...] = y + _dot(jnp.concatenate(heads, axis=1).astype(BF16), wo_ref[...])


def _outproj_xattn(x1, o_hg, o_nsa, w_out, nw, wq, k, v, wo):
    width = X_HEADS * X_DH
    tiles_per_b = SEQ // OUT_TM
    row = lambda w: pl.BlockSpec((OUT_TM, w), lambda i: (i, 0))
    const = lambda r, c: pl.BlockSpec((r, c), lambda i: (0, 0))
    kv = pl.BlockSpec((None, MEM_LEN, width), lambda i: (i // tiles_per_b, 0, 0))
    return pl.pallas_call(
        _outx_body,
        grid=(TOKENS // OUT_TM,),
        in_specs=[row(D_MODEL), row(HG_WIDTH), row(NSA_WIDTH), const(D_MODEL, D_MODEL), const(1, D_MODEL),
                  const(D_MODEL, width), kv, kv, const(width, D_MODEL)],
        out_specs=row(D_MODEL),
        out_shape=jax.ShapeDtypeStruct((TOKENS, D_MODEL), F32),
        compiler_params=_cparams("parallel"),
        name="out_proj_xattn",
    )(x1, o_hg, o_nsa, w_out, nw, wq, k, v, wo)


def _memkv_body(m_ref, nw_ref, wk_ref, wv_ref, k_ref, v_ref):
    hm = _rms(m_ref[...], nw_ref[...]).astype(BF16)
    k_ref[...] = _dot(hm, wk_ref[...]).astype(BF16)
    v_ref[...] = _dot(hm, wv_ref[...]).astype(BF16)


def _memkv(mem, nw, wk, wv):
    width = X_HEADS * X_DH
    wspec = pl.BlockSpec((D_MODEL, width), lambda b: (0, 0))
    ospec = pl.BlockSpec((None, MEM_LEN, width), lambda b: (b, 0, 0))
    osh = jax.ShapeDtypeStruct((BATCH, MEM_LEN, width), BF16)
    return pl.pallas_call(
        _memkv_body,
        grid=(BATCH,),
        in_specs=[pl.BlockSpec((None, MEM_LEN, D_MODEL), lambda b: (b, 0, 0)),
                  pl.BlockSpec((1, D_MODEL), lambda b: (0, 0)), wspec, wspec],
        out_specs=(ospec, ospec), out_shape=(osh, osh),
        compiler_params=_cparams("parallel"),
        name="xattn_memkv",
    )(mem, nw, wk, wv)


def _overlap_t():
    c0 = np.arange(N_CMP)[:, None] * CMP_STRIDE
    s0 = np.arange(N_SEL)[None, :] * SEL_LEN
    ov = np.clip(np.minimum(c0 + CMP_LEN, s0 + SEL_LEN) - np.maximum(c0, s0), 0, None) / CMP_LEN
    out = np.zeros((N_SEL, N_CMP_PAD), np.float32)
    out[:, :N_CMP] = ov.T
    return out


def kernel(x, mem, positions, ffn1_norm, ffn1_w_gate, ffn1_w_up, ffn1_w_down, mix_norm, w_in, hgrn_lb_logits, hgrn_out_norm, nsa_cmp_pe, nsa_cmp_k_w1, nsa_cmp_k_w2, nsa_cmp_v_w1, nsa_cmp_v_w2, w_out, xattn_norm, mem_norm, xattn_wq, xattn_wk, xattn_wv, xattn_wo, ffn2_norm, ffn2_w_gate, ffn2_w_up, ffn2_w_down, final_norm):
    bf = lambda a: a.astype(BF16)
    vec = lambda a: a.reshape(1, -1).astype(F32)
    x2d = x.reshape(TOKENS, D_MODEL)

    x1_head, h_head, w1_gate, w1_up, w1_down = _ffn(
        x2d, vec(ffn1_norm[0]), ffn1_w_gate[0], ffn1_w_up[0], ffn1_w_down[0], vec(mix_norm[0]),
        final=False, n_tiles=1, tm=FFN_HEAD_TM, tf=FFN_HEAD_TF)
    x1, h_mix, w2_gate, w2_up, w2_down, w_out_b, wq_b, wk_b, wv_b, wo_b = _ffn(
        x2d, vec(ffn1_norm[0]), w1_gate, w1_up, w1_down, vec(mix_norm[0]), final=False,
        head=(x1_head, h_head),
        step_casts=(ffn2_w_gate[0], ffn2_w_up[0], ffn2_w_down[0]),
        tile_casts=(w_out[0], xattn_wq[0], xattn_wk[0], xattn_wv[0], xattn_wo[0]))

    w_t = w_in[0].T
    w_tail = jnp.pad(w_t[PROJ_FULL_TILES * PROJ_TN:], ((0, PROJ_W - D_IN), (0, 0)))
    proj, proj_f = _proj(h_mix, w_t, w_tail)

    o_hg = _hgrn(proj, proj_f, hgrn_lb_logits.astype(F32), vec(hgrn_out_norm[0]))

    inv = ROPE_THETA ** (-jnp.arange(NSA_DH // 2, dtype=F32) / (NSA_DH // 2))
    inv128 = jnp.tile(inv, 128 // (NSA_DH // 2)).reshape(1, 128)
    pos_f = positions.astype(F32).reshape(BATCH, SEQ, 1)
    q_r, kc_tok, vc_tok, ks, kw, vst, vwt, gt = _nsa_prep(proj, pos_f, inv128)

    def over_groups(w):
        z = jnp.zeros_like(w)
        return bf(jnp.concatenate([jnp.concatenate([w, z], axis=-1), jnp.concatenate([z, w], axis=-1)], axis=-2))

    per_pos = lambda w1: over_groups(w1.reshape(CMP_LEN, NSA_DH, CMP_HIDDEN))
    pe = nsa_cmp_pe[0].astype(F32)
    kc, vct = _compress(kc_tok, vc_tok, jnp.concatenate([pe, pe], axis=1),
                        per_pos(nsa_cmp_k_w1[0]), over_groups(nsa_cmp_k_w2[0]),
                        per_pos(nsa_cmp_v_w1[0]), over_groups(nsa_cmp_v_w2[0]))
    o_nsa = _nsa_attn(q_r, kc, vct, ks, vst, kw, vwt, gt, jnp.asarray(_overlap_t(), dtype=BF16))

    km, vm = _memkv(mem, vec(mem_norm[0]), wk_b, wv_b)
    x3 = _outproj_xattn(x1, o_hg.reshape(TOKENS, HG_WIDTH), o_nsa.reshape(TOKENS, NSA_WIDTH),
                        w_out_b, vec(xattn_norm[0]), wq_b, km, vm, wo_b)

    (out,) = _ffn(x3, vec(ffn2_norm[0]), w2_gate, w2_up, w2_down, vec(final_norm), final=True)
    return out.reshape(BATCH, SEQ, D_MODEL)
```

```python
import functools

import numpy as np
import jax
import jax.numpy as jnp
from jax import lax
from jax.experimental import pallas as pl
from jax.experimental.pallas import tpu as pltpu

F32 = jnp.float32
BF16 = jnp.bfloat16

D_MODEL = 2048
BATCH = 2
SEQ = 4096
TOKENS = BATCH * SEQ
RMS_EPS = 1e-6
ROPE_THETA = 10000.0
HG_WIDTH = 1024
HG_HEADS = 8
HG_D = 128
HG_CHUNK = 128
HG_LEVELS = (64, 32, 16, 8, 4, 2, 1)
NSA_WIDTH = 1024
NSA_DH = 64
NSA_HEADS = 16
NSA_KV = 2
NSA_REP = 8
NSA_VROWS = NSA_DH + 16
CMP_LEN = 32
CMP_STRIDE = 16
CMP_HIDDEN = 256
N_CMP = (SEQ - CMP_LEN) // CMP_STRIDE + 1
N_CMP_PAD = 256
SEL_LEN = 64
N_SEL = SEQ // SEL_LEN
SEL_TOP = 16
WINDOW = 512
Q_BLOCK = 128
N_QB = SEQ // Q_BLOCK
SEL_SUB = 256
WIN_KEYS = WINDOW + Q_BLOCK
MEM_LEN = 256
X_HEADS = 4
X_DH = 128
D_FF = 5632
IN_SIZES = (1024, 1024, 1024, 1024, 1024, 128, 128, 128, 128, 128, 128, 48)
D_IN = sum(IN_SIZES)
PROJ_NSA_OFF = sum(IN_SIZES[:4])
NSA_PROJ = 2048
PROJ_W = PROJ_NSA_OFF + NSA_PROJ
NEG = -1e30
LOG2E = 1.4426950408889634

V7X_VMEM_BYTES = 64 * 1024 * 1024
VMEM_LIMIT = V7X_VMEM_BYTES - 8 * 1024 * 1024


def _cparams(*sem, flags=None):
    return pltpu.CompilerParams(dimension_semantics=sem, vmem_limit_bytes=VMEM_LIMIT, flags=flags)


def _rms(x, w):
    return x * lax.rsqrt(jnp.mean(x * x, axis=-1, keepdims=True) + RMS_EPS) * w


def _silu(x):
    return x * jax.nn.sigmoid(x)


def _dot(a, b):
    return jnp.dot(a, b, preferred_element_type=F32)


def _dot_f32_by_01(sel, x):
    hi = x.astype(BF16)
    r1 = x - hi.astype(F32)
    mid = r1.astype(BF16)
    lo = (r1 - mid.astype(F32)).astype(BF16)
    n = x.shape[1]
    y = _dot(sel, jnp.concatenate([hi, mid, lo], axis=1))
    return y[:, 0:n] + y[:, n:2 * n] + y[:, 2 * n:3 * n]


def _dot_nt(a, b):
    return lax.dot_general(a, b, (((1,), (1,)), ((), ())), preferred_element_type=F32)


FFN_TM = 512
FFN_TF = 512
FFN_HEAD_TM = 1024
FFN_HEAD_TF = 256


def _ffn_body(x_ref, nw_ref, wg_ref, wu_ref, wd_ref, nw2_ref, *rest,
              final, n_step_casts, n_tile_casts, n_head, f32_weights):
    n_casts = n_step_casts + n_tile_casts
    cast_in, rest = rest[:n_casts], rest[n_casts:]
    if n_head:
        (head_x_ref, head_hn_ref), rest = rest[:2], rest[2:]
    if final:
        o_ref, rest = rest[0], rest[1:]
    else:
        (o_ref, hn_ref), rest = rest[:2], rest[2:]
    cast_out, rest = rest[:n_casts], rest[n_casts:]
    if f32_weights:
        w_copy, rest = rest[:3], rest[3:]
    (h_scr,) = rest
    i = pl.program_id(0)
    j = pl.program_id(1)
    last = pl.num_programs(1) - 1

    def swiglu_tile(h):
        wg, wu, wd = wg_ref[...], wu_ref[...], wd_ref[...]
        if f32_weights:
            wg, wu, wd = wg.astype(BF16), wu.astype(BF16), wd.astype(BF16)
            for dst, w in zip(w_copy, (wg, wu, wd)):
                dst[...] = w
        g = _dot(h, wg)
        u = _dot(h, wu)
        return _dot((_silu(g) * u).astype(BF16), wd)

    def ride_along(first):
        n = n_casts if first else n_step_casts
        for src, dst in zip(cast_in[:n], cast_out[:n]):
            dst[...] = src[...].astype(BF16)

    def when(cond):
        return pl.when(cond & (i >= n_head)) if n_head else pl.when(cond)

    if n_head:
        @pl.when((i < n_head) & (j == 0))
        def _():
            o_ref[...] = head_x_ref[...]
            hn_ref[...] = head_hn_ref[...]
            ride_along(True)

        @pl.when((i < n_head) & (j > 0))
        def _():
            ride_along(False)

    @when(j == 0)
    def _():
        h = _rms(x_ref[...], nw_ref[...]).astype(BF16)
        h_scr[...] = h
        o_ref[...] = swiglu_tile(h)
        ride_along(True)

    @when((j > 0) & (j < last))
    def _():
        o_ref[...] += swiglu_tile(h_scr[...])
        ride_along(False)

    @when(j == last)
    def _():
        y = x_ref[...] + 0.5 * (o_ref[...] + swiglu_tile(h_scr[...]))
        if final:
            o_ref[...] = _rms(y, nw2_ref[...])
        else:
            o_ref[...] = y
            hn_ref[...] = _rms(y, nw2_ref[...]).astype(BF16)
        ride_along(False)


def _ffn(x, nw, wg, wu, wd, nw2, final, step_casts=(), tile_casts=(), head=None,
         n_tiles=None, tm=FFN_TM, tf=FFN_TF):
    ni, nj = (n_tiles or TOKENS // tm), D_FF // tf
    f32_weights = wg.dtype == F32
    n_head = head[0].shape[0] // tm if head is not None else 0
    once = dict(pipeline_mode=pl.Buffered(1)) if ni == 1 else {}
    row = pl.BlockSpec((tm, D_MODEL), lambda i, j: (i, 0), **once)
    vec = pl.BlockSpec((1, D_MODEL), lambda i, j: (0, 0))
    wj = (lambda i, j: jnp.where(i < n_head, 0, j)) if n_head else (lambda i, j: j)
    in_specs = [row, vec,
                pl.BlockSpec((D_MODEL, tf), lambda i, j: (0, wj(i, j))),
                pl.BlockSpec((D_MODEL, tf), lambda i, j: (0, wj(i, j))),
                pl.BlockSpec((tf, D_MODEL), lambda i, j: (wj(i, j), 0)),
                vec]
    cast_specs = []
    for a in step_casts:
        r, c = a.shape
        if r % ni == 0 and c % nj == 0:
            cast_specs.append(pl.BlockSpec((r // ni, c // nj), lambda i, j: (i, j)))
        else:
            cast_specs.append(pl.BlockSpec((r // nj, c // ni), lambda i, j: (j, i)))
    for a in tile_casts:
        cast_specs.append(pl.BlockSpec((a.shape[0] // ni, a.shape[1]), lambda i, j: (i, 0)))
    casts = tuple(step_casts) + tuple(tile_casts)
    cast_shapes = [jax.ShapeDtypeStruct(a.shape, BF16) for a in casts]
    main_shapes = [jax.ShapeDtypeStruct((ni * tm, D_MODEL), F32)]
    if not final:
        main_shapes.append(jax.ShapeDtypeStruct((ni * tm, D_MODEL), BF16))
    head_specs = [pl.BlockSpec((tm, D_MODEL), lambda i, j: (jnp.minimum(i, n_head - 1), 0),
                               pipeline_mode=pl.Buffered(1))] * 2 if n_head else []
    w_copy_specs = in_specs[2:5] if f32_weights else []
    w_copy_shapes = [jax.ShapeDtypeStruct(w.shape, BF16) for w in (wg, wu, wd)] if f32_weights else []
    return pl.pallas_call(
        functools.partial(_ffn_body, final=final, n_step_casts=len(step_casts), n_tile_casts=len(tile_casts),
                          n_head=n_head, f32_weights=f32_weights),
        grid=(ni, nj), in_specs=in_specs + cast_specs + head_specs,
        out_specs=tuple([row] * len(main_shapes) + cast_specs + w_copy_specs),
        out_shape=tuple(main_shapes + cast_shapes + w_copy_shapes),
        scratch_shapes=[pltpu.VMEM((tm, D_MODEL), BF16)],
        compiler_params=_cparams("parallel", "arbitrary"),
        name="ffn_final" if final else ("ffn_head" if f32_weights else "ffn"),
    )(x, nw, wg, wu, wd, nw2, *casts, *(head or ()))


PROJ_TM = 1024
PROJ_TN = 512


PROJ_F_TILE0 = IN_SIZES[0] // PROJ_TN
PROJ_F_TILES = IN_SIZES[1] // PROJ_TN


PROJ_FULL_TILES = D_IN // PROJ_TN


def _proj_body(a_ref, wt_ref, tail_ref, o_ref, f_ref):
    j = pl.program_id(1)
    w = jnp.where(j >= PROJ_FULL_TILES, tail_ref[...], wt_ref[...]).astype(BF16)
    y = _dot_nt(a_ref[...], w)
    o_ref[...] = y.astype(BF16)

    @pl.when((j >= PROJ_F_TILE0) & (j < PROJ_F_TILE0 + PROJ_F_TILES))
    def _():
        f_ref[...] = y


def _proj(a, wt, wt_tail):
    m, k = a.shape
    f_tile = lambda i, j: (i, jnp.clip(j - PROJ_F_TILE0, 0, PROJ_F_TILES - 1))
    return pl.pallas_call(
        _proj_body,
        grid=(m // PROJ_TM, PROJ_W // PROJ_TN),
        in_specs=[pl.BlockSpec((PROJ_TM, k), lambda i, j: (i, 0)),
                  pl.BlockSpec((PROJ_TN, k), lambda i, j: (jnp.minimum(j, PROJ_FULL_TILES - 1), 0)),
                  pl.BlockSpec((PROJ_TN, k), lambda i, j: (0, 0))],
        out_specs=(pl.BlockSpec((PROJ_TM, PROJ_TN), lambda i, j: (i, j)),
                   pl.BlockSpec((PROJ_TM, PROJ_TN), f_tile)),
        out_shape=(jax.ShapeDtypeStruct((m, PROJ_W), BF16),
                   jax.ShapeDtypeStruct((m, IN_SIZES[1]), F32)),
        compiler_params=_cparams("parallel", "arbitrary"),
        name="proj_in",
    )(a, wt, wt_tail)


HG_ROWS = 1024
HG_CUM = 256


def _hgrn_body(q_ref, f_ref, i_ref, g_ref, lbl_ref, nw_ref, o_ref, st_ref, k_s, b_s):
    c = pl.program_id(2)

    @pl.when(c == 0)
    def _():
        st_ref[...] = jnp.zeros_like(st_ref)

    l0 = lbl_ref[0:1, :]
    l1 = lbl_ref[1:2, :]
    lmax = jnp.maximum(l0, l1)
    e0 = jnp.exp(l0 - lmax)
    lb = e0 / (e0 + jnp.exp(l1 - lmax))

    C = HG_CHUNK
    f = lb + (1.0 - lb) * jax.nn.sigmoid(f_ref[...])
    k_s[...] = 1.0 - f
    r_i = lax.broadcasted_iota(jnp.int32, (HG_CUM, HG_CUM), 0)
    c_i = lax.broadcasted_iota(jnp.int32, (HG_CUM, HG_CUM), 1)
    tri = jnp.where((r_i >= c_i) & (r_i // C == c_i // C), 1.0, 0.0).astype(BF16)
    logf = jnp.log2(f)
    for r0 in range(0, HG_ROWS, HG_CUM):
        b_s[r0:r0 + HG_CUM, :] = _dot_f32_by_01(tri, logf[r0:r0 + HG_CUM])

    t_i = lax.broadcasted_iota(jnp.int32, (C, C), 0)
    s_i = lax.broadcasted_iota(jnp.int32, (C, C), 1)
    level_mask = [(t_i // (2 * w) == s_i // (2 * w)) & (t_i % (2 * w) >= w) & (s_i % (2 * w) < w)
                  for w in HG_LEVELS]
    sub_r = lax.broadcasted_iota(jnp.int32, (8, HG_D), 0)
    row_i = lax.broadcasted_iota(jnp.int32, (C, HG_D), 0)
    right_sign = {w: jnp.where(row_i % (2 * w) >= w, 1.0, -1.0) for w in HG_LEVELS if w < 8}

    def neg_abs_diff(w, r0, b):
        row = lambda r, n: jnp.broadcast_to(b_s[r0 + r:r0 + r + 1, :], (n, HG_D))
        if w >= 8:
            parts = []
            for p0 in range(0, C, 2 * w):
                ref = row(p0 + w - 1, w)
                parts += [ref - b[p0:p0 + w], b[p0 + w:p0 + 2 * w] - ref]
            return jnp.concatenate(parts, axis=0)
        if w == 4:
            bref = jnp.concatenate([row(p0 + 3, 8) for p0 in range(0, C, 8)], axis=0)
        elif w == 2:
            bref = jnp.concatenate([jnp.where(sub_r < 4, row(p0 + 1, 8), row(p0 + 5, 8))
                                    for p0 in range(0, C, 8)], axis=0)
        else:
            bref = jnp.where(row_i % 2 == 1, pltpu.roll(b, 1, 0), b)
        return (b - bref) * right_sign[w]

    chunks = [ci * C for ci in range(HG_ROWS // C)]
    rows = lambda ref, r0: ref[r0:r0 + C, :]
    att = [jnp.zeros((C, C), F32) for _ in chunks]
    q16 = [rows(q_ref, r0) for r0 in chunks]
    k16 = [rows(k_s, r0).astype(BF16) for r0 in chunks]
    for w, mask in zip(HG_LEVELS, level_mask):
        for n, r0 in enumerate(chunks):
            b = rows(b_s, r0)
            e = jnp.exp2(neg_abs_diff(w, r0, b)).astype(BF16)
            att[n] = jnp.where(mask, _dot_nt(q16[n] * e, k16[n] * e), att[n])
    o_intra = []
    for n, r0 in enumerate(chunks):
        q, k, v = q16[n].astype(F32), rows(k_s, r0), rows(i_ref, r0)
        o_intra.append(_dot(att[n].astype(BF16), v)
                       + jnp.sum(q * k, axis=-1, keepdims=True) * v.astype(F32))
    upd = []
    for r0 in chunks:
        bl = b_s[r0 + C - 1:r0 + C, :]
        kd = rows(k_s, r0) * jnp.exp2(bl - rows(b_s, r0))
        v_t = rows(i_ref, r0).astype(F32).T.astype(BF16)
        upd.append((jnp.exp2(bl), _dot(v_t, kd.astype(BF16))))
    st_t = st_ref[...]
    for n, r0 in enumerate(chunks):
        qe = (q16[n].astype(F32) * jnp.exp2(rows(b_s, r0))).astype(BF16)
        o = o_intra[n] + _dot_nt(qe, st_t.astype(BF16))
        st_t = st_t * upd[n][0] + upd[n][1]
        o = o * lax.rsqrt(jnp.mean(o * o, axis=-1, keepdims=True) + RMS_EPS)
        o_ref[r0:r0 + C, :] = (o * nw_ref[...] * _silu(rows(g_ref, r0).astype(F32))).astype(BF16)
    st_ref[...] = st_t


def _hgrn(proj, proj_f, lb_logits, norm_w):
    p3 = proj.reshape(BATCH, SEQ, PROJ_W)
    f3 = proj_f.reshape(BATCH, SEQ, HG_WIDTH)

    def col(off):
        return pl.BlockSpec((None, HG_ROWS, HG_D), lambda b, h, c: (b, c, off + h))

    return pl.pallas_call(
        _hgrn_body,
        grid=(BATCH, HG_HEADS, SEQ // HG_ROWS),
        in_specs=[col(0), col(0), col(2 * HG_HEADS), col(3 * HG_HEADS),
                  pl.BlockSpec((2, HG_D), lambda b, h, c: (0, h)),
                  pl.BlockSpec((1, HG_D), lambda b, h, c: (0, h))],
        out_specs=pl.BlockSpec((None, HG_ROWS, HG_D), lambda b, h, c: (b, c, h)),
        out_shape=jax.ShapeDtypeStruct((BATCH, SEQ, HG_WIDTH), BF16),
        scratch_shapes=[pltpu.VMEM((HG_D, HG_D), F32),
                        pltpu.VMEM((HG_ROWS, HG_D), F32),
                        pltpu.VMEM((HG_ROWS, HG_D), F32)],
        compiler_params=_cparams("parallel", "parallel", "arbitrary"),
        name="hgrn2",
    )(p3, f3, p3, p3, lb_logits, norm_w)


PREP_TM = 256


def _prep_body(p_ref, pos4_ref, inv_ref, q_ref, kc_ref, vc_ref, ks_ref, kw_ref,
               vst_ref, vwt_ref, gt_ref, cos_s, sin_s):
    n_freq = NSA_DH // 2
    ang4 = pos4_ref[...] * inv_ref[...]
    seg = lax.broadcasted_iota(jnp.int32, (PREP_TM // 4, 128), 1) // n_freq
    for table, dst in ((jnp.cos(ang4), cos_s), (jnp.sin(ang4), sin_s)):
        for u in range(4):
            one = jnp.where(seg == u, table, 0.0)
            full = one
            for k in range(1, 4):
                full = full + pltpu.roll(one, k * n_freq, 1)
            dst[pl.ds(u, PREP_TM // 4, stride=4), :] = full
    cos = cos_s[...]
    sin = sin_s[...]
    lane = lax.broadcasted_iota(jnp.int32, (PREP_TM, 128), 1)
    lo = (lane & (NSA_DH // 2)) == 0
    sin_signed = jnp.where(lo, -sin, sin)

    def rope(x):
        rot = jnp.where(lo, pltpu.roll(x, 128 - NSA_DH // 2, 1), pltpu.roll(x, NSA_DH // 2, 1))
        return x * cos + rot * sin_signed

    cols = lambda c0: p_ref[:, c0:c0 + 128].astype(F32)
    scale = NSA_DH ** -0.5 * LOG2E
    for cblk in range(NSA_WIDTH // 128):
        q_ref[cblk * 128:(cblk + 1) * 128, :] = (rope(cols(cblk * 128)) * scale).T.astype(BF16)
    c_kc, c_vc, c_ks, c_vs, c_kw, c_vw, c_gate = (int(c) for c in np.cumsum(IN_SIZES[4:11]))
    kc_ref[...] = rope(cols(c_kc))
    vc_ref[...] = cols(c_vc)
    ks_ref[:, 0:128] = rope(cols(c_ks)).astype(BF16)
    blk = (pl.program_id(1) * PREP_TM + lax.broadcasted_iota(jnp.int32, (PREP_TM, 128), 0)) // SEL_LEN
    ks_ref[:, 128:256] = jnp.where(lane == blk, 1.0, 0.0).astype(BF16)
    kw_ref[...] = rope(cols(c_kw)).astype(BF16)
    ones = jnp.ones((NSA_VROWS - NSA_DH, PREP_TM), BF16)
    for v_ref, c0 in ((vst_ref, c_vs), (vwt_ref, c_vw)):
        vt = cols(c0).T.astype(BF16)
        for g in range(NSA_KV):
            v_ref[g * NSA_VROWS:g * NSA_VROWS + NSA_DH, :] = vt[g * NSA_DH:(g + 1) * NSA_DH]
            v_ref[g * NSA_VROWS + NSA_DH:(g + 1) * NSA_VROWS, :] = ones
    gt_ref[...] = jax.nn.sigmoid(cols(c_gate)).T[0:3 * NSA_HEADS, :]


def _nsa_prep(proj, pos4, inv128):
    nt = SEQ // PREP_TM
    p3 = proj.reshape(BATCH, SEQ, PROJ_W)
    nat = lambda w: pl.BlockSpec((None, PREP_TM, w), lambda b, i: (b, i, 0))
    tr = lambda r: pl.BlockSpec((None, r, PREP_TM), lambda b, i: (b, 0, i))
    sds = jax.ShapeDtypeStruct
    return pl.pallas_call(
        _prep_body,
        grid=(BATCH, nt),
        in_specs=[pl.BlockSpec((None, PREP_TM, NSA_PROJ), lambda b, i: (b, i, PROJ_NSA_OFF // NSA_PROJ)),
                  pl.BlockSpec((None, PREP_TM // 4, 128), lambda b, i: (b, i, 0)),
                  pl.BlockSpec((1, 128), lambda b, i: (0, 0))],
        out_specs=(tr(NSA_WIDTH), nat(128), nat(128), nat(256), nat(128),
                   tr(NSA_KV * NSA_VROWS), tr(NSA_KV * NSA_VROWS), tr(3 * NSA_HEADS)),
        out_shape=(sds((BATCH, NSA_WIDTH, SEQ), BF16),
                   sds((BATCH, SEQ, 128), F32),
                   sds((BATCH, SEQ, 128), F32),
                   sds((BATCH, SEQ, 256), BF16),
                   sds((BATCH, SEQ, 128), BF16),
                   sds((BATCH, NSA_KV * NSA_VROWS, SEQ), BF16),
                   sds((BATCH, NSA_KV * NSA_VROWS, SEQ), BF16),
                   sds((BATCH, 3 * NSA_HEADS, SEQ), F32)),
        scratch_shapes=[pltpu.VMEM((PREP_TM, 128), F32)] * 2,
        compiler_params=_cparams("parallel", "parallel"),
        name="nsa_prep",
    )(p3, pos4, inv128)


def _cmp_body(tk_ref, tv_ref, pe_ref, kw1_ref, kw2_ref, vw1_ref, vw2_ref, kc_ref, vct_ref, y1_s, y2_s):
    row = lax.broadcasted_iota(jnp.int32, (N_CMP_PAD, NSA_KV * CMP_HIDDEN), 0)

    def mlp(t_ref, w1_ref, w2_ref):
        y1_s[...] = jnp.zeros_like(y1_s)
        y2_s[...] = jnp.zeros_like(y2_s)
        for l in range(CMP_STRIDE):
            x = t_ref[pl.ds(l, N_CMP_PAD, stride=CMP_STRIDE), :]
            y1_s[...] += _dot((x + pe_ref[l:l + 1, :]).astype(BF16), w1_ref[l])
            y2_s[...] += _dot((x + pe_ref[CMP_STRIDE + l:CMP_STRIDE + l + 1, :]).astype(BF16),
                              w1_ref[CMP_STRIDE + l])
        hid = jnp.where(row < N_CMP, y1_s[...] + pltpu.roll(y2_s[...], N_CMP_PAD - 1, 0), 0.0)
        return _dot(_silu(hid).astype(BF16), w2_ref[...])

    kc_ref[...] = mlp(tk_ref, kw1_ref, kw2_ref).astype(BF16)
    y1_s[:, 0:128] = mlp(tv_ref, vw1_ref, vw2_ref)
    vt = y1_s[:, 0:128].T.astype(BF16)
    ones = jnp.ones((NSA_VROWS - NSA_DH, N_CMP_PAD), BF16)
    for g in range(NSA_KV):
        vct_ref[g * NSA_VROWS:g * NSA_VROWS + NSA_DH, :] = vt[g * NSA_DH:(g + 1) * NSA_DH]
        vct_ref[g * NSA_VROWS + NSA_DH:(g + 1) * NSA_VROWS, :] = ones


def _compress(tk, tv, pe2, kw1, kw2, vw1, vw2):
    seg = pl.BlockSpec((None, SEQ, 128), lambda b: (b, 0, 0))
    full2 = lambda a: pl.BlockSpec(a.shape, lambda b: (0,) * a.ndim)
    return pl.pallas_call(
        _cmp_body,
        grid=(BATCH,),
        in_specs=[seg, seg, full2(pe2), full2(kw1), full2(kw2), full2(vw1), full2(vw2)],
        out_specs=(pl.BlockSpec((None, N_CMP_PAD, 128), lambda b: (b, 0, 0)),
                   pl.BlockSpec((None, NSA_KV * NSA_VROWS, N_CMP_PAD), lambda b: (b, 0, 0))),
        out_shape=(jax.ShapeDtypeStruct((BATCH, N_CMP_PAD, 128), BF16),
                   jax.ShapeDtypeStruct((BATCH, NSA_KV * NSA_VROWS, N_CMP_PAD), BF16)),
        scratch_shapes=[pltpu.VMEM((N_CMP_PAD, NSA_KV * CMP_HIDDEN), F32)] * 2,
        compiler_params=_cparams("parallel"),
        name="nsa_compress",
    )(tk, tv, pe2, kw1, kw2, vw1, vw2)


NSA_NL = NSA_REP * Q_BLOCK


def _nsa_body(q_ref, kc_ref, vct_ref, ks_ref, vst_ref, kw_ref, vwt_ref, gt_ref, ovt_ref,
              o_ref, qa_s, sc_s, rk_s, s0_s, s1_s, sc_buf, sw_buf, sd_buf, m_s, acc_s):
    g = pl.program_id(1)
    qb = pl.program_id(2)
    q0 = pl.multiple_of(qb * Q_BLOCK, Q_BLOCK)
    is_g0 = g == 0

    zero_slab = jnp.zeros((NSA_DH, Q_BLOCK), BF16)
    for r in range(NSA_REP):
        s = q_ref[r * NSA_DH:(r + 1) * NSA_DH, :]
        qa_s[0:NSA_DH, r * Q_BLOCK:(r + 1) * Q_BLOCK] = jnp.where(is_g0, s, zero_slab)
        qa_s[NSA_DH:2 * NSA_DH, r * Q_BLOCK:(r + 1) * Q_BLOCK] = jnp.where(is_g0, zero_slab, s)
    qa_s[128 + N_SEL:256, :] = jnp.zeros((128 - N_SEL, NSA_NL), BF16)

    tq = q0 + lax.broadcasted_iota(jnp.int32, (1, Q_BLOCK), 1)

    def mask_pair(s, valid):
        return jnp.concatenate([jnp.where(valid, s[:, r * Q_BLOCK:(r + 1) * Q_BLOCK], NEG)
                                for r in range(NSA_REP)], axis=1)

    n_i = lax.broadcasted_iota(jnp.int32, (N_CMP_PAD, Q_BLOCK), 0)
    valid_c = (n_i * CMP_STRIDE + (CMP_LEN - 1) <= tq) & (n_i < N_CMP)
    sc_buf[...] = mask_pair(_dot(kc_ref[...], qa_s[0:128, :]).astype(BF16), valid_c)

    w0 = pl.multiple_of(jnp.maximum(q0 - WINDOW, 0), Q_BLOCK)
    dpos = tq - (w0 + lax.broadcasted_iota(jnp.int32, (WIN_KEYS, Q_BLOCK), 0))
    sw_buf[...] = mask_pair(_dot(kw_ref[pl.ds(w0, WIN_KEYS), :], qa_s[0:128, :]).astype(BF16),
                            (dpos >= 0) & (dpos < WINDOW))

    sc = sc_buf[...]
    m_c = jnp.max(sc, axis=0, keepdims=True)
    e_c = jnp.exp2(sc - m_c)
    acc_c = _dot(vct_ref[...], e_c)
    inv_c = jnp.where(m_c.astype(F32) > 0.5 * NEG, 1.0 / jnp.maximum(acc_c[NSA_DH:NSA_DH + 1], 1e-30), 0.0)
    imp_h = _dot(ovt_ref[...], e_c) * inv_c

    d_i = lax.broadcasted_iota(jnp.int32, (Q_BLOCK, Q_BLOCK), 0)
    t_i = lax.broadcasted_iota(jnp.int32, (Q_BLOCK, Q_BLOCK), 1)
    sd_buf[...] = mask_pair(_dot(ks_ref[pl.ds(q0, Q_BLOCK), 0:128], qa_s[0:128, :]).astype(BF16),
                            d_i <= t_i)

    sw = sw_buf[...]
    acc_w = _dot(vwt_ref[:, pl.ds(w0, WIN_KEYS)], jnp.exp2(sw - jnp.max(sw, axis=0, keepdims=True)))
    inv_w = 1.0 / jnp.maximum(acc_w[NSA_DH:NSA_DH + 1], 1e-30)

    s = sd_buf[...]
    m16 = jnp.max(s, axis=0, keepdims=True)
    m_s[...] = m16.astype(F32)
    acc_s[...] = _dot(vst_ref[:, pl.ds(q0, Q_BLOCK)], jnp.exp2(s - m16))

    imp = imp_h[:, 0:Q_BLOCK]
    for r in range(1, NSA_REP):
        imp = imp + imp_h[:, r * Q_BLOCK:(r + 1) * Q_BLOCK]
    j_i = lax.broadcasted_iota(jnp.int32, (N_SEL, Q_BLOCK), 0)
    cur = tq // SEL_LEN
    forced = (j_i == 0) | (j_i == cur) | (j_i == cur - 1)
    score = jnp.where(forced, jnp.inf, jnp.where(j_i > cur, -jnp.inf, imp))
    sc_s[...] = score
    rk_s[...] = jnp.zeros_like(rk_s)
    sub8 = lax.broadcasted_iota(jnp.int32, (8, Q_BLOCK), 0)
    for grp in range(N_SEL // 8):
        @pl.when(8 * grp <= 2 * qb + 1)
        def _():
            for v in range(N_SEL // 8):
                sv = sc_s[8 * v:8 * v + 8, :]
                part = jnp.zeros((8, Q_BLOCK), F32)
                for jp in range(8 * grp, 8 * grp + 8):
                    row = sc_s[jp:jp + 1, :]
                    if v > grp:
                        part = part + jnp.where(row >= sv, 1.0, 0.0)
                    elif v < grp:
                        part = part + jnp.where(row > sv, 1.0, 0.0)
                    else:
                        part = part + jnp.where(sub8 + 8 * v > jp, jnp.where(row >= sv, 1.0, 0.0),
                                                jnp.where(row > sv, 1.0, 0.0))
                rk_s[8 * v:8 * v + 8, :] += part
    bias = jnp.where((rk_s[...] < SEL_TOP) & (j_i < 2 * qb), 0.0, NEG).astype(BF16)
    for r in range(NSA_REP):
        qa_s[128:128 + N_SEL, r * Q_BLOCK:(r + 1) * Q_BLOCK] = bias

    last_sub = SEQ // SEL_SUB - 1

    def scores_into(buf, c):
        k0 = pl.multiple_of(jnp.minimum(c, last_sub) * SEL_SUB, SEL_SUB)
        buf[...] = _dot(ks_ref[pl.ds(k0, SEL_SUB), :], qa_s[...]).astype(BF16)

    def softmax_from(buf, c):
        k0 = pl.multiple_of(c * SEL_SUB, SEL_SUB)
        s = buf[...]
        m = m_s[...]
        m_new = jnp.maximum(m, jnp.max(s, axis=0, keepdims=True).astype(F32))
        m_s[...] = m_new
        pr = jnp.exp2(s - m_new.astype(BF16))
        acc_s[...] = acc_s[...] * jnp.exp2(m - m_new) + _dot(vst_ref[:, pl.ds(k0, SEL_SUB)], pr)

    scores_into(s0_s, 0)

    def sel_step(i, carry):
        c = 2 * i
        scores_into(s1_s, c + 1)
        softmax_from(s0_s, c)
        scores_into(s0_s, c + 2)
        softmax_from(s1_s, c + 1)
        return carry

    n_main = (qb * Q_BLOCK + 2 * SEL_SUB - 1) // (2 * SEL_SUB)
    lax.fori_loop(0, n_main, sel_step, 0)
    inv_s = 1.0 / jnp.maximum(acc_s[NSA_DH:NSA_DH + 1, :], 1e-30)

    gall = gt_ref[...]
    ggrp = jnp.where(is_g0, gall[0:3 * NSA_REP], gall[3 * NSA_REP:3 * NSA_HEADS])
    for p2 in range(NSA_REP // 2):
        halves = []
        for hh in range(2):
            r = 2 * p2 + hh
            sl = slice(r * Q_BLOCK, (r + 1) * Q_BLOCK)
            g_c, g_s, g_w = (ggrp[3 * r + br:3 * r + br + 1, :] for br in range(3))
            halves.append((g_c * inv_c[:, sl]) * acc_c[0:NSA_DH, sl]
                          + (g_s * inv_s[:, sl]) * acc_s[0:NSA_DH, sl]
                          + (g_w * inv_w[:, sl]) * acc_w[0:NSA_DH, sl])
        o_ref[:, p2 * 128:(p2 + 1) * 128] = jnp.concatenate(halves, axis=0).T.astype(BF16)


def _nsa_attn(q_r, kc, vct, ks, vst, kw, vwt, gt, ovt):
    per_b = lambda r, c: pl.BlockSpec((None, r, c), lambda b, g, i: (b, 0, 0))
    per_bg = lambda r, c: pl.BlockSpec((None, r, c), lambda b, g, i: (b, g, 0))
    const = lambda a: pl.BlockSpec(a.shape, lambda b, g, i: (0, 0))
    return pl.pallas_call(
        _nsa_body,
        grid=(BATCH, NSA_KV, N_QB),
        in_specs=[pl.BlockSpec((None, NSA_REP * NSA_DH, Q_BLOCK), lambda b, g, i: (b, g, i)),
                  per_b(N_CMP_PAD, 128), per_bg(NSA_VROWS, N_CMP_PAD),
                  per_b(SEQ, 256), per_bg(NSA_VROWS, SEQ),
                  per_b(SEQ, 128), per_bg(NSA_VROWS, SEQ),
                  pl.BlockSpec((None, 3 * NSA_HEADS, Q_BLOCK), lambda b, g, i: (b, 0, i)),
                  const(ovt)],
        out_specs=pl.BlockSpec((None, Q_BLOCK, NSA_REP * NSA_DH), lambda b, g, i: (b, i, g)),
        out_shape=jax.ShapeDtypeStruct((BATCH, SEQ, NSA_WIDTH), BF16),
        scratch_shapes=[pltpu.VMEM((256, NSA_NL), BF16),
                        pltpu.VMEM((N_SEL, Q_BLOCK), F32), pltpu.VMEM((N_SEL, Q_BLOCK), F32),
                        pltpu.VMEM((SEL_SUB, NSA_NL), BF16), pltpu.VMEM((SEL_SUB, NSA_NL), BF16),
                        pltpu.VMEM((N_CMP_PAD, NSA_NL), BF16), pltpu.VMEM((WIN_KEYS, NSA_NL), BF16),
                        pltpu.VMEM((Q_BLOCK, NSA_NL), BF16),
                        pltpu.VMEM((1, NSA_NL), F32), pltpu.VMEM((NSA_VROWS, NSA_NL), F32)],
        compiler_params=_cparams("parallel", "parallel", "arbitrary"),
        name="nsa_attn",
    )(q_r, kc, vct, ks, vst, kw, vwt, gt, ovt)


OUT_TM = 512


def _outx_body(x_ref, oh_ref, on_ref, w_ref, nw_ref, wq_ref, k_ref, v_ref, wo_ref, o_ref):
    y = (x_ref[...] + _dot(oh_ref[...], w_ref[0:HG_WIDTH, :])
         + _dot(on_ref[...], w_ref[HG_WIDTH:HG_WIDTH + NSA_WIDTH, :]))
    hx = _rms(y, nw_ref[...]).astype(BF16)
    q = (_dot(hx, wq_ref[...]) * (X_DH ** -0.5)).astype(BF16)
    heads = []
    for h in range(X_HEADS):
        sl = slice(h * X_DH, (h + 1) * X_DH)
        s = _dot_nt(q[:, sl], k_ref[:, sl])
        e = jnp.exp(s - jnp.max(s, axis=-1, keepdims=True))
        p = e / jnp.sum(e, axis=-1, keepdims=True)
        heads.append(_dot(p.astype(BF16), v_ref[:, sl]))
    o_ref[...] = y + _dot(jnp.concatenate(heads, axis=1).astype(BF16), wo_ref[...])


def _outproj_xattn(x1, o_hg, o_nsa, w_out, nw, wq, k, v, wo):
    width = X_HEADS * X_DH
    tiles_per_b = SEQ // OUT_TM
    row = lambda w: pl.BlockSpec((OUT_TM, w), lambda i: (i, 0))
    const = lambda r, c: pl.BlockSpec((r, c), lambda i: (0, 0))
    kv = pl.BlockSpec((None, MEM_LEN, width), lambda i: (i // tiles_per_b, 0, 0))
    return pl.pallas_call(
        _outx_body,
        grid=(TOKENS // OUT_TM,),
        in_specs=[row(D_MODEL), row(HG_WIDTH), row(NSA_WIDTH), const(D_MODEL, D_MODEL), const(1, D_MODEL),
                  const(D_MODEL, width), kv, kv, const(width, D_MODEL)],
        out_specs=row(D_MODEL),
        out_shape=jax.ShapeDtypeStruct((TOKENS, D_MODEL), F32),
        compiler_params=_cparams("parallel"),
        name="out_proj_xattn",
    )(x1, o_hg, o_nsa, w_out, nw, wq, k, v, wo)


def _memkv_body(m_ref, nw_ref, wk_ref, wv_ref, k_ref, v_ref):
    hm = _rms(m_ref[...], nw_ref[...]).astype(BF16)
    k_ref[...] = _dot(hm, wk_ref[...]).astype(BF16)
    v_ref[...] = _dot(hm, wv_ref[...]).astype(BF16)


def _memkv(mem, nw, wk, wv):
    width = X_HEADS * X_DH
    wspec = pl.BlockSpec((D_MODEL, width), lambda b: (0, 0))
    ospec = pl.BlockSpec((None, MEM_LEN, width), lambda b: (b, 0, 0))
    osh = jax.ShapeDtypeStruct((BATCH, MEM_LEN, width), BF16)
    return pl.pallas_call(
        _memkv_body,
        grid=(BATCH,),
        in_specs=[pl.BlockSpec((None, MEM_LEN, D_MODEL), lambda b: (b, 0, 0)),
                  pl.BlockSpec((1, D_MODEL), lambda b: (0, 0)), wspec, wspec],
        out_specs=(ospec, ospec), out_shape=(osh, osh),
        compiler_params=_cparams("parallel"),
        name="xattn_memkv",
    )(mem, nw, wk, wv)


def _overlap_t():
    c0 = np.arange(N_CMP)[:, None] * CMP_STRIDE
    s0 = np.arange(N_SEL)[None, :] * SEL_LEN
    ov = np.clip(np.minimum(c0 + CMP_LEN, s0 + SEL_LEN) - np.maximum(c0, s0), 0, None) / CMP_LEN
    out = np.zeros((N_SEL, N_CMP_PAD), np.float32)
    out[:, :N_CMP] = ov.T
    return out


def kernel(x, mem, positions, ffn1_norm, ffn1_w_gate, ffn1_w_up, ffn1_w_down, mix_norm, w_in, hgrn_lb_logits, hgrn_out_norm, nsa_cmp_pe, nsa_cmp_k_w1, nsa_cmp_k_w2, nsa_cmp_v_w1, nsa_cmp_v_w2, w_out, xattn_norm, mem_norm, xattn_wq, xattn_wk, xattn_wv, xattn_wo, ffn2_norm, ffn2_w_gate, ffn2_w_up, ffn2_w_down, final_norm):
    bf = lambda a: a.astype(BF16)
    vec = lambda a: a.reshape(1, -1).astype(F32)
    x2d = x.reshape(TOKENS, D_MODEL)

    x1_head, h_head, w1_gate, w1_up, w1_down = _ffn(
        x2d, vec(ffn1_norm[0]), ffn1_w_gate[0], ffn1_w_up[0], ffn1_w_down[0], vec(mix_norm[0]),
        final=False, n_tiles=1, tm=FFN_HEAD_TM, tf=FFN_HEAD_TF)
    x1, h_mix, w2_gate, w2_up, w2_down, w_out_b, wq_b, wk_b, wv_b, wo_b = _ffn(
        x2d, vec(ffn1_norm[0]), w1_gate, w1_up, w1_down, vec(mix_norm[0]), final=False,
        head=(x1_head, h_head),
        step_casts=(ffn2_w_gate[0], ffn2_w_up[0], ffn2_w_down[0]),
        tile_casts=(w_out[0], xattn_wq[0], xattn_wk[0], xattn_wv[0], xattn_wo[0]))

    w_t = w_in[0].T
    w_tail = jnp.pad(w_t[PROJ_FULL_TILES * PROJ_TN:], ((0, PROJ_W - D_IN), (0, 0)))
    proj, proj_f = _proj(h_mix, w_t, w_tail)

    o_hg = _hgrn(proj, proj_f, hgrn_lb_logits.astype(F32), vec(hgrn_out_norm[0]))

    inv = ROPE_THETA ** (-jnp.arange(NSA_DH // 2, dtype=F32) / (NSA_DH // 2))
    inv128 = jnp.tile(inv, 128 // (NSA_DH // 2)).reshape(1, 128)
    pos4 = jnp.repeat(positions.astype(F32).reshape(BATCH, SEQ // 4, 4), NSA_DH // 2, axis=-1)
    q_r, kc_tok, vc_tok, ks, kw, vst, vwt, gt = _nsa_prep(proj, pos4, inv128)

    def over_groups(w):
        z = jnp.zeros_like(w)
        return bf(jnp.concatenate([jnp.concatenate([w, z], axis=-1), jnp.concatenate([z, w], axis=-1)], axis=-2))

    per_pos = lambda w1: over_groups(w1.reshape(CMP_LEN, NSA_DH, CMP_HIDDEN))
    pe = nsa_cmp_pe[0].astype(F32)
    kc, vct = _compress(kc_tok, vc_tok, jnp.concatenate([pe, pe], axis=1),
                        per_pos(nsa_cmp_k_w1[0]), over_groups(nsa_cmp_k_w2[0]),
                        per_pos(nsa_cmp_v_w1[0]), over_groups(nsa_cmp_v_w2[0]))
    o_nsa = _nsa_attn(q_r, kc, vct, ks, vst, kw, vwt, gt, jnp.asarray(_overlap_t(), dtype=BF16))

    km, vm = _memkv(mem, vec(mem_norm[0]), wk_b, wv_b)
    x3 = _outproj_xattn(x1, o_hg.reshape(TOKENS, HG_WIDTH), o_nsa.reshape(TOKENS, NSA_WIDTH),
                        w_out_b, vec(xattn_norm[0]), wq_b, km, vm, wo_b)

    (out,) = _ffn(x3, vec(ffn2_norm[0]), w2_gate, w2_up, w2_down, vec(final_norm), final=True)
    return out.reshape(BATCH, SEQ, D_MODEL)
```

```python
import functools

import numpy as np
import jax
import jax.numpy as jnp
from jax import lax
from jax.experimental import pallas as pl
from jax.experimental.pallas import tpu as pltpu

F32 = jnp.float32
BF16 = jnp.bfloat16

D_MODEL = 2048
BATCH = 2
SEQ = 4096
TOKENS = BATCH * SEQ
RMS_EPS = 1e-6
ROPE_THETA = 10000.0
HG_WIDTH = 1024
HG_HEADS = 8
HG_D = 128
HG_CHUNK = 128
HG_LEVELS = (64, 32, 16, 8, 4, 2, 1)
NSA_WIDTH = 1024
NSA_DH = 64
NSA_HEADS = 16
NSA_KV = 2
NSA_REP = 8
NSA_VROWS = NSA_DH + 16
CMP_LEN = 32
CMP_STRIDE = 16
CMP_HIDDEN = 256
N_CMP = (SEQ - CMP_LEN) // CMP_STRIDE + 1
N_CMP_PAD = 256
SEL_LEN = 64
N_SEL = SEQ // SEL_LEN
SEL_TOP = 16
WINDOW = 512
Q_BLOCK = 128
N_QB = SEQ // Q_BLOCK
SEL_SUB = 256
WIN_KEYS = WINDOW + Q_BLOCK
MEM_LEN = 256
X_HEADS = 4
X_DH = 128
D_FF = 5632
IN_SIZES = (1024, 1024, 1024, 1024, 1024, 128, 128, 128, 128, 128, 128, 48)
D_IN = sum(IN_SIZES)
PROJ_NSA_OFF = sum(IN_SIZES[:4])
NSA_PROJ = 2048
PROJ_W = PROJ_NSA_OFF + NSA_PROJ
NEG = -1e30
LOG2E = 1.4426950408889634

V7X_VMEM_BYTES = 64 * 1024 * 1024
VMEM_LIMIT = V7X_VMEM_BYTES - 8 * 1024 * 1024


def _cparams(*sem, flags=None):
    return pltpu.CompilerParams(dimension_semantics=sem, vmem_limit_bytes=VMEM_LIMIT, flags=flags)


def _rms(x, w):
    return x * lax.rsqrt(jnp.mean(x * x, axis=-1, keepdims=True) + RMS_EPS) * w


def _silu(x):
    return x * jax.nn.sigmoid(x)


def _dot(a, b):
    return jnp.dot(a, b, preferred_element_type=F32)


def _dot_f32_by_01(sel, x):
    hi = x.astype(BF16)
    r1 = x - hi.astype(F32)
    mid = r1.astype(BF16)
    lo = (r1 - mid.astype(F32)).astype(BF16)
    n = x.shape[1]
    y = _dot(sel, jnp.concatenate([hi, mid, lo], axis=1))
    return y[:, 0:n] + y[:, n:2 * n] + y[:, 2 * n:3 * n]


def _dot_nt(a, b):
    return lax.dot_general(a, b, (((1,), (1,)), ((), ())), preferred_element_type=F32)


FFN_TM = 512
FFN_TF = 512
FFN_HEAD_TM = 1024
FFN_HEAD_TF = 256


def _ffn_body(x_ref, nw_ref, wg_ref, wu_ref, wd_ref, nw2_ref, *rest,
              final, n_head, f32_weights):
    if n_head:
        (head_x_ref, head_hn_ref), rest = rest[:2], rest[2:]
    if final:
        o_ref, rest = rest[0], rest[1:]
    else:
        (o_ref, hn_ref), rest = rest[:2], rest[2:]
    if f32_weights:
        w_copy, rest = rest[:3], rest[3:]
    (h_scr,) = rest
    i = pl.program_id(0)
    j = pl.program_id(1)
    last = pl.num_programs(1) - 1

    def swiglu_tile(h):
        wg, wu, wd = wg_ref[...], wu_ref[...], wd_ref[...]
        if f32_weights:
            wg, wu, wd = wg.astype(BF16), wu.astype(BF16), wd.astype(BF16)
            for dst, w in zip(w_copy, (wg, wu, wd)):
                dst[...] = w
        g = _dot(h, wg)
        u = _dot(h, wu)
        return _dot((_silu(g) * u).astype(BF16), wd)

    def when(cond):
        return pl.when(cond & (i >= n_head)) if n_head else pl.when(cond)

    if n_head:
        @pl.when((i < n_head) & (j == 0))
        def _():
            o_ref[...] = head_x_ref[...]
            hn_ref[...] = head_hn_ref[...]

    @when(j == 0)
    def _():
        h = _rms(x_ref[...], nw_ref[...]).astype(BF16)
        h_scr[...] = h
        o_ref[...] = swiglu_tile(h)

    @when((j > 0) & (j < last))
    def _():
        o_ref[...] += swiglu_tile(h_scr[...])

    @when(j == last)
    def _():
        y = x_ref[...] + 0.5 * (o_ref[...] + swiglu_tile(h_scr[...]))
        if final:
            o_ref[...] = _rms(y, nw2_ref[...])
        else:
            o_ref[...] = y
            hn_ref[...] = _rms(y, nw2_ref[...]).astype(BF16)


def _ffn(x, nw, wg, wu, wd, nw2, final, head=None, n_tiles=None, tm=FFN_TM, tf=FFN_TF):
    ni, nj = (n_tiles or TOKENS // tm), D_FF // tf
    f32_weights = wg.dtype == F32
    n_head = head[0].shape[0] // tm if head is not None else 0
    once = dict(pipeline_mode=pl.Buffered(1)) if ni == 1 else {}
    row = pl.BlockSpec((tm, D_MODEL), lambda i, j: (i, 0), **once)
    vec = pl.BlockSpec((1, D_MODEL), lambda i, j: (0, 0))
    wj = (lambda i, j: jnp.where(i < n_head, 0, j)) if n_head else (lambda i, j: j)
    in_specs = [row, vec,
                pl.BlockSpec((D_MODEL, tf), lambda i, j: (0, wj(i, j))),
                pl.BlockSpec((D_MODEL, tf), lambda i, j: (0, wj(i, j))),
                pl.BlockSpec((tf, D_MODEL), lambda i, j: (wj(i, j), 0)),
                vec]
    main_shapes = [jax.ShapeDtypeStruct((ni * tm, D_MODEL), F32)]
    if not final:
        main_shapes.append(jax.ShapeDtypeStruct((ni * tm, D_MODEL), BF16))
    head_specs = [pl.BlockSpec((tm, D_MODEL), lambda i, j: (jnp.minimum(i, n_head - 1), 0),
                               pipeline_mode=pl.Buffered(1))] * 2 if n_head else []
    w_copy_specs = in_specs[2:5] if f32_weights else []
    w_copy_shapes = [jax.ShapeDtypeStruct(w.shape, BF16) for w in (wg, wu, wd)] if f32_weights else []
    return pl.pallas_call(
        functools.partial(_ffn_body, final=final, n_head=n_head, f32_weights=f32_weights),
        grid=(ni, nj), in_specs=in_specs + head_specs,
        out_specs=tuple([row] * len(main_shapes) + w_copy_specs),
        out_shape=tuple(main_shapes + w_copy_shapes),
        scratch_shapes=[pltpu.VMEM((tm, D_MODEL), BF16)],
        compiler_params=_cparams("parallel", "arbitrary"),
        name="ffn_final" if final else ("ffn_head" if f32_weights else "ffn"),
    )(x, nw, wg, wu, wd, nw2, *(head or ()))


PROJ_TM = 1024
PROJ_TN = 512


PROJ_F_TILE0 = IN_SIZES[0] // PROJ_TN
PROJ_F_TILES = IN_SIZES[1] // PROJ_TN


PROJ_FULL_TILES = D_IN // PROJ_TN


def _proj_body(a_ref, wt_ref, tail_ref, o_ref, f_ref):
    j = pl.program_id(1)
    w = jnp.where(j >= PROJ_FULL_TILES, tail_ref[...], wt_ref[...]).astype(BF16)
    y = _dot_nt(a_ref[...], w)
    o_ref[...] = y.astype(BF16)

    @pl.when((j >= PROJ_F_TILE0) & (j < PROJ_F_TILE0 + PROJ_F_TILES))
    def _():
        f_ref[...] = y


def _proj(a, wt, wt_tail):
    m, k = a.shape
    f_tile = lambda i, j: (i, jnp.clip(j - PROJ_F_TILE0, 0, PROJ_F_TILES - 1))
    return pl.pallas_call(
        _proj_body,
        grid=(m // PROJ_TM, PROJ_W // PROJ_TN),
        in_specs=[pl.BlockSpec((PROJ_TM, k), lambda i, j: (i, 0)),
                  pl.BlockSpec((PROJ_TN, k), lambda i, j: (jnp.minimum(j, PROJ_FULL_TILES - 1), 0)),
                  pl.BlockSpec((PROJ_TN, k), lambda i, j: (0, 0))],
        out_specs=(pl.BlockSpec((PROJ_TM, PROJ_TN), lambda i, j: (i, j)),
                   pl.BlockSpec((PROJ_TM, PROJ_TN), f_tile)),
        out_shape=(jax.ShapeDtypeStruct((m, PROJ_W), BF16),
                   jax.ShapeDtypeStruct((m, IN_SIZES[1]), F32)),
        compiler_params=_cparams("parallel", "arbitrary"),
        name="proj_in",
    )(a, wt, wt_tail)


HG_ROWS = 1024
HG_CUM = 256


def _hgrn_body(q_ref, f_ref, i_ref, g_ref, lbl_ref, nw_ref, o_ref, st_ref, k_s, b_s):
    c = pl.program_id(2)

    @pl.when(c == 0)
    def _():
        st_ref[...] = jnp.zeros_like(st_ref)

    l0 = lbl_ref[0:1, :]
    l1 = lbl_ref[1:2, :]
    lmax = jnp.maximum(l0, l1)
    e0 = jnp.exp(l0 - lmax)
    lb = e0 / (e0 + jnp.exp(l1 - lmax))

    C = HG_CHUNK
    f = lb + (1.0 - lb) * jax.nn.sigmoid(f_ref[...])
    k_s[...] = 1.0 - f
    r_i = lax.broadcasted_iota(jnp.int32, (HG_CUM, HG_CUM), 0)
    c_i = lax.broadcasted_iota(jnp.int32, (HG_CUM, HG_CUM), 1)
    tri = jnp.where((r_i >= c_i) & (r_i // C == c_i // C), 1.0, 0.0).astype(BF16)
    logf = jnp.log2(f)
    for r0 in range(0, HG_ROWS, HG_CUM):
        b_s[r0:r0 + HG_CUM, :] = _dot_f32_by_01(tri, logf[r0:r0 + HG_CUM])

    t_i = lax.broadcasted_iota(jnp.int32, (C, C), 0)
    s_i = lax.broadcasted_iota(jnp.int32, (C, C), 1)
    level_mask = [(t_i // (2 * w) == s_i // (2 * w)) & (t_i % (2 * w) >= w) & (s_i % (2 * w) < w)
                  for w in HG_LEVELS]
    sub_r = lax.broadcasted_iota(jnp.int32, (8, HG_D), 0)
    row_i = lax.broadcasted_iota(jnp.int32, (C, HG_D), 0)
    right_sign = {w: jnp.where(row_i % (2 * w) >= w, 1.0, -1.0) for w in HG_LEVELS if w < 8}

    def neg_abs_diff(w, r0, b):
        row = lambda r, n: jnp.broadcast_to(b_s[r0 + r:r0 + r + 1, :], (n, HG_D))
        if w >= 8:
            parts = []
            for p0 in range(0, C, 2 * w):
                ref = row(p0 + w - 1, w)
                parts += [ref - b[p0:p0 + w], b[p0 + w:p0 + 2 * w] - ref]
            return jnp.concatenate(parts, axis=0)
        if w == 4:
            bref = jnp.concatenate([row(p0 + 3, 8) for p0 in range(0, C, 8)], axis=0)
        elif w == 2:
            bref = jnp.concatenate([jnp.where(sub_r < 4, row(p0 + 1, 8), row(p0 + 5, 8))
                                    for p0 in range(0, C, 8)], axis=0)
        else:
            bref = jnp.where(row_i % 2 == 1, pltpu.roll(b, 1, 0), b)
        return (b - bref) * right_sign[w]

    chunks = [ci * C for ci in range(HG_ROWS // C)]
    rows = lambda ref, r0: ref[r0:r0 + C, :]
    att = [jnp.zeros((C, C), F32) for _ in chunks]
    q16 = [rows(q_ref, r0) for r0 in chunks]
    k16 = [rows(k_s, r0).astype(BF16) for r0 in chunks]
    for w, mask in zip(HG_LEVELS, level_mask):
        for n, r0 in enumerate(chunks):
            b = rows(b_s, r0)
            e = jnp.exp2(neg_abs_diff(w, r0, b)).astype(BF16)
            att[n] = jnp.where(mask, _dot_nt(q16[n] * e, k16[n] * e), att[n])
    o_intra = []
    for n, r0 in enumerate(chunks):
        q, k, v = q16[n].astype(F32), rows(k_s, r0), rows(i_ref, r0)
        o_intra.append(_dot(att[n].astype(BF16), v)
                       + jnp.sum(q * k, axis=-1, keepdims=True) * v.astype(F32))
    upd = []
    for r0 in chunks:
        bl = b_s[r0 + C - 1:r0 + C, :]
        kd = rows(k_s, r0) * jnp.exp2(bl - rows(b_s, r0))
        v_t = rows(i_ref, r0).astype(F32).T.astype(BF16)
        upd.append((jnp.exp2(bl), _dot(v_t, kd.astype(BF16))))
    st_t = st_ref[...]
    for n, r0 in enumerate(chunks):
        qe = (q16[n].astype(F32) * jnp.exp2(rows(b_s, r0))).astype(BF16)
        o = o_intra[n] + _dot_nt(qe, st_t.astype(BF16))
        st_t = st_t * upd[n][0] + upd[n][1]
        o = o * lax.rsqrt(jnp.mean(o * o, axis=-1, keepdims=True) + RMS_EPS)
        o_ref[r0:r0 + C, :] = (o * nw_ref[...] * _silu(rows(g_ref, r0).astype(F32))).astype(BF16)
    st_ref[...] = st_t


def _hgrn(proj, proj_f, lb_logits, norm_w):
    p3 = proj.reshape(BATCH, SEQ, PROJ_W)
    f3 = proj_f.reshape(BATCH, SEQ, HG_WIDTH)

    def col(off):
        return pl.BlockSpec((None, HG_ROWS, HG_D), lambda b, h, c: (b, c, off + h))

    return pl.pallas_call(
        _hgrn_body,
        grid=(BATCH, HG_HEADS, SEQ // HG_ROWS),
        in_specs=[col(0), col(0), col(2 * HG_HEADS), col(3 * HG_HEADS),
                  pl.BlockSpec((2, HG_D), lambda b, h, c: (0, h)),
                  pl.BlockSpec((1, HG_D), lambda b, h, c: (0, h))],
        out_specs=pl.BlockSpec((None, HG_ROWS, HG_D), lambda b, h, c: (b, c, h)),
        out_shape=jax.ShapeDtypeStruct((BATCH, SEQ, HG_WIDTH), BF16),
        scratch_shapes=[pltpu.VMEM((HG_D, HG_D), F32),
                        pltpu.VMEM((HG_ROWS, HG_D), F32),
                        pltpu.VMEM((HG_ROWS, HG_D), F32)],
        compiler_params=_cparams("parallel", "parallel", "arbitrary"),
        name="hgrn2",
    )(p3, f3, p3, p3, lb_logits, norm_w)


PREP_TM = 256


def _prep_body(p_ref, pos4_ref, inv_ref, q_ref, kc_ref, vc_ref, ks_ref, kw_ref,
               vst_ref, vwt_ref, gt_ref, cos_s, sin_s):
    n_freq = NSA_DH // 2
    ang4 = pos4_ref[...] * inv_ref[...]
    seg = lax.broadcasted_iota(jnp.int32, (PREP_TM // 4, 128), 1) // n_freq
    for table, dst in ((jnp.cos(ang4), cos_s), (jnp.sin(ang4), sin_s)):
        for u in range(4):
            one = jnp.where(seg == u, table, 0.0)
            full = one
            for k in range(1, 4):
                full = full + pltpu.roll(one, k * n_freq, 1)
            dst[pl.ds(u, PREP_TM // 4, stride=4), :] = full
    cos = cos_s[...]
    sin = sin_s[...]
    lane = lax.broadcasted_iota(jnp.int32, (PREP_TM, 128), 1)
    lo = (lane & (NSA_DH // 2)) == 0
    sin_signed = jnp.where(lo, -sin, sin)

    def rope(x):
        rot = jnp.where(lo, pltpu.roll(x, 128 - NSA_DH // 2, 1), pltpu.roll(x, NSA_DH // 2, 1))
        return x * cos + rot * sin_signed

    cols = lambda c0: p_ref[:, c0:c0 + 128].astype(F32)
    scale = NSA_DH ** -0.5 * LOG2E
    for cblk in range(NSA_WIDTH // 128):
        q_ref[cblk * 128:(cblk + 1) * 128, :] = (rope(cols(cblk * 128)) * scale).T.astype(BF16)
    c_kc, c_vc, c_ks, c_vs, c_kw, c_vw, c_gate = (int(c) for c in np.cumsum(IN_SIZES[4:11]))
    kc_ref[...] = rope(cols(c_kc))
    vc_ref[...] = cols(c_vc)
    ks_ref[:, 0:128] = rope(cols(c_ks)).astype(BF16)
    blk = (pl.program_id(1) * PREP_TM + lax.broadcasted_iota(jnp.int32, (PREP_TM, 128), 0)) // SEL_LEN
    ks_ref[:, 128:256] = jnp.where(lane == blk, 1.0, 0.0).astype(BF16)
    kw_ref[...] = rope(cols(c_kw)).astype(BF16)
    ones = jnp.ones((NSA_VROWS - NSA_DH, PREP_TM), BF16)
    for v_ref, c0 in ((vst_ref, c_vs), (vwt_ref, c_vw)):
        vt = cols(c0).T.astype(BF16)
        for g in range(NSA_KV):
            v_ref[g * NSA_VROWS:g * NSA_VROWS + NSA_DH, :] = vt[g * NSA_DH:(g + 1) * NSA_DH]
            v_ref[g * NSA_VROWS + NSA_DH:(g + 1) * NSA_VROWS, :] = ones
    gt_ref[...] = jax.nn.sigmoid(cols(c_gate)).T[0:3 * NSA_HEADS, :]


def _nsa_prep(proj, pos4, inv128):
    nt = SEQ // PREP_TM
    p3 = proj.reshape(BATCH, SEQ, PROJ_W)
    nat = lambda w: pl.BlockSpec((None, PREP_TM, w), lambda b, i: (b, i, 0))
    tr = lambda r: pl.BlockSpec((None, r, PREP_TM), lambda b, i: (b, 0, i))
    sds = jax.ShapeDtypeStruct
    return pl.pallas_call(
        _prep_body,
        grid=(BATCH, nt),
        in_specs=[pl.BlockSpec((None, PREP_TM, NSA_PROJ), lambda b, i: (b, i, PROJ_NSA_OFF // NSA_PROJ)),
                  pl.BlockSpec((None, PREP_TM // 4, 128), lambda b, i: (b, i, 0)),
                  pl.BlockSpec((1, 128), lambda b, i: (0, 0))],
        out_specs=(tr(NSA_WIDTH), nat(128), nat(128), nat(256), nat(128),
                   tr(NSA_KV * NSA_VROWS), tr(NSA_KV * NSA_VROWS), tr(3 * NSA_HEADS)),
        out_shape=(sds((BATCH, NSA_WIDTH, SEQ), BF16),
                   sds((BATCH, SEQ, 128), F32),
                   sds((BATCH, SEQ, 128), F32),
                   sds((BATCH, SEQ, 256), BF16),
                   sds((BATCH, SEQ, 128), BF16),
                   sds((BATCH, NSA_KV * NSA_VROWS, SEQ), BF16),
                   sds((BATCH, NSA_KV * NSA_VROWS, SEQ), BF16),
                   sds((BATCH, 3 * NSA_HEADS, SEQ), F32)),
        scratch_shapes=[pltpu.VMEM((PREP_TM, 128), F32)] * 2,
        compiler_params=_cparams("parallel", "parallel"),
        name="nsa_prep",
    )(p3, pos4, inv128)


def _cmp_body(tk_ref, tv_ref, pe_ref, kw1_ref, kw2_ref, vw1_ref, vw2_ref, kc_ref, vct_ref, y1_s, y2_s):
    row = lax.broadcasted_iota(jnp.int32, (N_CMP_PAD, NSA_KV * CMP_HIDDEN), 0)

    def mlp(t_ref, w1_ref, w2_ref):
        y1_s[...] = jnp.zeros_like(y1_s)
        y2_s[...] = jnp.zeros_like(y2_s)
        for l in range(CMP_STRIDE):
            x = t_ref[pl.ds(l, N_CMP_PAD, stride=CMP_STRIDE), :]
            y1_s[...] += _dot((x + pe_ref[l:l + 1, :]).astype(BF16), w1_ref[l])
            y2_s[...] += _dot((x + pe_ref[CMP_STRIDE + l:CMP_STRIDE + l + 1, :]).astype(BF16),
                              w1_ref[CMP_STRIDE + l])
        hid = jnp.where(row < N_CMP, y1_s[...] + pltpu.roll(y2_s[...], N_CMP_PAD - 1, 0), 0.0)
        return _dot(_silu(hid).astype(BF16), w2_ref[...])

    kc_ref[...] = mlp(tk_ref, kw1_ref, kw2_ref).astype(BF16)
    y1_s[:, 0:128] = mlp(tv_ref, vw1_ref, vw2_ref)
    vt = y1_s[:, 0:128].T.astype(BF16)
    ones = jnp.ones((NSA_VROWS - NSA_DH, N_CMP_PAD), BF16)
    for g in range(NSA_KV):
        vct_ref[g * NSA_VROWS:g * NSA_VROWS + NSA_DH, :] = vt[g * NSA_DH:(g + 1) * NSA_DH]
        vct_ref[g * NSA_VROWS + NSA_DH:(g + 1) * NSA_VROWS, :] = ones


def _compress(tk, tv, pe2, kw1, kw2, vw1, vw2):
    seg = pl.BlockSpec((None, SEQ, 128), lambda b: (b, 0, 0))
    full2 = lambda a: pl.BlockSpec(a.shape, lambda b: (0,) * a.ndim)
    return pl.pallas_call(
        _cmp_body,
        grid=(BATCH,),
        in_specs=[seg, seg, full2(pe2), full2(kw1), full2(kw2), full2(vw1), full2(vw2)],
        out_specs=(pl.BlockSpec((None, N_CMP_PAD, 128), lambda b: (b, 0, 0)),
                   pl.BlockSpec((None, NSA_KV * NSA_VROWS, N_CMP_PAD), lambda b: (b, 0, 0))),
        out_shape=(jax.ShapeDtypeStruct((BATCH, N_CMP_PAD, 128), BF16),
                   jax.ShapeDtypeStruct((BATCH, NSA_KV * NSA_VROWS, N_CMP_PAD), BF16)),
        scratch_shapes=[pltpu.VMEM((N_CMP_PAD, NSA_KV * CMP_HIDDEN), F32)] * 2,
        compiler_params=_cparams("parallel"),
        name="nsa_compress",
    )(tk, tv, pe2, kw1, kw2, vw1, vw2)


NSA_NL = NSA_REP * Q_BLOCK


def _nsa_body(q_ref, kc_ref, vct_ref, ks_ref, vst_ref, kw_ref, vwt_ref, gt_ref, ovt_ref, *rest, n_casts):
    cast_in, o_ref, cast_out = rest[:n_casts], rest[n_casts], rest[n_casts + 1:2 * n_casts + 1]
    qa_s, sc_s, rk_s, s0_s, s1_s, sc_buf, sw_buf, sd_buf, m_s, acc_s = rest[2 * n_casts + 1:]
    g = pl.program_id(1)
    qb = pl.program_id(2)
    q0 = pl.multiple_of(qb * Q_BLOCK, Q_BLOCK)
    is_g0 = g == 0

    zero_slab = jnp.zeros((NSA_DH, Q_BLOCK), BF16)
    for r in range(NSA_REP):
        s = q_ref[r * NSA_DH:(r + 1) * NSA_DH, :]
        qa_s[0:NSA_DH, r * Q_BLOCK:(r + 1) * Q_BLOCK] = jnp.where(is_g0, s, zero_slab)
        qa_s[NSA_DH:2 * NSA_DH, r * Q_BLOCK:(r + 1) * Q_BLOCK] = jnp.where(is_g0, zero_slab, s)
    qa_s[128 + N_SEL:256, :] = jnp.zeros((128 - N_SEL, NSA_NL), BF16)

    tq = q0 + lax.broadcasted_iota(jnp.int32, (1, Q_BLOCK), 1)

    def mask_pair(s, valid):
        return jnp.concatenate([jnp.where(valid, s[:, r * Q_BLOCK:(r + 1) * Q_BLOCK], NEG)
                                for r in range(NSA_REP)], axis=1)

    n_i = lax.broadcasted_iota(jnp.int32, (N_CMP_PAD, Q_BLOCK), 0)
    valid_c = (n_i * CMP_STRIDE + (CMP_LEN - 1) <= tq) & (n_i < N_CMP)
    sc_buf[...] = mask_pair(_dot(kc_ref[...], qa_s[0:128, :]).astype(BF16), valid_c)

    w0 = pl.multiple_of(jnp.maximum(q0 - WINDOW, 0), Q_BLOCK)
    dpos = tq - (w0 + lax.broadcasted_iota(jnp.int32, (WIN_KEYS, Q_BLOCK), 0))
    sw_buf[...] = mask_pair(_dot(kw_ref[pl.ds(w0, WIN_KEYS), :], qa_s[0:128, :]).astype(BF16),
                            (dpos >= 0) & (dpos < WINDOW))

    for src, dst in zip(cast_in, cast_out):
        dst[...] = src[...].astype(BF16)

    sc = sc_buf[...]
    m_c = jnp.max(sc, axis=0, keepdims=True)
    e_c = jnp.exp2(sc - m_c)
    acc_c = _dot(vct_ref[...], e_c)
    inv_c = jnp.where(m_c.astype(F32) > 0.5 * NEG, 1.0 / jnp.maximum(acc_c[NSA_DH:NSA_DH + 1], 1e-30), 0.0)
    imp_h = _dot(ovt_ref[...], e_c) * inv_c

    d_i = lax.broadcasted_iota(jnp.int32, (Q_BLOCK, Q_BLOCK), 0)
    t_i = lax.broadcasted_iota(jnp.int32, (Q_BLOCK, Q_BLOCK), 1)
    sd_buf[...] = mask_pair(_dot(ks_ref[pl.ds(q0, Q_BLOCK), 0:128], qa_s[0:128, :]).astype(BF16),
                            d_i <= t_i)

    sw = sw_buf[...]
    acc_w = _dot(vwt_ref[:, pl.ds(w0, WIN_KEYS)], jnp.exp2(sw - jnp.max(sw, axis=0, keepdims=True)))
    inv_w = 1.0 / jnp.maximum(acc_w[NSA_DH:NSA_DH + 1], 1e-30)

    s = sd_buf[...]
    m16 = jnp.max(s, axis=0, keepdims=True)
    m_s[...] = m16.astype(F32)
    acc_s[...] = _dot(vst_ref[:, pl.ds(q0, Q_BLOCK)], jnp.exp2(s - m16))

    imp = imp_h[:, 0:Q_BLOCK]
    for r in range(1, NSA_REP):
        imp = imp + imp_h[:, r * Q_BLOCK:(r + 1) * Q_BLOCK]
    j_i = lax.broadcasted_iota(jnp.int32, (N_SEL, Q_BLOCK), 0)
    cur = tq // SEL_LEN
    forced = (j_i == 0) | (j_i == cur) | (j_i == cur - 1)
    score = jnp.where(forced, jnp.inf, jnp.where(j_i > cur, -jnp.inf, imp))
    sc_s[...] = score
    rk_s[...] = jnp.zeros_like(rk_s)
    sub8 = lax.broadcasted_iota(jnp.int32, (8, Q_BLOCK), 0)
    for grp in range(N_SEL // 8):
        @pl.when(8 * grp <= 2 * qb + 1)
        def _():
            for v in range(N_SEL // 8):
                sv = sc_s[8 * v:8 * v + 8, :]
                part = jnp.zeros((8, Q_BLOCK), F32)
                for jp in range(8 * grp, 8 * grp + 8):
                    row = sc_s[jp:jp + 1, :]
                    if v > grp:
                        part = part + jnp.where(row >= sv, 1.0, 0.0)
                    elif v < grp:
                        part = part + jnp.where(row > sv, 1.0, 0.0)
                    else:
                        part = part + jnp.where(sub8 + 8 * v > jp, jnp.where(row >= sv, 1.0, 0.0),
                                                jnp.where(row > sv, 1.0, 0.0))
                rk_s[8 * v:8 * v + 8, :] += part
    bias = jnp.where((rk_s[...] < SEL_TOP) & (j_i < 2 * qb), 0.0, NEG).astype(BF16)
    for r in range(NSA_REP):
        qa_s[128:128 + N_SEL, r * Q_BLOCK:(r + 1) * Q_BLOCK] = bias

    last_sub = SEQ // SEL_SUB - 1

    def scores_into(buf, c):
        k0 = pl.multiple_of(jnp.minimum(c, last_sub) * SEL_SUB, SEL_SUB)
        buf[...] = _dot(ks_ref[pl.ds(k0, SEL_SUB), :], qa_s[...]).astype(BF16)

    def softmax_from(buf, c):
        k0 = pl.multiple_of(c * SEL_SUB, SEL_SUB)
        s = buf[...]
        m = m_s[...]
        m_new = jnp.maximum(m, jnp.max(s, axis=0, keepdims=True).astype(F32))
        m_s[...] = m_new
        pr = jnp.exp2(s - m_new.astype(BF16))
        acc_s[...] = acc_s[...] * jnp.exp2(m - m_new) + _dot(vst_ref[:, pl.ds(k0, SEL_SUB)], pr)

    scores_into(s0_s, 0)

    def sel_step(i, carry):
        c = 2 * i
        scores_into(s1_s, c + 1)
        softmax_from(s0_s, c)
        scores_into(s0_s, c + 2)
        softmax_from(s1_s, c + 1)
        return carry

    n_main = (qb * Q_BLOCK + 2 * SEL_SUB - 1) // (2 * SEL_SUB)
    lax.fori_loop(0, n_main, sel_step, 0)
    inv_s = 1.0 / jnp.maximum(acc_s[NSA_DH:NSA_DH + 1, :], 1e-30)

    gall = gt_ref[...]
    ggrp = jnp.where(is_g0, gall[0:3 * NSA_REP], gall[3 * NSA_REP:3 * NSA_HEADS])
    for p2 in range(NSA_REP // 2):
        halves = []
        for hh in range(2):
            r = 2 * p2 + hh
            sl = slice(r * Q_BLOCK, (r + 1) * Q_BLOCK)
            g_c, g_s, g_w = (ggrp[3 * r + br:3 * r + br + 1, :] for br in range(3))
            halves.append((g_c * inv_c[:, sl]) * acc_c[0:NSA_DH, sl]
                          + (g_s * inv_s[:, sl]) * acc_s[0:NSA_DH, sl]
                          + (g_w * inv_w[:, sl]) * acc_w[0:NSA_DH, sl])
        o_ref[:, p2 * 128:(p2 + 1) * 128] = jnp.concatenate(halves, axis=0).T.astype(BF16)


def _nsa_attn(q_r, kc, vct, ks, vst, kw, vwt, gt, ovt, casts=()):
    per_b = lambda r, c: pl.BlockSpec((None, r, c), lambda b, g, i: (b, 0, 0))
    per_bg = lambda r, c: pl.BlockSpec((None, r, c), lambda b, g, i: (b, g, 0))
    const = lambda a: pl.BlockSpec(a.shape, lambda b, g, i: (0, 0))
    n_steps = BATCH * NSA_KV * N_QB
    cast_specs = []
    for a in casts:
        rows = a.shape[0]
        if rows % (16 * n_steps) == 0:
            cast_specs.append(pl.BlockSpec((rows // n_steps, a.shape[1]),
                                           lambda b, g, i: ((b * NSA_KV + g) * N_QB + i, 0)))
        else:
            cast_specs.append(pl.BlockSpec((rows // N_QB, a.shape[1]),
                                           lambda b, g, i: (jnp.where(b + g == 0, i, N_QB - 1), 0)))
    return pl.pallas_call(
        functools.partial(_nsa_body, n_casts=len(casts)),
        grid=(BATCH, NSA_KV, N_QB),
        in_specs=[pl.BlockSpec((None, NSA_REP * NSA_DH, Q_BLOCK), lambda b, g, i: (b, g, i)),
                  per_b(N_CMP_PAD, 128), per_bg(NSA_VROWS, N_CMP_PAD),
                  per_b(SEQ, 256), per_bg(NSA_VROWS, SEQ),
                  per_b(SEQ, 128), per_bg(NSA_VROWS, SEQ),
                  pl.BlockSpec((None, 3 * NSA_HEADS, Q_BLOCK), lambda b, g, i: (b, 0, i)),
                  const(ovt)] + cast_specs,
        out_specs=tuple([pl.BlockSpec((None, Q_BLOCK, NSA_REP * NSA_DH), lambda b, g, i: (b, i, g))]
                        + cast_specs),
        out_shape=tuple([jax.ShapeDtypeStruct((BATCH, SEQ, NSA_WIDTH), BF16)]
                        + [jax.ShapeDtypeStruct(a.shape, BF16) for a in casts]),
        scratch_shapes=[pltpu.VMEM((256, NSA_NL), BF16),
                        pltpu.VMEM((N_SEL, Q_BLOCK), F32), pltpu.VMEM((N_SEL, Q_BLOCK), F32),
                        pltpu.VMEM((SEL_SUB, NSA_NL), BF16), pltpu.VMEM((SEL_SUB, NSA_NL), BF16),
                        pltpu.VMEM((N_CMP_PAD, NSA_NL), BF16), pltpu.VMEM((WIN_KEYS, NSA_NL), BF16),
                        pltpu.VMEM((Q_BLOCK, NSA_NL), BF16),
                        pltpu.VMEM((1, NSA_NL), F32), pltpu.VMEM((NSA_VROWS, NSA_NL), F32)],
        compiler_params=_cparams(*(("arbitrary",) * 3 if casts else ("parallel", "parallel", "arbitrary"))),
        name="nsa_attn",
    )(q_r, kc, vct, ks, vst, kw, vwt, gt, ovt, *casts)


OUT_TM = 512


def _outx_body(x_ref, oh_ref, on_ref, w_ref, nw_ref, wq_ref, k_ref, v_ref, wo_ref, o_ref):
    y = (x_ref[...] + _dot(oh_ref[...], w_ref[0:HG_WIDTH, :])
         + _dot(on_ref[...], w_ref[HG_WIDTH:HG_WIDTH + NSA_WIDTH, :]))
    hx = _rms(y, nw_ref[...]).astype(BF16)
    q = (_dot(hx, wq_ref[...]) * (X_DH ** -0.5)).astype(BF16)
    heads = []
    for h in range(X_HEADS):
        sl = slice(h * X_DH, (h + 1) * X_DH)
        s = _dot_nt(q[:, sl], k_ref[:, sl])
        e = jnp.exp(s - jnp.max(s, axis=-1, keepdims=True))
        p = e / jnp.sum(e, axis=-1, keepdims=True)
        heads.append(_dot(p.astype(BF16), v_ref[:, sl]))
    o_ref[...] = y + _dot(jnp.concatenate(heads, axis=1).astype(BF16), wo_ref[...])


def _outproj_xattn(x1, o_hg, o_nsa, w_out, nw, wq, k, v, wo):
    width = X_HEADS * X_DH
    tiles_per_b = SEQ // OUT_TM
    row = lambda w: pl.BlockSpec((OUT_TM, w), lambda i: (i, 0))
    const = lambda r, c: pl.BlockSpec((r, c), lambda i: (0, 0))
    kv = pl.BlockSpec((None, MEM_LEN, width), lambda i: (i // tiles_per_b, 0, 0))
    return pl.pallas_call(
        _outx_body,
        grid=(TOKENS // OUT_TM,),
        in_specs=[row(D_MODEL), row(HG_WIDTH), row(NSA_WIDTH), const(D_MODEL, D_MODEL), const(1, D_MODEL),
                  const(D_MODEL, width), kv, kv, const(width, D_MODEL)],
        out_specs=row(D_MODEL),
        out_shape=jax.ShapeDtypeStruct((TOKENS, D_MODEL), F32),
        compiler_params=_cparams("parallel"),
        name="out_proj_xattn",
    )(x1, o_hg, o_nsa, w_out, nw, wq, k, v, wo)


def _memkv_body(m_ref, nw_ref, wk_ref, wv_ref, k_ref, v_ref):
    hm = _rms(m_ref[...], nw_ref[...]).astype(BF16)
    k_ref[...] = _dot(hm, wk_ref[...]).astype(BF16)
    v_ref[...] = _dot(hm, wv_ref[...]).astype(BF16)


def _memkv(mem, nw, wk, wv):
    width = X_HEADS * X_DH
    wspec = pl.BlockSpec((D_MODEL, width), lambda b: (0, 0))
    ospec = pl.BlockSpec((None, MEM_LEN, width), lambda b: (b, 0, 0))
    osh = jax.ShapeDtypeStruct((BATCH, MEM_LEN, width), BF16)
    return pl.pallas_call(
        _memkv_body,
        grid=(BATCH,),
        in_specs=[pl.BlockSpec((None, MEM_LEN, D_MODEL), lambda b: (b, 0, 0)),
                  pl.BlockSpec((1, D_MODEL), lambda b: (0, 0)), wspec, wspec],
        out_specs=(ospec, ospec), out_shape=(osh, osh),
        compiler_params=_cparams("parallel"),
        name="xattn_memkv",
    )(mem, nw, wk, wv)


def _overlap_t():
    c0 = np.arange(N_CMP)[:, None] * CMP_STRIDE
    s0 = np.arange(N_SEL)[None, :] * SEL_LEN
    ov = np.clip(np.minimum(c0 + CMP_LEN, s0 + SEL_LEN) - np.maximum(c0, s0), 0, None) / CMP_LEN
    out = np.zeros((N_SEL, N_CMP_PAD), np.float32)
    out[:, :N_CMP] = ov.T
    return out


def kernel(x, mem, positions, ffn1_norm, ffn1_w_gate, ffn1_w_up, ffn1_w_down, mix_norm, w_in, hgrn_lb_logits, hgrn_out_norm, nsa_cmp_pe, nsa_cmp_k_w1, nsa_cmp_k_w2, nsa_cmp_v_w1, nsa_cmp_v_w2, w_out, xattn_norm, mem_norm, xattn_wq, xattn_wk, xattn_wv, xattn_wo, ffn2_norm, ffn2_w_gate, ffn2_w_up, ffn2_w_down, final_norm):
    bf = lambda a: a.astype(BF16)
    vec = lambda a: a.reshape(1, -1).astype(F32)
    x2d = x.reshape(TOKENS, D_MODEL)

    x1_head, h_head, w1_gate, w1_up, w1_down = _ffn(
        x2d, vec(ffn1_norm[0]), ffn1_w_gate[0], ffn1_w_up[0], ffn1_w_down[0], vec(mix_norm[0]),
        final=False, n_tiles=1, tm=FFN_HEAD_TM, tf=FFN_HEAD_TF)
    x1, h_mix = _ffn(x2d, vec(ffn1_norm[0]), w1_gate, w1_up, w1_down, vec(mix_norm[0]), final=False,
                     head=(x1_head, h_head))

    w_t = w_in[0].T
    w_tail = jnp.pad(w_t[PROJ_FULL_TILES * PROJ_TN:], ((0, PROJ_W - D_IN), (0, 0)))
    proj, proj_f = _proj(h_mix, w_t, w_tail)

    o_hg = _hgrn(proj, proj_f, hgrn_lb_logits.astype(F32), vec(hgrn_out_norm[0]))

    inv = ROPE_THETA ** (-jnp.arange(NSA_DH // 2, dtype=F32) / (NSA_DH // 2))
    inv128 = jnp.tile(inv, 128 // (NSA_DH // 2)).reshape(1, 128)
    pos4 = jnp.repeat(positions.astype(F32).reshape(BATCH, SEQ // 4, 4), NSA_DH // 2, axis=-1)
    q_r, kc_tok, vc_tok, ks, kw, vst, vwt, gt = _nsa_prep(proj, pos4, inv128)

    def over_groups(w):
        z = jnp.zeros_like(w)
        return bf(jnp.concatenate([jnp.concatenate([w, z], axis=-1), jnp.concatenate([z, w], axis=-1)], axis=-2))

    per_pos = lambda w1: over_groups(w1.reshape(CMP_LEN, NSA_DH, CMP_HIDDEN))
    pe = nsa_cmp_pe[0].astype(F32)
    kc, vct = _compress(kc_tok, vc_tok, jnp.concatenate([pe, pe], axis=1),
                        per_pos(nsa_cmp_k_w1[0]), over_groups(nsa_cmp_k_w2[0]),
                        per_pos(nsa_cmp_v_w1[0]), over_groups(nsa_cmp_v_w2[0]))
    o_nsa, w2_gate, w2_up, w2_down, w_out_b, wq_b, wk_b, wv_b, wo_b = _nsa_attn(
        q_r, kc, vct, ks, vst, kw, vwt, gt, jnp.asarray(_overlap_t(), dtype=BF16),
        casts=(ffn2_w_gate[0], ffn2_w_up[0], ffn2_w_down[0],
               w_out[0], xattn_wq[0], xattn_wk[0], xattn_wv[0], xattn_wo[0]))

    km, vm = _memkv(mem, vec(mem_norm[0]), wk_b, wv_b)
    x3 = _outproj_xattn(x1, o_hg.reshape(TOKENS, HG_WIDTH), o_nsa.reshape(TOKENS, NSA_WIDTH),
                        w_out_b, vec(xattn_norm[0]), wq_b, km, vm, wo_b)

    (out,) = _ffn(x3, vec(ffn2_norm[0]), w2_gate, w2_up, w2_down, vec(final_norm), final=True)
    return out.reshape(BATCH, SEQ, D_MODEL)
```

```python
import functools

import numpy as np
import jax
import jax.numpy as jnp
from jax import lax
from jax.experimental import pallas as pl
from jax.experimental.pallas import tpu as pltpu

F32 = jnp.float32
BF16 = jnp.bfloat16

D_MODEL = 2048
BATCH = 2
SEQ = 4096
TOKENS = BATCH * SEQ
RMS_EPS = 1e-6
ROPE_THETA = 10000.0
HG_WIDTH = 1024
HG_HEADS = 8
HG_D = 128
HG_CHUNK = 128
HG_LEVELS = (64, 32, 16, 8, 4, 2, 1)
NSA_WIDTH = 1024
NSA_DH = 64
NSA_HEADS = 16
NSA_KV = 2
NSA_REP = 8
NSA_VROWS = NSA_DH + 16
CMP_LEN = 32
CMP_STRIDE = 16
CMP_HIDDEN = 256
N_CMP = (SEQ - CMP_LEN) // CMP_STRIDE + 1
N_CMP_PAD = 256
SEL_LEN = 64
N_SEL = SEQ // SEL_LEN
SEL_TOP = 16
WINDOW = 512
Q_BLOCK = 128
N_QB = SEQ // Q_BLOCK
SEL_SUB = 256
WIN_KEYS = WINDOW + Q_BLOCK
MEM_LEN = 256
X_HEADS = 4
X_DH = 128
D_FF = 5632
IN_SIZES = (1024, 1024, 1024, 1024, 1024, 128, 128, 128, 128, 128, 128, 48)
D_IN = sum(IN_SIZES)
PROJ_NSA_OFF = sum(IN_SIZES[:4])
NSA_PROJ = 2048
PROJ_W = PROJ_NSA_OFF + NSA_PROJ
NEG = -1e30
LOG2E = 1.4426950408889634

V7X_VMEM_BYTES = 64 * 1024 * 1024
VMEM_LIMIT = V7X_VMEM_BYTES - 8 * 1024 * 1024


def _cparams(*sem, flags=None):
    return pltpu.CompilerParams(dimension_semantics=sem, vmem_limit_bytes=VMEM_LIMIT, flags=flags)


def _rms(x, w):
    return x * lax.rsqrt(jnp.mean(x * x, axis=-1, keepdims=True) + RMS_EPS) * w


def _silu(x):
    return x * jax.nn.sigmoid(x)


def _dot(a, b):
    return jnp.dot(a, b, preferred_element_type=F32)


def _dot_f32_by_01(sel, x):
    hi = x.astype(BF16)
    r1 = x - hi.astype(F32)
    mid = r1.astype(BF16)
    lo = (r1 - mid.astype(F32)).astype(BF16)
    n = x.shape[1]
    y = _dot(sel, jnp.concatenate([hi, mid, lo], axis=1))
    return y[:, 0:n] + y[:, n:2 * n] + y[:, 2 * n:3 * n]


def _dot_nt(a, b):
    return lax.dot_general(a, b, (((1,), (1,)), ((), ())), preferred_element_type=F32)


FFN_TM = 512
FFN_TF = 512
FFN_HEAD_TM = 1024
FFN_HEAD_TF = 256


def _ffn_body(x_ref, nw_ref, wg_ref, wu_ref, wd_ref, nw2_ref, *rest,
              final, f32_weights):
    if final:
        o_ref, rest = rest[0], rest[1:]
    else:
        (o_ref, hn_ref), rest = rest[:2], rest[2:]
    if f32_weights:
        w_copy, rest = rest[:3], rest[3:]
    (h_scr,) = rest
    j = pl.program_id(1)
    last = pl.num_programs(1) - 1

    def swiglu_tile(h):
        wg, wu, wd = wg_ref[...], wu_ref[...], wd_ref[...]
        if f32_weights:
            wg, wu, wd = wg.astype(BF16), wu.astype(BF16), wd.astype(BF16)
            for dst, w in zip(w_copy, (wg, wu, wd)):
                dst[...] = w
        g = _dot(h, wg)
        u = _dot(h, wu)
        return _dot((_silu(g) * u).astype(BF16), wd)

    @pl.when(j == 0)
    def _():
        h = _rms(x_ref[...], nw_ref[...]).astype(BF16)
        h_scr[...] = h
        o_ref[...] = swiglu_tile(h)

    @pl.when((j > 0) & (j < last))
    def _():
        o_ref[...] += swiglu_tile(h_scr[...])

    @pl.when(j == last)
    def _():
        y = x_ref[...] + 0.5 * (o_ref[...] + swiglu_tile(h_scr[...]))
        if final:
            o_ref[...] = _rms(y, nw2_ref[...])
        else:
            o_ref[...] = y
            hn_ref[...] = _rms(y, nw2_ref[...]).astype(BF16)


def _ffn(x, nw, wg, wu, wd, nw2, final, first_tile=0, n_tiles=None, tm=FFN_TM, tf=FFN_TF):
    ni, nj = (n_tiles or TOKENS // tm), D_FF // tf
    f32_weights = wg.dtype == F32
    once = dict(pipeline_mode=pl.Buffered(1)) if ni == 1 else {}
    row_in = pl.BlockSpec((tm, D_MODEL), lambda i, j: (i + first_tile, 0), **once)
    row = pl.BlockSpec((tm, D_MODEL), lambda i, j: (i, 0), **once)
    vec = pl.BlockSpec((1, D_MODEL), lambda i, j: (0, 0))
    in_specs = [row_in, vec,
                pl.BlockSpec((D_MODEL, tf), lambda i, j: (0, j)),
                pl.BlockSpec((D_MODEL, tf), lambda i, j: (0, j)),
                pl.BlockSpec((tf, D_MODEL), lambda i, j: (j, 0)),
                vec]
    main_shapes = [jax.ShapeDtypeStruct((ni * tm, D_MODEL), F32)]
    if not final:
        main_shapes.append(jax.ShapeDtypeStruct((ni * tm, D_MODEL), BF16))
    w_copy_specs = in_specs[2:5] if f32_weights else []
    w_copy_shapes = [jax.ShapeDtypeStruct(w.shape, BF16) for w in (wg, wu, wd)] if f32_weights else []
    return pl.pallas_call(
        functools.partial(_ffn_body, final=final, f32_weights=f32_weights),
        grid=(ni, nj), in_specs=in_specs,
        out_specs=tuple([row] * len(main_shapes) + w_copy_specs),
        out_shape=tuple(main_shapes + w_copy_shapes),
        scratch_shapes=[pltpu.VMEM((tm, D_MODEL), BF16)],
        compiler_params=_cparams("parallel", "arbitrary"),
        name="ffn_final" if final else ("ffn_head" if f32_weights else "ffn"),
    )(x, nw, wg, wu, wd, nw2)


PROJ_TM = 1024
PROJ_TN = 512


PROJ_F_TILE0 = IN_SIZES[0] // PROJ_TN
PROJ_F_TILES = IN_SIZES[1] // PROJ_TN


PROJ_FULL_TILES = D_IN // PROJ_TN


def _proj_body(a0_ref, a_ref, wt_ref, tail_ref, o_ref, f_ref, *, n0):
    i = pl.program_id(0)
    j = pl.program_id(1)

    def tile(a):
        w = jnp.where(j >= PROJ_FULL_TILES, tail_ref[...], wt_ref[...]).astype(BF16)
        y = _dot_nt(a, w)
        o_ref[...] = y.astype(BF16)

        @pl.when((j >= PROJ_F_TILE0) & (j < PROJ_F_TILE0 + PROJ_F_TILES))
        def _():
            f_ref[...] = y

    pl.when(i < n0)(lambda: tile(a0_ref[...]))
    pl.when(i >= n0)(lambda: tile(a_ref[...]))


def _proj(a0, a, wt, wt_tail):
    k = a.shape[1]
    n0 = a0.shape[0] // PROJ_TM
    assert a0.shape[0] == n0 * PROJ_TM and a.shape[0] % PROJ_TM == 0
    m = a0.shape[0] + a.shape[0]
    f_tile = lambda i, j: (i, jnp.clip(j - PROJ_F_TILE0, 0, PROJ_F_TILES - 1))
    return pl.pallas_call(
        functools.partial(_proj_body, n0=n0),
        grid=(m // PROJ_TM, PROJ_W // PROJ_TN),
        in_specs=[pl.BlockSpec((PROJ_TM, k), lambda i, j: (jnp.minimum(i, n0 - 1), 0)),
                  pl.BlockSpec((PROJ_TM, k), lambda i, j: (jnp.maximum(i - n0, 0), 0)),
                  pl.BlockSpec((PROJ_TN, k), lambda i, j: (jnp.minimum(j, PROJ_FULL_TILES - 1), 0)),
                  pl.BlockSpec((PROJ_TN, k), lambda i, j: (0, 0))],
        out_specs=(pl.BlockSpec((PROJ_TM, PROJ_TN), lambda i, j: (i, j)),
                   pl.BlockSpec((PROJ_TM, PROJ_TN), f_tile)),
        out_shape=(jax.ShapeDtypeStruct((m, PROJ_W), BF16),
                   jax.ShapeDtypeStruct((m, IN_SIZES[1]), F32)),
        compiler_params=_cparams("parallel", "arbitrary"),
        name="proj_in",
    )(a0, a, wt, wt_tail)


HG_ROWS = 1024
HG_CUM = 256


def _hgrn_body(q_ref, f_ref, i_ref, g_ref, lbl_ref, nw_ref, o_ref, st_ref, k_s, b_s):
    c = pl.program_id(2)

    @pl.when(c == 0)
    def _():
        st_ref[...] = jnp.zeros_like(st_ref)

    l0 = lbl_ref[0:1, :]
    l1 = lbl_ref[1:2, :]
    lmax = jnp.maximum(l0, l1)
    e0 = jnp.exp(l0 - lmax)
    lb = e0 / (e0 + jnp.exp(l1 - lmax))

    C = HG_CHUNK
    f = lb + (1.0 - lb) * jax.nn.sigmoid(f_ref[...])
    k_s[...] = 1.0 - f
    r_i = lax.broadcasted_iota(jnp.int32, (HG_CUM, HG_CUM), 0)
    c_i = lax.broadcasted_iota(jnp.int32, (HG_CUM, HG_CUM), 1)
    tri = jnp.where((r_i >= c_i) & (r_i // C == c_i // C), 1.0, 0.0).astype(BF16)
    logf = jnp.log2(f)
    for r0 in range(0, HG_ROWS, HG_CUM):
        b_s[r0:r0 + HG_CUM, :] = _dot_f32_by_01(tri, logf[r0:r0 + HG_CUM])

    t_i = lax.broadcasted_iota(jnp.int32, (C, C), 0)
    s_i = lax.broadcasted_iota(jnp.int32, (C, C), 1)
    level_mask = [(t_i // (2 * w) == s_i // (2 * w)) & (t_i % (2 * w) >= w) & (s_i % (2 * w) < w)
                  for w in HG_LEVELS]
    sub_r = lax.broadcasted_iota(jnp.int32, (8, HG_D), 0)
    row_i = lax.broadcasted_iota(jnp.int32, (C, HG_D), 0)
    right_sign = {w: jnp.where(row_i % (2 * w) >= w, 1.0, -1.0) for w in HG_LEVELS if w < 8}

    def neg_abs_diff(w, r0, b):
        row = lambda r, n: jnp.broadcast_to(b_s[r0 + r:r0 + r + 1, :], (n, HG_D))
        if w >= 8:
            parts = []
            for p0 in range(0, C, 2 * w):
                ref = row(p0 + w - 1, w)
                parts += [ref - b[p0:p0 + w], b[p0 + w:p0 + 2 * w] - ref]
            return jnp.concatenate(parts, axis=0)
        if w == 4:
            bref = jnp.concatenate([row(p0 + 3, 8) for p0 in range(0, C, 8)], axis=0)
        elif w == 2:
            bref = jnp.concatenate([jnp.where(sub_r < 4, row(p0 + 1, 8), row(p0 + 5, 8))
                                    for p0 in range(0, C, 8)], axis=0)
        else:
            bref = jnp.where(row_i % 2 == 1, pltpu.roll(b, 1, 0), b)
        return (b - bref) * right_sign[w]

    chunks = [ci * C for ci in range(HG_ROWS // C)]
    rows = lambda ref, r0: ref[r0:r0 + C, :]
    att = [jnp.zeros((C, C), F32) for _ in chunks]
    q16 = [rows(q_ref, r0) for r0 in chunks]
    k16 = [rows(k_s, r0).astype(BF16) for r0 in chunks]
    for w, mask in zip(HG_LEVELS, level_mask):
        for n, r0 in enumerate(chunks):
            b = rows(b_s, r0)
            e = jnp.exp2(neg_abs_diff(w, r0, b)).astype(BF16)
            att[n] = jnp.where(mask, _dot_nt(q16[n] * e, k16[n] * e), att[n])
    o_intra = []
    for n, r0 in enumerate(chunks):
        q, k, v = q16[n].astype(F32), rows(k_s, r0), rows(i_ref, r0)
        o_intra.append(_dot(att[n].astype(BF16), v)
                       + jnp.sum(q * k, axis=-1, keepdims=True) * v.astype(F32))
    upd = []
    for r0 in chunks:
        bl = b_s[r0 + C - 1:r0 + C, :]
        kd = rows(k_s, r0) * jnp.exp2(bl - rows(b_s, r0))
        v_t = rows(i_ref, r0).astype(F32).T.astype(BF16)
        upd.append((jnp.exp2(bl), _dot(v_t, kd.astype(BF16))))
    st_t = st_ref[...]
    for n, r0 in enumerate(chunks):
        qe = (q16[n].astype(F32) * jnp.exp2(rows(b_s, r0))).astype(BF16)
        o = o_intra[n] + _dot_nt(qe, st_t.astype(BF16))
        st_t = st_t * upd[n][0] + upd[n][1]
        o = o * lax.rsqrt(jnp.mean(o * o, axis=-1, keepdims=True) + RMS_EPS)
        o_ref[r0:r0 + C, :] = (o * nw_ref[...] * _silu(rows(g_ref, r0).astype(F32))).astype(BF16)
    st_ref[...] = st_t


def _hgrn(proj, proj_f, lb_logits, norm_w):
    p3 = proj.reshape(BATCH, SEQ, PROJ_W)
    f3 = proj_f.reshape(BATCH, SEQ, HG_WIDTH)

    def col(off):
        return pl.BlockSpec((None, HG_ROWS, HG_D), lambda b, h, c: (b, c, off + h))

    return pl.pallas_call(
        _hgrn_body,
        grid=(BATCH, HG_HEADS, SEQ // HG_ROWS),
        in_specs=[col(0), col(0), col(2 * HG_HEADS), col(3 * HG_HEADS),
                  pl.BlockSpec((2, HG_D), lambda b, h, c: (0, h)),
                  pl.BlockSpec((1, HG_D), lambda b, h, c: (0, h))],
        out_specs=pl.BlockSpec((None, HG_ROWS, HG_D), lambda b, h, c: (b, c, h)),
        out_shape=jax.ShapeDtypeStruct((BATCH, SEQ, HG_WIDTH), BF16),
        scratch_shapes=[pltpu.VMEM((HG_D, HG_D), F32),
                        pltpu.VMEM((HG_ROWS, HG_D), F32),
                        pltpu.VMEM((HG_ROWS, HG_D), F32)],
        compiler_params=_cparams("parallel", "parallel", "arbitrary"),
        name="hgrn2",
    )(p3, f3, p3, p3, lb_logits, norm_w)


PREP_TM = 256


def _prep_body(p_ref, pos4_ref, inv_ref, q_ref, kc_ref, vc_ref, ks_ref, kw_ref,
               vst_ref, vwt_ref, gt_ref, cos_s, sin_s):
    n_freq = NSA_DH // 2
    ang4 = pos4_ref[...] * inv_ref[...]
    seg = lax.broadcasted_iota(jnp.int32, (PREP_TM // 4, 128), 1) // n_freq
    for table, dst in ((jnp.cos(ang4), cos_s), (jnp.sin(ang4), sin_s)):
        for u in range(4):
            one = jnp.where(seg == u, table, 0.0)
            full = one
            for k in range(1, 4):
                full = full + pltpu.roll(one, k * n_freq, 1)
            dst[pl.ds(u, PREP_TM // 4, stride=4), :] = full
    cos = cos_s[...]
    sin = sin_s[...]
    lane = lax.broadcasted_iota(jnp.int32, (PREP_TM, 128), 1)
    lo = (lane & (NSA_DH // 2)) == 0
    sin_signed = jnp.where(lo, -sin, sin)

    def rope(x):
        rot = jnp.where(lo, pltpu.roll(x, 128 - NSA_DH // 2, 1), pltpu.roll(x, NSA_DH // 2, 1))
        return x * cos + rot * sin_signed

    cols = lambda c0: p_ref[:, c0:c0 + 128].astype(F32)
    scale = NSA_DH ** -0.5 * LOG2E
    for cblk in range(NSA_WIDTH // 128):
        q_ref[cblk * 128:(cblk + 1) * 128, :] = (rope(cols(cblk * 128)) * scale).T.astype(BF16)
    c_kc, c_vc, c_ks, c_vs, c_kw, c_vw, c_gate = (int(c) for c in np.cumsum(IN_SIZES[4:11]))
    kc_ref[...] = rope(cols(c_kc))
    vc_ref[...] = cols(c_vc)
    ks_ref[:, 0:128] = rope(cols(c_ks)).astype(BF16)
    blk = (pl.program_id(1) * PREP_TM + lax.broadcasted_iota(jnp.int32, (PREP_TM, 128), 0)) // SEL_LEN
    ks_ref[:, 128:256] = jnp.where(lane == blk, 1.0, 0.0).astype(BF16)
    kw_ref[...] = rope(cols(c_kw)).astype(BF16)
    ones = jnp.ones((NSA_VROWS - NSA_DH, PREP_TM), BF16)
    for v_ref, c0 in ((vst_ref, c_vs), (vwt_ref, c_vw)):
        vt = cols(c0).T.astype(BF16)
        for g in range(NSA_KV):
            v_ref[g * NSA_VROWS:g * NSA_VROWS + NSA_DH, :] = vt[g * NSA_DH:(g + 1) * NSA_DH]
            v_ref[g * NSA_VROWS + NSA_DH:(g + 1) * NSA_VROWS, :] = ones
    gt_ref[...] = jax.nn.sigmoid(cols(c_gate)).T[0:3 * NSA_HEADS, :]


def _nsa_prep(proj, pos4, inv128):
    nt = SEQ // PREP_TM
    p3 = proj.reshape(BATCH, SEQ, PROJ_W)
    nat = lambda w: pl.BlockSpec((None, PREP_TM, w), lambda b, i: (b, i, 0))
    tr = lambda r: pl.BlockSpec((None, r, PREP_TM), lambda b, i: (b, 0, i))
    sds = jax.ShapeDtypeStruct
    return pl.pallas_call(
        _prep_body,
        grid=(BATCH, nt),
        in_specs=[pl.BlockSpec((None, PREP_TM, NSA_PROJ), lambda b, i: (b, i, PROJ_NSA_OFF // NSA_PROJ)),
                  pl.BlockSpec((None, PREP_TM // 4, 128), lambda b, i: (b, i, 0)),
                  pl.BlockSpec((1, 128), lambda b, i: (0, 0))],
        out_specs=(tr(NSA_WIDTH), nat(128), nat(128), nat(256), nat(128),
                   tr(NSA_KV * NSA_VROWS), tr(NSA_KV * NSA_VROWS), tr(3 * NSA_HEADS)),
        out_shape=(sds((BATCH, NSA_WIDTH, SEQ), BF16),
                   sds((BATCH, SEQ, 128), F32),
                   sds((BATCH, SEQ, 128), F32),
                   sds((BATCH, SEQ, 256), BF16),
                   sds((BATCH, SEQ, 128), BF16),
                   sds((BATCH, NSA_KV * NSA_VROWS, SEQ), BF16),
                   sds((BATCH, NSA_KV * NSA_VROWS, SEQ), BF16),
                   sds((BATCH, 3 * NSA_HEADS, SEQ), F32)),
        scratch_shapes=[pltpu.VMEM((PREP_TM, 128), F32)] * 2,
        compiler_params=_cparams("parallel", "parallel"),
        name="nsa_prep",
    )(p3, pos4, inv128)


def _cmp_body(tk_ref, tv_ref, pe_ref, kw1_ref, kw2_ref, vw1_ref, vw2_ref, kc_ref, vct_ref, y1_s, y2_s):
    row = lax.broadcasted_iota(jnp.int32, (N_CMP_PAD, NSA_KV * CMP_HIDDEN), 0)

    def mlp(t_ref, w1_ref, w2_ref):
        y1_s[...] = jnp.zeros_like(y1_s)
        y2_s[...] = jnp.zeros_like(y2_s)
        for l in range(CMP_STRIDE):
            x = t_ref[pl.ds(l, N_CMP_PAD, stride=CMP_STRIDE), :]
            y1_s[...] += _dot((x + pe_ref[l:l + 1, :]).astype(BF16), w1_ref[l])
            y2_s[...] += _dot((x + pe_ref[CMP_STRIDE + l:CMP_STRIDE + l + 1, :]).astype(BF16),
                              w1_ref[CMP_STRIDE + l])
        hid = jnp.where(row < N_CMP, y1_s[...] + pltpu.roll(y2_s[...], N_CMP_PAD - 1, 0), 0.0)
        return _dot(_silu(hid).astype(BF16), w2_ref[...])

    kc_ref[...] = mlp(tk_ref, kw1_ref, kw2_ref).astype(BF16)
    y1_s[:, 0:128] = mlp(tv_ref, vw1_ref, vw2_ref)
    vt = y1_s[:, 0:128].T.astype(BF16)
    ones = jnp.ones((NSA_VROWS - NSA_DH, N_CMP_PAD), BF16)
    for g in range(NSA_KV):
        vct_ref[g * NSA_VROWS:g * NSA_VROWS + NSA_DH, :] = vt[g * NSA_DH:(g + 1) * NSA_DH]
        vct_ref[g * NSA_VROWS + NSA_DH:(g + 1) * NSA_VROWS, :] = ones


def _compress(tk, tv, pe2, kw1, kw2, vw1, vw2):
    seg = pl.BlockSpec((None, SEQ, 128), lambda b: (b, 0, 0))
    full2 = lambda a: pl.BlockSpec(a.shape, lambda b: (0,) * a.ndim)
    return pl.pallas_call(
        _cmp_body,
        grid=(BATCH,),
        in_specs=[seg, seg, full2(pe2), full2(kw1), full2(kw2), full2(vw1), full2(vw2)],
        out_specs=(pl.BlockSpec((None, N_CMP_PAD, 128), lambda b: (b, 0, 0)),
                   pl.BlockSpec((None, NSA_KV * NSA_VROWS, N_CMP_PAD), lambda b: (b, 0, 0))),
        out_shape=(jax.ShapeDtypeStruct((BATCH, N_CMP_PAD, 128), BF16),
                   jax.ShapeDtypeStruct((BATCH, NSA_KV * NSA_VROWS, N_CMP_PAD), BF16)),
        scratch_shapes=[pltpu.VMEM((N_CMP_PAD, NSA_KV * CMP_HIDDEN), F32)] * 2,
        compiler_params=_cparams("parallel"),
        name="nsa_compress",
    )(tk, tv, pe2, kw1, kw2, vw1, vw2)


NSA_NL = NSA_REP * Q_BLOCK


def _nsa_body(q_ref, kc_ref, vct_ref, ks_ref, vst_ref, kw_ref, vwt_ref, gt_ref, ovt_ref, *rest, n_casts):
    cast_in, o_ref, cast_out = rest[:n_casts], rest[n_casts], rest[n_casts + 1:2 * n_casts + 1]
    qa_s, sc_s, rk_s, s0_s, s1_s, sc_buf, sw_buf, sd_buf, m_s, acc_s = rest[2 * n_casts + 1:]
    g = pl.program_id(1)
    qb = pl.program_id(2)
    q0 = pl.multiple_of(qb * Q_BLOCK, Q_BLOCK)
    is_g0 = g == 0

    zero_slab = jnp.zeros((NSA_DH, Q_BLOCK), BF16)
    for r in range(NSA_REP):
        s = q_ref[r * NSA_DH:(r + 1) * NSA_DH, :]
        qa_s[0:NSA_DH, r * Q_BLOCK:(r + 1) * Q_BLOCK] = jnp.where(is_g0, s, zero_slab)
        qa_s[NSA_DH:2 * NSA_DH, r * Q_BLOCK:(r + 1) * Q_BLOCK] = jnp.where(is_g0, zero_slab, s)
    qa_s[128 + N_SEL:256, :] = jnp.zeros((128 - N_SEL, NSA_NL), BF16)

    tq = q0 + lax.broadcasted_iota(jnp.int32, (1, Q_BLOCK), 1)

    def mask_pair(s, valid):
        return jnp.concatenate([jnp.where(valid, s[:, r * Q_BLOCK:(r + 1) * Q_BLOCK], NEG)
                                for r in range(NSA_REP)], axis=1)

    n_i = lax.broadcasted_iota(jnp.int32, (N_CMP_PAD, Q_BLOCK), 0)
    valid_c = (n_i * CMP_STRIDE + (CMP_LEN - 1) <= tq) & (n_i < N_CMP)
    sc_buf[...] = mask_pair(_dot(kc_ref[...], qa_s[0:128, :]).astype(BF16), valid_c)

    w0 = pl.multiple_of(jnp.maximum(q0 - WINDOW, 0), Q_BLOCK)
    dpos = tq - (w0 + lax.broadcasted_iota(jnp.int32, (WIN_KEYS, Q_BLOCK), 0))
    sw_buf[...] = mask_pair(_dot(kw_ref[pl.ds(w0, WIN_KEYS), :], qa_s[0:128, :]).astype(BF16),
                            (dpos >= 0) & (dpos < WINDOW))

    for src, dst in zip(cast_in, cast_out):
        dst[...] = src[...].astype(BF16)

    sc = sc_buf[...]
    m_c = jnp.max(sc, axis=0, keepdims=True)
    e_c = jnp.exp2(sc - m_c)
    acc_c = _dot(vct_ref[...], e_c)
    inv_c = jnp.where(m_c.astype(F32) > 0.5 * NEG, 1.0 / jnp.maximum(acc_c[NSA_DH:NSA_DH + 1], 1e-30), 0.0)
    imp_h = _dot(ovt_ref[...], e_c) * inv_c

    d_i = lax.broadcasted_iota(jnp.int32, (Q_BLOCK, Q_BLOCK), 0)
    t_i = lax.broadcasted_iota(jnp.int32, (Q_BLOCK, Q_BLOCK), 1)
    sd_buf[...] = mask_pair(_dot(ks_ref[pl.ds(q0, Q_BLOCK), 0:128], qa_s[0:128, :]).astype(BF16),
                            d_i <= t_i)

    sw = sw_buf[...]
    acc_w = _dot(vwt_ref[:, pl.ds(w0, WIN_KEYS)], jnp.exp2(sw - jnp.max(sw, axis=0, keepdims=True)))
    inv_w = 1.0 / jnp.maximum(acc_w[NSA_DH:NSA_DH + 1], 1e-30)

    s = sd_buf[...]
    m16 = jnp.max(s, axis=0, keepdims=True)
    m_s[...] = m16.astype(F32)
    acc_s[...] = _dot(vst_ref[:, pl.ds(q0, Q_BLOCK)], jnp.exp2(s - m16))

    imp = imp_h[:, 0:Q_BLOCK]
    for r in range(1, NSA_REP):
        imp = imp + imp_h[:, r * Q_BLOCK:(r + 1) * Q_BLOCK]
    j_i = lax.broadcasted_iota(jnp.int32, (N_SEL, Q_BLOCK), 0)
    cur = tq // SEL_LEN
    forced = (j_i == 0) | (j_i == cur) | (j_i == cur - 1)
    score = jnp.where(forced, jnp.inf, jnp.where(j_i > cur, -jnp.inf, imp))
    sc_s[...] = score
    rk_s[...] = jnp.zeros_like(rk_s)
    sub8 = lax.broadcasted_iota(jnp.int32, (8, Q_BLOCK), 0)
    for grp in range(N_SEL // 8):
        @pl.when(8 * grp <= 2 * qb + 1)
        def _():
            for v in range(N_SEL // 8):
                sv = sc_s[8 * v:8 * v + 8, :]
                part = jnp.zeros((8, Q_BLOCK), F32)
                for jp in range(8 * grp, 8 * grp + 8):
                    row = sc_s[jp:jp + 1, :]
                    if v > grp:
                        part = part + jnp.where(row >= sv, 1.0, 0.0)
                    elif v < grp:
                        part = part + jnp.where(row > sv, 1.0, 0.0)
                    else:
                        part = part + jnp.where(sub8 + 8 * v > jp, jnp.where(row >= sv, 1.0, 0.0),
                                                jnp.where(row > sv, 1.0, 0.0))
                rk_s[8 * v:8 * v + 8, :] += part
    bias = jnp.where((rk_s[...] < SEL_TOP) & (j_i < 2 * qb), 0.0, NEG).astype(BF16)
    for r in range(NSA_REP):
        qa_s[128:128 + N_SEL, r * Q_BLOCK:(r + 1) * Q_BLOCK] = bias

    last_sub = SEQ // SEL_SUB - 1

    def scores_into(buf, c):
        k0 = pl.multiple_of(jnp.minimum(c, last_sub) * SEL_SUB, SEL_SUB)
        buf[...] = _dot(ks_ref[pl.ds(k0, SEL_SUB), :], qa_s[...]).astype(BF16)

    def softmax_from(buf, c):
        k0 = pl.multiple_of(c * SEL_SUB, SEL_SUB)
        s = buf[...]
        m = m_s[...]
        m_new = jnp.maximum(m, jnp.max(s, axis=0, keepdims=True).astype(F32))
        m_s[...] = m_new
        pr = jnp.exp2(s - m_new.astype(BF16))
        acc_s[...] = acc_s[...] * jnp.exp2(m - m_new) + _dot(vst_ref[:, pl.ds(k0, SEL_SUB)], pr)

    scores_into(s0_s, 0)

    def sel_step(i, carry):
        c = 2 * i
        scores_into(s1_s, c + 1)
        softmax_from(s0_s, c)
        scores_into(s0_s, c + 2)
        softmax_from(s1_s, c + 1)
        return carry

    n_main = (qb * Q_BLOCK + 2 * SEL_SUB - 1) // (2 * SEL_SUB)
    lax.fori_loop(0, n_main, sel_step, 0)
    inv_s = 1.0 / jnp.maximum(acc_s[NSA_DH:NSA_DH + 1, :], 1e-30)

    gall = gt_ref[...]
    ggrp = jnp.where(is_g0, gall[0:3 * NSA_REP], gall[3 * NSA_REP:3 * NSA_HEADS])
    for p2 in range(NSA_REP // 2):
        halves = []
        for hh in range(2):
            r = 2 * p2 + hh
            sl = slice(r * Q_BLOCK, (r + 1) * Q_BLOCK)
            g_c, g_s, g_w = (ggrp[3 * r + br:3 * r + br + 1, :] for br in range(3))
            halves.append((g_c * inv_c[:, sl]) * acc_c[0:NSA_DH, sl]
                          + (g_s * inv_s[:, sl]) * acc_s[0:NSA_DH, sl]
                          + (g_w * inv_w[:, sl]) * acc_w[0:NSA_DH, sl])
        o_ref[:, p2 * 128:(p2 + 1) * 128] = jnp.concatenate(halves, axis=0).T.astype(BF16)


def _nsa_attn(q_r, kc, vct, ks, vst, kw, vwt, gt, ovt, casts=()):
    per_b = lambda r, c: pl.BlockSpec((None, r, c), lambda b, g, i: (b, 0, 0))
    per_bg = lambda r, c: pl.BlockSpec((None, r, c), lambda b, g, i: (b, g, 0))
    const = lambda a: pl.BlockSpec(a.shape, lambda b, g, i: (0, 0))
    n_steps = BATCH * NSA_KV * N_QB
    cast_specs = []
    for a in casts:
        rows = a.shape[0]
        if rows % (16 * n_steps) == 0:
            cast_specs.append(pl.BlockSpec((rows // n_steps, a.shape[1]),
                                           lambda b, g, i: ((b * NSA_KV + g) * N_QB + i, 0)))
        else:
            cast_specs.append(pl.BlockSpec((rows // N_QB, a.shape[1]),
                                           lambda b, g, i: (jnp.where(b + g == 0, i, N_QB - 1), 0)))
    return pl.pallas_call(
        functools.partial(_nsa_body, n_casts=len(casts)),
        grid=(BATCH, NSA_KV, N_QB),
        in_specs=[pl.BlockSpec((None, NSA_REP * NSA_DH, Q_BLOCK), lambda b, g, i: (b, g, i)),
                  per_b(N_CMP_PAD, 128), per_bg(NSA_VROWS, N_CMP_PAD),
                  per_b(SEQ, 256), per_bg(NSA_VROWS, SEQ),
                  per_b(SEQ, 128), per_bg(NSA_VROWS, SEQ),
                  pl.BlockSpec((None, 3 * NSA_HEADS, Q_BLOCK), lambda b, g, i: (b, 0, i)),
                  const(ovt)] + cast_specs,
        out_specs=tuple([pl.BlockSpec((None, Q_BLOCK, NSA_REP * NSA_DH), lambda b, g, i: (b, i, g))]
                        + cast_specs),
        out_shape=tuple([jax.ShapeDtypeStruct((BATCH, SEQ, NSA_WIDTH), BF16)]
                        + [jax.ShapeDtypeStruct(a.shape, BF16) for a in casts]),
        scratch_shapes=[pltpu.VMEM((256, NSA_NL), BF16),
                        pltpu.VMEM((N_SEL, Q_BLOCK), F32), pltpu.VMEM((N_SEL, Q_BLOCK), F32),
                        pltpu.VMEM((SEL_SUB, NSA_NL), BF16), pltpu.VMEM((SEL_SUB, NSA_NL), BF16),
                        pltpu.VMEM((N_CMP_PAD, NSA_NL), BF16), pltpu.VMEM((WIN_KEYS, NSA_NL), BF16),
                        pltpu.VMEM((Q_BLOCK, NSA_NL), BF16),
                        pltpu.VMEM((1, NSA_NL), F32), pltpu.VMEM((NSA_VROWS, NSA_NL), F32)],
        compiler_params=_cparams(*(("arbitrary",) * 3 if casts else ("parallel", "parallel", "arbitrary"))),
        name="nsa_attn",
    )(q_r, kc, vct, ks, vst, kw, vwt, gt, ovt, *casts)


OUT_TM = 512


def _outx_body(x0_ref, x_ref, oh_ref, on_ref, w_ref, nw_ref, wq_ref, k_ref, v_ref, wo_ref, o_ref, *, n0):
    x = jnp.where(pl.program_id(0) < n0, x0_ref[...], x_ref[...])
    y = (x + _dot(oh_ref[...], w_ref[0:HG_WIDTH, :])
         + _dot(on_ref[...], w_ref[HG_WIDTH:HG_WIDTH + NSA_WIDTH, :]))
    hx = _rms(y, nw_ref[...]).astype(BF16)
    q = (_dot(hx, wq_ref[...]) * (X_DH ** -0.5)).astype(BF16)
    heads = []
    for h in range(X_HEADS):
        sl = slice(h * X_DH, (h + 1) * X_DH)
        s = _dot_nt(q[:, sl], k_ref[:, sl])
        e = jnp.exp(s - jnp.max(s, axis=-1, keepdims=True))
        p = e / jnp.sum(e, axis=-1, keepdims=True)
        heads.append(_dot(p.astype(BF16), v_ref[:, sl]))
    o_ref[...] = y + _dot(jnp.concatenate(heads, axis=1).astype(BF16), wo_ref[...])


def _outproj_xattn(x1_0, x1, o_hg, o_nsa, w_out, nw, wq, k, v, wo):
    width = X_HEADS * X_DH
    tiles_per_b = SEQ // OUT_TM
    n0 = x1_0.shape[0] // OUT_TM
    assert x1_0.shape[0] == n0 * OUT_TM and x1_0.shape[0] + x1.shape[0] == TOKENS
    row = lambda w: pl.BlockSpec((OUT_TM, w), lambda i: (i, 0))
    const = lambda r, c: pl.BlockSpec((r, c), lambda i: (0, 0))
    kv = pl.BlockSpec((None, MEM_LEN, width), lambda i: (i // tiles_per_b, 0, 0))
    return pl.pallas_call(
        functools.partial(_outx_body, n0=n0),
        grid=(TOKENS // OUT_TM,),
        in_specs=[pl.BlockSpec((OUT_TM, D_MODEL), lambda i: (jnp.minimum(i, n0 - 1), 0)),
                  pl.BlockSpec((OUT_TM, D_MODEL), lambda i: (jnp.maximum(i - n0, 0), 0)),
                  row(HG_WIDTH), row(NSA_WIDTH), const(D_MODEL, D_MODEL), const(1, D_MODEL),
                  const(D_MODEL, width), kv, kv, const(width, D_MODEL)],
        out_specs=row(D_MODEL),
        out_shape=jax.ShapeDtypeStruct((TOKENS, D_MODEL), F32),
        compiler_params=_cparams("parallel"),
        name="out_proj_xattn",
    )(x1_0, x1, o_hg, o_nsa, w_out, nw, wq, k, v, wo)


def _memkv_body(m_ref, nw_ref, wk_ref, wv_ref, k_ref, v_ref):
    hm = _rms(m_ref[...], nw_ref[...]).astype(BF16)
    k_ref[...] = _dot(hm, wk_ref[...]).astype(BF16)
    v_ref[...] = _dot(hm, wv_ref[...]).astype(BF16)


def _memkv(mem, nw, wk, wv):
    width = X_HEADS * X_DH
    wspec = pl.BlockSpec((D_MODEL, width), lambda b: (0, 0))
    ospec = pl.BlockSpec((None, MEM_LEN, width), lambda b: (b, 0, 0))
    osh = jax.ShapeDtypeStruct((BATCH, MEM_LEN, width), BF16)
    return pl.pallas_call(
        _memkv_body,
        grid=(BATCH,),
        in_specs=[pl.BlockSpec((None, MEM_LEN, D_MODEL), lambda b: (b, 0, 0)),
                  pl.BlockSpec((1, D_MODEL), lambda b: (0, 0)), wspec, wspec],
        out_specs=(ospec, ospec), out_shape=(osh, osh),
        compiler_params=_cparams("parallel"),
        name="xattn_memkv",
    )(mem, nw, wk, wv)


def _overlap_t():
    c0 = np.arange(N_CMP)[:, None] * CMP_STRIDE
    s0 = np.arange(N_SEL)[None, :] * SEL_LEN
    ov = np.clip(np.minimum(c0 + CMP_LEN, s0 + SEL_LEN) - np.maximum(c0, s0), 0, None) / CMP_LEN
    out = np.zeros((N_SEL, N_CMP_PAD), np.float32)
    out[:, :N_CMP] = ov.T
    return out


def kernel(x, mem, positions, ffn1_norm, ffn1_w_gate, ffn1_w_up, ffn1_w_down, mix_norm, w_in, hgrn_lb_logits, hgrn_out_norm, nsa_cmp_pe, nsa_cmp_k_w1, nsa_cmp_k_w2, nsa_cmp_v_w1, nsa_cmp_v_w2, w_out, xattn_norm, mem_norm, xattn_wq, xattn_wk, xattn_wv, xattn_wo, ffn2_norm, ffn2_w_gate, ffn2_w_up, ffn2_w_down, final_norm):
    bf = lambda a: a.astype(BF16)
    vec = lambda a: a.reshape(1, -1).astype(F32)
    x2d = x.reshape(TOKENS, D_MODEL)

    x1_head, h_head, w1_gate, w1_up, w1_down = _ffn(
        x2d, vec(ffn1_norm[0]), ffn1_w_gate[0], ffn1_w_up[0], ffn1_w_down[0], vec(mix_norm[0]),
        final=False, n_tiles=1, tm=FFN_HEAD_TM, tf=FFN_HEAD_TF)
    x1, h_mix = _ffn(x2d, vec(ffn1_norm[0]), w1_gate, w1_up, w1_down, vec(mix_norm[0]), final=False,
                     first_tile=FFN_HEAD_TM // FFN_TM, n_tiles=(TOKENS - FFN_HEAD_TM) // FFN_TM)

    w_t = w_in[0].T
    w_tail = jnp.pad(w_t[PROJ_FULL_TILES * PROJ_TN:], ((0, PROJ_W - D_IN), (0, 0)))
    proj, proj_f = _proj(h_head, h_mix, w_t, w_tail)

    o_hg = _hgrn(proj, proj_f, hgrn_lb_logits.astype(F32), vec(hgrn_out_norm[0]))

    inv = ROPE_THETA ** (-jnp.arange(NSA_DH // 2, dtype=F32) / (NSA_DH // 2))
    inv128 = jnp.tile(inv, 128 // (NSA_DH // 2)).reshape(1, 128)
    pos4 = jnp.repeat(positions.astype(F32).reshape(BATCH, SEQ // 4, 4), NSA_DH // 2, axis=-1)
    q_r, kc_tok, vc_tok, ks, kw, vst, vwt, gt = _nsa_prep(proj, pos4, inv128)

    def over_groups(w):
        z = jnp.zeros_like(w)
        return bf(jnp.concatenate([jnp.concatenate([w, z], axis=-1), jnp.concatenate([z, w], axis=-1)], axis=-2))

    per_pos = lambda w1: over_groups(w1.reshape(CMP_LEN, NSA_DH, CMP_HIDDEN))
    pe = nsa_cmp_pe[0].astype(F32)
    kc, vct = _compress(kc_tok, vc_tok, jnp.concatenate([pe, pe], axis=1),
                        per_pos(nsa_cmp_k_w1[0]), over_groups(nsa_cmp_k_w2[0]),
                        per_pos(nsa_cmp_v_w1[0]), over_groups(nsa_cmp_v_w2[0]))
    o_nsa, w2_gate, w2_up, w2_down, w_out_b, wq_b, wk_b, wv_b, wo_b = _nsa_attn(
        q_r, kc, vct, ks, vst, kw, vwt, gt, jnp.asarray(_overlap_t(), dtype=BF16),
        casts=(ffn2_w_gate[0], ffn2_w_up[0], ffn2_w_down[0],
               w_out[0], xattn_wq[0], xattn_wk[0], xattn_wv[0], xattn_wo[0]))

    km, vm = _memkv(mem, vec(mem_norm[0]), wk_b, wv_b)
    x3 = _outproj_xattn(x1_head, x1, o_hg.reshape(TOKENS, HG_WIDTH), o_nsa.reshape(TOKENS, NSA_WIDTH),
                        w_out_b, vec(xattn_norm[0]), wq_b, km, vm, wo_b)

    (out,) = _ffn(x3, vec(ffn2_norm[0]), w2_gate, w2_up, w2_down, vec(final_norm), final=True)
    return out.reshape(BATCH, SEQ, D_MODEL)
```

```python
import functools

import numpy as np
import jax
import jax.numpy as jnp
from jax import lax
from jax.experimental import pallas as pl
from jax.experimental.pallas import tpu as pltpu

F32 = jnp.float32
BF16 = jnp.bfloat16

D_MODEL = 2048
BATCH = 2
SEQ = 4096
TOKENS = BATCH * SEQ
RMS_EPS = 1e-6
ROPE_THETA = 10000.0
HG_WIDTH = 1024
HG_HEADS = 8
HG_D = 128
HG_CHUNK = 128
HG_LEVELS = (64, 32, 16, 8, 4, 2, 1)
NSA_WIDTH = 1024
NSA_DH = 64
NSA_HEADS = 16
NSA_KV = 2
NSA_REP = 8
NSA_VROWS = NSA_DH + 16
CMP_LEN = 32
CMP_STRIDE = 16
CMP_HIDDEN = 256
N_CMP = (SEQ - CMP_LEN) // CMP_STRIDE + 1
N_CMP_PAD = 256
SEL_LEN = 64
N_SEL = SEQ // SEL_LEN
SEL_TOP = 16
WINDOW = 512
Q_BLOCK = 128
N_QB = SEQ // Q_BLOCK
SEL_SUB = 256
WIN_KEYS = WINDOW + Q_BLOCK
MEM_LEN = 256
X_HEADS = 4
X_DH = 128
D_FF = 5632
IN_SIZES = (1024, 1024, 1024, 1024, 1024, 128, 128, 128, 128, 128, 128, 48)
D_IN = sum(IN_SIZES)
PROJ_NSA_OFF = sum(IN_SIZES[:4])
NSA_PROJ = 2048
PROJ_W = PROJ_NSA_OFF + NSA_PROJ
NEG = -1e30
LOG2E = 1.4426950408889634

V7X_VMEM_BYTES = 64 * 1024 * 1024
VMEM_LIMIT = V7X_VMEM_BYTES - 8 * 1024 * 1024


def _cparams(*sem, flags=None):
    return pltpu.CompilerParams(dimension_semantics=sem, vmem_limit_bytes=VMEM_LIMIT, flags=flags)


def _rms(x, w):
    return x * lax.rsqrt(jnp.mean(x * x, axis=-1, keepdims=True) + RMS_EPS) * w


def _silu(x):
    return x * jax.nn.sigmoid(x)


def _dot(a, b):
    return jnp.dot(a, b, preferred_element_type=F32)


def _dot_f32_by_01(sel, x):
    hi = x.astype(BF16)
    r1 = x - hi.astype(F32)
    mid = r1.astype(BF16)
    lo = (r1 - mid.astype(F32)).astype(BF16)
    n = x.shape[1]
    y = _dot(sel, jnp.concatenate([hi, mid, lo], axis=1))
    return y[:, 0:n] + y[:, n:2 * n] + y[:, 2 * n:3 * n]


def _dot_nt(a, b):
    return lax.dot_general(a, b, (((1,), (1,)), ((), ())), preferred_element_type=F32)


FFN_TM = 512
FFN_TF = 512
FFN_HEAD_TM = 1024
FFN_HEAD_TF = 256


def _ffn_body(x_ref, nw_ref, wg_ref, wu_ref, wd_ref, nw2_ref, *rest,
              final, f32_weights):
    if final:
        o_ref, rest = rest[0], rest[1:]
    else:
        (o_ref, hn_ref), rest = rest[:2], rest[2:]
    if f32_weights:
        w_copy, rest = rest[:3], rest[3:]
    (h_scr,) = rest
    j = pl.program_id(1)
    last = pl.num_programs(1) - 1

    def swiglu_tile(h):
        wg, wu, wd = wg_ref[...], wu_ref[...], wd_ref[...]
        if f32_weights:
            wg, wu, wd = wg.astype(BF16), wu.astype(BF16), wd.astype(BF16)
            for dst, w in zip(w_copy, (wg, wu, wd)):
                dst[...] = w
        g = _dot(h, wg)
        u = _dot(h, wu)
        return _dot((_silu(g) * u).astype(BF16), wd)

    @pl.when(j == 0)
    def _():
        h = _rms(x_ref[...], nw_ref[...]).astype(BF16)
        h_scr[...] = h
        o_ref[...] = swiglu_tile(h)

    @pl.when((j > 0) & (j < last))
    def _():
        o_ref[...] += swiglu_tile(h_scr[...])

    @pl.when(j == last)
    def _():
        y = x_ref[...] + 0.5 * (o_ref[...] + swiglu_tile(h_scr[...]))
        if final:
            o_ref[...] = _rms(y, nw2_ref[...])
        else:
            o_ref[...] = y
            hn_ref[...] = _rms(y, nw2_ref[...]).astype(BF16)


def _ffn(x, nw, wg, wu, wd, nw2, final, first_tile=0, n_tiles=None, tm=FFN_TM, tf=FFN_TF):
    ni, nj = (n_tiles or TOKENS // tm), D_FF // tf
    f32_weights = wg.dtype == F32
    once = dict(pipeline_mode=pl.Buffered(1)) if ni == 1 else {}
    row_in = pl.BlockSpec((tm, D_MODEL), lambda i, j: (i + first_tile, 0), **once)
    row = pl.BlockSpec((tm, D_MODEL), lambda i, j: (i, 0), **once)
    vec = pl.BlockSpec((1, D_MODEL), lambda i, j: (0, 0))
    in_specs = [row_in, vec,
                pl.BlockSpec((D_MODEL, tf), lambda i, j: (0, j)),
                pl.BlockSpec((D_MODEL, tf), lambda i, j: (0, j)),
                pl.BlockSpec((tf, D_MODEL), lambda i, j: (j, 0)),
                vec]
    main_shapes = [jax.ShapeDtypeStruct((ni * tm, D_MODEL), F32)]
    if not final:
        main_shapes.append(jax.ShapeDtypeStruct((ni * tm, D_MODEL), BF16))
    w_copy_specs = in_specs[2:5] if f32_weights else []
    w_copy_shapes = [jax.ShapeDtypeStruct(w.shape, BF16) for w in (wg, wu, wd)] if f32_weights else []
    return pl.pallas_call(
        functools.partial(_ffn_body, final=final, f32_weights=f32_weights),
        grid=(ni, nj), in_specs=in_specs,
        out_specs=tuple([row] * len(main_shapes) + w_copy_specs),
        out_shape=tuple(main_shapes + w_copy_shapes),
        scratch_shapes=[pltpu.VMEM((tm, D_MODEL), BF16)],
        compiler_params=_cparams("parallel", "arbitrary"),
        name="ffn_final" if final else ("ffn_head" if f32_weights else "ffn"),
    )(x, nw, wg, wu, wd, nw2)


PROJ_TM = 1024
PROJ_TN = 512


PROJ_F_TILE0 = IN_SIZES[0] // PROJ_TN
PROJ_F_TILES = IN_SIZES[1] // PROJ_TN


PROJ_FULL_TILES = D_IN // PROJ_TN


def _proj_body(a0_ref, a_ref, wt_ref, tail_ref, o_ref, f_ref, *, n0):
    i = pl.program_id(0)
    j = pl.program_id(1)

    def tile(a):
        w = jnp.where(j >= PROJ_FULL_TILES, tail_ref[...], wt_ref[...]).astype(BF16)
        y = _dot_nt(a, w)
        o_ref[...] = y.astype(BF16)

        @pl.when((j >= PROJ_F_TILE0) & (j < PROJ_F_TILE0 + PROJ_F_TILES))
        def _():
            f_ref[...] = y

    pl.when(i < n0)(lambda: tile(a0_ref[...]))
    pl.when(i >= n0)(lambda: tile(a_ref[...]))


def _proj(a0, a, wt, wt_tail):
    k = a.shape[1]
    n0 = a0.shape[0] // PROJ_TM
    assert a0.shape[0] == n0 * PROJ_TM and a.shape[0] % PROJ_TM == 0
    m = a0.shape[0] + a.shape[0]
    f_tile = lambda i, j: (i, jnp.clip(j - PROJ_F_TILE0, 0, PROJ_F_TILES - 1))
    return pl.pallas_call(
        functools.partial(_proj_body, n0=n0),
        grid=(m // PROJ_TM, PROJ_W // PROJ_TN),
        in_specs=[pl.BlockSpec((PROJ_TM, k), lambda i, j: (jnp.minimum(i, n0 - 1), 0)),
                  pl.BlockSpec((PROJ_TM, k), lambda i, j: (jnp.maximum(i - n0, 0), 0)),
                  pl.BlockSpec((PROJ_TN, k), lambda i, j: (jnp.minimum(j, PROJ_FULL_TILES - 1), 0)),
                  pl.BlockSpec((PROJ_TN, k), lambda i, j: (0, 0))],
        out_specs=(pl.BlockSpec((PROJ_TM, PROJ_TN), lambda i, j: (i, j)),
                   pl.BlockSpec((PROJ_TM, PROJ_TN), f_tile)),
        out_shape=(jax.ShapeDtypeStruct((m, PROJ_W), BF16),
                   jax.ShapeDtypeStruct((m, IN_SIZES[1]), F32)),
        compiler_params=_cparams("parallel", "arbitrary"),
        name="proj_in",
    )(a0, a, wt, wt_tail)


HG_ROWS = 1024
HG_CUM = 256


def _hgrn_body(q_ref, f_ref, i_ref, g_ref, lbl_ref, nw_ref, o_ref, st_ref, k_s, b_s):
    c = pl.program_id(2)

    @pl.when(c == 0)
    def _():
        st_ref[...] = jnp.zeros_like(st_ref)

    l0 = lbl_ref[0:1, :]
    l1 = lbl_ref[1:2, :]
    lmax = jnp.maximum(l0, l1)
    e0 = jnp.exp(l0 - lmax)
    lb = e0 / (e0 + jnp.exp(l1 - lmax))

    C = HG_CHUNK
    f = lb + (1.0 - lb) * jax.nn.sigmoid(f_ref[...])
    k_s[...] = 1.0 - f
    r_i = lax.broadcasted_iota(jnp.int32, (HG_CUM, HG_CUM), 0)
    c_i = lax.broadcasted_iota(jnp.int32, (HG_CUM, HG_CUM), 1)
    tri = jnp.where((r_i >= c_i) & (r_i // C == c_i // C), 1.0, 0.0).astype(BF16)
    logf = jnp.log2(f)
    for r0 in range(0, HG_ROWS, HG_CUM):
        b_s[r0:r0 + HG_CUM, :] = _dot_f32_by_01(tri, logf[r0:r0 + HG_CUM])

    t_i = lax.broadcasted_iota(jnp.int32, (C, C), 0)
    s_i = lax.broadcasted_iota(jnp.int32, (C, C), 1)
    level_mask = [(t_i // (2 * w) == s_i // (2 * w)) & (t_i % (2 * w) >= w) & (s_i % (2 * w) < w)
                  for w in HG_LEVELS]
    sub_r = lax.broadcasted_iota(jnp.int32, (8, HG_D), 0)
    row_i = lax.broadcasted_iota(jnp.int32, (C, HG_D), 0)
    right_sign = {w: jnp.where(row_i % (2 * w) >= w, 1.0, -1.0) for w in HG_LEVELS if w < 8}

    def neg_abs_diff(w, r0, b):
        row = lambda r, n: jnp.broadcast_to(b_s[r0 + r:r0 + r + 1, :], (n, HG_D))
        if w >= 8:
            parts = []
            for p0 in range(0, C, 2 * w):
                ref = row(p0 + w - 1, w)
                parts += [ref - b[p0:p0 + w], b[p0 + w:p0 + 2 * w] - ref]
            return jnp.concatenate(parts, axis=0)
        if w == 4:
            bref = jnp.concatenate([row(p0 + 3, 8) for p0 in range(0, C, 8)], axis=0)
        elif w == 2:
            bref = jnp.concatenate([jnp.where(sub_r < 4, row(p0 + 1, 8), row(p0 + 5, 8))
                                    for p0 in range(0, C, 8)], axis=0)
        else:
            bref = jnp.where(row_i % 2 == 1, pltpu.roll(b, 1, 0), b)
        return (b - bref) * right_sign[w]

    chunks = [ci * C for ci in range(HG_ROWS // C)]
    rows = lambda ref, r0: ref[r0:r0 + C, :]
    att = [jnp.zeros((C, C), F32) for _ in chunks]
    q16 = [rows(q_ref, r0) for r0 in chunks]
    k16 = [rows(k_s, r0).astype(BF16) for r0 in chunks]
    for w, mask in zip(HG_LEVELS, level_mask):
        for n, r0 in enumerate(chunks):
            b = rows(b_s, r0)
            e = jnp.exp2(neg_abs_diff(w, r0, b)).astype(BF16)
            att[n] = jnp.where(mask, _dot_nt(q16[n] * e, k16[n] * e), att[n])
    o_intra = []
    for n, r0 in enumerate(chunks):
        q, k, v = q16[n].astype(F32), rows(k_s, r0), rows(i_ref, r0)
        o_intra.append(_dot(att[n].astype(BF16), v)
                       + jnp.sum(q * k, axis=-1, keepdims=True) * v.astype(F32))
    upd = []
    for r0 in chunks:
        bl = b_s[r0 + C - 1:r0 + C, :]
        kd = rows(k_s, r0) * jnp.exp2(bl - rows(b_s, r0))
        v_t = rows(i_ref, r0).astype(F32).T.astype(BF16)
        upd.append((jnp.exp2(bl), _dot(v_t, kd.astype(BF16))))
    st_t = st_ref[...]
    for n, r0 in enumerate(chunks):
        qe = (q16[n].astype(F32) * jnp.exp2(rows(b_s, r0))).astype(BF16)
        o = o_intra[n] + _dot_nt(qe, st_t.astype(BF16))
        st_t = st_t * upd[n][0] + upd[n][1]
        o = o * lax.rsqrt(jnp.mean(o * o, axis=-1, keepdims=True) + RMS_EPS)
        o_ref[r0:r0 + C, :] = (o * nw_ref[...] * _silu(rows(g_ref, r0).astype(F32))).astype(BF16)
    st_ref[...] = st_t


def _hgrn(proj, proj_f, lb_logits, norm_w):
    p3 = proj.reshape(BATCH, SEQ, PROJ_W)
    f3 = proj_f.reshape(BATCH, SEQ, HG_WIDTH)

    def col(off):
        return pl.BlockSpec((None, HG_ROWS, HG_D), lambda b, h, c: (b, c, off + h))

    return pl.pallas_call(
        _hgrn_body,
        grid=(BATCH, HG_HEADS, SEQ // HG_ROWS),
        in_specs=[col(0), col(0), col(2 * HG_HEADS), col(3 * HG_HEADS),
                  pl.BlockSpec((2, HG_D), lambda b, h, c: (0, h)),
                  pl.BlockSpec((1, HG_D), lambda b, h, c: (0, h))],
        out_specs=pl.BlockSpec((None, HG_ROWS, HG_D), lambda b, h, c: (b, c, h)),
        out_shape=jax.ShapeDtypeStruct((BATCH, SEQ, HG_WIDTH), BF16),
        scratch_shapes=[pltpu.VMEM((HG_D, HG_D), F32),
                        pltpu.VMEM((HG_ROWS, HG_D), F32),
                        pltpu.VMEM((HG_ROWS, HG_D), F32)],
        compiler_params=_cparams("parallel", "parallel", "arbitrary"),
        name="hgrn2",
    )(p3, f3, p3, p3, lb_logits, norm_w)


PREP_TM = 256


def _prep_body(p_ref, pos4_ref, inv_ref, q_ref, kc_ref, vc_ref, ks_ref, kw_ref,
               vst_ref, vwt_ref, gt_ref, cos_s, sin_s):
    n_freq = NSA_DH // 2
    ang4 = pos4_ref[...] * inv_ref[...]
    seg = lax.broadcasted_iota(jnp.int32, (PREP_TM // 4, 128), 1) // n_freq
    for table, dst in ((jnp.cos(ang4), cos_s), (jnp.sin(ang4), sin_s)):
        for u in range(4):
            one = jnp.where(seg == u, table, 0.0)
            full = one
            for k in range(1, 4):
                full = full + pltpu.roll(one, k * n_freq, 1)
            dst[pl.ds(u, PREP_TM // 4, stride=4), :] = full
    cos = cos_s[...]
    sin = sin_s[...]
    lane = lax.broadcasted_iota(jnp.int32, (PREP_TM, 128), 1)
    lo = (lane & (NSA_DH // 2)) == 0
    sin_signed = jnp.where(lo, -sin, sin)

    def rope(x):
        rot = jnp.where(lo, pltpu.roll(x, 128 - NSA_DH // 2, 1), pltpu.roll(x, NSA_DH // 2, 1))
        return x * cos + rot * sin_signed

    cols = lambda c0: p_ref[:, c0:c0 + 128].astype(F32)
    scale = NSA_DH ** -0.5 * LOG2E
    for cblk in range(NSA_WIDTH // 128):
        q_ref[cblk * 128:(cblk + 1) * 128, :] = (rope(cols(cblk * 128)) * scale).T.astype(BF16)
    c_kc, c_vc, c_ks, c_vs, c_kw, c_vw, c_gate = (int(c) for c in np.cumsum(IN_SIZES[4:11]))
    kc_ref[...] = rope(cols(c_kc))
    vc_ref[...] = cols(c_vc)
    ks_ref[:, 0:128] = rope(cols(c_ks)).astype(BF16)
    blk = (pl.program_id(1) * PREP_TM + lax.broadcasted_iota(jnp.int32, (PREP_TM, 128), 0)) // SEL_LEN
    ks_ref[:, 128:256] = jnp.where(lane == blk, 1.0, 0.0).astype(BF16)
    kw_ref[...] = rope(cols(c_kw)).astype(BF16)
    ones = jnp.ones((NSA_VROWS - NSA_DH, PREP_TM), BF16)
    for v_ref, c0 in ((vst_ref, c_vs), (vwt_ref, c_vw)):
        vt = cols(c0).T.astype(BF16)
        for g in range(NSA_KV):
            v_ref[g * NSA_VROWS:g * NSA_VROWS + NSA_DH, :] = vt[g * NSA_DH:(g + 1) * NSA_DH]
            v_ref[g * NSA_VROWS + NSA_DH:(g + 1) * NSA_VROWS, :] = ones
    gt_ref[...] = jax.nn.sigmoid(cols(c_gate)).T[0:3 * NSA_HEADS, :]


def _nsa_prep(proj, pos4, inv128):
    nt = SEQ // PREP_TM
    p3 = proj.reshape(BATCH, SEQ, PROJ_W)
    nat = lambda w: pl.BlockSpec((None, PREP_TM, w), lambda b, i: (b, i, 0))
    tr = lambda r: pl.BlockSpec((None, r, PREP_TM), lambda b, i: (b, 0, i))
    sds = jax.ShapeDtypeStruct
    return pl.pallas_call(
        _prep_body,
        grid=(BATCH, nt),
        in_specs=[pl.BlockSpec((None, PREP_TM, NSA_PROJ), lambda b, i: (b, i, PROJ_NSA_OFF // NSA_PROJ)),
                  pl.BlockSpec((None, PREP_TM // 4, 128), lambda b, i: (b, i, 0)),
                  pl.BlockSpec((1, 128), lambda b, i: (0, 0))],
        out_specs=(tr(NSA_WIDTH), nat(128), nat(128), nat(256), nat(128),
                   tr(NSA_KV * NSA_VROWS), tr(NSA_KV * NSA_VROWS), tr(3 * NSA_HEADS)),
        out_shape=(sds((BATCH, NSA_WIDTH, SEQ), BF16),
                   sds((BATCH, SEQ, 128), F32),
                   sds((BATCH, SEQ, 128), F32),
                   sds((BATCH, SEQ, 256), BF16),
                   sds((BATCH, SEQ, 128), BF16),
                   sds((BATCH, NSA_KV * NSA_VROWS, SEQ), BF16),
                   sds((BATCH, NSA_KV * NSA_VROWS, SEQ), BF16),
                   sds((BATCH, 3 * NSA_HEADS, SEQ), F32)),
        scratch_shapes=[pltpu.VMEM((PREP_TM, 128), F32)] * 2,
        compiler_params=_cparams("parallel", "parallel"),
        name="nsa_prep",
    )(p3, pos4, inv128)


def _cmp_body(tk_ref, tv_ref, pe_ref, kw1_ref, kw2_ref, vw1_ref, vw2_ref, kc_ref, vct_ref, y1_s, y2_s):
    row = lax.broadcasted_iota(jnp.int32, (N_CMP_PAD, NSA_KV * CMP_HIDDEN), 0)

    def mlp(t_ref, w1_ref, w2_ref):
        y1_s[...] = jnp.zeros_like(y1_s)
        y2_s[...] = jnp.zeros_like(y2_s)
        for l in range(CMP_STRIDE):
            x = t_ref[pl.ds(l, N_CMP_PAD, stride=CMP_STRIDE), :]
            y1_s[...] += _dot((x + pe_ref[l:l + 1, :]).astype(BF16), w1_ref[l])
            y2_s[...] += _dot((x + pe_ref[CMP_STRIDE + l:CMP_STRIDE + l + 1, :]).astype(BF16),
                              w1_ref[CMP_STRIDE + l])
        hid = jnp.where(row < N_CMP, y1_s[...] + pltpu.roll(y2_s[...], N_CMP_PAD - 1, 0), 0.0)
        return _dot(_silu(hid).astype(BF16), w2_ref[...])

    kc_ref[...] = mlp(tk_ref, kw1_ref, kw2_ref).astype(BF16)
    y1_s[:, 0:128] = mlp(tv_ref, vw1_ref, vw2_ref)
    vt = y1_s[:, 0:128].T.astype(BF16)
    ones = jnp.ones((NSA_VROWS - NSA_DH, N_CMP_PAD), BF16)
    for g in range(NSA_KV):
        vct_ref[g * NSA_VROWS:g * NSA_VROWS + NSA_DH, :] = vt[g * NSA_DH:(g + 1) * NSA_DH]
        vct_ref[g * NSA_VROWS + NSA_DH:(g + 1) * NSA_VROWS, :] = ones


def _compress(tk, tv, pe2, kw1, kw2, vw1, vw2):
    seg = pl.BlockSpec((None, SEQ, 128), lambda b: (b, 0, 0))
    full2 = lambda a: pl.BlockSpec(a.shape, lambda b: (0,) * a.ndim)
    return pl.pallas_call(
        _cmp_body,
        grid=(BATCH,),
        in_specs=[seg, seg, full2(pe2), full2(kw1), full2(kw2), full2(vw1), full2(vw2)],
        out_specs=(pl.BlockSpec((None, N_CMP_PAD, 128), lambda b: (b, 0, 0)),
                   pl.BlockSpec((None, NSA_KV * NSA_VROWS, N_CMP_PAD), lambda b: (b, 0, 0))),
        out_shape=(jax.ShapeDtypeStruct((BATCH, N_CMP_PAD, 128), BF16),
                   jax.ShapeDtypeStruct((BATCH, NSA_KV * NSA_VROWS, N_CMP_PAD), BF16)),
        scratch_shapes=[pltpu.VMEM((N_CMP_PAD, NSA_KV * CMP_HIDDEN), F32)] * 2,
        compiler_params=_cparams("parallel"),
        name="nsa_compress",
    )(tk, tv, pe2, kw1, kw2, vw1, vw2)


NSA_NL = NSA_REP * Q_BLOCK


def _nsa_body(q_ref, kc_ref, vct_ref, ks_ref, vst_ref, kw_ref, vwt_ref, gt_ref, ovt_ref, *rest, n_casts):
    cast_in, o_ref, cast_out = rest[:n_casts], rest[n_casts], rest[n_casts + 1:2 * n_casts + 1]
    qa_s, sc_s, rk_s, s0_s, s1_s, sc_buf, sw_buf, sd_buf, m_s, acc_s = rest[2 * n_casts + 1:]
    g = pl.program_id(1)
    qb = pl.program_id(2)
    q0 = pl.multiple_of(qb * Q_BLOCK, Q_BLOCK)
    is_g0 = g == 0

    zero_slab = jnp.zeros((NSA_DH, Q_BLOCK), BF16)
    for r in range(NSA_REP):
        s = q_ref[r * NSA_DH:(r + 1) * NSA_DH, :]
        qa_s[0:NSA_DH, r * Q_BLOCK:(r + 1) * Q_BLOCK] = jnp.where(is_g0, s, zero_slab)
        qa_s[NSA_DH:2 * NSA_DH, r * Q_BLOCK:(r + 1) * Q_BLOCK] = jnp.where(is_g0, zero_slab, s)
    qa_s[128 + N_SEL:256, :] = jnp.zeros((128 - N_SEL, NSA_NL), BF16)

    tq = q0 + lax.broadcasted_iota(jnp.int32, (1, Q_BLOCK), 1)

    def mask_pair(s, valid):
        return jnp.concatenate([jnp.where(valid, s[:, r * Q_BLOCK:(r + 1) * Q_BLOCK], NEG)
                                for r in range(NSA_REP)], axis=1)

    n_i = lax.broadcasted_iota(jnp.int32, (N_CMP_PAD, Q_BLOCK), 0)
    valid_c = (n_i * CMP_STRIDE + (CMP_LEN - 1) <= tq) & (n_i < N_CMP)
    sc_buf[...] = mask_pair(_dot(kc_ref[...], qa_s[0:128, :]).astype(BF16), valid_c)

    w0 = pl.multiple_of(jnp.maximum(q0 - WINDOW, 0), Q_BLOCK)
    dpos = tq - (w0 + lax.broadcasted_iota(jnp.int32, (WIN_KEYS, Q_BLOCK), 0))
    sw_buf[...] = mask_pair(_dot(kw_ref[pl.ds(w0, WIN_KEYS), :], qa_s[0:128, :]).astype(BF16),
                            (dpos >= 0) & (dpos < WINDOW))

    sc = sc_buf[...]
    m_c = jnp.max(sc, axis=0, keepdims=True)
    e_c = jnp.exp2(sc - m_c)
    acc_c = _dot(vct_ref[...], e_c)
    inv_c = jnp.where(m_c.astype(F32) > 0.5 * NEG, 1.0 / jnp.maximum(acc_c[NSA_DH:NSA_DH + 1], 1e-30), 0.0)
    imp_h = _dot(ovt_ref[...], e_c) * inv_c

    d_i = lax.broadcasted_iota(jnp.int32, (Q_BLOCK, Q_BLOCK), 0)
    t_i = lax.broadcasted_iota(jnp.int32, (Q_BLOCK, Q_BLOCK), 1)
    sd_buf[...] = mask_pair(_dot(ks_ref[pl.ds(q0, Q_BLOCK), 0:128], qa_s[0:128, :]).astype(BF16),
                            d_i <= t_i)

    sw = sw_buf[...]
    acc_w = _dot(vwt_ref[:, pl.ds(w0, WIN_KEYS)], jnp.exp2(sw - jnp.max(sw, axis=0, keepdims=True)))
    inv_w = 1.0 / jnp.maximum(acc_w[NSA_DH:NSA_DH + 1], 1e-30)

    s = sd_buf[...]
    m16 = jnp.max(s, axis=0, keepdims=True)
    m_s[...] = m16.astype(F32)
    acc_s[...] = _dot(vst_ref[:, pl.ds(q0, Q_BLOCK)], jnp.exp2(s - m16))

    imp = imp_h[:, 0:Q_BLOCK]
    for r in range(1, NSA_REP):
        imp = imp + imp_h[:, r * Q_BLOCK:(r + 1) * Q_BLOCK]
    j_i = lax.broadcasted_iota(jnp.int32, (N_SEL, Q_BLOCK), 0)
    cur = tq // SEL_LEN
    forced = (j_i == 0) | (j_i == cur) | (j_i == cur - 1)
    score = jnp.where(forced, jnp.inf, jnp.where(j_i > cur, -jnp.inf, imp))
    sc_s[...] = score
    rk_s[...] = jnp.zeros_like(rk_s)
    sub8 = lax.broadcasted_iota(jnp.int32, (8, Q_BLOCK), 0)
    for grp in range(N_SEL // 8):
        @pl.when(8 * grp <= 2 * qb + 1)
        def _():
            for v in range(N_SEL // 8):
                sv = sc_s[8 * v:8 * v + 8, :]
                part = jnp.zeros((8, Q_BLOCK), F32)
                for jp in range(8 * grp, 8 * grp + 8):
                    row = sc_s[jp:jp + 1, :]
                    if v > grp:
                        part = part + jnp.where(row >= sv, 1.0, 0.0)
                    elif v < grp:
                        part = part + jnp.where(row > sv, 1.0, 0.0)
                    else:
                        part = part + jnp.where(sub8 + 8 * v > jp, jnp.where(row >= sv, 1.0, 0.0),
                                                jnp.where(row > sv, 1.0, 0.0))
                rk_s[8 * v:8 * v + 8, :] += part
    bias = jnp.where((rk_s[...] < SEL_TOP) & (j_i < 2 * qb), 0.0, NEG).astype(BF16)
    for r in range(NSA_REP):
        qa_s[128:128 + N_SEL, r * Q_BLOCK:(r + 1) * Q_BLOCK] = bias

    last_sub = SEQ // SEL_SUB - 1

    def scores_into(buf, c):
        k0 = pl.multiple_of(jnp.minimum(c, last_sub) * SEL_SUB, SEL_SUB)
        buf[...] = _dot(ks_ref[pl.ds(k0, SEL_SUB), :], qa_s[...]).astype(BF16)

    def softmax_from(buf, c):
        k0 = pl.multiple_of(c * SEL_SUB, SEL_SUB)
        s = buf[...]
        m = m_s[...]
        m_new = jnp.maximum(m, jnp.max(s, axis=0, keepdims=True).astype(F32))
        m_s[...] = m_new
        pr = jnp.exp2(s - m_new.astype(BF16))
        acc_s[...] = acc_s[...] * jnp.exp2(m - m_new) + _dot(vst_ref[:, pl.ds(k0, SEL_SUB)], pr)

    scores_into(s0_s, 0)

    def sel_step(i, carry):
        c = 2 * i
        scores_into(s1_s, c + 1)
        softmax_from(s0_s, c)
        scores_into(s0_s, c + 2)
        softmax_from(s1_s, c + 1)
        return carry

    n_main = (qb * Q_BLOCK + 2 * SEL_SUB - 1) // (2 * SEL_SUB)
    lax.fori_loop(0, n_main, sel_step, 0)
    inv_s = 1.0 / jnp.maximum(acc_s[NSA_DH:NSA_DH + 1, :], 1e-30)

    gall = gt_ref[...]
    ggrp = jnp.where(is_g0, gall[0:3 * NSA_REP], gall[3 * NSA_REP:3 * NSA_HEADS])

    for src, dst in zip(cast_in, cast_out):
        dst[...] = src[...].astype(BF16)

    for p2 in range(NSA_REP // 2):
        halves = []
        for hh in range(2):
            r = 2 * p2 + hh
            sl = slice(r * Q_BLOCK, (r + 1) * Q_BLOCK)
            g_c, g_s, g_w = (ggrp[3 * r + br:3 * r + br + 1, :] for br in range(3))
            halves.append((g_c * inv_c[:, sl]) * acc_c[0:NSA_DH, sl]
                          + (g_s * inv_s[:, sl]) * acc_s[0:NSA_DH, sl]
                          + (g_w * inv_w[:, sl]) * acc_w[0:NSA_DH, sl])
        o_ref[:, p2 * 128:(p2 + 1) * 128] = jnp.concatenate(halves, axis=0).T.astype(BF16)


def _nsa_attn(q_r, kc, vct, ks, vst, kw, vwt, gt, ovt, casts=()):
    per_b = lambda r, c: pl.BlockSpec((None, r, c), lambda b, g, i: (b, 0, 0))
    per_bg = lambda r, c: pl.BlockSpec((None, r, c), lambda b, g, i: (b, g, 0))
    const = lambda a: pl.BlockSpec(a.shape, lambda b, g, i: (0, 0))
    n_steps = BATCH * NSA_KV * N_QB
    cast_specs = []
    for a in casts:
        rows = a.shape[0]
        if rows % (16 * n_steps) == 0:
            cast_specs.append(pl.BlockSpec((rows // n_steps, a.shape[1]),
                                           lambda b, g, i: ((b * NSA_KV + g) * N_QB + i, 0)))
        else:
            cast_specs.append(pl.BlockSpec((rows // N_QB, a.shape[1]),
                                           lambda b, g, i: (jnp.where(b + g == 0, i, N_QB - 1), 0)))
    return pl.pallas_call(
        functools.partial(_nsa_body, n_casts=len(casts)),
        grid=(BATCH, NSA_KV, N_QB),
        in_specs=[pl.BlockSpec((None, NSA_REP * NSA_DH, Q_BLOCK), lambda b, g, i: (b, g, i)),
                  per_b(N_CMP_PAD, 128), per_bg(NSA_VROWS, N_CMP_PAD),
                  per_b(SEQ, 256), per_bg(NSA_VROWS, SEQ),
                  per_b(SEQ, 128), per_bg(NSA_VROWS, SEQ),
                  pl.BlockSpec((None, 3 * NSA_HEADS, Q_BLOCK), lambda b, g, i: (b, 0, i)),
                  const(ovt)] + cast_specs,
        out_specs=tuple([pl.BlockSpec((None, Q_BLOCK, NSA_REP * NSA_DH), lambda b, g, i: (b, i, g))]
                        + cast_specs),
        out_shape=tuple([jax.ShapeDtypeStruct((BATCH, SEQ, NSA_WIDTH), BF16)]
                        + [jax.ShapeDtypeStruct(a.shape, BF16) for a in casts]),
        scratch_shapes=[pltpu.VMEM((256, NSA_NL), BF16),
                        pltpu.VMEM((N_SEL, Q_BLOCK), F32), pltpu.VMEM((N_SEL, Q_BLOCK), F32),
                        pltpu.VMEM((SEL_SUB, NSA_NL), BF16), pltpu.VMEM((SEL_SUB, NSA_NL), BF16),
                        pltpu.VMEM((N_CMP_PAD, NSA_NL), BF16), pltpu.VMEM((WIN_KEYS, NSA_NL), BF16),
                        pltpu.VMEM((Q_BLOCK, NSA_NL), BF16),
                        pltpu.VMEM((1, NSA_NL), F32), pltpu.VMEM((NSA_VROWS, NSA_NL), F32)],
        compiler_params=_cparams(*(("arbitrary",) * 3 if casts else ("parallel", "parallel", "arbitrary"))),
        name="nsa_attn",
    )(q_r, kc, vct, ks, vst, kw, vwt, gt, ovt, *casts)


OUT_TM = 512


def _outx_body(x0_ref, x_ref, oh_ref, on_ref, w_ref, nw_ref, wq_ref, k_ref, v_ref, wo_ref, o_ref, *, n0):
    def tile(x):
        y = (x + _dot(oh_ref[...], w_ref[0:HG_WIDTH, :])
             + _dot(on_ref[...], w_ref[HG_WIDTH:HG_WIDTH + NSA_WIDTH, :]))
        hx = _rms(y, nw_ref[...]).astype(BF16)
        q = (_dot(hx, wq_ref[...]) * (X_DH ** -0.5)).astype(BF16)
        heads = []
        for h in range(X_HEADS):
            sl = slice(h * X_DH, (h + 1) * X_DH)
            s = _dot_nt(q[:, sl], k_ref[:, sl])
            e = jnp.exp(s - jnp.max(s, axis=-1, keepdims=True))
            p = e / jnp.sum(e, axis=-1, keepdims=True)
            heads.append(_dot(p.astype(BF16), v_ref[:, sl]))
        o_ref[...] = y + _dot(jnp.concatenate(heads, axis=1).astype(BF16), wo_ref[...])

    i = pl.program_id(0)
    pl.when(i < n0)(lambda: tile(x0_ref[...]))
    pl.when(i >= n0)(lambda: tile(x_ref[...]))


def _outproj_xattn(x1_0, x1, o_hg, o_nsa, w_out, nw, wq, k, v, wo):
    width = X_HEADS * X_DH
    tiles_per_b = SEQ // OUT_TM
    n0 = x1_0.shape[0] // OUT_TM
    assert x1_0.shape[0] == n0 * OUT_TM and x1_0.shape[0] + x1.shape[0] == TOKENS
    row = lambda w: pl.BlockSpec((OUT_TM, w), lambda i: (i, 0))
    const = lambda r, c: pl.BlockSpec((r, c), lambda i: (0, 0))
    kv = pl.BlockSpec((None, MEM_LEN, width), lambda i: (i // tiles_per_b, 0, 0))
    return pl.pallas_call(
        functools.partial(_outx_body, n0=n0),
        grid=(TOKENS // OUT_TM,),
        in_specs=[pl.BlockSpec((OUT_TM, D_MODEL), lambda i: (jnp.minimum(i, n0 - 1), 0)),
                  pl.BlockSpec((OUT_TM, D_MODEL), lambda i: (jnp.maximum(i - n0, 0), 0)),
                  row(HG_WIDTH), row(NSA_WIDTH), const(D_MODEL, D_MODEL), const(1, D_MODEL),
                  const(D_MODEL, width), kv, kv, const(width, D_MODEL)],
        out_specs=row(D_MODEL),
        out_shape=jax.ShapeDtypeStruct((TOKENS, D_MODEL), F32),
        compiler_params=_cparams("parallel"),
        name="out_proj_xattn",
    )(x1_0, x1, o_hg, o_nsa, w_out, nw, wq, k, v, wo)


def _memkv_body(m_ref, nw_ref, wk_ref, wv_ref, k_ref, v_ref):
    hm = _rms(m_ref[...], nw_ref[...]).astype(BF16)
    k_ref[...] = _dot(hm, wk_ref[...]).astype(BF16)
    v_ref[...] = _dot(hm, wv_ref[...]).astype(BF16)


def _memkv(mem, nw, wk, wv):
    width = X_HEADS * X_DH
    wspec = pl.BlockSpec((D_MODEL, width), lambda b: (0, 0))
    ospec = pl.BlockSpec((None, MEM_LEN, width), lambda b: (b, 0, 0))
    osh = jax.ShapeDtypeStruct((BATCH, MEM_LEN, width), BF16)
    return pl.pallas_call(
        _memkv_body,
        grid=(BATCH,),
        in_specs=[pl.BlockSpec((None, MEM_LEN, D_MODEL), lambda b: (b, 0, 0)),
                  pl.BlockSpec((1, D_MODEL), lambda b: (0, 0)), wspec, wspec],
        out_specs=(ospec, ospec), out_shape=(osh, osh),
        compiler_params=_cparams("parallel"),
        name="xattn_memkv",
    )(mem, nw, wk, wv)


def _overlap_t():
    c0 = np.arange(N_CMP)[:, None] * CMP_STRIDE
    s0 = np.arange(N_SEL)[None, :] * SEL_LEN
    ov = np.clip(np.minimum(c0 + CMP_LEN, s0 + SEL_LEN) - np.maximum(c0, s0), 0, None) / CMP_LEN
    out = np.zeros((N_SEL, N_CMP_PAD), np.float32)
    out[:, :N_CMP] = ov.T
    return out


def kernel(x, mem, positions, ffn1_norm, ffn1_w_gate, ffn1_w_up, ffn1_w_down, mix_norm, w_in, hgrn_lb_logits, hgrn_out_norm, nsa_cmp_pe, nsa_cmp_k_w1, nsa_cmp_k_w2, nsa_cmp_v_w1, nsa_cmp_v_w2, w_out, xattn_norm, mem_norm, xattn_wq, xattn_wk, xattn_wv, xattn_wo, ffn2_norm, ffn2_w_gate, ffn2_w_up, ffn2_w_down, final_norm):
    bf = lambda a: a.astype(BF16)
    vec = lambda a: a.reshape(1, -1).astype(F32)
    x2d = x.reshape(TOKENS, D_MODEL)

    x1_head, h_head, w1_gate, w1_up, w1_down = _ffn(
        x2d, vec(ffn1_norm[0]), ffn1_w_gate[0], ffn1_w_up[0], ffn1_w_down[0], vec(mix_norm[0]),
        final=False, n_tiles=1, tm=FFN_HEAD_TM, tf=FFN_HEAD_TF)
    x1, h_mix = _ffn(x2d, vec(ffn1_norm[0]), w1_gate, w1_up, w1_down, vec(mix_norm[0]), final=False,
                     first_tile=FFN_HEAD_TM // FFN_TM, n_tiles=(TOKENS - FFN_HEAD_TM) // FFN_TM)

    w_t = w_in[0].T
    w_tail = jnp.pad(w_t[PROJ_FULL_TILES * PROJ_TN:], ((0, PROJ_W - D_IN), (0, 0)))
    proj, proj_f = _proj(h_head, h_mix, w_t, w_tail)

    o_hg = _hgrn(proj, proj_f, hgrn_lb_logits.astype(F32), vec(hgrn_out_norm[0]))

    inv = ROPE_THETA ** (-jnp.arange(NSA_DH // 2, dtype=F32) / (NSA_DH // 2))
    inv128 = jnp.tile(inv, 128 // (NSA_DH // 2)).reshape(1, 128)
    pos4 = jnp.repeat(positions.astype(F32).reshape(BATCH, SEQ // 4, 4), NSA_DH // 2, axis=-1)
    q_r, kc_tok, vc_tok, ks, kw, vst, vwt, gt = _nsa_prep(proj, pos4, inv128)

    def over_groups(w):
        z = jnp.zeros_like(w)
        return bf(jnp.concatenate([jnp.concatenate([w, z], axis=-1), jnp.concatenate([z, w], axis=-1)], axis=-2))

    per_pos = lambda w1: over_groups(w1.reshape(CMP_LEN, NSA_DH, CMP_HIDDEN))
    pe = nsa_cmp_pe[0].astype(F32)
    kc, vct = _compress(kc_tok, vc_tok, jnp.concatenate([pe, pe], axis=1),
                        per_pos(nsa_cmp_k_w1[0]), over_groups(nsa_cmp_k_w2[0]),
                        per_pos(nsa_cmp_v_w1[0]), over_groups(nsa_cmp_v_w2[0]))
    o_nsa, w2_gate, w2_up, w2_down, w_out_b, wq_b, wk_b, wv_b, wo_b = _nsa_attn(
        q_r, kc, vct, ks, vst, kw, vwt, gt, jnp.asarray(_overlap_t(), dtype=BF16),
        casts=(ffn2_w_gate[0], ffn2_w_up[0], ffn2_w_down[0],
               w_out[0], xattn_wq[0], xattn_wk[0], xattn_wv[0], xattn_wo[0]))

    km, vm = _memkv(mem, vec(mem_norm[0]), wk_b, wv_b)
    x3 = _outproj_xattn(x1_head, x1, o_hg.reshape(TOKENS, HG_WIDTH), o_nsa.reshape(TOKENS, NSA_WIDTH),
                        w_out_b, vec(xattn_norm[0]), wq_b, km, vm, wo_b)

    (out,) = _ffn(x3, vec(ffn2_norm[0]), w2_gate, w2_up, w2_down, vec(final_norm), final=True)
    return out.reshape(BATCH, SEQ, D_MODEL)
```

```python
import functools

import numpy as np
import jax
import jax.numpy as jnp
from jax import lax
from jax.experimental import pallas as pl
from jax.experimental.pallas import tpu as pltpu

F32 = jnp.float32
BF16 = jnp.bfloat16

D_MODEL = 2048
BATCH = 2
SEQ = 4096
TOKENS = BATCH * SEQ
RMS_EPS = 1e-6
ROPE_THETA = 10000.0
HG_WIDTH = 1024
HG_HEADS = 8
HG_D = 128
HG_CHUNK = 128
HG_LEVELS = (64, 32, 16, 8, 4, 2, 1)
NSA_WIDTH = 1024
NSA_DH = 64
NSA_HEADS = 16
NSA_KV = 2
NSA_REP = 8
NSA_VROWS = NSA_DH + 16
CMP_LEN = 32
CMP_STRIDE = 16
CMP_HIDDEN = 256
N_CMP = (SEQ - CMP_LEN) // CMP_STRIDE + 1
N_CMP_PAD = 256
SEL_LEN = 64
N_SEL = SEQ // SEL_LEN
SEL_TOP = 16
WINDOW = 512
Q_BLOCK = 128
N_QB = SEQ // Q_BLOCK
SEL_SUB = 256
WIN_KEYS = WINDOW + Q_BLOCK
MEM_LEN = 256
X_HEADS = 4
X_DH = 128
D_FF = 5632
IN_SIZES = (1024, 1024, 1024, 1024, 1024, 128, 128, 128, 128, 128, 128, 48)
D_IN = sum(IN_SIZES)
PROJ_NSA_OFF = sum(IN_SIZES[:4])
NSA_PROJ = 2048
PROJ_W = PROJ_NSA_OFF + NSA_PROJ
NEG = -1e30
LOG2E = 1.4426950408889634

V7X_VMEM_BYTES = 64 * 1024 * 1024
VMEM_LIMIT = V7X_VMEM_BYTES - 8 * 1024 * 1024


def _cparams(*sem, flags=None):
    return pltpu.CompilerParams(dimension_semantics=sem, vmem_limit_bytes=VMEM_LIMIT, flags=flags)


def _rms(x, w):
    return x * lax.rsqrt(jnp.mean(x * x, axis=-1, keepdims=True) + RMS_EPS) * w


def _silu(x):
    return x * jax.nn.sigmoid(x)


def _dot(a, b):
    return jnp.dot(a, b, preferred_element_type=F32)


def _dot_f32_by_01(sel, x):
    hi = x.astype(BF16)
    r1 = x - hi.astype(F32)
    mid = r1.astype(BF16)
    lo = (r1 - mid.astype(F32)).astype(BF16)
    n = x.shape[1]
    y = _dot(sel, jnp.concatenate([hi, mid, lo], axis=1))
    return y[:, 0:n] + y[:, n:2 * n] + y[:, 2 * n:3 * n]


def _dot_nt(a, b):
    return lax.dot_general(a, b, (((1,), (1,)), ((), ())), preferred_element_type=F32)


FFN_TM = 512
FFN_TF = 512
FFN_HEAD_TM = 1024
FFN_HEAD_TF = 256


def _ffn_body(x_ref, nw_ref, wg_ref, wu_ref, wd_ref, nw2_ref, *rest,
              final, f32_weights):
    if final:
        o_ref, rest = rest[0], rest[1:]
    else:
        (o_ref, hn_ref), rest = rest[:2], rest[2:]
    if f32_weights:
        w_copy, rest = rest[:3], rest[3:]
    (h_scr,) = rest
    j = pl.program_id(1)
    last = pl.num_programs(1) - 1

    def swiglu_tile(h):
        wg, wu, wd = wg_ref[...], wu_ref[...], wd_ref[...]
        if f32_weights:
            wg, wu, wd = wg.astype(BF16), wu.astype(BF16), wd.astype(BF16)
            for dst, w in zip(w_copy, (wg, wu, wd)):
                dst[...] = w
        g = _dot(h, wg)
        u = _dot(h, wu)
        return _dot((_silu(g) * u).astype(BF16), wd)

    @pl.when(j == 0)
    def _():
        h = _rms(x_ref[...], nw_ref[...]).astype(BF16)
        h_scr[...] = h
        o_ref[...] = swiglu_tile(h)

    @pl.when((j > 0) & (j < last))
    def _():
        o_ref[...] += swiglu_tile(h_scr[...])

    @pl.when(j == last)
    def _():
        y = x_ref[...] + 0.5 * (o_ref[...] + swiglu_tile(h_scr[...]))
        if final:
            o_ref[...] = _rms(y, nw2_ref[...])
        else:
            o_ref[...] = y
            hn_ref[...] = _rms(y, nw2_ref[...]).astype(BF16)


def _ffn(x, nw, wg, wu, wd, nw2, final, first_tile=0, n_tiles=None, tm=FFN_TM, tf=FFN_TF):
    ni, nj = (n_tiles or TOKENS // tm), D_FF // tf
    f32_weights = wg.dtype == F32
    once = dict(pipeline_mode=pl.Buffered(1)) if ni == 1 else {}
    row_in = pl.BlockSpec((tm, D_MODEL), lambda i, j: (i + first_tile, 0), **once)
    row = pl.BlockSpec((tm, D_MODEL), lambda i, j: (i, 0), **once)
    vec = pl.BlockSpec((1, D_MODEL), lambda i, j: (0, 0))
    in_specs = [row_in, vec,
                pl.BlockSpec((D_MODEL, tf), lambda i, j: (0, j)),
                pl.BlockSpec((D_MODEL, tf), lambda i, j: (0, j)),
                pl.BlockSpec((tf, D_MODEL), lambda i, j: (j, 0)),
                vec]
    main_shapes = [jax.ShapeDtypeStruct((ni * tm, D_MODEL), F32)]
    if not final:
        main_shapes.append(jax.ShapeDtypeStruct((ni * tm, D_MODEL), BF16))
    w_copy_specs = in_specs[2:5] if f32_weights else []
    w_copy_shapes = [jax.ShapeDtypeStruct(w.shape, BF16) for w in (wg, wu, wd)] if f32_weights else []
    return pl.pallas_call(
        functools.partial(_ffn_body, final=final, f32_weights=f32_weights),
        grid=(ni, nj), in_specs=in_specs,
        out_specs=tuple([row] * len(main_shapes) + w_copy_specs),
        out_shape=tuple(main_shapes + w_copy_shapes),
        scratch_shapes=[pltpu.VMEM((tm, D_MODEL), BF16)],
        compiler_params=_cparams("parallel", "arbitrary"),
        name="ffn_final" if final else ("ffn_head" if f32_weights else "ffn"),
    )(x, nw, wg, wu, wd, nw2)


PROJ_TM = 1024
PROJ_TN = 512


PROJ_F_TILE0 = IN_SIZES[0] // PROJ_TN
PROJ_F_TILES = IN_SIZES[1] // PROJ_TN


PROJ_FULL_TILES = D_IN // PROJ_TN


def _proj_body(a0_ref, a_ref, wt_ref, tail_ref, o_ref, f_ref, *, n0):
    i = pl.program_id(0)
    j = pl.program_id(1)

    def tile(a):
        w = jnp.where(j >= PROJ_FULL_TILES, tail_ref[...], wt_ref[...]).astype(BF16)
        y = _dot_nt(a, w)
        o_ref[...] = y.astype(BF16)

        @pl.when((j >= PROJ_F_TILE0) & (j < PROJ_F_TILE0 + PROJ_F_TILES))
        def _():
            f_ref[...] = y

    pl.when(i < n0)(lambda: tile(a0_ref[...]))
    pl.when(i >= n0)(lambda: tile(a_ref[...]))


def _proj(a0, a, wt, wt_tail):
    k = a.shape[1]
    n0 = a0.shape[0] // PROJ_TM
    assert a0.shape[0] == n0 * PROJ_TM and a.shape[0] % PROJ_TM == 0
    m = a0.shape[0] + a.shape[0]
    f_tile = lambda i, j: (i, jnp.clip(j - PROJ_F_TILE0, 0, PROJ_F_TILES - 1))
    return pl.pallas_call(
        functools.partial(_proj_body, n0=n0),
        grid=(m // PROJ_TM, PROJ_W // PROJ_TN),
        in_specs=[pl.BlockSpec((PROJ_TM, k), lambda i, j: (jnp.minimum(i, n0 - 1), 0)),
                  pl.BlockSpec((PROJ_TM, k), lambda i, j: (jnp.maximum(i - n0, 0), 0)),
                  pl.BlockSpec((PROJ_TN, k), lambda i, j: (jnp.minimum(j, PROJ_FULL_TILES - 1), 0)),
                  pl.BlockSpec((PROJ_TN, k), lambda i, j: (0, 0))],
        out_specs=(pl.BlockSpec((PROJ_TM, PROJ_TN), lambda i, j: (i, j)),
                   pl.BlockSpec((PROJ_TM, PROJ_TN), f_tile)),
        out_shape=(jax.ShapeDtypeStruct((m, PROJ_W), BF16),
                   jax.ShapeDtypeStruct((m, IN_SIZES[1]), F32)),
        compiler_params=_cparams("parallel", "arbitrary"),
        name="proj_in",
    )(a0, a, wt, wt_tail)


HG_ROWS = 1024
HG_CUM = 256


def _hgrn_body(q_ref, f_ref, i_ref, g_ref, lbl_ref, nw_ref, o_ref, st_ref, k_s, b_s):
    c = pl.program_id(2)

    @pl.when(c == 0)
    def _():
        st_ref[...] = jnp.zeros_like(st_ref)

    l0 = lbl_ref[0:1, :]
    l1 = lbl_ref[1:2, :]
    lmax = jnp.maximum(l0, l1)
    e0 = jnp.exp(l0 - lmax)
    lb = e0 / (e0 + jnp.exp(l1 - lmax))

    C = HG_CHUNK
    f = lb + (1.0 - lb) * jax.nn.sigmoid(f_ref[...])
    k_s[...] = 1.0 - f
    r_i = lax.broadcasted_iota(jnp.int32, (HG_CUM, HG_CUM), 0)
    c_i = lax.broadcasted_iota(jnp.int32, (HG_CUM, HG_CUM), 1)
    tri = jnp.where((r_i >= c_i) & (r_i // C == c_i // C), 1.0, 0.0).astype(BF16)
    logf = jnp.log2(f)
    for r0 in range(0, HG_ROWS, HG_CUM):
        b_s[r0:r0 + HG_CUM, :] = _dot_f32_by_01(tri, logf[r0:r0 + HG_CUM])

    t_i = lax.broadcasted_iota(jnp.int32, (C, C), 0)
    s_i = lax.broadcasted_iota(jnp.int32, (C, C), 1)
    level_mask = [(t_i // (2 * w) == s_i // (2 * w)) & (t_i % (2 * w) >= w) & (s_i % (2 * w) < w)
                  for w in HG_LEVELS]
    sub_r = lax.broadcasted_iota(jnp.int32, (8, HG_D), 0)
    row_i = lax.broadcasted_iota(jnp.int32, (C, HG_D), 0)
    right_sign = {w: jnp.where(row_i % (2 * w) >= w, 1.0, -1.0) for w in HG_LEVELS if w < 8}

    def neg_abs_diff(w, r0, b):
        row = lambda r, n: jnp.broadcast_to(b_s[r0 + r:r0 + r + 1, :], (n, HG_D))
        if w >= 8:
            parts = []
            for p0 in range(0, C, 2 * w):
                ref = row(p0 + w - 1, w)
                parts += [ref - b[p0:p0 + w], b[p0 + w:p0 + 2 * w] - ref]
            return jnp.concatenate(parts, axis=0)
        if w == 4:
            bref = jnp.concatenate([row(p0 + 3, 8) for p0 in range(0, C, 8)], axis=0)
        elif w == 2:
            bref = jnp.concatenate([jnp.where(sub_r < 4, row(p0 + 1, 8), row(p0 + 5, 8))
                                    for p0 in range(0, C, 8)], axis=0)
        else:
            bref = jnp.where(row_i % 2 == 1, pltpu.roll(b, 1, 0), b)
        return (b - bref) * right_sign[w]

    chunks = [ci * C for ci in range(HG_ROWS // C)]
    rows = lambda ref, r0: ref[r0:r0 + C, :]
    att = [jnp.zeros((C, C), F32) for _ in chunks]
    q16 = [rows(q_ref, r0) for r0 in chunks]
    k16 = [rows(k_s, r0).astype(BF16) for r0 in chunks]
    for w, mask in zip(HG_LEVELS, level_mask):
        for n, r0 in enumerate(chunks):
            b = rows(b_s, r0)
            e = jnp.exp2(neg_abs_diff(w, r0, b)).astype(BF16)
            att[n] = jnp.where(mask, _dot_nt(q16[n] * e, k16[n] * e), att[n])
    o_intra = []
    for n, r0 in enumerate(chunks):
        q, k, v = q16[n].astype(F32), rows(k_s, r0), rows(i_ref, r0)
        o_intra.append(_dot(att[n].astype(BF16), v)
                       + jnp.sum(q * k, axis=-1, keepdims=True) * v.astype(F32))
    upd = []
    for r0 in chunks:
        bl = b_s[r0 + C - 1:r0 + C, :]
        kd = rows(k_s, r0) * jnp.exp2(bl - rows(b_s, r0))
        v_t = rows(i_ref, r0).astype(F32).T.astype(BF16)
        upd.append((jnp.exp2(bl), _dot(v_t, kd.astype(BF16))))
    st_t = st_ref[...]
    for n, r0 in enumerate(chunks):
        qe = (q16[n].astype(F32) * jnp.exp2(rows(b_s, r0))).astype(BF16)
        o = o_intra[n] + _dot_nt(qe, st_t.astype(BF16))
        st_t = st_t * upd[n][0] + upd[n][1]
        o = o * lax.rsqrt(jnp.mean(o * o, axis=-1, keepdims=True) + RMS_EPS)
        o_ref[r0:r0 + C, :] = (o * nw_ref[...] * _silu(rows(g_ref, r0).astype(F32))).astype(BF16)
    st_ref[...] = st_t


def _hgrn(proj, proj_f, lb_logits, norm_w):
    p3 = proj.reshape(BATCH, SEQ, PROJ_W)
    f3 = proj_f.reshape(BATCH, SEQ, HG_WIDTH)

    def col(off):
        return pl.BlockSpec((None, HG_ROWS, HG_D), lambda b, h, c: (b, c, off + h))

    return pl.pallas_call(
        _hgrn_body,
        grid=(BATCH, HG_HEADS, SEQ // HG_ROWS),
        in_specs=[col(0), col(0), col(2 * HG_HEADS), col(3 * HG_HEADS),
                  pl.BlockSpec((2, HG_D), lambda b, h, c: (0, h)),
                  pl.BlockSpec((1, HG_D), lambda b, h, c: (0, h))],
        out_specs=pl.BlockSpec((None, HG_ROWS, HG_D), lambda b, h, c: (b, c, h)),
        out_shape=jax.ShapeDtypeStruct((BATCH, SEQ, HG_WIDTH), BF16),
        scratch_shapes=[pltpu.VMEM((HG_D, HG_D), F32),
                        pltpu.VMEM((HG_ROWS, HG_D), F32),
                        pltpu.VMEM((HG_ROWS, HG_D), F32)],
        compiler_params=_cparams("parallel", "parallel", "arbitrary"),
        name="hgrn2",
    )(p3, f3, p3, p3, lb_logits, norm_w)


PREP_TM = 256


def _prep_body(p_ref, pos4_ref, inv_ref, q_ref, kc_ref, vc_ref, ks_ref, kw_ref,
               vst_ref, vwt_ref, gt_ref, cos_s, sin_s):
    n_freq = NSA_DH // 2
    ang4 = pos4_ref[...] * inv_ref[...]
    seg = lax.broadcasted_iota(jnp.int32, (PREP_TM // 4, 128), 1) // n_freq
    for table, dst in ((jnp.cos(ang4), cos_s), (jnp.sin(ang4), sin_s)):
        for u in range(4):
            one = jnp.where(seg == u, table, 0.0)
            full = one
            for k in range(1, 4):
                full = full + pltpu.roll(one, k * n_freq, 1)
            dst[pl.ds(u, PREP_TM // 4, stride=4), :] = full
    cos = cos_s[...]
    sin = sin_s[...]
    lane = lax.broadcasted_iota(jnp.int32, (PREP_TM, 128), 1)
    lo = (lane & (NSA_DH // 2)) == 0
    sin_signed = jnp.where(lo, -sin, sin)

    def rope(x):
        rot = jnp.where(lo, pltpu.roll(x, 128 - NSA_DH // 2, 1), pltpu.roll(x, NSA_DH // 2, 1))
        return x * cos + rot * sin_signed

    cols = lambda c0: p_ref[:, c0:c0 + 128].astype(F32)
    scale = NSA_DH ** -0.5 * LOG2E
    for cblk in range(NSA_WIDTH // 128):
        q_ref[cblk * 128:(cblk + 1) * 128, :] = (rope(cols(cblk * 128)) * scale).T.astype(BF16)
    c_kc, c_vc, c_ks, c_vs, c_kw, c_vw, c_gate = (int(c) for c in np.cumsum(IN_SIZES[4:11]))
    kc_ref[...] = rope(cols(c_kc))
    vc_ref[...] = cols(c_vc)
    ks_ref[:, 0:128] = rope(cols(c_ks)).astype(BF16)
    blk = (pl.program_id(1) * PREP_TM + lax.broadcasted_iota(jnp.int32, (PREP_TM, 128), 0)) // SEL_LEN
    ks_ref[:, 128:256] = jnp.where(lane == blk, 1.0, 0.0).astype(BF16)
    kw_ref[...] = rope(cols(c_kw)).astype(BF16)
    ones = jnp.ones((NSA_VROWS - NSA_DH, PREP_TM), BF16)
    for v_ref, c0 in ((vst_ref, c_vs), (vwt_ref, c_vw)):
        vt = cols(c0).T.astype(BF16)
        for g in range(NSA_KV):
            v_ref[g * NSA_VROWS:g * NSA_VROWS + NSA_DH, :] = vt[g * NSA_DH:(g + 1) * NSA_DH]
            v_ref[g * NSA_VROWS + NSA_DH:(g + 1) * NSA_VROWS, :] = ones
    gt_ref[...] = jax.nn.sigmoid(cols(c_gate)).T[0:3 * NSA_HEADS, :]


def _nsa_prep(proj, pos4, inv128):
    nt = SEQ // PREP_TM
    p3 = proj.reshape(BATCH, SEQ, PROJ_W)
    nat = lambda w: pl.BlockSpec((None, PREP_TM, w), lambda b, i: (b, i, 0))
    tr = lambda r: pl.BlockSpec((None, r, PREP_TM), lambda b, i: (b, 0, i))
    sds = jax.ShapeDtypeStruct
    return pl.pallas_call(
        _prep_body,
        grid=(BATCH, nt),
        in_specs=[pl.BlockSpec((None, PREP_TM, NSA_PROJ), lambda b, i: (b, i, PROJ_NSA_OFF // NSA_PROJ)),
                  pl.BlockSpec((None, PREP_TM // 4, 128), lambda b, i: (b, i, 0)),
                  pl.BlockSpec((1, 128), lambda b, i: (0, 0))],
        out_specs=(tr(NSA_WIDTH), nat(128), nat(128), nat(256), nat(128),
                   tr(NSA_KV * NSA_VROWS), tr(NSA_KV * NSA_VROWS), tr(3 * NSA_HEADS)),
        out_shape=(sds((BATCH, NSA_WIDTH, SEQ), BF16),
                   sds((BATCH, SEQ, 128), F32),
                   sds((BATCH, SEQ, 128), F32),
                   sds((BATCH, SEQ, 256), BF16),
                   sds((BATCH, SEQ, 128), BF16),
                   sds((BATCH, NSA_KV * NSA_VROWS, SEQ), BF16),
                   sds((BATCH, NSA_KV * NSA_VROWS, SEQ), BF16),
                   sds((BATCH, 3 * NSA_HEADS, SEQ), F32)),
        scratch_shapes=[pltpu.VMEM((PREP_TM, 128), F32)] * 2,
        compiler_params=_cparams("parallel", "parallel"),
        name="nsa_prep",
    )(p3, pos4, inv128)


def _cmp_body(tk_ref, tv_ref, pe_ref, kw1_ref, kw2_ref, vw1_ref, vw2_ref, kc_ref, vct_ref, y1_s, y2_s):
    row = lax.broadcasted_iota(jnp.int32, (N_CMP_PAD, NSA_KV * CMP_HIDDEN), 0)

    def mlp(t_ref, w1_ref, w2_ref):
        y1_s[...] = jnp.zeros_like(y1_s)
        y2_s[...] = jnp.zeros_like(y2_s)
        for l in range(CMP_STRIDE):
            x = t_ref[pl.ds(l, N_CMP_PAD, stride=CMP_STRIDE), :]
            y1_s[...] += _dot((x + pe_ref[l:l + 1, :]).astype(BF16), w1_ref[l])
            y2_s[...] += _dot((x + pe_ref[CMP_STRIDE + l:CMP_STRIDE + l + 1, :]).astype(BF16),
                              w1_ref[CMP_STRIDE + l])
        hid = jnp.where(row < N_CMP, y1_s[...] + pltpu.roll(y2_s[...], N_CMP_PAD - 1, 0), 0.0)
        return _dot(_silu(hid).astype(BF16), w2_ref[...])

    kc_ref[...] = mlp(tk_ref, kw1_ref, kw2_ref).astype(BF16)
    y1_s[:, 0:128] = mlp(tv_ref, vw1_ref, vw2_ref)
    vt = y1_s[:, 0:128].T.astype(BF16)
    ones = jnp.ones((NSA_VROWS - NSA_DH, N_CMP_PAD), BF16)
    for g in range(NSA_KV):
        vct_ref[g * NSA_VROWS:g * NSA_VROWS + NSA_DH, :] = vt[g * NSA_DH:(g + 1) * NSA_DH]
        vct_ref[g * NSA_VROWS + NSA_DH:(g + 1) * NSA_VROWS, :] = ones


def _compress(tk, tv, pe2, kw1, kw2, vw1, vw2):
    seg = pl.BlockSpec((None, SEQ, 128), lambda b: (b, 0, 0))
    full2 = lambda a: pl.BlockSpec(a.shape, lambda b: (0,) * a.ndim)
    return pl.pallas_call(
        _cmp_body,
        grid=(BATCH,),
        in_specs=[seg, seg, full2(pe2), full2(kw1), full2(kw2), full2(vw1), full2(vw2)],
        out_specs=(pl.BlockSpec((None, N_CMP_PAD, 128), lambda b: (b, 0, 0)),
                   pl.BlockSpec((None, NSA_KV * NSA_VROWS, N_CMP_PAD), lambda b: (b, 0, 0))),
        out_shape=(jax.ShapeDtypeStruct((BATCH, N_CMP_PAD, 128), BF16),
                   jax.ShapeDtypeStruct((BATCH, NSA_KV * NSA_VROWS, N_CMP_PAD), BF16)),
        scratch_shapes=[pltpu.VMEM((N_CMP_PAD, NSA_KV * CMP_HIDDEN), F32)] * 2,
        compiler_params=_cparams("parallel"),
        name="nsa_compress",
    )(tk, tv, pe2, kw1, kw2, vw1, vw2)


NSA_NL = NSA_REP * Q_BLOCK


def _nsa_body(q_ref, kc_ref, vct_ref, ks_ref, vst_ref, kw_ref, vwt_ref, gt_ref, ovt_ref, *rest, n_casts):
    cast_in, o_ref, cast_out = rest[:n_casts], rest[n_casts], rest[n_casts + 1:2 * n_casts + 1]
    qa_s, sc_s, rk_s, s0_s, s1_s, sc_buf, sw_buf, sd_buf, m_s, acc_s = rest[2 * n_casts + 1:]
    g = pl.program_id(1)
    qb = pl.program_id(2)
    q0 = pl.multiple_of(qb * Q_BLOCK, Q_BLOCK)
    is_g0 = g == 0

    zero_slab = jnp.zeros((NSA_DH, Q_BLOCK), BF16)
    for r in range(NSA_REP):
        s = q_ref[r * NSA_DH:(r + 1) * NSA_DH, :]
        qa_s[0:NSA_DH, r * Q_BLOCK:(r + 1) * Q_BLOCK] = jnp.where(is_g0, s, zero_slab)
        qa_s[NSA_DH:2 * NSA_DH, r * Q_BLOCK:(r + 1) * Q_BLOCK] = jnp.where(is_g0, zero_slab, s)
    qa_s[128 + N_SEL:256, :] = jnp.zeros((128 - N_SEL, NSA_NL), BF16)

    tq = q0 + lax.broadcasted_iota(jnp.int32, (1, Q_BLOCK), 1)

    def mask_pair(s, valid):
        return jnp.concatenate([jnp.where(valid, s[:, r * Q_BLOCK:(r + 1) * Q_BLOCK], NEG)
                                for r in range(NSA_REP)], axis=1)

    n_i = lax.broadcasted_iota(jnp.int32, (N_CMP_PAD, Q_BLOCK), 0)
    valid_c = (n_i * CMP_STRIDE + (CMP_LEN - 1) <= tq) & (n_i < N_CMP)
    sc_buf[...] = mask_pair(_dot(kc_ref[...], qa_s[0:128, :]).astype(BF16), valid_c)

    w0 = pl.multiple_of(jnp.maximum(q0 - WINDOW, 0), Q_BLOCK)
    dpos = tq - (w0 + lax.broadcasted_iota(jnp.int32, (WIN_KEYS, Q_BLOCK), 0))
    sw_buf[...] = mask_pair(_dot(kw_ref[pl.ds(w0, WIN_KEYS), :], qa_s[0:128, :]).astype(BF16),
                            (dpos >= 0) & (dpos < WINDOW))

    sc = sc_buf[...]
    m_c = jnp.max(sc, axis=0, keepdims=True)
    e_c = jnp.exp2(sc - m_c)
    acc_c = _dot(vct_ref[...], e_c)
    inv_c = jnp.where(m_c.astype(F32) > 0.5 * NEG, 1.0 / jnp.maximum(acc_c[NSA_DH:NSA_DH + 1], 1e-30), 0.0)
    imp_h = _dot(ovt_ref[...], e_c) * inv_c

    d_i = lax.broadcasted_iota(jnp.int32, (Q_BLOCK, Q_BLOCK), 0)
    t_i = lax.broadcasted_iota(jnp.int32, (Q_BLOCK, Q_BLOCK), 1)
    sd_buf[...] = mask_pair(_dot(ks_ref[pl.ds(q0, Q_BLOCK), 0:128], qa_s[0:128, :]).astype(BF16),
                            d_i <= t_i)

    sw = sw_buf[...]
    acc_w = _dot(vwt_ref[:, pl.ds(w0, WIN_KEYS)], jnp.exp2(sw - jnp.max(sw, axis=0, keepdims=True)))
    inv_w = 1.0 / jnp.maximum(acc_w[NSA_DH:NSA_DH + 1], 1e-30)

    s = sd_buf[...]
    m16 = jnp.max(s, axis=0, keepdims=True)
    m_s[...] = m16.astype(F32)
    acc_s[...] = _dot(vst_ref[:, pl.ds(q0, Q_BLOCK)], jnp.exp2(s - m16))

    imp = imp_h[:, 0:Q_BLOCK]
    for r in range(1, NSA_REP):
        imp = imp + imp_h[:, r * Q_BLOCK:(r + 1) * Q_BLOCK]
    j_i = lax.broadcasted_iota(jnp.int32, (N_SEL, Q_BLOCK), 0)
    cur = tq // SEL_LEN
    forced = (j_i == 0) | (j_i == cur) | (j_i == cur - 1)
    score = jnp.where(forced, jnp.inf, jnp.where(j_i > cur, -jnp.inf, imp))
    sc_s[...] = score
    rk_s[...] = jnp.zeros_like(rk_s)
    sub8 = lax.broadcasted_iota(jnp.int32, (8, Q_BLOCK), 0)
    for grp in range(N_SEL // 8):
        @pl.when(8 * grp <= 2 * qb + 1)
        def _():
            for v in range(N_SEL // 8):
                sv = sc_s[8 * v:8 * v + 8, :]
                part = jnp.zeros((8, Q_BLOCK), F32)
                for jp in range(8 * grp, 8 * grp + 8):
                    row = sc_s[jp:jp + 1, :]
                    if v > grp:
                        part = part + jnp.where(row >= sv, 1.0, 0.0)
                    elif v < grp:
                        part = part + jnp.where(row > sv, 1.0, 0.0)
                    else:
                        part = part + jnp.where(sub8 + 8 * v > jp, jnp.where(row >= sv, 1.0, 0.0),
                                                jnp.where(row > sv, 1.0, 0.0))
                rk_s[8 * v:8 * v + 8, :] += part
    bias = jnp.where((rk_s[...] < SEL_TOP) & (j_i < 2 * qb), 0.0, NEG).astype(BF16)
    for r in range(NSA_REP):
        qa_s[128:128 + N_SEL, r * Q_BLOCK:(r + 1) * Q_BLOCK] = bias

    last_sub = SEQ // SEL_SUB - 1

    def scores_into(buf, c):
        k0 = pl.multiple_of(jnp.minimum(c, last_sub) * SEL_SUB, SEL_SUB)
        buf[...] = _dot(ks_ref[pl.ds(k0, SEL_SUB), :], qa_s[...]).astype(BF16)

    def softmax_from(buf, c):
        k0 = pl.multiple_of(c * SEL_SUB, SEL_SUB)
        s = buf[...]
        m = m_s[...]
        m_new = jnp.maximum(m, jnp.max(s, axis=0, keepdims=True).astype(F32))
        m_s[...] = m_new
        pr = jnp.exp2(s - m_new.astype(BF16))
        acc_s[...] = acc_s[...] * jnp.exp2(m - m_new) + _dot(vst_ref[:, pl.ds(k0, SEL_SUB)], pr)

    scores_into(s0_s, 0)

    n_late = min(2, n_casts)
    for src, dst in zip(cast_in[n_late:], cast_out[n_late:]):
        dst[...] = src[...].astype(BF16)

    def sel_step(i, carry):
        c = 2 * i
        scores_into(s1_s, c + 1)
        softmax_from(s0_s, c)
        scores_into(s0_s, c + 2)
        softmax_from(s1_s, c + 1)
        return carry

    n_main = (qb * Q_BLOCK + 2 * SEL_SUB - 1) // (2 * SEL_SUB)
    lax.fori_loop(0, n_main, sel_step, 0)
    inv_s = 1.0 / jnp.maximum(acc_s[NSA_DH:NSA_DH + 1, :], 1e-30)

    gall = gt_ref[...]
    ggrp = jnp.where(is_g0, gall[0:3 * NSA_REP], gall[3 * NSA_REP:3 * NSA_HEADS])

    for src, dst in zip(cast_in[:n_late], cast_out[:n_late]):
        dst[...] = src[...].astype(BF16)

    for p2 in range(NSA_REP // 2):
        halves = []
        for hh in range(2):
            r = 2 * p2 + hh
            sl = slice(r * Q_BLOCK, (r + 1) * Q_BLOCK)
            g_c, g_s, g_w = (ggrp[3 * r + br:3 * r + br + 1, :] for br in range(3))
            halves.append((g_c * inv_c[:, sl]) * acc_c[0:NSA_DH, sl]
                          + (g_s * inv_s[:, sl]) * acc_s[0:NSA_DH, sl]
                          + (g_w * inv_w[:, sl]) * acc_w[0:NSA_DH, sl])
        o_ref[:, p2 * 128:(p2 + 1) * 128] = jnp.concatenate(halves, axis=0).T.astype(BF16)


def _nsa_attn(q_r, kc, vct, ks, vst, kw, vwt, gt, ovt, casts=()):
    per_b = lambda r, c: pl.BlockSpec((None, r, c), lambda b, g, i: (b, 0, 0))
    per_bg = lambda r, c: pl.BlockSpec((None, r, c), lambda b, g, i: (b, g, 0))
    const = lambda a: pl.BlockSpec(a.shape, lambda b, g, i: (0, 0))
    n_steps = BATCH * NSA_KV * N_QB
    cast_specs = []
    for a in casts:
        rows = a.shape[0]
        if rows % (16 * n_steps) == 0:
            cast_specs.append(pl.BlockSpec((rows // n_steps, a.shape[1]),
                                           lambda b, g, i: ((b * NSA_KV + g) * N_QB + i, 0)))
        else:
            cast_specs.append(pl.BlockSpec((rows // N_QB, a.shape[1]),
                                           lambda b, g, i: (jnp.where(b + g == 0, i, N_QB - 1), 0)))
    return pl.pallas_call(
        functools.partial(_nsa_body, n_casts=len(casts)),
        grid=(BATCH, NSA_KV, N_QB),
        in_specs=[pl.BlockSpec((None, NSA_REP * NSA_DH, Q_BLOCK), lambda b, g, i: (b, g, i)),
                  per_b(N_CMP_PAD, 128), per_bg(NSA_VROWS, N_CMP_PAD),
                  per_b(SEQ, 256), per_bg(NSA_VROWS, SEQ),
                  per_b(SEQ, 128), per_bg(NSA_VROWS, SEQ),
                  pl.BlockSpec((None, 3 * NSA_HEADS, Q_BLOCK), lambda b, g, i: (b, 0, i)),
                  const(ovt)] + cast_specs,
        out_specs=tuple([pl.BlockSpec((None, Q_BLOCK, NSA_REP * NSA_DH), lambda b, g, i: (b, i, g))]
                        + cast_specs),
        out_shape=tuple([jax.ShapeDtypeStruct((BATCH, SEQ, NSA_WIDTH), BF16)]
                        + [jax.ShapeDtypeStruct(a.shape, BF16) for a in casts]),
        scratch_shapes=[pltpu.VMEM((256, NSA_NL), BF16),
                        pltpu.VMEM((N_SEL, Q_BLOCK), F32), pltpu.VMEM((N_SEL, Q_BLOCK), F32),
                        pltpu.VMEM((SEL_SUB, NSA_NL), BF16), pltpu.VMEM((SEL_SUB, NSA_NL), BF16),
                        pltpu.VMEM((N_CMP_PAD, NSA_NL), BF16), pltpu.VMEM((WIN_KEYS, NSA_NL), BF16),
                        pltpu.VMEM((Q_BLOCK, NSA_NL), BF16),
                        pltpu.VMEM((1, NSA_NL), F32), pltpu.VMEM((NSA_VROWS, NSA_NL), F32)],
        compiler_params=_cparams(*(("arbitrary",) * 3 if casts else ("parallel", "parallel", "arbitrary"))),
        name="nsa_attn",
    )(q_r, kc, vct, ks, vst, kw, vwt, gt, ovt, *casts)


OUT_TM = 512


def _outx_body(x0_ref, x_ref, oh_ref, on_ref, w_ref, nw_ref, wq_ref, k_ref, v_ref, wo_ref, o_ref, *, n0):
    def tile(x):
        y = (x + _dot(oh_ref[...], w_ref[0:HG_WIDTH, :])
             + _dot(on_ref[...], w_ref[HG_WIDTH:HG_WIDTH + NSA_WIDTH, :]))
        hx = _rms(y, nw_ref[...]).astype(BF16)
        q = (_dot(hx, wq_ref[...]) * (X_DH ** -0.5)).astype(BF16)
        heads = []
        for h in range(X_HEADS):
            sl = slice(h * X_DH, (h + 1) * X_DH)
            s = _dot_nt(q[:, sl], k_ref[:, sl])
            e = jnp.exp(s - jnp.max(s, axis=-1, keepdims=True))
            p = e / jnp.sum(e, axis=-1, keepdims=True)
            heads.append(_dot(p.astype(BF16), v_ref[:, sl]))
        o_ref[...] = y + _dot(jnp.concatenate(heads, axis=1).astype(BF16), wo_ref[...])

    i = pl.program_id(0)
    pl.when(i < n0)(lambda: tile(x0_ref[...]))
    pl.when(i >= n0)(lambda: tile(x_ref[...]))


def _outproj_xattn(x1_0, x1, o_hg, o_nsa, w_out, nw, wq, k, v, wo):
    width = X_HEADS * X_DH
    tiles_per_b = SEQ // OUT_TM
    n0 = x1_0.shape[0] // OUT_TM
    assert x1_0.shape[0] == n0 * OUT_TM and x1_0.shape[0] + x1.shape[0] == TOKENS
    row = lambda w: pl.BlockSpec((OUT_TM, w), lambda i: (i, 0))
    const = lambda r, c: pl.BlockSpec((r, c), lambda i: (0, 0))
    kv = pl.BlockSpec((None, MEM_LEN, width), lambda i: (i // tiles_per_b, 0, 0))
    return pl.pallas_call(
        functools.partial(_outx_body, n0=n0),
        grid=(TOKENS // OUT_TM,),
        in_specs=[pl.BlockSpec((OUT_TM, D_MODEL), lambda i: (jnp.minimum(i, n0 - 1), 0)),
                  pl.BlockSpec((OUT_TM, D_MODEL), lambda i: (jnp.maximum(i - n0, 0), 0)),
                  row(HG_WIDTH), row(NSA_WIDTH), const(D_MODEL, D_MODEL), const(1, D_MODEL),
                  const(D_MODEL, width), kv, kv, const(width, D_MODEL)],
        out_specs=row(D_MODEL),
        out_shape=jax.ShapeDtypeStruct((TOKENS, D_MODEL), F32),
        compiler_params=_cparams("parallel"),
        name="out_proj_xattn",
    )(x1_0, x1, o_hg, o_nsa, w_out, nw, wq, k, v, wo)


def _memkv_body(m_ref, nw_ref, wk_ref, wv_ref, k_ref, v_ref):
    hm = _rms(m_ref[...], nw_ref[...]).astype(BF16)
    k_ref[...] = _dot(hm, wk_ref[...]).astype(BF16)
    v_ref[...] = _dot(hm, wv_ref[...]).astype(BF16)


def _memkv(mem, nw, wk, wv):
    width = X_HEADS * X_DH
    wspec = pl.BlockSpec((D_MODEL, width), lambda b: (0, 0))
    ospec = pl.BlockSpec((None, MEM_LEN, width), lambda b: (b, 0, 0))
    osh = jax.ShapeDtypeStruct((BATCH, MEM_LEN, width), BF16)
    return pl.pallas_call(
        _memkv_body,
        grid=(BATCH,),
        in_specs=[pl.BlockSpec((None, MEM_LEN, D_MODEL), lambda b: (b, 0, 0)),
                  pl.BlockSpec((1, D_MODEL), lambda b: (0, 0)), wspec, wspec],
        out_specs=(ospec, ospec), out_shape=(osh, osh),
        compiler_params=_cparams("parallel"),
        name="xattn_memkv",
    )(mem, nw, wk, wv)


def _overlap_t():
    c0 = np.arange(N_CMP)[:, None] * CMP_STRIDE
    s0 = np.arange(N_SEL)[None, :] * SEL_LEN
    ov = np.clip(np.minimum(c0 + CMP_LEN, s0 + SEL_LEN) - np.maximum(c0, s0), 0, None) / CMP_LEN
    out = np.zeros((N_SEL, N_CMP_PAD), np.float32)
    out[:, :N_CMP] = ov.T
    return out


def kernel(x, mem, positions, ffn1_norm, ffn1_w_gate, ffn1_w_up, ffn1_w_down, mix_norm, w_in, hgrn_lb_logits, hgrn_out_norm, nsa_cmp_pe, nsa_cmp_k_w1, nsa_cmp_k_w2, nsa_cmp_v_w1, nsa_cmp_v_w2, w_out, xattn_norm, mem_norm, xattn_wq, xattn_wk, xattn_wv, xattn_wo, ffn2_norm, ffn2_w_gate, ffn2_w_up, ffn2_w_down, final_norm):
    bf = lambda a: a.astype(BF16)
    vec = lambda a: a.reshape(1, -1).astype(F32)
    x2d = x.reshape(TOKENS, D_MODEL)

    x1_head, h_head, w1_gate, w1_up, w1_down = _ffn(
        x2d, vec(ffn1_norm[0]), ffn1_w_gate[0], ffn1_w_up[0], ffn1_w_down[0], vec(mix_norm[0]),
        final=False, n_tiles=1, tm=FFN_HEAD_TM, tf=FFN_HEAD_TF)
    x1, h_mix = _ffn(x2d, vec(ffn1_norm[0]), w1_gate, w1_up, w1_down, vec(mix_norm[0]), final=False,
                     first_tile=FFN_HEAD_TM // FFN_TM, n_tiles=(TOKENS - FFN_HEAD_TM) // FFN_TM)

    w_t = w_in[0].T
    w_tail = jnp.pad(w_t[PROJ_FULL_TILES * PROJ_TN:], ((0, PROJ_W - D_IN), (0, 0)))
    proj, proj_f = _proj(h_head, h_mix, w_t, w_tail)

    o_hg = _hgrn(proj, proj_f, hgrn_lb_logits.astype(F32), vec(hgrn_out_norm[0]))

    inv = ROPE_THETA ** (-jnp.arange(NSA_DH // 2, dtype=F32) / (NSA_DH // 2))
    inv128 = jnp.tile(inv, 128 // (NSA_DH // 2)).reshape(1, 128)
    pos4 = jnp.repeat(positions.astype(F32).reshape(BATCH, SEQ // 4, 4), NSA_DH // 2, axis=-1)
    q_r, kc_tok, vc_tok, ks, kw, vst, vwt, gt = _nsa_prep(proj, pos4, inv128)

    def over_groups(w):
        z = jnp.zeros_like(w)
        return bf(jnp.concatenate([jnp.concatenate([w, z], axis=-1), jnp.concatenate([z, w], axis=-1)], axis=-2))

    per_pos = lambda w1: over_groups(w1.reshape(CMP_LEN, NSA_DH, CMP_HIDDEN))
    pe = nsa_cmp_pe[0].astype(F32)
    kc, vct = _compress(kc_tok, vc_tok, jnp.concatenate([pe, pe], axis=1),
                        per_pos(nsa_cmp_k_w1[0]), over_groups(nsa_cmp_k_w2[0]),
                        per_pos(nsa_cmp_v_w1[0]), over_groups(nsa_cmp_v_w2[0]))
    o_nsa, w2_gate, w2_up, w2_down, w_out_b, wq_b, wk_b, wv_b, wo_b = _nsa_attn(
        q_r, kc, vct, ks, vst, kw, vwt, gt, jnp.asarray(_overlap_t(), dtype=BF16),
        casts=(ffn2_w_gate[0], ffn2_w_up[0], ffn2_w_down[0],
               w_out[0], xattn_wq[0], xattn_wk[0], xattn_wv[0], xattn_wo[0]))

    km, vm = _memkv(mem, vec(mem_norm[0]), wk_b, wv_b)
    x3 = _outproj_xattn(x1_head, x1, o_hg.reshape(TOKENS, HG_WIDTH), o_nsa.reshape(TOKENS, NSA_WIDTH),
                        w_out_b, vec(xattn_norm[0]), wq_b, km, vm, wo_b)

    (out,) = _ffn(x3, vec(ffn2_norm[0]), w2_gate, w2_up, w2_down, vec(final_norm), final=True)
    return out.reshape(BATCH, SEQ, D_MODEL)
```
